```python
import math
import jax, jax.numpy as jnp
from jax import lax
import numpy as np

D_MODEL = 1024
BATCH = 8
SEQ = 2048
DEPTH = 1
DEC_BATCH = 128
DEC_SEQ = 1
PAST_LEN = 16384
PAGE_SIZE = 128

GDN_HEADS = 4
GDN_HEAD_DIM = 128
GDN_WIDTH = GDN_HEADS * GDN_HEAD_DIM
CONV_W = 4
GDN_CHUNK = 64
S5_WIDTH = D_MODEL - GDN_WIDTH
S5_GROUP = 16
S5_GROUPS = S5_WIDTH // S5_GROUP
S5_STATE = 64
MIX_WIDTH = GDN_WIDTH + S5_WIDTH
D_FF = 4 * D_MODEL
N_PROJ = 3 * GDN_WIDTH + GDN_WIDTH + 2 * GDN_HEADS + S5_WIDTH
DN_ALPHA = (2.0 * DEPTH) ** 0.25
DN_BETA = (8.0 * DEPTH) ** -0.25
NORM_EPS = 1e-6

kernel_name = "hymba_gdn_s5_deepnorm_step"


def layernorm(x, g, b):
    xf = x.astype(jnp.float32)
    mu = jnp.mean(xf, axis=-1, keepdims=True)
    var = jnp.mean(jnp.square(xf - mu), axis=-1, keepdims=True)
    y = (xf - mu) * lax.rsqrt(var + NORM_EPS) * g.astype(jnp.float32) + b.astype(jnp.float32)
    return y.astype(x.dtype)


def l2norm(a):
    af = a.astype(jnp.float32)
    return af * lax.rsqrt(jnp.sum(af * af, axis=-1, keepdims=True) + NORM_EPS)


def causal_short_conv(x, buf, w):
    xc = jnp.concatenate([buf.astype(x.dtype), x], axis=1)
    y = lax.conv_general_dilated(xc, w[:, None, :].astype(x.dtype), window_strides=(1,), padding='VALID',
                                 dimension_numbers=('NWC', 'WIO', 'NWC'), feature_group_count=x.shape[-1])
    return jax.nn.silu(y), xc[:, xc.shape[1] - (CONV_W - 1):]


def gated_delta_chunked(q, k, v, g, beta, s0):
    f32 = jnp.float32
    bsz, T = q.shape[0], q.shape[1]
    C = min(GDN_CHUNK, T)
    n = -(-T // C)
    pad = n * C - T

    def prep(a):
        a = a.astype(f32)
        if pad:
            a = jnp.pad(a, [(0, 0), (0, pad)] + [(0, 0)] * (a.ndim - 2))
        a = a.reshape((bsz, n, C) + a.shape[2:])
        return jnp.moveaxis(a, (1, 3), (0, 2))

    q, k, v, g, beta = prep(q), prep(k), prep(v), prep(g), prep(beta)
    q = q * (GDN_HEAD_DIM ** -0.5)
    gc = jnp.cumsum(g, axis=-1)
    tri = jnp.tril(jnp.ones((C, C), bool))
    strict = jnp.tril(jnp.ones((C, C), bool), -1)
    diff = gc[..., :, None] - gc[..., None, :]
    decay = jnp.where(tri, jnp.exp(jnp.where(tri, diff, 0.0)), 0.0)
    k_beta = k * beta[..., None]
    v_beta = v * beta[..., None]
    L = jnp.where(strict, jnp.einsum('nbhid,nbhjd->nbhij', k_beta, k) * decay, 0.0)
    rhs = jnp.concatenate([v_beta, k_beta * jnp.exp(gc)[..., None]], axis=-1)
    sol = lax.linalg.triangular_solve(L + jnp.eye(C, dtype=f32), rhs, left_side=True, lower=True,
                                      unit_diagonal=True)
    u = sol[..., :GDN_HEAD_DIM]
    w = sol[..., GDN_HEAD_DIM:]
    qk = jnp.where(tri, jnp.einsum('nbhid,nbhjd->nbhij', q, k) * decay, 0.0)

    def step(S, xs):
        q_c, k_c, u_c, w_c, qk_c, gc_c = xs
        v_new = u_c - jnp.einsum('bhcd,bhde->bhce', w_c, S)
        o = (jnp.einsum('bhcd,bhde->bhce', q_c * jnp.exp(gc_c)[..., None], S)
             + jnp.einsum('bhij,bhje->bhie', qk_c, v_new))
        g_last = gc_c[..., -1]
        S = (S * jnp.exp(g_last)[..., None, None]
             + jnp.einsum('bhcd,bhce->bhde', k_c * jnp.exp(g_last[..., None] - gc_c)[..., None], v_new))
        return S, o

    S, o = lax.scan(step, s0.astype(f32), (q, k, u, w, qk, gc))
    o = jnp.moveaxis(o, (0, 2), (1, 3)).reshape(bsz, n * C, GDN_HEADS, GDN_HEAD_DIM)[:, :T]
    return o, S


def s5_scan(u, h0_re, h0_im, a_re, a_im, b_re, b_im, c_re, c_im, d_skip, log_dt):
    f32 = jnp.float32
    bsz, T = u.shape[0], u.shape[1]
    uf = u.astype(f32)
    lam = lax.complex(a_re.astype(f32), a_im.astype(f32))
    dt = jnp.exp(log_dt.astype(f32))[:, None]
    lam_bar = jnp.exp(lam * dt)
    b_bar = ((lam_bar - 1.0) / lam)[..., None] * lax.complex(b_re.astype(f32), b_im.astype(f32))
    bu = jnp.einsum('gpc,btgc->btgp', b_bar, uf.reshape(bsz, T, S5_GROUPS, S5_GROUP).astype(jnp.complex64))
    h0 = lax.complex(h0_re.astype(f32), h0_im.astype(f32))
    bu = bu.at[:, 0].add(lam_bar * h0)
    a = jnp.broadcast_to(lam_bar, bu.shape)

    def combine(left, right):
        a1, b1 = left
        a2, b2 = right
        return a1 * a2, a2 * b1 + b2

    _, h = lax.associative_scan(combine, (a, bu), axis=1)
    c = lax.complex(c_re.astype(f32), c_im.astype(f32))
    y = jnp.real(jnp.einsum('gcp,btgp->btgc', c, h)).reshape(bsz, T, S5_WIDTH) + d_skip.astype(f32) * uf
    h_last = h[:, -1]
    return y, jnp.real(h_last), jnp.imag(h_last)


def hybrid_layer(x, conv_buf, s_gdn, h_re, h_im, w_in, w_conv, gdn_a_log, gdn_dt_bias, gdn_norm_w,
                 s5_a_re, s5_a_im, s5_b_re, s5_b_im, s5_c_re, s5_c_im, s5_d, s5_log_dt, w_glu, b_glu,
                 w_out, ln1_g, ln1_b, w_ff1, w_ff2, ln2_g, ln2_b):
    f32 = jnp.float32
    bsz, T, _ = x.shape
    proj = jnp.einsum('btd,dn->btn', x, w_in)
    i0 = 3 * GDN_WIDTH
    i1 = i0 + GDN_WIDTH
    i2 = i1 + GDN_HEADS
    i3 = i2 + GDN_HEADS
    qkv_raw, z, b_logit, a_logit, u = (proj[..., :i0], proj[..., i0:i1], proj[..., i1:i2],
                                        proj[..., i2:i3], proj[..., i3:])
    qkv, new_conv = causal_short_conv(qkv_raw, conv_buf, w_conv)
    q = l2norm(qkv[..., :GDN_WIDTH].reshape(bsz, T, GDN_HEADS, GDN_HEAD_DIM))
    k = l2norm(qkv[..., GDN_WIDTH:2 * GDN_WIDTH].reshape(bsz, T, GDN_HEADS, GDN_HEAD_DIM))
    v = qkv[..., 2 * GDN_WIDTH:].reshape(bsz, T, GDN_HEADS, GDN_HEAD_DIM)
    beta = jax.nn.sigmoid(b_logit.astype(f32))
    g = -jnp.exp(gdn_a_log.astype(f32)) * jax.nn.softplus(a_logit.astype(f32) + gdn_dt_bias.astype(f32))
    o, s_new = gated_delta_chunked(q, k, v, g, beta, s_gdn)
    o = (o * lax.rsqrt(jnp.mean(o * o, axis=-1, keepdims=True) + NORM_EPS) * gdn_norm_w.astype(f32)
         * jax.nn.silu(z.astype(f32).reshape(bsz, T, GDN_HEADS, GDN_HEAD_DIM)))
    o = o.reshape(bsz, T, GDN_WIDTH).astype(x.dtype)
    ys, hr, hi = s5_scan(u, h_re, h_im, s5_a_re, s5_a_im, s5_b_re, s5_b_im, s5_c_re, s5_c_im, s5_d, s5_log_dt)
    ys = jax.nn.gelu(ys).astype(x.dtype)
    ys = ys * jax.nn.sigmoid(jnp.einsum('btc,ce->bte', ys, w_glu) + b_glu)
    mix = jnp.einsum('btc,cd->btd', jnp.concatenate([o, ys], axis=-1), w_out)
    x = layernorm(DN_ALPHA * x + mix, ln1_g, ln1_b)
    h = jnp.square(jax.nn.relu(jnp.einsum('btd,df->btf', x, w_ff1)))
    x = layernorm(DN_ALPHA * x + jnp.einsum('btf,fd->btd', h, w_ff2), ln2_g, ln2_b)
    return x, new_conv, s_new, hr, hi


def setup_inputs(seed: int = 0) -> dict:
    key = jax.random.key(seed)
    ks = jax.random.split(key, 32)
    f32 = jnp.float32

    def nrm(k, shape, s):
        return s * jax.random.normal(k, shape, f32)

    x_prompt = nrm(ks[0], (BATCH, SEQ, D_MODEL), 1.0)
    x_sample = nrm(ks[1], (DEC_BATCH, DEC_SEQ, D_MODEL), 1.0)
    state_gdn = nrm(ks[2], (DEPTH, DEC_BATCH, GDN_HEADS, GDN_HEAD_DIM, GDN_HEAD_DIM), 0.1)
    state_conv = nrm(ks[3], (DEPTH, DEC_BATCH, CONV_W - 1, 3 * GDN_WIDTH), 1.0)
    state_ssm_re = nrm(ks[4], (DEPTH, DEC_BATCH, S5_GROUPS, S5_STATE), 0.1)
    state_ssm_im = nrm(ks[5], (DEPTH, DEC_BATCH, S5_GROUPS, S5_STATE), 0.1)
    col_scale = jnp.concatenate([jnp.ones((2 * GDN_WIDTH,), f32), jnp.full((GDN_WIDTH,), DN_BETA, f32),
                                 jnp.ones((GDN_WIDTH + 2 * GDN_HEADS,), f32), jnp.full((S5_WIDTH,), DN_BETA, f32)])
    w_in = nrm(ks[6], (DEPTH, D_MODEL, N_PROJ), D_MODEL ** -0.5) * col_scale
    w_conv = nrm(ks[7], (DEPTH, CONV_W, 3 * GDN_WIDTH), CONV_W ** -0.5)
    gdn_a_log = jnp.log(jax.random.uniform(ks[8], (DEPTH, GDN_HEADS), f32, 1.0, 16.0))
    dt = jnp.exp(jax.random.uniform(ks[9], (DEPTH, GDN_HEADS), f32, math.log(1e-3), math.log(1e-1)))
    gdn_dt_bias = dt + jnp.log(-jnp.expm1(-dt))
    gdn_norm_w = 1.0 + nrm(ks[10], (DEPTH, GDN_HEAD_DIM), 0.02)
    s5_a_re = -0.5 + nrm(ks[11], (DEPTH, S5_GROUPS, S5_STATE), 0.01)
    s5_a_im = jnp.pi * jnp.arange(S5_STATE, dtype=f32) + nrm(ks[12], (DEPTH, S5_GROUPS, S5_STATE), 0.01)
    s5_b_re = nrm(ks[13], (DEPTH, S5_GROUPS, S5_STATE, S5_GROUP), (2 * S5_GROUP) ** -0.5)
    s5_b_im = nrm(ks[14], (DEPTH, S5_GROUPS, S5_STATE, S5_GROUP), (2 * S5_GROUP) ** -0.5)
    s5_c_re = nrm(ks[15], (DEPTH, S5_GROUPS, S5_GROUP, S5_STATE), S5_STATE ** -0.5)
    s5_c_im = nrm(ks[16], (DEPTH, S5_GROUPS, S5_GROUP, S5_STATE), S5_STATE ** -0.5)
    s5_d = nrm(ks[17], (DEPTH, S5_WIDTH), 1.0)
    s5_log_dt = jax.random.uniform(ks[18], (DEPTH, S5_GROUPS), f32, math.log(1e-3), math.log(1e-1))
    w_glu = nrm(ks[19], (DEPTH, S5_WIDTH, S5_WIDTH), S5_WIDTH ** -0.5)
    b_glu = nrm(ks[20], (DEPTH, S5_WIDTH), 0.01)
    w_out = nrm(ks[21], (DEPTH, MIX_WIDTH, D_MODEL), MIX_WIDTH ** -0.5 * DN_BETA)
    ln1_g = 1.0 + nrm(ks[22], (DEPTH, D_MODEL), 0.02)
    ln1_b = nrm(ks[23], (DEPTH, D_MODEL), 0.01)
    w_ff1 = nrm(ks[24], (DEPTH, D_MODEL, D_FF), D_MODEL ** -0.5)
    w_ff2 = nrm(ks[25], (DEPTH, D_FF, D_MODEL), D_FF ** -0.5 * DN_BETA)
    ln2_g = 1.0 + nrm(ks[26], (DEPTH, D_MODEL), 0.02)
    ln2_b = nrm(ks[27], (DEPTH, D_MODEL), 0.01)
    return {"x_prompt": x_prompt, "x_sample": x_sample, "state_gdn": state_gdn, "state_conv": state_conv,
            "state_ssm_re": state_ssm_re, "state_ssm_im": state_ssm_im, "w_in": w_in, "w_conv": w_conv,
            "gdn_a_log": gdn_a_log, "gdn_dt_bias": gdn_dt_bias, "gdn_norm_w": gdn_norm_w,
            "s5_a_re": s5_a_re, "s5_a_im": s5_a_im, "s5_b_re": s5_b_re, "s5_b_im": s5_b_im,
            "s5_c_re": s5_c_re, "s5_c_im": s5_c_im, "s5_d": s5_d, "s5_log_dt": s5_log_dt,
            "w_glu": w_glu, "b_glu": b_glu, "w_out": w_out, "ln1_g": ln1_g, "ln1_b": ln1_b,
            "w_ff1": w_ff1, "w_ff2": w_ff2, "ln2_g": ln2_g, "ln2_b": ln2_b}


def reference(x_prompt, x_sample, state_gdn, state_conv, state_ssm_re, state_ssm_im, w_in, w_conv,
              gdn_a_log, gdn_dt_bias, gdn_norm_w, s5_a_re, s5_a_im, s5_b_re, s5_b_im, s5_c_re, s5_c_im,
              s5_d, s5_log_dt, w_glu, b_glu, w_out, ln1_g, ln1_b, w_ff1, w_ff2, ln2_g, ln2_b):
    f32 = jnp.float32
    bp = x_prompt.shape[0]
    yp, ys = x_prompt, x_sample
    p_gdn, p_conv, p_re, p_im = [], [], [], []
    s_gdn, s_conv, s_re, s_im = [], [], [], []
    for l in range(DEPTH):
        params = (w_in[l], w_conv[l], gdn_a_log[l], gdn_dt_bias[l], gdn_norm_w[l], s5_a_re[l], s5_a_im[l],
                  s5_b_re[l], s5_b_im[l], s5_c_re[l], s5_c_im[l], s5_d[l], s5_log_dt[l], w_glu[l], b_glu[l],
                  w_out[l], ln1_g[l], ln1_b[l], w_ff1[l], w_ff2[l], ln2_g[l], ln2_b[l])
        yp, c, s, hr, hi = hybrid_layer(
            yp, jnp.zeros((bp, CONV_W - 1, 3 * GDN_WIDTH), x_prompt.dtype),
            jnp.zeros((bp, GDN_HEADS, GDN_HEAD_DIM, GDN_HEAD_DIM), f32),
            jnp.zeros((bp, S5_GROUPS, S5_STATE), f32), jnp.zeros((bp, S5_GROUPS, S5_STATE), f32), *params)
        p_gdn.append(s)
        p_conv.append(c)
        p_re.append(hr)
        p_im.append(hi)
        ys, c, s, hr, hi = hybrid_layer(ys, state_conv[l], state_gdn[l], state_ssm_re[l], state_ssm_im[l], *params)
        s_gdn.append(s)
        s_conv.append(c)
        s_re.append(hr)
        s_im.append(hi)
    return (yp, ys, jnp.stack(p_gdn), jnp.stack(p_conv), jnp.stack(p_re), jnp.stack(p_im),
            jnp.stack(s_gdn), jnp.stack(s_conv), jnp.stack(s_re), jnp.stack(s_im))
```

```python
import functools

import jax
import jax.numpy as jnp
from jax import lax
from jax.experimental import pallas as pl
from jax.experimental.pallas import tpu as pltpu

F32 = jnp.float32
BF16 = jnp.bfloat16

D_MODEL = 1024
GDN_HEADS = 4
GDN_HEAD_DIM = 128
GDN_WIDTH = GDN_HEADS * GDN_HEAD_DIM
CONV_W = 4
GDN_CHUNK = 64
S5_WIDTH = D_MODEL - GDN_WIDTH
S5_GROUP = 16
S5_GROUPS = S5_WIDTH // S5_GROUP
S5_STATE = 64
D_FF = 4 * D_MODEL
DEPTH = 1
DN_ALPHA = (2.0 * DEPTH) ** 0.25
NORM_EPS = 1e-6
QKV_W = 3 * GDN_WIDTH
QKVZ_W = QKV_W + GDN_WIDTH
GATE_LANES = 128
GATE_ROWS = 16
S5_CHUNK = 16
S5_FLAT = S5_CHUNK * S5_GROUP
GDN_STEP = 2 * GDN_CHUNK
VMEM_LIMIT = 56 * 1024 * 1024


def _dot(a, b):
    return jnp.dot(a.astype(BF16), b.astype(BF16), preferred_element_type=F32)


def _dot_nt(a, b):
    return lax.dot_general(a.astype(BF16), b.astype(BF16), (((1,), (1,)), ((), ())), preferred_element_type=F32)


def _dot_tn(a, b):
    return lax.dot_general(a.astype(BF16), b.astype(BF16), (((0,), (0,)), ((), ())), preferred_element_type=F32)


def _split(x):
    hi = x.astype(BF16)
    lo = (x - hi.astype(F32)).astype(BF16)
    return hi, lo


def _dot_hi(a, b, dims=(((1,), (0,)), ((), ()))):
    ah, al = _split(a)
    bh, bl = _split(b)
    d = functools.partial(lax.dot_general, dimension_numbers=dims, preferred_element_type=F32)
    return d(ah, bh) + (d(al, bh) + d(ah, bl))


def _dot_x3(a, b_exact):
    a1 = a.astype(BF16)
    r1 = a - a1.astype(F32)
    a2 = r1.astype(BF16)
    a3 = (r1 - a2.astype(F32)).astype(BF16)
    return _dot(a1, b_exact) + (_dot(a2, b_exact) + _dot(a3, b_exact))


def _dot_x3_left(a_exact, b):
    b1 = b.astype(BF16)
    r1 = b - b1.astype(F32)
    b2 = r1.astype(BF16)
    b3 = (r1 - b2.astype(F32)).astype(BF16)
    return _dot(a_exact, b1) + (_dot(a_exact, b2) + _dot(a_exact, b3))


def _silu(x):
    return x * jax.nn.sigmoid(x)


def _layernorm(x, g, b):
    mu = jnp.mean(x, axis=-1, keepdims=True)
    xc = x - mu
    var = jnp.mean(xc * xc, axis=-1, keepdims=True)
    return xc * lax.rsqrt(var + NORM_EPS) * g + b


def _l2norm(a):
    return a * lax.rsqrt(jnp.sum(a * a, axis=-1, keepdims=True) + NORM_EPS)


def _compiler_params(semantics):
    return pltpu.CompilerParams(dimension_semantics=semantics, vmem_limit_bytes=VMEM_LIMIT)


def _whole(shape):
    n = len(shape)
    return pl.BlockSpec(shape, lambda *_: (0,) * n)


def _proj_kernel(x_ref, wm_ref, wg_ref, wgt_ref, wu_ref, qkvz_ref, gates_ref, gates_t_ref, u_ref):
    xb = x_ref[...].astype(BF16)
    qkvz_ref[...] = jnp.dot(xb, wm_ref[...], preferred_element_type=F32)
    gates_ref[...] = jnp.dot(xb, wg_ref[...], preferred_element_type=F32)
    gates_t_ref[...] = lax.dot_general(wgt_ref[...], xb, (((1,), (1,)), ((), ())), preferred_element_type=F32)
    u_ref[...] = jnp.dot(xb, wu_ref[...], preferred_element_type=F32)


def _proj(x, wm, wg, wgt, wu, tile):
    n = x.shape[0]
    return pl.pallas_call(
        _proj_kernel,
        grid=(n // tile,),
        in_specs=[pl.BlockSpec((tile, D_MODEL), lambda i: (i, 0)),
                  _whole(wm.shape), _whole(wg.shape), _whole(wgt.shape), _whole(wu.shape)],
        out_specs=[pl.BlockSpec((tile, QKVZ_W), lambda i: (i, 0)),
                   pl.BlockSpec((tile, GATE_LANES), lambda i: (i, 0)),
                   pl.BlockSpec((GATE_ROWS, tile), lambda i: (0, i)),
                   pl.BlockSpec((tile, S5_WIDTH), lambda i: (i, 0))],
        out_shape=[jax.ShapeDtypeStruct((n, QKVZ_W), F32),
                   jax.ShapeDtypeStruct((n, GATE_LANES), F32),
                   jax.ShapeDtypeStruct((GATE_ROWS, n), F32),
                   jax.ShapeDtypeStruct((n, S5_WIDTH), F32)],
        compiler_params=_compiler_params(("arbitrary",)),
        name="proj",
    )(x, wm, wg, wgt, wu)


def _gate_values(logits, a_log, dt_bias):
    beta = jax.nn.sigmoid(logits)
    g = -jnp.exp(a_log) * jax.nn.softplus(logits + dt_bias)
    return beta, g


def _gdn_kernel(qkvz_ref, gates_ref, gates_t_ref, wconv_ref, alog_row_ref, dtb_row_ref, alog_col_ref, dtb_col_ref,
                normw_ref, o_ref, s_ref, xpad_ref):
    step = pl.program_id(1)
    tt = GDN_STEP
    c = GDN_CHUNK
    dk = GDN_HEAD_DIM

    @pl.when(step == 0)
    def _():
        s_ref[...] = jnp.zeros_like(s_ref)
        xpad_ref[0:8, :] = jnp.zeros((8, QKV_W), F32)

    xpad_ref[8:8 + tt, :] = qkvz_ref[:, 0:QKV_W]
    wc = wconv_ref[...]
    conv = xpad_ref[8:8 + tt, :] * wc[3:4, :]
    for j in range(1, CONV_W):
        conv = conv + xpad_ref[8 - j:8 - j + tt, :] * wc[3 - j:4 - j, :]
    xpad_ref[0:8, :] = xpad_ref[tt:tt + 8, :]
    qkv = _silu(conv)

    beta_c, g_c = _gate_values(gates_ref[...], alog_row_ref[...], dtb_row_ref[...])
    _, g_r = _gate_values(gates_t_ref[...], alog_col_ref[...], dtb_col_ref[...])

    ri = lax.broadcasted_iota(jnp.int32, (tt, tt), 0)
    ci = lax.broadcasted_iota(jnp.int32, (tt, tt), 1)
    same_chunk = (ri // c) == (ci // c)
    tril_bd = jnp.where(same_chunk & (ci <= ri), 1.0, 0.0).astype(BF16)
    triu_bd = jnp.where(same_chunk & (ri <= ci), 1.0, 0.0).astype(BF16)
    gc_c = _dot_x3_left(tril_bd, g_c)
    gc_r = _dot_x3(g_r, triu_bd)

    r64 = lax.broadcasted_iota(jnp.int32, (c, c), 0)
    c64 = lax.broadcasted_iota(jnp.int32, (c, c), 1)
    tri = c64 <= r64
    strict = c64 < r64
    eye = jnp.where(r64 == c64, 1.0, 0.0).astype(F32)
    scale = GDN_HEAD_DIM ** -0.5
    normw = normw_ref[...]

    for ch in range(tt // c):
        r0 = ch * c
        for h in range(GDN_HEADS):
            q = _l2norm(qkv[r0:r0 + c, h * dk:(h + 1) * dk]) * scale
            k = _l2norm(qkv[r0:r0 + c, GDN_WIDTH + h * dk:GDN_WIDTH + (h + 1) * dk])
            v = qkv[r0:r0 + c, 2 * GDN_WIDTH + h * dk:2 * GDN_WIDTH + (h + 1) * dk]
            z = qkvz_ref[r0:r0 + c, QKV_W + h * dk:QKV_W + (h + 1) * dk]
            beta = beta_c[r0:r0 + c, h:h + 1]
            gcol = gc_c[r0:r0 + c, 4 + h:5 + h]
            grow = gc_r[4 + h:5 + h, r0:r0 + c]
            diff = gcol - grow
            decay = jnp.where(tri, jnp.exp(jnp.where(tri, diff, 0.0)), 0.0)
            eg = jnp.exp(gcol)
            g_last = gcol[c - 1:c, :]
            kb = k * beta
            vb = v * beta
            a1 = _dot_nt(jnp.concatenate([kb, q], axis=0), k)
            lmat = jnp.where(strict, a1[0:c] * decay, 0.0)
            qk = jnp.where(tri, a1[c:2 * c] * decay, 0.0)
            p = -lmat
            tinv = eye + p
            for _ in range(5):
                p = _dot_hi(p, p)
                tinv = tinv + _dot_hi(tinv, p)
            sol = _dot_hi(tinv, jnp.concatenate([vb, kb * eg], axis=1))
            u = sol[:, 0:dk]
            w = sol[:, dk:2 * dk]
            s_old = s_ref[0, h]
            m1 = _dot(jnp.concatenate([w, q * eg], axis=0), s_old)
            v_new = u - m1[0:c]
            o = m1[c:2 * c] + _dot(qk, v_new)
            kd = k * jnp.exp(g_last - gcol)
            s_ref[0, h] = s_old * jnp.exp(g_last) + _dot_tn(kd, v_new)
            o = o * lax.rsqrt(jnp.mean(o * o, axis=-1, keepdims=True) + NORM_EPS) * normw * _silu(z)
            o_ref[r0:r0 + c, h * dk:(h + 1) * dk] = o


def _gdn_prompt(qkvz, gates, gates_t, wconv, alog_row, dtb_row, alog_col, dtb_col, normw, bsz, seq):
    steps = seq // GDN_STEP
    return pl.pallas_call(
        _gdn_kernel,
        grid=(bsz, steps),
        in_specs=[pl.BlockSpec((GDN_STEP, QKVZ_W), lambda b, i: (b * steps + i, 0)),
                  pl.BlockSpec((GDN_STEP, GATE_LANES), lambda b, i: (b * steps + i, 0)),
                  pl.BlockSpec((GATE_ROWS, GDN_STEP), lambda b, i: (0, b * steps + i)),
                  _whole(wconv.shape), _whole(alog_row.shape), _whole(dtb_row.shape),
                  _whole(alog_col.shape), _whole(dtb_col.shape), _whole(normw.shape)],
        out_specs=[pl.BlockSpec((GDN_STEP, GDN_WIDTH), lambda b, i: (b * steps + i, 0)),
                   pl.BlockSpec((1, GDN_HEADS, GDN_HEAD_DIM, GDN_HEAD_DIM), lambda b, i: (b, 0, 0, 0))],
        out_shape=[jax.ShapeDtypeStruct((bsz * seq, GDN_WIDTH), F32),
                   jax.ShapeDtypeStruct((bsz, GDN_HEADS, GDN_HEAD_DIM, GDN_HEAD_DIM), F32)],
        scratch_shapes=[pltpu.VMEM((8 + GDN_STEP, QKV_W), F32)],
        compiler_params=_compiler_params(("arbitrary", "arbitrary")),
        name="gdn_prompt",
    )(qkvz, gates, gates_t, wconv, alog_row, dtb_row, alog_col, dtb_col, normw)


def _s5_kernel(u_ref, us_ref, h0r_ref, h0i_ref, ar_row_ref, ai_row_ref, ar_col_ref, ai_col_ref, ldt_ref,
               br_ref, bi_ref, cr_ref, ci_ref,
               y_ref, hlr_ref, hli_ref, ys_ref, hsr_ref, hsi_ref,
               er_ref, ei_ref, hinr_ref, hini_ref, *, nseq, nchunk):
    L = S5_CHUNK
    P = S5_STATE
    dt = jnp.exp(ldt_ref[0])
    ar_row = ar_row_ref[0] * dt
    ai_row = ai_row_ref[0] * dt
    ar_col = ar_col_ref[0] * dt
    ai_col = ai_col_ref[0] * dt

    ea = jnp.exp(ar_col)
    lbr = ea * jnp.cos(ai_col)
    lbi = ea * jnp.sin(ai_col)
    lam_r = ar_col_ref[0]
    lam_i = ai_col_ref[0]
    den = lam_r * lam_r + lam_i * lam_i
    fr = ((lbr - 1.0) * lam_r + lbi * lam_i) / den
    fi = (lbi * lam_r - (lbr - 1.0) * lam_i) / den
    b_r = br_ref[0]
    b_i = bi_ref[0]
    bbr = fr * b_r - fi * b_i
    bbi = fr * b_i + fi * b_r

    rt = lax.broadcasted_iota(jnp.int32, (S5_GROUP, S5_FLAT), 0)
    lt = lax.broadcasted_iota(jnp.int32, (S5_GROUP, S5_FLAT), 1)
    tile_mat = jnp.where(lt % S5_GROUP == rt, 1.0, 0.0).astype(BF16)
    bwr = _dot_x3(bbr, tile_mat)
    bwi = _dot_x3(bbi, tile_mat)

    s_lane = (lax.broadcasted_iota(jnp.int32, (1, S5_FLAT), 1) // S5_GROUP).astype(F32)
    tau_g = (L - 1.0) - s_lane
    mag = jnp.exp(ar_col * tau_g)
    pgr = mag * jnp.cos(ai_col * tau_g)
    pgi = mag * jnp.sin(ai_col * tau_g)
    gtr = pgr * bwr - pgi * bwi
    gti = pgr * bwi + pgi * bwr

    nt = L + 1
    tau_w = (lax.broadcasted_iota(jnp.int32, (nt * S5_GROUP, 1), 0) // S5_GROUP).astype(F32)
    magw = jnp.exp(tau_w * ar_row)
    pwr = magw * jnp.cos(tau_w * ai_row)
    pwi = magw * jnp.sin(tau_w * ai_row)
    c_r = jnp.concatenate([cr_ref[0]] * nt, axis=0)
    c_i = jnp.concatenate([ci_ref[0]] * nt, axis=0)
    wr = c_r * pwr - c_i * pwi
    wi = -(c_r * pwi + c_i * pwr)

    zw = _dot_hi(wr[0:S5_FLAT], bwr) + _dot_hi(wi[0:S5_FLAT], bwi)
    s_of_lane = lax.broadcasted_iota(jnp.int32, (S5_FLAT, S5_FLAT), 1) // S5_GROUP
    mt = zw
    for j in range(4):
        sh = S5_GROUP << j
        shifted = jnp.concatenate([jnp.zeros((sh, S5_FLAT), F32), mt[0:S5_FLAT - sh]], axis=0)
        mt = jnp.where(((s_of_lane >> j) & 1) == 1, shifted, mt)

    ub = u_ref[0].astype(BF16)
    y_ref[0] = _dot_nt(ub, mt)
    er_ref[...] = _dot_nt(ub, gtr)
    ei_ref[...] = _dot_nt(ub, gti)

    m16 = jnp.exp(ar_row * float(L))
    p16r = m16 * jnp.cos(ai_row * float(L))
    p16i = m16 * jnp.sin(ai_row * float(L))

    def body(cidx, carry):
        hr, hi = carry
        rows = pl.ds(pl.multiple_of(cidx * nseq, nseq), nseq)
        hinr_ref[rows, :] = hr
        hini_ref[rows, :] = hi
        nr = p16r * hr - p16i * hi + er_ref[rows, :]
        ni = p16r * hi + p16i * hr + ei_ref[rows, :]
        return nr, ni

    zero = jnp.zeros((nseq, P), F32)
    hlr, hli = lax.fori_loop(0, nchunk, body, (zero, zero))
    hlr_ref[0] = hlr
    hli_ref[0] = hli
    y_ref[0] = y_ref[0] + _dot_nt(hinr_ref[...], wr[S5_GROUP:]) + _dot_nt(hini_ref[...], wi[S5_GROUP:])

    e1 = jnp.exp(ar_row)
    l1r = e1 * jnp.cos(ai_row)
    l1i = e1 * jnp.sin(ai_row)
    us = us_ref[0]
    nt_dims = (((1,), (1,)), ((), ()))
    bur = _dot_hi(us, bbr, nt_dims)
    bui = _dot_hi(us, bbi, nt_dims)
    h0r = h0r_ref[0]
    h0i = h0i_ref[0]
    hsr = l1r * h0r - l1i * h0i + bur
    hsi = l1r * h0i + l1i * h0r + bui
    hsr_ref[0] = hsr
    hsi_ref[0] = hsi
    ys_ref[0] = _dot_nt(hsr, cr_ref[0]) - _dot_nt(hsi, ci_ref[0])


def _s5(u_flat, us_t, h0r_t, h0i_t, ar, ai, ldt, b_r, b_i, c_r, c_i, nseq, nchunk):
    g = S5_GROUPS
    rows = nseq * nchunk
    ns = us_t.shape[1]
    ar_row = ar.reshape(g, 1, S5_STATE)
    ai_row = ai.reshape(g, 1, S5_STATE)
    ar_col = ar.reshape(g, S5_STATE, 1)
    ai_col = ai.reshape(g, S5_STATE, 1)
    ldt3 = ldt.reshape(g, 1, 1)

    def blk(shape):
        return pl.BlockSpec((1,) + shape, lambda i: (i,) + (0,) * len(shape))

    return pl.pallas_call(
        functools.partial(_s5_kernel, nseq=nseq, nchunk=nchunk),
        grid=(g,),
        in_specs=[blk((rows, S5_FLAT)), blk((ns, S5_GROUP)), blk((ns, S5_STATE)), blk((ns, S5_STATE)),
                  blk((1, S5_STATE)), blk((1, S5_STATE)), blk((S5_STATE, 1)), blk((S5_STATE, 1)), blk((1, 1)),
                  blk((S5_STATE, S5_GROUP)), blk((S5_STATE, S5_GROUP)),
                  blk((S5_GROUP, S5_STATE)), blk((S5_GROUP, S5_STATE))],
        out_specs=[blk((rows, S5_FLAT)), blk((nseq, S5_STATE)), blk((nseq, S5_STATE)),
                   blk((ns, S5_GROUP)), blk((ns, S5_STATE)), blk((ns, S5_STATE))],
        out_shape=[jax.ShapeDtypeStruct((g, rows, S5_FLAT), F32),
                   jax.ShapeDtypeStruct((g, nseq, S5_STATE), F32),
                   jax.ShapeDtypeStruct((g, nseq, S5_STATE), F32),
                   jax.ShapeDtypeStruct((g, ns, S5_GROUP), F32),
                   jax.ShapeDtypeStruct((g, ns, S5_STATE), F32),
                   jax.ShapeDtypeStruct((g, ns, S5_STATE), F32)],
        scratch_shapes=[pltpu.VMEM((rows, S5_STATE), F32)] * 4,
        compiler_params=_compiler_params(("arbitrary",)),
        name="s5",
    )(u_flat, us_t, h0r_t, h0i_t, ar_row, ai_row, ar_col, ai_col, ldt3, b_r, b_i, c_r, c_i)


def _sample_pre_kernel(qkvz_ref, gates_ref, conv_ref, wconv_ref, alog_row_ref, dtb_row_ref,
                       newconv_ref, ops_ref, sc_ref):
    dk = GDN_HEAD_DIM
    x_new = qkvz_ref[:, 0:QKV_W]
    wc = wconv_ref[...]
    conv = x_new * wc[3:4, :]
    for j in range(CONV_W - 1):
        conv = conv + conv_ref[j] * wc[j:j + 1, :]
    qkv = _silu(conv)
    newconv_ref[0] = conv_ref[1]
    newconv_ref[1] = conv_ref[2]
    newconv_ref[2] = x_new

    beta_c, g_c = _gate_values(gates_ref[...], alog_row_ref[...], dtb_row_ref[...])
    scale = GDN_HEAD_DIM ** -0.5
    lane = lax.broadcasted_iota(jnp.int32, sc_ref.shape, 1)
    sc = jnp.zeros(sc_ref.shape, F32)
    for h in range(GDN_HEADS):
        q = _l2norm(qkv[:, h * dk:(h + 1) * dk]) * scale
        k = _l2norm(qkv[:, GDN_WIDTH + h * dk:GDN_WIDTH + (h + 1) * dk])
        v = qkv[:, 2 * GDN_WIDTH + h * dk:2 * GDN_WIDTH + (h + 1) * dk]
        beta = beta_c[:, h:h + 1]
        eg = jnp.exp(g_c[:, 4 + h:5 + h])
        cols = slice(h * dk, (h + 1) * dk)
        ops_ref[0, :, cols] = (beta * eg) * k
        ops_ref[1, :, cols] = q * eg
        ops_ref[2, :, cols] = k
        ops_ref[3, :, cols] = beta * v
        sc = jnp.where(lane == h, jnp.sum(q * k, axis=-1, keepdims=True), sc)
        sc = jnp.where(lane == 4 + h, eg, sc)
    sc_ref[...] = sc


def _sample_pre(qkvz_s, gates_s, conv_t, wconv, alog_row, dtb_row):
    ns = qkvz_s.shape[0]
    return pl.pallas_call(
        _sample_pre_kernel,
        out_shape=[jax.ShapeDtypeStruct((CONV_W - 1, ns, QKV_W), F32),
                   jax.ShapeDtypeStruct((4, ns, GDN_WIDTH), F32),
                   jax.ShapeDtypeStruct((ns, GATE_LANES), F32)],
        compiler_params=pltpu.CompilerParams(vmem_limit_bytes=VMEM_LIMIT),
        name="sample_pre",
    )(qkvz_s, gates_s, conv_t, wconv, alog_row, dtb_row)


def _sample_state_kernel(s_ref, ops_ref, sc_ref, z_ref, normw_ref, snew_ref, o_ref, *, nb):
    dk = GDN_HEAD_DIM
    row = lax.broadcasted_iota(jnp.int32, (8, dk), 0)
    for j in range(nb):
        for h in range(GDN_HEADS):
            cols = slice(h * dk, (h + 1) * dk)
            w = ops_ref[0, j:j + 1, cols]
            qg = ops_ref[1, j:j + 1, cols]
            k = ops_ref[2, j:j + 1, cols]
            u = ops_ref[3, j:j + 1, cols]
            lhs = jnp.where(row == 0, jnp.broadcast_to(w, (8, dk)), jnp.broadcast_to(qg, (8, dk)))
            s_old = s_ref[j, h]
            m1 = _dot(lhs, s_old)
            v_new = u - m1[0:1, :]
            qk = sc_ref[j:j + 1, h:h + 1]
            eg = sc_ref[j:j + 1, 4 + h:5 + h]
            o_ref[j:j + 1, cols] = m1[1:2, :] + qk * v_new
            k8 = jnp.where(row == 0, jnp.broadcast_to(k, (8, dk)), 0.0)
            snew_ref[j, h] = s_old * eg + _dot_tn(k8, jnp.broadcast_to(v_new, (8, dk)))
    normw = normw_ref[...]
    for h in range(GDN_HEADS):
        o = o_ref[:, h * dk:(h + 1) * dk]
        z = z_ref[:, h * dk:(h + 1) * dk]
        o_ref[:, h * dk:(h + 1) * dk] = (o * lax.rsqrt(jnp.mean(o * o, axis=-1, keepdims=True) + NORM_EPS)
                                         * normw * _silu(z))


def _sample_state(state, ops, sc, z, normw, nb=8):
    ns = state.shape[0]
    hd = (GDN_HEADS, GDN_HEAD_DIM, GDN_HEAD_DIM)
    return pl.pallas_call(
        functools.partial(_sample_state_kernel, nb=nb),
        grid=(ns // nb,),
        in_specs=[pl.BlockSpec((nb,) + hd, lambda i: (i, 0, 0, 0)),
                  pl.BlockSpec((4, nb, GDN_WIDTH), lambda i: (0, i, 0)),
                  pl.BlockSpec((nb, GATE_LANES), lambda i: (i, 0)),
                  pl.BlockSpec((nb, GDN_WIDTH), lambda i: (i, 0)),
                  _whole(normw.shape)],
        out_specs=[pl.BlockSpec((nb,) + hd, lambda i: (i, 0, 0, 0)),
                   pl.BlockSpec((nb, GDN_WIDTH), lambda i: (i, 0))],
        out_shape=[jax.ShapeDtypeStruct(state.shape, F32),
                   jax.ShapeDtypeStruct((ns, GDN_WIDTH), F32)],
        compiler_params=_compiler_params(("arbitrary",)),
        name="sample_state",
    )(state, ops, sc, z, normw)


def _post_kernel(x_ref, o_ref, ys_ref, u_ref, d_ref, wglu_ref, bglu_ref, wout_ref, g1_ref, b1_ref,
                 wff1_ref, wff2_ref, g2_ref, b2_ref, y_ref, *, ff_chunk):
    ys = jax.nn.gelu(ys_ref[...] + d_ref[...] * u_ref[...])
    ys = ys * jax.nn.sigmoid(_dot(ys, wglu_ref[...]) + bglu_ref[...])
    mix = _dot(o_ref[...], wout_ref[0:GDN_WIDTH, :]) + _dot(ys, wout_ref[GDN_WIDTH:, :])
    x1 = _layernorm(DN_ALPHA * x_ref[...] + mix, g1_ref[...], b1_ref[...])
    x1b = x1.astype(BF16)
    acc = jnp.zeros(x1.shape, F32)
    for f in range(D_FF // ff_chunk):
        hcol = jnp.dot(x1b, wff1_ref[:, f * ff_chunk:(f + 1) * ff_chunk], preferred_element_type=F32)
        hcol = jnp.square(jnp.maximum(hcol, 0.0))
        acc = acc + _dot(hcol, wff2_ref[f * ff_chunk:(f + 1) * ff_chunk, :])
    y_ref[...] = _layernorm(DN_ALPHA * x1 + acc, g2_ref[...], b2_ref[...])


def _post(x, o, ys, u, d, wglu, bglu, wout, g1, b1, wff1, wff2, g2, b2, tile):
    n = x.shape[0]

    def tok(w):
        return pl.BlockSpec((tile, w), lambda i: (i, 0))

    def resident(a):
        return pl.BlockSpec(a.shape, lambda i: (0,) * a.ndim, pipeline_mode=pl.Buffered(1))

    return pl.pallas_call(
        functools.partial(_post_kernel, ff_chunk=1024),
        grid=(n // tile,),
        in_specs=[tok(D_MODEL), tok(GDN_WIDTH), tok(S5_WIDTH), tok(S5_WIDTH)]
                 + [resident(a) for a in (d, wglu, bglu, wout, g1, b1, wff1, wff2, g2, b2)],
        out_specs=tok(D_MODEL),
        out_shape=jax.ShapeDtypeStruct((n, D_MODEL), F32),
        compiler_params=_compiler_params(("arbitrary",)),
        name="post",
    )(x, o, ys, u, d, wglu, bglu, wout, g1, b1, wff1, wff2, g2, b2)


def kernel(x_prompt, x_sample, state_gdn, state_conv, state_ssm_re, state_ssm_im, w_in, w_conv, gdn_a_log,
           gdn_dt_bias, gdn_norm_w, s5_a_re, s5_a_im, s5_b_re, s5_b_im, s5_c_re, s5_c_im, s5_d, s5_log_dt,
           w_glu, b_glu, w_out, ln1_g, ln1_b, w_ff1, w_ff2, ln2_g, ln2_b):
    bsz, seq, _ = x_prompt.shape
    ns = x_sample.shape[0]
    nchunk = seq // S5_CHUNK
    l = 0

    w = w_in[l]
    i_gate = QKVZ_W
    i_u = QKVZ_W + 2 * GDN_HEADS
    wm = w[:, :QKVZ_W].astype(BF16)
    wgate = w[:, i_gate:i_u]
    wg = jnp.pad(wgate, ((0, 0), (0, GATE_LANES - 2 * GDN_HEADS))).astype(BF16)
    wgt = jnp.pad(wgate.T, ((0, GATE_ROWS - 2 * GDN_HEADS), (0, 0))).astype(BF16)
    wu = w[:, i_u:].astype(BF16)
    pad_h = jnp.zeros((GDN_HEADS,), F32)
    alog8 = jnp.concatenate([pad_h, gdn_a_log[l]])
    dtb8 = jnp.concatenate([pad_h, gdn_dt_bias[l]])
    alog_row = jnp.pad(alog8, (0, GATE_LANES - 8)).reshape(1, GATE_LANES)
    dtb_row = jnp.pad(dtb8, (0, GATE_LANES - 8)).reshape(1, GATE_LANES)
    alog_col = jnp.pad(alog8, (0, GATE_ROWS - 8)).reshape(GATE_ROWS, 1)
    dtb_col = jnp.pad(dtb8, (0, GATE_ROWS - 8)).reshape(GATE_ROWS, 1)
    normw = gdn_norm_w[l].reshape(1, GDN_HEAD_DIM)
    wconv = w_conv[l]
    row = lambda a: a.reshape(1, -1)
    post_w = (row(s5_d[l]), w_glu[l].astype(BF16), row(b_glu[l]), w_out[l].astype(BF16), row(ln1_g[l]), row(ln1_b[l]),
              w_ff1[l].astype(BF16), w_ff2[l].astype(BF16), row(ln2_g[l]), row(ln2_b[l]))

    xp = x_prompt.reshape(bsz * seq, D_MODEL)
    xs = x_sample.reshape(ns, D_MODEL)
    qkvz_p, gates_p, gates_t_p, u_p = _proj(xp, wm, wg, wgt, wu, tile=512)
    qkvz_s, gates_s, _, u_s = _proj(xs, wm, wg, wgt, wu, tile=ns)

    o_p, gdn_p = _gdn_prompt(qkvz_p, gates_p, gates_t_p, wconv, alog_row, dtb_row, alog_col, dtb_col, normw, bsz, seq)
    conv_p = qkvz_p.reshape(bsz, seq, QKVZ_W)[:, seq - (CONV_W - 1):, :QKV_W]

    conv_t = jnp.transpose(state_conv[l], (1, 0, 2))
    newconv_t, ops_s, sc = _sample_pre(qkvz_s, gates_s, conv_t, wconv, alog_row, dtb_row)
    gdn_s, o_s = _sample_state(state_gdn[l], ops_s, sc, qkvz_s[:, QKV_W:], normw)
    conv_s = jnp.transpose(newconv_t, (1, 0, 2))

    u_flat = (u_p.reshape(bsz, nchunk, S5_CHUNK, S5_GROUPS, S5_GROUP).transpose(3, 1, 0, 2, 4)
              .reshape(S5_GROUPS, nchunk * bsz, S5_FLAT))
    us_t = u_s.reshape(ns, S5_GROUPS, S5_GROUP).transpose(1, 0, 2)
    h0r_t = state_ssm_re[l].transpose(1, 0, 2)
    h0i_t = state_ssm_im[l].transpose(1, 0, 2)
    y_flat, hlr, hli, ys_t, hsr, hsi = _s5(u_flat, us_t, h0r_t, h0i_t, s5_a_re[l], s5_a_im[l], s5_log_dt[l],
                                           s5_b_re[l], s5_b_im[l], s5_c_re[l], s5_c_im[l], bsz, nchunk)
    ys_p = (y_flat.reshape(S5_GROUPS, nchunk, bsz, S5_CHUNK, S5_GROUP).transpose(2, 1, 3, 0, 4)
            .reshape(bsz * seq, S5_WIDTH))
    ys_s = ys_t.transpose(1, 0, 2).reshape(ns, S5_WIDTH)

    y_p = _post(xp, o_p, ys_p, u_p, *post_w, tile=512)
    y_s = _post(xs, o_s, ys_s, u_s, *post_w, tile=ns)

    t3 = lambda a: a.transpose(1, 0, 2)[None]
    return (y_p.reshape(bsz, seq, D_MODEL), y_s.reshape(ns, 1, D_MODEL),
            gdn_p[None], conv_p[None], t3(hlr), t3(hli),
            gdn_s[None], conv_s[None], t3(hsr), t3(hsi))
```

```python
import functools

import jax
import jax.numpy as jnp
from jax import lax
from jax.experimental import pallas as pl
from jax.experimental.pallas import tpu as pltpu

F32 = jnp.float32
BF16 = jnp.bfloat16

D_MODEL = 1024
GDN_HEADS = 4
GDN_HEAD_DIM = 128
GDN_WIDTH = GDN_HEADS * GDN_HEAD_DIM
CONV_W = 4
GDN_CHUNK = 64
S5_WIDTH = D_MODEL - GDN_WIDTH
S5_GROUP = 16
S5_GROUPS = S5_WIDTH // S5_GROUP
S5_STATE = 64
D_FF = 4 * D_MODEL
DEPTH = 1
DN_ALPHA = (2.0 * DEPTH) ** 0.25
NORM_EPS = 1e-6
QKV_W = 3 * GDN_WIDTH
QKVZ_W = QKV_W + GDN_WIDTH
GATE_LANES = 128
GATE_ROWS = 16
S5_CHUNK = 16
S5_FLAT = S5_CHUNK * S5_GROUP
GDN_STEP = 2 * GDN_CHUNK
VMEM_LIMIT = 56 * 1024 * 1024


def _dot(a, b):
    return jnp.dot(a.astype(BF16), b.astype(BF16), preferred_element_type=F32)


def _dot_nt(a, b):
    return lax.dot_general(a.astype(BF16), b.astype(BF16), (((1,), (1,)), ((), ())), preferred_element_type=F32)


def _dot_tn(a, b):
    return lax.dot_general(a.astype(BF16), b.astype(BF16), (((0,), (0,)), ((), ())), preferred_element_type=F32)


def _split(x):
    hi = x.astype(BF16)
    lo = (x - hi.astype(F32)).astype(BF16)
    return hi, lo


def _dot_hi(a, b, dims=(((1,), (0,)), ((), ()))):
    ah, al = _split(a)
    bh, bl = _split(b)
    d = functools.partial(lax.dot_general, dimension_numbers=dims, preferred_element_type=F32)
    return d(ah, bh) + (d(al, bh) + d(ah, bl))


def _dot_x3(a, b_exact):
    a1 = a.astype(BF16)
    r1 = a - a1.astype(F32)
    a2 = r1.astype(BF16)
    a3 = (r1 - a2.astype(F32)).astype(BF16)
    return _dot(a1, b_exact) + (_dot(a2, b_exact) + _dot(a3, b_exact))


def _dot_x3_left(a_exact, b):
    b1 = b.astype(BF16)
    r1 = b - b1.astype(F32)
    b2 = r1.astype(BF16)
    b3 = (r1 - b2.astype(F32)).astype(BF16)
    return _dot(a_exact, b1) + (_dot(a_exact, b2) + _dot(a_exact, b3))


def _silu(x):
    return x * jax.nn.sigmoid(x)


def _layernorm(x, g, b):
    mu = jnp.mean(x, axis=-1, keepdims=True)
    xc = x - mu
    var = jnp.mean(xc * xc, axis=-1, keepdims=True)
    return xc * lax.rsqrt(var + NORM_EPS) * g + b


def _l2norm(a):
    return a * lax.rsqrt(jnp.sum(a * a, axis=-1, keepdims=True) + NORM_EPS)


def _compiler_params(semantics):
    return pltpu.CompilerParams(dimension_semantics=semantics, vmem_limit_bytes=VMEM_LIMIT)


def _whole(shape):
    n = len(shape)
    return pl.BlockSpec(shape, lambda *_: (0,) * n)


def _proj_kernel(x_ref, wm_ref, wg_ref, wgt_ref, wu_ref, qkvz_ref, gates_ref, gates_t_ref, u_ref):
    xb = x_ref[...].astype(BF16)
    qkvz_ref[...] = jnp.dot(xb, wm_ref[...], preferred_element_type=F32)
    gates_ref[...] = jnp.dot(xb, wg_ref[...], preferred_element_type=F32)
    gates_t_ref[...] = lax.dot_general(wgt_ref[...], xb, (((1,), (1,)), ((), ())), preferred_element_type=F32)
    u_ref[...] = jnp.dot(xb, wu_ref[...], preferred_element_type=F32)


def _proj(x, wm, wg, wgt, wu, tile):
    n = x.shape[0]
    return pl.pallas_call(
        _proj_kernel,
        grid=(n // tile,),
        in_specs=[pl.BlockSpec((tile, D_MODEL), lambda i: (i, 0)),
                  _whole(wm.shape), _whole(wg.shape), _whole(wgt.shape), _whole(wu.shape)],
        out_specs=[pl.BlockSpec((tile, QKVZ_W), lambda i: (i, 0)),
                   pl.BlockSpec((tile, GATE_LANES), lambda i: (i, 0)),
                   pl.BlockSpec((GATE_ROWS, tile), lambda i: (0, i)),
                   pl.BlockSpec((tile, S5_WIDTH), lambda i: (i, 0))],
        out_shape=[jax.ShapeDtypeStruct((n, QKVZ_W), F32),
                   jax.ShapeDtypeStruct((n, GATE_LANES), F32),
                   jax.ShapeDtypeStruct((GATE_ROWS, n), F32),
                   jax.ShapeDtypeStruct((n, S5_WIDTH), F32)],
        compiler_params=_compiler_params(("arbitrary",)),
        name="proj",
    )(x, wm, wg, wgt, wu)


def _gate_values(logits, a_log, dt_bias):
    beta = jax.nn.sigmoid(logits)
    g = -jnp.exp(a_log) * jax.nn.softplus(logits + dt_bias)
    return beta, g


def _gdn_kernel(qkvz_ref, gates_ref, gates_t_ref, wconv_ref, alog_row_ref, dtb_row_ref, alog_col_ref, dtb_col_ref,
                normw_ref, o_ref, s_ref, xpad_ref):
    step = pl.program_id(1)
    tt = GDN_STEP
    c = GDN_CHUNK
    dk = GDN_HEAD_DIM

    @pl.when(step == 0)
    def _():
        s_ref[...] = jnp.zeros_like(s_ref)
        xpad_ref[0:8, :] = jnp.zeros((8, QKV_W), F32)

    xpad_ref[8:8 + tt, :] = qkvz_ref[:, 0:QKV_W]
    wc = wconv_ref[...]
    conv = xpad_ref[8:8 + tt, :] * wc[3:4, :]
    for j in range(1, CONV_W):
        conv = conv + xpad_ref[8 - j:8 - j + tt, :] * wc[3 - j:4 - j, :]
    xpad_ref[0:8, :] = xpad_ref[tt:tt + 8, :]
    qkv = _silu(conv)

    beta_c, g_c = _gate_values(gates_ref[...], alog_row_ref[...], dtb_row_ref[...])
    _, g_r = _gate_values(gates_t_ref[...], alog_col_ref[...], dtb_col_ref[...])

    ri = lax.broadcasted_iota(jnp.int32, (tt, tt), 0)
    ci = lax.broadcasted_iota(jnp.int32, (tt, tt), 1)
    same_chunk = (ri // c) == (ci // c)
    tril_bd = jnp.where(same_chunk & (ci <= ri), 1.0, 0.0).astype(BF16)
    triu_bd = jnp.where(same_chunk & (ri <= ci), 1.0, 0.0).astype(BF16)
    gc_c = _dot_x3_left(tril_bd, g_c)
    gc_r = _dot_x3(g_r, triu_bd)

    r64 = lax.broadcasted_iota(jnp.int32, (c, c), 0)
    c64 = lax.broadcasted_iota(jnp.int32, (c, c), 1)
    tri = c64 <= r64
    strict = c64 < r64
    eye = jnp.where(r64 == c64, 1.0, 0.0).astype(F32)
    scale = GDN_HEAD_DIM ** -0.5
    normw = normw_ref[...]

    units = [(ch, h) for ch in range(tt // c) for h in range(GDN_HEADS)]
    qs, ks, qk, eg, gcol_, glast, rhs, p, tinv = {}, {}, {}, {}, {}, {}, {}, {}, {}
    for un in units:
        ch, h = un
        r0 = ch * c
        q = _l2norm(qkv[r0:r0 + c, h * dk:(h + 1) * dk]) * scale
        k = _l2norm(qkv[r0:r0 + c, GDN_WIDTH + h * dk:GDN_WIDTH + (h + 1) * dk])
        v = qkv[r0:r0 + c, 2 * GDN_WIDTH + h * dk:2 * GDN_WIDTH + (h + 1) * dk]
        beta = beta_c[r0:r0 + c, h:h + 1]
        gcol = gc_c[r0:r0 + c, 4 + h:5 + h]
        grow = gc_r[4 + h:5 + h, r0:r0 + c]
        diff = gcol - grow
        decay = jnp.where(tri, jnp.exp(jnp.where(tri, diff, 0.0)), 0.0)
        eg[un] = jnp.exp(gcol)
        gcol_[un] = gcol
        glast[un] = gcol[c - 1:c, :]
        kb = k * beta
        a1 = _dot_nt(jnp.concatenate([kb, q], axis=0), k)
        qk[un] = jnp.where(tri, a1[c:2 * c] * decay, 0.0)
        rhs[un] = jnp.concatenate([v * beta, kb * eg[un]], axis=1)
        qs[un], ks[un] = q, k
        p[un] = -jnp.where(strict, a1[0:c] * decay, 0.0)
        tinv[un] = eye + p[un]
    for _ in range(5):
        for un in units:
            p[un] = _dot_hi(p[un], p[un])
        for un in units:
            tinv[un] = tinv[un] + _dot_hi(tinv[un], p[un])
    sol = {un: _dot_hi(tinv[un], rhs[un]) for un in units}

    for ch in range(tt // c):
        r0 = ch * c
        m1, s_old, v_new = {}, {}, {}
        for h in range(GDN_HEADS):
            un = (ch, h)
            s_old[h] = s_ref[0, h]
            m1[h] = _dot(jnp.concatenate([sol[un][:, dk:2 * dk], qs[un] * eg[un]], axis=0), s_old[h])
        for h in range(GDN_HEADS):
            un = (ch, h)
            v_new[h] = sol[un][:, 0:dk] - m1[h][0:c]
            kd = ks[un] * jnp.exp(glast[un] - gcol_[un])
            s_ref[0, h] = s_old[h] * jnp.exp(glast[un]) + _dot_tn(kd, v_new[h])
        for h in range(GDN_HEADS):
            un = (ch, h)
            o = m1[h][c:2 * c] + _dot(qk[un], v_new[h])
            z = qkvz_ref[r0:r0 + c, QKV_W + h * dk:QKV_W + (h + 1) * dk]
            o = o * lax.rsqrt(jnp.mean(o * o, axis=-1, keepdims=True) + NORM_EPS) * normw * _silu(z)
            o_ref[r0:r0 + c, h * dk:(h + 1) * dk] = o


def _gdn_prompt(qkvz, gates, gates_t, wconv, alog_row, dtb_row, alog_col, dtb_col, normw, bsz, seq):
    steps = seq // GDN_STEP
    return pl.pallas_call(
        _gdn_kernel,
        grid=(bsz, steps),
        in_specs=[pl.BlockSpec((GDN_STEP, QKVZ_W), lambda b, i: (b * steps + i, 0)),
                  pl.BlockSpec((GDN_STEP, GATE_LANES), lambda b, i: (b * steps + i, 0)),
                  pl.BlockSpec((GATE_ROWS, GDN_STEP), lambda b, i: (0, b * steps + i)),
                  _whole(wconv.shape), _whole(alog_row.shape), _whole(dtb_row.shape),
                  _whole(alog_col.shape), _whole(dtb_col.shape), _whole(normw.shape)],
        out_specs=[pl.BlockSpec((GDN_STEP, GDN_WIDTH), lambda b, i: (b * steps + i, 0)),
                   pl.BlockSpec((1, GDN_HEADS, GDN_HEAD_DIM, GDN_HEAD_DIM), lambda b, i: (b, 0, 0, 0))],
        out_shape=[jax.ShapeDtypeStruct((bsz * seq, GDN_WIDTH), F32),
                   jax.ShapeDtypeStruct((bsz, GDN_HEADS, GDN_HEAD_DIM, GDN_HEAD_DIM), F32)],
        scratch_shapes=[pltpu.VMEM((8 + GDN_STEP, QKV_W), F32)],
        compiler_params=_compiler_params(("arbitrary", "arbitrary")),
        name="gdn_prompt",
    )(qkvz, gates, gates_t, wconv, alog_row, dtb_row, alog_col, dtb_col, normw)


def _s5_kernel(u_ref, us_ref, h0r_ref, h0i_ref, ar_row_ref, ai_row_ref, ar_col_ref, ai_col_ref, ldt_ref,
               br_ref, bi_ref, cr_ref, ci_ref,
               y_ref, hlr_ref, hli_ref, ys_ref, hsr_ref, hsi_ref,
               er_ref, ei_ref, hinr_ref, hini_ref, *, nseq, nchunk):
    L = S5_CHUNK
    P = S5_STATE
    dt = jnp.exp(ldt_ref[0])
    ar_row = ar_row_ref[0] * dt
    ai_row = ai_row_ref[0] * dt
    ar_col = ar_col_ref[0] * dt
    ai_col = ai_col_ref[0] * dt

    ea = jnp.exp(ar_col)
    lbr = ea * jnp.cos(ai_col)
    lbi = ea * jnp.sin(ai_col)
    lam_r = ar_col_ref[0]
    lam_i = ai_col_ref[0]
    den = lam_r * lam_r + lam_i * lam_i
    fr = ((lbr - 1.0) * lam_r + lbi * lam_i) / den
    fi = (lbi * lam_r - (lbr - 1.0) * lam_i) / den
    b_r = br_ref[0]
    b_i = bi_ref[0]
    bbr = fr * b_r - fi * b_i
    bbi = fr * b_i + fi * b_r

    rt = lax.broadcasted_iota(jnp.int32, (S5_GROUP, S5_FLAT), 0)
    lt = lax.broadcasted_iota(jnp.int32, (S5_GROUP, S5_FLAT), 1)
    tile_mat = jnp.where(lt % S5_GROUP == rt, 1.0, 0.0).astype(BF16)
    bwr = _dot_x3(bbr, tile_mat)
    bwi = _dot_x3(bbi, tile_mat)

    s_lane = (lax.broadcasted_iota(jnp.int32, (1, S5_FLAT), 1) // S5_GROUP).astype(F32)
    tau_g = (L - 1.0) - s_lane
    mag = jnp.exp(ar_col * tau_g)
    pgr = mag * jnp.cos(ai_col * tau_g)
    pgi = mag * jnp.sin(ai_col * tau_g)
    gtr = pgr * bwr - pgi * bwi
    gti = pgr * bwi + pgi * bwr

    nt = L + 1
    tau_w = (lax.broadcasted_iota(jnp.int32, (nt * S5_GROUP, 1), 0) // S5_GROUP).astype(F32)
    magw = jnp.exp(tau_w * ar_row)
    pwr = magw * jnp.cos(tau_w * ai_row)
    pwi = magw * jnp.sin(tau_w * ai_row)
    c_r = jnp.concatenate([cr_ref[0]] * nt, axis=0)
    c_i = jnp.concatenate([ci_ref[0]] * nt, axis=0)
    wr = c_r * pwr - c_i * pwi
    wi = -(c_r * pwi + c_i * pwr)

    zw = _dot_hi(wr[0:S5_FLAT], bwr) + _dot_hi(wi[0:S5_FLAT], bwi)
    s_of_lane = lax.broadcasted_iota(jnp.int32, (S5_FLAT, S5_FLAT), 1) // S5_GROUP
    mt = zw
    for j in range(4):
        sh = S5_GROUP << j
        shifted = jnp.concatenate([jnp.zeros((sh, S5_FLAT), F32), mt[0:S5_FLAT - sh]], axis=0)
        mt = jnp.where(((s_of_lane >> j) & 1) == 1, shifted, mt)

    ub = u_ref[0].astype(BF16)
    y_ref[0] = _dot_nt(ub, mt)
    er_ref[...] = _dot_nt(ub, gtr)
    ei_ref[...] = _dot_nt(ub, gti)

    m16 = jnp.exp(ar_row * float(L))
    p16r = m16 * jnp.cos(ai_row * float(L))
    p16i = m16 * jnp.sin(ai_row * float(L))

    def body(cidx, carry):
        hr, hi = carry
        rows = pl.ds(pl.multiple_of(cidx * nseq, nseq), nseq)
        hinr_ref[rows, :] = hr
        hini_ref[rows, :] = hi
        nr = p16r * hr - p16i * hi + er_ref[rows, :]
        ni = p16r * hi + p16i * hr + ei_ref[rows, :]
        return nr, ni

    zero = jnp.zeros((nseq, P), F32)
    hlr, hli = lax.fori_loop(0, nchunk, body, (zero, zero))
    hlr_ref[0] = hlr
    hli_ref[0] = hli
    y_ref[0] = y_ref[0] + _dot_nt(hinr_ref[...], wr[S5_GROUP:]) + _dot_nt(hini_ref[...], wi[S5_GROUP:])

    e1 = jnp.exp(ar_row)
    l1r = e1 * jnp.cos(ai_row)
    l1i = e1 * jnp.sin(ai_row)
    us = us_ref[0]
    nt_dims = (((1,), (1,)), ((), ()))
    bur = _dot_hi(us, bbr, nt_dims)
    bui = _dot_hi(us, bbi, nt_dims)
    h0r = h0r_ref[0]
    h0i = h0i_ref[0]
    hsr = l1r * h0r - l1i * h0i + bur
    hsi = l1r * h0i + l1i * h0r + bui
    hsr_ref[0] = hsr
    hsi_ref[0] = hsi
    ys_ref[0] = _dot_nt(hsr, cr_ref[0]) - _dot_nt(hsi, ci_ref[0])


def _s5(u_flat, us_t, h0r_t, h0i_t, ar, ai, ldt, b_r, b_i, c_r, c_i, nseq, nchunk):
    g = S5_GROUPS
    rows = nseq * nchunk
    ns = us_t.shape[1]
    ar_row = ar.reshape(g, 1, S5_STATE)
    ai_row = ai.reshape(g, 1, S5_STATE)
    ar_col = ar.reshape(g, S5_STATE, 1)
    ai_col = ai.reshape(g, S5_STATE, 1)
    ldt3 = ldt.reshape(g, 1, 1)

    def blk(shape):
        return pl.BlockSpec((1,) + shape, lambda i: (i,) + (0,) * len(shape))

    return pl.pallas_call(
        functools.partial(_s5_kernel, nseq=nseq, nchunk=nchunk),
        grid=(g,),
        in_specs=[blk((rows, S5_FLAT)), blk((ns, S5_GROUP)), blk((ns, S5_STATE)), blk((ns, S5_STATE)),
                  blk((1, S5_STATE)), blk((1, S5_STATE)), blk((S5_STATE, 1)), blk((S5_STATE, 1)), blk((1, 1)),
                  blk((S5_STATE, S5_GROUP)), blk((S5_STATE, S5_GROUP)),
                  blk((S5_GROUP, S5_STATE)), blk((S5_GROUP, S5_STATE))],
        out_specs=[blk((rows, S5_FLAT)), blk((nseq, S5_STATE)), blk((nseq, S5_STATE)),
                   blk((ns, S5_GROUP)), blk((ns, S5_STATE)), blk((ns, S5_STATE))],
        out_shape=[jax.ShapeDtypeStruct((g, rows, S5_FLAT), F32),
                   jax.ShapeDtypeStruct((g, nseq, S5_STATE), F32),
                   jax.ShapeDtypeStruct((g, nseq, S5_STATE), F32),
                   jax.ShapeDtypeStruct((g, ns, S5_GROUP), F32),
                   jax.ShapeDtypeStruct((g, ns, S5_STATE), F32),
                   jax.ShapeDtypeStruct((g, ns, S5_STATE), F32)],
        scratch_shapes=[pltpu.VMEM((rows, S5_STATE), F32)] * 4,
        compiler_params=_compiler_params(("arbitrary",)),
        name="s5",
    )(u_flat, us_t, h0r_t, h0i_t, ar_row, ai_row, ar_col, ai_col, ldt3, b_r, b_i, c_r, c_i)


def _sample_pre_kernel(qkvz_ref, gates_ref, conv_ref, wconv_ref, alog_row_ref, dtb_row_ref,
                       newconv_ref, ops_ref, sc_ref):
    dk = GDN_HEAD_DIM
    x_new = qkvz_ref[:, 0:QKV_W]
    wc = wconv_ref[...]
    conv = x_new * wc[3:4, :]
    for j in range(CONV_W - 1):
        conv = conv + conv_ref[j] * wc[j:j + 1, :]
    qkv = _silu(conv)
    newconv_ref[0] = conv_ref[1]
    newconv_ref[1] = conv_ref[2]
    newconv_ref[2] = x_new

    beta_c, g_c = _gate_values(gates_ref[...], alog_row_ref[...], dtb_row_ref[...])
    scale = GDN_HEAD_DIM ** -0.5
    lane = lax.broadcasted_iota(jnp.int32, sc_ref.shape, 1)
    sc = jnp.zeros(sc_ref.shape, F32)
    for h in range(GDN_HEADS):
        q = _l2norm(qkv[:, h * dk:(h + 1) * dk]) * scale
        k = _l2norm(qkv[:, GDN_WIDTH + h * dk:GDN_WIDTH + (h + 1) * dk])
        v = qkv[:, 2 * GDN_WIDTH + h * dk:2 * GDN_WIDTH + (h + 1) * dk]
        beta = beta_c[:, h:h + 1]
        eg = jnp.exp(g_c[:, 4 + h:5 + h])
        cols = slice(h * dk, (h + 1) * dk)
        ops_ref[0, :, cols] = (beta * eg) * k
        ops_ref[1, :, cols] = q * eg
        ops_ref[2, :, cols] = k
        ops_ref[3, :, cols] = beta * v
        sc = jnp.where(lane == h, jnp.sum(q * k, axis=-1, keepdims=True), sc)
        sc = jnp.where(lane == 4 + h, eg, sc)
    sc_ref[...] = sc


def _sample_pre(qkvz_s, gates_s, conv_t, wconv, alog_row, dtb_row):
    ns = qkvz_s.shape[0]
    return pl.pallas_call(
        _sample_pre_kernel,
        out_shape=[jax.ShapeDtypeStruct((CONV_W - 1, ns, QKV_W), F32),
                   jax.ShapeDtypeStruct((4, ns, GDN_WIDTH), F32),
                   jax.ShapeDtypeStruct((ns, GATE_LANES), F32)],
        compiler_params=pltpu.CompilerParams(vmem_limit_bytes=VMEM_LIMIT),
        name="sample_pre",
    )(qkvz_s, gates_s, conv_t, wconv, alog_row, dtb_row)


def _sample_state_kernel(s_ref, ops_ref, sc_ref, z_ref, normw_ref, snew_ref, o_ref, *, nb):
    dk = GDN_HEAD_DIM
    row = lax.broadcasted_iota(jnp.int32, (8, dk), 0)
    for j in range(nb):
        for h in range(GDN_HEADS):
            cols = slice(h * dk, (h + 1) * dk)
            w = ops_ref[0, j:j + 1, cols]
            qg = ops_ref[1, j:j + 1, cols]
            k = ops_ref[2, j:j + 1, cols]
            u = ops_ref[3, j:j + 1, cols]
            lhs = jnp.where(row == 0, jnp.broadcast_to(w, (8, dk)), jnp.broadcast_to(qg, (8, dk)))
            s_old = s_ref[j, h]
            m1 = _dot(lhs, s_old)
            v_new = u - m1[0:1, :]
            qk = sc_ref[j:j + 1, h:h + 1]
            eg = sc_ref[j:j + 1, 4 + h:5 + h]
            o_ref[j:j + 1, cols] = m1[1:2, :] + qk * v_new
            k8 = jnp.where(row == 0, jnp.broadcast_to(k, (8, dk)), 0.0)
            snew_ref[j, h] = s_old * eg + _dot_tn(k8, jnp.broadcast_to(v_new, (8, dk)))
    normw = normw_ref[...]
    for h in range(GDN_HEADS):
        o = o_ref[:, h * dk:(h + 1) * dk]
        z = z_ref[:, h * dk:(h + 1) * dk]
        o_ref[:, h * dk:(h + 1) * dk] = (o * lax.rsqrt(jnp.mean(o * o, axis=-1, keepdims=True) + NORM_EPS)
                                         * normw * _silu(z))


def _sample_state(state, ops, sc, z, normw, nb=8):
    ns = state.shape[0]
    hd = (GDN_HEADS, GDN_HEAD_DIM, GDN_HEAD_DIM)
    return pl.pallas_call(
        functools.partial(_sample_state_kernel, nb=nb),
        grid=(ns // nb,),
        in_specs=[pl.BlockSpec((nb,) + hd, lambda i: (i, 0, 0, 0)),
                  pl.BlockSpec((4, nb, GDN_WIDTH), lambda i: (0, i, 0)),
                  pl.BlockSpec((nb, GATE_LANES), lambda i: (i, 0)),
                  pl.BlockSpec((nb, GDN_WIDTH), lambda i: (i, 0)),
                  _whole(normw.shape)],
        out_specs=[pl.BlockSpec((nb,) + hd, lambda i: (i, 0, 0, 0)),
                   pl.BlockSpec((nb, GDN_WIDTH), lambda i: (i, 0))],
        out_shape=[jax.ShapeDtypeStruct(state.shape, F32),
                   jax.ShapeDtypeStruct((ns, GDN_WIDTH), F32)],
        compiler_params=_compiler_params(("arbitrary",)),
        name="sample_state",
    )(state, ops, sc, z, normw)


def _post_kernel(x_ref, o_ref, ys_ref, u_ref, d_ref, wglu_ref, bglu_ref, wout_ref, g1_ref, b1_ref,
                 wff1_ref, wff2_ref, g2_ref, b2_ref, y_ref, *, ff_chunk):
    ys = jax.nn.gelu(ys_ref[...] + d_ref[...] * u_ref[...])
    ys = ys * jax.nn.sigmoid(_dot(ys, wglu_ref[...]) + bglu_ref[...])
    mix = _dot(o_ref[...], wout_ref[0:GDN_WIDTH, :]) + _dot(ys, wout_ref[GDN_WIDTH:, :])
    x1 = _layernorm(DN_ALPHA * x_ref[...] + mix, g1_ref[...], b1_ref[...])
    x1b = x1.astype(BF16)
    acc = jnp.zeros(x1.shape, F32)
    for f in range(D_FF // ff_chunk):
        hcol = jnp.dot(x1b, wff1_ref[:, f * ff_chunk:(f + 1) * ff_chunk], preferred_element_type=F32)
        hcol = jnp.square(jnp.maximum(hcol, 0.0))
        acc = acc + _dot(hcol, wff2_ref[f * ff_chunk:(f + 1) * ff_chunk, :])
    y_ref[...] = _layernorm(DN_ALPHA * x1 + acc, g2_ref[...], b2_ref[...])


def _post(x, o, ys, u, d, wglu, bglu, wout, g1, b1, wff1, wff2, g2, b2, tile):
    n = x.shape[0]

    def tok(w):
        return pl.BlockSpec((tile, w), lambda i: (i, 0))

    def resident(a):
        return pl.BlockSpec(a.shape, lambda i: (0,) * a.ndim, pipeline_mode=pl.Buffered(1))

    return pl.pallas_call(
        functools.partial(_post_kernel, ff_chunk=1024),
        grid=(n // tile,),
        in_specs=[tok(D_MODEL), tok(GDN_WIDTH), tok(S5_WIDTH), tok(S5_WIDTH)]
                 + [resident(a) for a in (d, wglu, bglu, wout, g1, b1, wff1, wff2, g2, b2)],
        out_specs=tok(D_MODEL),
        out_shape=jax.ShapeDtypeStruct((n, D_MODEL), F32),
        compiler_params=_compiler_params(("arbitrary",)),
        name="post",
    )(x, o, ys, u, d, wglu, bglu, wout, g1, b1, wff1, wff2, g2, b2)


def kernel(x_prompt, x_sample, state_gdn, state_conv, state_ssm_re, state_ssm_im, w_in, w_conv, gdn_a_log,
           gdn_dt_bias, gdn_norm_w, s5_a_re, s5_a_im, s5_b_re, s5_b_im, s5_c_re, s5_c_im, s5_d, s5_log_dt,
           w_glu, b_glu, w_out, ln1_g, ln1_b, w_ff1, w_ff2, ln2_g, ln2_b):
    bsz, seq, _ = x_prompt.shape
    ns = x_sample.shape[0]
    nchunk = seq // S5_CHUNK
    l = 0

    w = w_in[l]
    i_gate = QKVZ_W
    i_u = QKVZ_W + 2 * GDN_HEADS
    wm = w[:, :QKVZ_W].astype(BF16)
    wgate = w[:, i_gate:i_u]
    wg = jnp.pad(wgate, ((0, 0), (0, GATE_LANES - 2 * GDN_HEADS))).astype(BF16)
    wgt = jnp.pad(wgate.T, ((0, GATE_ROWS - 2 * GDN_HEADS), (0, 0))).astype(BF16)
    wu = w[:, i_u:].astype(BF16)
    pad_h = jnp.zeros((GDN_HEADS,), F32)
    alog8 = jnp.concatenate([pad_h, gdn_a_log[l]])
    dtb8 = jnp.concatenate([pad_h, gdn_dt_bias[l]])
    alog_row = jnp.pad(alog8, (0, GATE_LANES - 8)).reshape(1, GATE_LANES)
    dtb_row = jnp.pad(dtb8, (0, GATE_LANES - 8)).reshape(1, GATE_LANES)
    alog_col = jnp.pad(alog8, (0, GATE_ROWS - 8)).reshape(GATE_ROWS, 1)
    dtb_col = jnp.pad(dtb8, (0, GATE_ROWS - 8)).reshape(GATE_ROWS, 1)
    normw = gdn_norm_w[l].reshape(1, GDN_HEAD_DIM)
    wconv = w_conv[l]
    row = lambda a: a.reshape(1, -1)
    post_w = (row(s5_d[l]), w_glu[l].astype(BF16), row(b_glu[l]), w_out[l].astype(BF16), row(ln1_g[l]), row(ln1_b[l]),
              w_ff1[l].astype(BF16), w_ff2[l].astype(BF16), row(ln2_g[l]), row(ln2_b[l]))

    xp = x_prompt.reshape(bsz * seq, D_MODEL)
    xs = x_sample.reshape(ns, D_MODEL)
    qkvz_p, gates_p, gates_t_p, u_p = _proj(xp, wm, wg, wgt, wu, tile=512)
    qkvz_s, gates_s, _, u_s = _proj(xs, wm, wg, wgt, wu, tile=ns)

    o_p, gdn_p = _gdn_prompt(qkvz_p, gates_p, gates_t_p, wconv, alog_row, dtb_row, alog_col, dtb_col, normw, bsz, seq)
    conv_p = qkvz_p.reshape(bsz, seq, QKVZ_W)[:, seq - (CONV_W - 1):, :QKV_W]

    conv_t = jnp.transpose(state_conv[l], (1, 0, 2))
    newconv_t, ops_s, sc = _sample_pre(qkvz_s, gates_s, conv_t, wconv, alog_row, dtb_row)
    gdn_s, o_s = _sample_state(state_gdn[l], ops_s, sc, qkvz_s[:, QKV_W:], normw)
    conv_s = jnp.transpose(newconv_t, (1, 0, 2))

    u_flat = (u_p.reshape(bsz, nchunk, S5_CHUNK, S5_GROUPS, S5_GROUP).transpose(3, 1, 0, 2, 4)
              .reshape(S5_GROUPS, nchunk * bsz, S5_FLAT))
    us_t = u_s.reshape(ns, S5_GROUPS, S5_GROUP).transpose(1, 0, 2)
    h0r_t = state_ssm_re[l].transpose(1, 0, 2)
    h0i_t = state_ssm_im[l].transpose(1, 0, 2)
    y_flat, hlr, hli, ys_t, hsr, hsi = _s5(u_flat, us_t, h0r_t, h0i_t, s5_a_re[l], s5_a_im[l], s5_log_dt[l],
                                           s5_b_re[l], s5_b_im[l], s5_c_re[l], s5_c_im[l], bsz, nchunk)
    ys_p = (y_flat.reshape(S5_GROUPS, nchunk, bsz, S5_CHUNK, S5_GROUP).transpose(2, 1, 3, 0, 4)
            .reshape(bsz * seq, S5_WIDTH))
    ys_s = ys_t.transpose(1, 0, 2).reshape(ns, S5_WIDTH)

    y_p = _post(xp, o_p, ys_p, u_p, *post_w, tile=512)
    y_s = _post(xs, o_s, ys_s, u_s, *post_w, tile=ns)

    t3 = lambda a: a.transpose(1, 0, 2)[None]
    return (y_p.reshape(bsz, seq, D_MODEL), y_s.reshape(ns, 1, D_MODEL),
            gdn_p[None], conv_p[None], t3(hlr), t3(hli),
            gdn_s[None], conv_s[None], t3(hsr), t3(hsi))
```

```python
import functools

import jax
import jax.numpy as jnp
from jax import lax
from jax.experimental import pallas as pl
from jax.experimental.pallas import tpu as pltpu

F32 = jnp.float32
BF16 = jnp.bfloat16

D_MODEL = 1024
GDN_HEADS = 4
GDN_HEAD_DIM = 128
GDN_WIDTH = GDN_HEADS * GDN_HEAD_DIM
CONV_W = 4
GDN_CHUNK = 64
S5_WIDTH = D_MODEL - GDN_WIDTH
S5_GROUP = 16
S5_GROUPS = S5_WIDTH // S5_GROUP
S5_STATE = 64
D_FF = 4 * D_MODEL
DEPTH = 1
DN_ALPHA = (2.0 * DEPTH) ** 0.25
NORM_EPS = 1e-6
QKV_W = 3 * GDN_WIDTH
QKVZ_W = QKV_W + GDN_WIDTH
GATE_LANES = 128
GATE_ROWS = 16
S5_CHUNK = 16
S5_FLAT = S5_CHUNK * S5_GROUP
GDN_STEP = 2 * GDN_CHUNK
VMEM_LIMIT = 56 * 1024 * 1024


def _dot(a, b):
    return jnp.dot(a.astype(BF16), b.astype(BF16), preferred_element_type=F32)


def _dot_nt(a, b):
    return lax.dot_general(a.astype(BF16), b.astype(BF16), (((1,), (1,)), ((), ())), preferred_element_type=F32)


def _dot_tn(a, b):
    return lax.dot_general(a.astype(BF16), b.astype(BF16), (((0,), (0,)), ((), ())), preferred_element_type=F32)


def _split(x):
    hi = x.astype(BF16)
    lo = (x - hi.astype(F32)).astype(BF16)
    return hi, lo


def _dot_hi(a, b, dims=(((1,), (0,)), ((), ()))):
    ah, al = _split(a)
    bh, bl = _split(b)
    d = functools.partial(lax.dot_general, dimension_numbers=dims, preferred_element_type=F32)
    return d(ah, bh) + (d(al, bh) + d(ah, bl))


def _dot_x3(a, b_exact):
    a1 = a.astype(BF16)
    r1 = a - a1.astype(F32)
    a2 = r1.astype(BF16)
    a3 = (r1 - a2.astype(F32)).astype(BF16)
    return _dot(a1, b_exact) + (_dot(a2, b_exact) + _dot(a3, b_exact))


def _dot_x3_left(a_exact, b):
    b1 = b.astype(BF16)
    r1 = b - b1.astype(F32)
    b2 = r1.astype(BF16)
    b3 = (r1 - b2.astype(F32)).astype(BF16)
    return _dot(a_exact, b1) + (_dot(a_exact, b2) + _dot(a_exact, b3))


def _silu(x):
    return x * jax.nn.sigmoid(x)


def _layernorm(x, g, b):
    mu = jnp.mean(x, axis=-1, keepdims=True)
    xc = x - mu
    var = jnp.mean(xc * xc, axis=-1, keepdims=True)
    return xc * lax.rsqrt(var + NORM_EPS) * g + b


def _l2norm(a):
    return a * lax.rsqrt(jnp.sum(a * a, axis=-1, keepdims=True) + NORM_EPS)


def _compiler_params(semantics):
    return pltpu.CompilerParams(dimension_semantics=semantics, vmem_limit_bytes=VMEM_LIMIT)


def _whole(shape):
    n = len(shape)
    return pl.BlockSpec(shape, lambda *_: (0,) * n)


def _proj_kernel(x_ref, wm_ref, wg_ref, wgt_ref, wu_ref, qkvz_ref, gates_ref, gates_t_ref, u_ref):
    xb = x_ref[...].astype(BF16)
    qkvz_ref[...] = jnp.dot(xb, wm_ref[...], preferred_element_type=F32)
    gates_ref[...] = jnp.dot(xb, wg_ref[...], preferred_element_type=F32)
    gates_t_ref[...] = lax.dot_general(wgt_ref[...], xb, (((1,), (1,)), ((), ())), preferred_element_type=F32)
    u_ref[...] = jnp.dot(xb, wu_ref[...], preferred_element_type=F32)


def _proj(x, wm, wg, wgt, wu, tile):
    n = x.shape[0]
    return pl.pallas_call(
        _proj_kernel,
        grid=(n // tile,),
        in_specs=[pl.BlockSpec((tile, D_MODEL), lambda i: (i, 0)),
                  _whole(wm.shape), _whole(wg.shape), _whole(wgt.shape), _whole(wu.shape)],
        out_specs=[pl.BlockSpec((tile, QKVZ_W), lambda i: (i, 0)),
                   pl.BlockSpec((tile, GATE_LANES), lambda i: (i, 0)),
                   pl.BlockSpec((GATE_ROWS, tile), lambda i: (0, i)),
                   pl.BlockSpec((tile, S5_WIDTH), lambda i: (i, 0))],
        out_shape=[jax.ShapeDtypeStruct((n, QKVZ_W), F32),
                   jax.ShapeDtypeStruct((n, GATE_LANES), F32),
                   jax.ShapeDtypeStruct((GATE_ROWS, n), F32),
                   jax.ShapeDtypeStruct((n, S5_WIDTH), F32)],
        compiler_params=_compiler_params(("arbitrary",)),
        name="proj",
    )(x, wm, wg, wgt, wu)


def _gate_values(logits, a_log, dt_bias):
    beta = jax.nn.sigmoid(logits)
    g = -jnp.exp(a_log) * jax.nn.softplus(logits + dt_bias)
    return beta, g


def _gdn_kernel(qkvz_ref, gates_ref, gates_t_ref, wconv_ref, alog_row_ref, dtb_row_ref, alog_col_ref, dtb_col_ref,
                normw_ref, o_ref, s_ref, xpad_ref):
    step = pl.program_id(1)
    tt = GDN_STEP
    c = GDN_CHUNK
    dk = GDN_HEAD_DIM

    @pl.when(step == 0)
    def _():
        s_ref[...] = jnp.zeros_like(s_ref)
        xpad_ref[0:8, :] = jnp.zeros((8, QKV_W), F32)

    xpad_ref[8:8 + tt, :] = qkvz_ref[:, 0:QKV_W]
    wc = wconv_ref[...]
    conv = xpad_ref[8:8 + tt, :] * wc[3:4, :]
    for j in range(1, CONV_W):
        conv = conv + xpad_ref[8 - j:8 - j + tt, :] * wc[3 - j:4 - j, :]
    xpad_ref[0:8, :] = xpad_ref[tt:tt + 8, :]
    qkv = _silu(conv)

    beta_c, g_c = _gate_values(gates_ref[...], alog_row_ref[...], dtb_row_ref[...])
    _, g_r = _gate_values(gates_t_ref[...], alog_col_ref[...], dtb_col_ref[...])

    ri = lax.broadcasted_iota(jnp.int32, (tt, tt), 0)
    ci = lax.broadcasted_iota(jnp.int32, (tt, tt), 1)
    same_chunk = (ri // c) == (ci // c)
    tril_bd = jnp.where(same_chunk & (ci <= ri), 1.0, 0.0).astype(BF16)
    triu_bd = jnp.where(same_chunk & (ri <= ci), 1.0, 0.0).astype(BF16)
    gc_c = _dot_x3_left(tril_bd, g_c)
    gc_r = _dot_x3(g_r, triu_bd)

    r64 = lax.broadcasted_iota(jnp.int32, (c, c), 0)
    c64 = lax.broadcasted_iota(jnp.int32, (c, c), 1)
    tri = c64 <= r64
    strict = c64 < r64
    eye = jnp.where(r64 == c64, 1.0, 0.0).astype(F32)
    scale = GDN_HEAD_DIM ** -0.5
    normw = normw_ref[...]

    units = [(ch, h) for ch in range(tt // c) for h in range(GDN_HEADS)]
    qs, ks, qk, eg, gcol_, glast, rhs, p, tinv = {}, {}, {}, {}, {}, {}, {}, {}, {}
    for un in units:
        ch, h = un
        r0 = ch * c
        q = _l2norm(qkv[r0:r0 + c, h * dk:(h + 1) * dk]) * scale
        k = _l2norm(qkv[r0:r0 + c, GDN_WIDTH + h * dk:GDN_WIDTH + (h + 1) * dk])
        v = qkv[r0:r0 + c, 2 * GDN_WIDTH + h * dk:2 * GDN_WIDTH + (h + 1) * dk]
        beta = beta_c[r0:r0 + c, h:h + 1]
        gcol = gc_c[r0:r0 + c, 4 + h:5 + h]
        grow = gc_r[4 + h:5 + h, r0:r0 + c]
        diff = gcol - grow
        decay = jnp.where(tri, jnp.exp(jnp.where(tri, diff, 0.0)), 0.0)
        eg[un] = jnp.exp(gcol)
        gcol_[un] = gcol
        glast[un] = gcol[c - 1:c, :]
        kb = k * beta
        a1 = _dot_nt(jnp.concatenate([kb, q], axis=0), k)
        qk[un] = jnp.where(tri, a1[c:2 * c] * decay, 0.0)
        rhs[un] = jnp.concatenate([v * beta, kb * eg[un]], axis=1)
        qs[un], ks[un] = q, k
        p[un] = -jnp.where(strict, a1[0:c] * decay, 0.0)
        tinv[un] = eye + p[un]
    for _ in range(5):
        for un in units:
            p[un] = _dot_hi(p[un], p[un])
        for un in units:
            tinv[un] = tinv[un] + _dot_hi(tinv[un], p[un])
    sol = {un: _dot_hi(tinv[un], rhs[un]) for un in units}

    for ch in range(tt // c):
        r0 = ch * c
        m1, s_old, v_new = {}, {}, {}
        for h in range(GDN_HEADS):
            un = (ch, h)
            s_old[h] = s_ref[0, h]
            m1[h] = _dot(jnp.concatenate([sol[un][:, dk:2 * dk], qs[un] * eg[un]], axis=0), s_old[h])
        for h in range(GDN_HEADS):
            un = (ch, h)
            v_new[h] = sol[un][:, 0:dk] - m1[h][0:c]
            kd = ks[un] * jnp.exp(glast[un] - gcol_[un])
            s_ref[0, h] = s_old[h] * jnp.exp(glast[un]) + _dot_tn(kd, v_new[h])
        for h in range(GDN_HEADS):
            un = (ch, h)
            o = m1[h][c:2 * c] + _dot(qk[un], v_new[h])
            z = qkvz_ref[r0:r0 + c, QKV_W + h * dk:QKV_W + (h + 1) * dk]
            o = o * lax.rsqrt(jnp.mean(o * o, axis=-1, keepdims=True) + NORM_EPS) * normw * _silu(z)
            o_ref[r0:r0 + c, h * dk:(h + 1) * dk] = o


def _gdn_prompt(qkvz, gates, gates_t, wconv, alog_row, dtb_row, alog_col, dtb_col, normw, bsz, seq):
    steps = seq // GDN_STEP
    return pl.pallas_call(
        _gdn_kernel,
        grid=(bsz, steps),
        in_specs=[pl.BlockSpec((GDN_STEP, QKVZ_W), lambda b, i: (b * steps + i, 0)),
                  pl.BlockSpec((GDN_STEP, GATE_LANES), lambda b, i: (b * steps + i, 0)),
                  pl.BlockSpec((GATE_ROWS, GDN_STEP), lambda b, i: (0, b * steps + i)),
                  _whole(wconv.shape), _whole(alog_row.shape), _whole(dtb_row.shape),
                  _whole(alog_col.shape), _whole(dtb_col.shape), _whole(normw.shape)],
        out_specs=[pl.BlockSpec((GDN_STEP, GDN_WIDTH), lambda b, i: (b * steps + i, 0)),
                   pl.BlockSpec((1, GDN_HEADS, GDN_HEAD_DIM, GDN_HEAD_DIM), lambda b, i: (b, 0, 0, 0))],
        out_shape=[jax.ShapeDtypeStruct((bsz * seq, GDN_WIDTH), F32),
                   jax.ShapeDtypeStruct((bsz, GDN_HEADS, GDN_HEAD_DIM, GDN_HEAD_DIM), F32)],
        scratch_shapes=[pltpu.VMEM((8 + GDN_STEP, QKV_W), F32)],
        compiler_params=_compiler_params(("arbitrary", "arbitrary")),
        name="gdn_prompt",
    )(qkvz, gates, gates_t, wconv, alog_row, dtb_row, alog_col, dtb_col, normw)


def _s5_params_kernel(us_ref, h0r_ref, h0i_ref, ar_row_ref, ai_row_ref, ar_col_ref, ai_col_ref, ldt_ref,
                      br_ref, bi_ref, cr_ref, ci_ref,
                      mt_ref, gt_ref, gts_ref, wf_ref, coef_ref, ys_ref, hsr_ref, hsi_ref):
    L = S5_CHUNK
    P = S5_STATE
    dt = jnp.exp(ldt_ref[0])
    ar_row = ar_row_ref[0] * dt
    ai_row = ai_row_ref[0] * dt
    ar_col = ar_col_ref[0] * dt
    ai_col = ai_col_ref[0] * dt
    first = lax.broadcasted_iota(jnp.int32, (1, 2 * P), 1) < P

    ea = jnp.exp(ar_col)
    lbr = ea * jnp.cos(ai_col)
    lbi = ea * jnp.sin(ai_col)
    lam_r = ar_col_ref[0]
    lam_i = ai_col_ref[0]
    den = lam_r * lam_r + lam_i * lam_i
    fr = ((lbr - 1.0) * lam_r + lbi * lam_i) / den
    fi = (lbi * lam_r - (lbr - 1.0) * lam_i) / den
    b_r = br_ref[0]
    b_i = bi_ref[0]
    bbr = fr * b_r - fi * b_i
    bbi = fr * b_i + fi * b_r

    rt = lax.broadcasted_iota(jnp.int32, (S5_GROUP, S5_FLAT), 0)
    lt = lax.broadcasted_iota(jnp.int32, (S5_GROUP, S5_FLAT), 1)
    tile_mat = jnp.where(lt % S5_GROUP == rt, 1.0, 0.0).astype(BF16)
    bwr = _dot_x3(bbr, tile_mat)
    bwi = _dot_x3(bbi, tile_mat)

    s_lane = (lax.broadcasted_iota(jnp.int32, (1, S5_FLAT), 1) // S5_GROUP).astype(F32)
    tau_g = (L - 1.0) - s_lane
    mag = jnp.exp(ar_col * tau_g)
    pgr = mag * jnp.cos(ai_col * tau_g)
    pgi = mag * jnp.sin(ai_col * tau_g)
    gtr = pgr * bwr - pgi * bwi
    gti = pgr * bwi + pgi * bwr
    gt_ref[0] = jnp.concatenate([gtr, gti], axis=0).astype(BF16)
    gts_ref[0] = jnp.concatenate([gti, gtr], axis=0).astype(BF16)

    nt = L + 1
    tau_w = (lax.broadcasted_iota(jnp.int32, (nt * S5_GROUP, 1), 0) // S5_GROUP).astype(F32)
    magw = jnp.exp(tau_w * ar_row)
    pwr = magw * jnp.cos(tau_w * ai_row)
    pwi = magw * jnp.sin(tau_w * ai_row)
    c_r = jnp.concatenate([cr_ref[0]] * nt, axis=0)
    c_i = jnp.concatenate([ci_ref[0]] * nt, axis=0)
    wall = jnp.where(first, c_r * pwr - c_i * pwi, -(c_r * pwi + c_i * pwr))
    wf_ref[0] = wall[S5_GROUP:].astype(BF16)

    zw = _dot_hi(wall[0:S5_FLAT], jnp.concatenate([bwr, bwi], axis=0))
    s_of_lane = lax.broadcasted_iota(jnp.int32, (S5_FLAT, S5_FLAT), 1) // S5_GROUP
    mt = zw
    for j in range(4):
        sh = S5_GROUP << j
        shifted = jnp.concatenate([jnp.zeros((sh, S5_FLAT), F32), mt[0:S5_FLAT - sh]], axis=0)
        mt = jnp.where(((s_of_lane >> j) & 1) == 1, shifted, mt)
    mt_ref[0] = mt.astype(BF16)

    m16 = jnp.exp(ar_row * float(L))
    p16r = m16 * jnp.cos(ai_row * float(L))
    p16i = m16 * jnp.sin(ai_row * float(L))
    coef_b = jnp.where(first, -p16i, p16i)
    srow = lax.broadcasted_iota(jnp.int32, (8, 2 * P), 0)
    coef_ref[0] = jnp.where(srow == 0, p16r, jnp.where(srow == 1, coef_b, jnp.where(srow == 2, -coef_b, 0.0)))

    e1 = jnp.exp(ar_row[:, 0:P])
    l1r = e1 * jnp.cos(ai_row[:, 0:P])
    l1i = e1 * jnp.sin(ai_row[:, 0:P])
    us = us_ref[0]
    nt_dims = (((1,), (1,)), ((), ()))
    bur = _dot_hi(us, bbr, nt_dims)
    bui = _dot_hi(us, bbi, nt_dims)
    h0r = h0r_ref[0]
    h0i = h0i_ref[0]
    hsr = l1r * h0r - l1i * h0i + bur
    hsi = l1r * h0i + l1i * h0r + bui
    hsr_ref[0] = hsr
    hsi_ref[0] = hsi
    ys_ref[0] = _dot_nt(hsr, cr_ref[0][:, 0:P]) - _dot_nt(hsi, ci_ref[0][:, 0:P])


def _s5_params(us_t, h0r_t, h0i_t, ar, ai, ldt, b_r, b_i, c_r, c_i):
    g = S5_GROUPS
    ns = us_t.shape[1]
    twice = lambda a: jnp.concatenate([a, a], axis=-1)
    ar_row = twice(ar).reshape(g, 1, 2 * S5_STATE)
    ai_row = twice(ai).reshape(g, 1, 2 * S5_STATE)
    ar_col = ar.reshape(g, S5_STATE, 1)
    ai_col = ai.reshape(g, S5_STATE, 1)
    ldt3 = ldt.reshape(g, 1, 1)

    def blk(shape):
        return pl.BlockSpec((1,) + shape, lambda i: (i,) + (0,) * len(shape))

    return pl.pallas_call(
        _s5_params_kernel,
        grid=(g,),
        in_specs=[blk((ns, S5_GROUP)), blk((ns, S5_STATE)), blk((ns, S5_STATE)),
                  blk((1, 2 * S5_STATE)), blk((1, 2 * S5_STATE)), blk((S5_STATE, 1)), blk((S5_STATE, 1)), blk((1, 1)),
                  blk((S5_STATE, S5_GROUP)), blk((S5_STATE, S5_GROUP)),
                  blk((S5_GROUP, 2 * S5_STATE)), blk((S5_GROUP, 2 * S5_STATE))],
        out_specs=[blk((S5_FLAT, S5_FLAT)), blk((2 * S5_STATE, S5_FLAT)), blk((2 * S5_STATE, S5_FLAT)),
                   blk((S5_FLAT, 2 * S5_STATE)), blk((8, 2 * S5_STATE)),
                   blk((ns, S5_GROUP)), blk((ns, S5_STATE)), blk((ns, S5_STATE))],
        out_shape=[jax.ShapeDtypeStruct((g, S5_FLAT, S5_FLAT), BF16),
                   jax.ShapeDtypeStruct((g, 2 * S5_STATE, S5_FLAT), BF16),
                   jax.ShapeDtypeStruct((g, 2 * S5_STATE, S5_FLAT), BF16),
                   jax.ShapeDtypeStruct((g, S5_FLAT, 2 * S5_STATE), BF16),
                   jax.ShapeDtypeStruct((g, 8, 2 * S5_STATE), F32),
                   jax.ShapeDtypeStruct((g, ns, S5_GROUP), F32),
                   jax.ShapeDtypeStruct((g, ns, S5_STATE), F32),
                   jax.ShapeDtypeStruct((g, ns, S5_STATE), F32)],
        compiler_params=_compiler_params(("arbitrary",)),
        name="s5_params",
    )(us_t, h0r_t, h0i_t, ar_row, ai_row, ar_col, ai_col, ldt3, b_r, b_i, twice(c_r), twice(c_i))


LANE_TILE = 128
PIECES = LANE_TILE // S5_GROUP
RELAYOUT_ROWS = S5_CHUNK * S5_CHUNK


def _lane_gather(srcs, src_piece):
    blk = lax.broadcasted_iota(jnp.int32, srcs[0].shape, 1) // S5_GROUP
    acc = None
    for k, src in enumerate(srcs):
        shift = (S5_GROUP * (k - src_piece)) % LANE_TILE
        r = pltpu.roll(src, shift, axis=1) if shift else src
        acc = r if acc is None else jnp.where(blk == k, r, acc)
    return acc


def _s5_seq_kernel(u_ref, mt_ref, gt_ref, gts_ref, wf_ref, coef_ref, ys_ref, hl_ref,
                   uflat_ref, e_ref, es_ref, hin_ref, yflat_ref, xp_ref, *, nchunk):
    rr = RELAYOUT_ROWS
    ro = lax.broadcasted_iota(jnp.int32, (rr, rr), 0)
    ri = lax.broadcasted_iota(jnp.int32, (rr, rr), 1)
    perm = jnp.where((ro // S5_CHUNK == ri % S5_CHUNK) & (ro % S5_CHUNK == ri // S5_CHUNK), 1.0, 0.0).astype(BF16)

    def gather_in(m, carry):
        rows = pl.ds(pl.multiple_of(m * rr, rr), rr)
        xp_ref[...] = jnp.dot(perm, u_ref[rows, :].astype(BF16), preferred_element_type=F32)
        crow = pl.ds(pl.multiple_of(m * S5_CHUNK, S5_CHUNK), S5_CHUNK)
        for j in range(S5_FLAT // LANE_TILE):
            for g in range(S5_GROUPS):
                col = (g // PIECES) * LANE_TILE
                srcs = [xp_ref[(PIECES * j + k) * S5_CHUNK:(PIECES * j + k + 1) * S5_CHUNK, col:col + LANE_TILE]
                        for k in range(PIECES)]
                uflat_ref[g, crow, j * LANE_TILE:(j + 1) * LANE_TILE] = _lane_gather(srcs, g % PIECES).astype(BF16)
        return carry

    lax.fori_loop(0, nchunk // S5_CHUNK, gather_in, 0)

    for g in range(S5_GROUPS):
        lanes = slice(g * LANE_TILE, (g + 1) * LANE_TILE)
        ug = uflat_ref[g]
        e_ref[:, lanes] = _dot_nt(ug, gt_ref[g])
        es_ref[:, lanes] = _dot_nt(ug, gts_ref[g])

    ca = coef_ref[0:1, :]
    cb = coef_ref[1:2, :]
    cbs = coef_ref[2:3, :]

    def scan(c, carry):
        h, hs = carry
        row = pl.ds(c, 1)
        hin_ref[row, :] = h
        return ca * h + cb * hs + e_ref[row, :], ca * hs + cbs * h + es_ref[row, :]

    zero = jnp.zeros((1, S5_GROUPS * LANE_TILE), F32)
    h_last, _ = lax.fori_loop(0, nchunk, scan, (zero, zero))
    hl_ref[0] = h_last

    for g in range(S5_GROUPS):
        lanes = slice(g * LANE_TILE, (g + 1) * LANE_TILE)
        yflat_ref[g] = _dot_nt(uflat_ref[g], mt_ref[g]) + _dot_nt(hin_ref[:, lanes], wf_ref[g])

    def gather_out(m, carry):
        crow = pl.ds(pl.multiple_of(m * S5_CHUNK, S5_CHUNK), S5_CHUNK)
        for t in range(S5_CHUNK):
            j = (t // PIECES) * LANE_TILE
            for col in range(S5_WIDTH // LANE_TILE):
                srcs = [yflat_ref[PIECES * col + k, crow, j:j + LANE_TILE] for k in range(PIECES)]
                xp_ref[t * S5_CHUNK:(t + 1) * S5_CHUNK, col * LANE_TILE:(col + 1) * LANE_TILE] = (
                    _lane_gather(srcs, t % PIECES))
        rows = pl.ds(pl.multiple_of(m * rr, rr), rr)
        ys_ref[rows, :] = _dot_x3_left(perm, xp_ref[...])
        return carry

    lax.fori_loop(0, nchunk // S5_CHUNK, gather_out, 0)


def _s5_seq(u, mt, gt, gts, wf, coef, bsz, seq):
    nchunk = seq // S5_CHUNK
    width = S5_GROUPS * LANE_TILE

    def resident(a):
        return pl.BlockSpec(a.shape, lambda b: (0,) * a.ndim, pipeline_mode=pl.Buffered(1))

    return pl.pallas_call(
        functools.partial(_s5_seq_kernel, nchunk=nchunk),
        grid=(bsz,),
        in_specs=[pl.BlockSpec((seq, S5_WIDTH), lambda b: (b, 0))] + [resident(a) for a in (mt, gt, gts, wf, coef)],
        out_specs=[pl.BlockSpec((seq, S5_WIDTH), lambda b: (b, 0)),
                   pl.BlockSpec((1, 1, width), lambda b: (b, 0, 0))],
        out_shape=[jax.ShapeDtypeStruct((bsz * seq, S5_WIDTH), F32),
                   jax.ShapeDtypeStruct((bsz, 1, width), F32)],
        scratch_shapes=[pltpu.VMEM((S5_GROUPS, nchunk, S5_FLAT), BF16),
                        pltpu.VMEM((nchunk, width), F32), pltpu.VMEM((nchunk, width), F32),
                        pltpu.VMEM((nchunk, width), F32),
                        pltpu.VMEM((S5_GROUPS, nchunk, S5_FLAT), F32),
                        pltpu.VMEM((RELAYOUT_ROWS, S5_WIDTH), F32)],
        compiler_params=_compiler_params(("arbitrary",)),
        name="s5_seq",
    )(u, mt, gt, gts, wf, coef)


def _sample_pre_kernel(qkvz_ref, gates_ref, conv_ref, wconv_ref, alog_row_ref, dtb_row_ref,
                       newconv_ref, ops_ref, sc_ref):
    dk = GDN_HEAD_DIM
    x_new = qkvz_ref[:, 0:QKV_W]
    wc = wconv_ref[...]
    conv = x_new * wc[3:4, :]
    for j in range(CONV_W - 1):
        conv = conv + conv_ref[j] * wc[j:j + 1, :]
    qkv = _silu(conv)
    newconv_ref[0] = conv_ref[1]
    newconv_ref[1] = conv_ref[2]
    newconv_ref[2] = x_new

    beta_c, g_c = _gate_values(gates_ref[...], alog_row_ref[...], dtb_row_ref[...])
    scale = GDN_HEAD_DIM ** -0.5
    lane = lax.broadcasted_iota(jnp.int32, sc_ref.shape, 1)
    sc = jnp.zeros(sc_ref.shape, F32)
    for h in range(GDN_HEADS):
        q = _l2norm(qkv[:, h * dk:(h + 1) * dk]) * scale
        k = _l2norm(qkv[:, GDN_WIDTH + h * dk:GDN_WIDTH + (h + 1) * dk])
        v = qkv[:, 2 * GDN_WIDTH + h * dk:2 * GDN_WIDTH + (h + 1) * dk]
        beta = beta_c[:, h:h + 1]
        eg = jnp.exp(g_c[:, 4 + h:5 + h])
        cols = slice(h * dk, (h + 1) * dk)
        ops_ref[0, :, cols] = (beta * eg) * k
        ops_ref[1, :, cols] = q * eg
        ops_ref[2, :, cols] = k
        ops_ref[3, :, cols] = beta * v
        sc = jnp.where(lane == h, jnp.sum(q * k, axis=-1, keepdims=True), sc)
        sc = jnp.where(lane == 4 + h, eg, sc)
    sc_ref[...] = sc


def _sample_pre(qkvz_s, gates_s, conv_t, wconv, alog_row, dtb_row):
    ns = qkvz_s.shape[0]
    return pl.pallas_call(
        _sample_pre_kernel,
        out_shape=[jax.ShapeDtypeStruct((CONV_W - 1, ns, QKV_W), F32),
                   jax.ShapeDtypeStruct((4, ns, GDN_WIDTH), F32),
                   jax.ShapeDtypeStruct((ns, GATE_LANES), F32)],
        compiler_params=pltpu.CompilerParams(vmem_limit_bytes=VMEM_LIMIT),
        name="sample_pre",
    )(qkvz_s, gates_s, conv_t, wconv, alog_row, dtb_row)


def _sample_state_kernel(s_ref, ops_ref, sc_ref, z_ref, normw_ref, snew_ref, o_ref, *, nb):
    dk = GDN_HEAD_DIM
    row = lax.broadcasted_iota(jnp.int32, (8, dk), 0)
    for j in range(nb):
        for h in range(GDN_HEADS):
            cols = slice(h * dk, (h + 1) * dk)
            w = ops_ref[0, j:j + 1, cols]
            qg = ops_ref[1, j:j + 1, cols]
            k = ops_ref[2, j:j + 1, cols]
            u = ops_ref[3, j:j + 1, cols]
            lhs = jnp.where(row == 0, jnp.broadcast_to(w, (8, dk)), jnp.broadcast_to(qg, (8, dk)))
            s_old = s_ref[j, h]
            m1 = _dot(lhs, s_old)
            v_new = u - m1[0:1, :]
            qk = sc_ref[j:j + 1, h:h + 1]
            eg = sc_ref[j:j + 1, 4 + h:5 + h]
            o_ref[j:j + 1, cols] = m1[1:2, :] + qk * v_new
            k8 = jnp.where(row == 0, jnp.broadcast_to(k, (8, dk)), 0.0)
            snew_ref[j, h] = s_old * eg + _dot_tn(k8, jnp.broadcast_to(v_new, (8, dk)))
    normw = normw_ref[...]
    for h in range(GDN_HEADS):
        o = o_ref[:, h * dk:(h + 1) * dk]
        z = z_ref[:, h * dk:(h + 1) * dk]
        o_ref[:, h * dk:(h + 1) * dk] = (o * lax.rsqrt(jnp.mean(o * o, axis=-1, keepdims=True) + NORM_EPS)
                                         * normw * _silu(z))


def _sample_state(state, ops, sc, z, normw, nb=8):
    ns = state.shape[0]
    hd = (GDN_HEADS, GDN_HEAD_DIM, GDN_HEAD_DIM)
    return pl.pallas_call(
        functools.partial(_sample_state_kernel, nb=nb),
        grid=(ns // nb,),
        in_specs=[pl.BlockSpec((nb,) + hd, lambda i: (i, 0, 0, 0)),
                  pl.BlockSpec((4, nb, GDN_WIDTH), lambda i: (0, i, 0)),
                  pl.BlockSpec((nb, GATE_LANES), lambda i: (i, 0)),
                  pl.BlockSpec((nb, GDN_WIDTH), lambda i: (i, 0)),
                  _whole(normw.shape)],
        out_specs=[pl.BlockSpec((nb,) + hd, lambda i: (i, 0, 0, 0)),
                   pl.BlockSpec((nb, GDN_WIDTH), lambda i: (i, 0))],
        out_shape=[jax.ShapeDtypeStruct(state.shape, F32),
                   jax.ShapeDtypeStruct((ns, GDN_WIDTH), F32)],
        compiler_params=_compiler_params(("arbitrary",)),
        name="sample_state",
    )(state, ops, sc, z, normw)


def _post_kernel(x_ref, o_ref, ys_ref, u_ref, d_ref, wglu_ref, bglu_ref, wout_ref, g1_ref, b1_ref,
                 wff1_ref, wff2_ref, g2_ref, b2_ref, y_ref, *, ff_chunk):
    rows = lambda ref: ref[...].reshape(-1, ref.shape[-1])
    ys = jax.nn.gelu(rows(ys_ref) + d_ref[...] * rows(u_ref))
    ys = ys * jax.nn.sigmoid(_dot(ys, wglu_ref[...]) + bglu_ref[...])
    mix = _dot(rows(o_ref), wout_ref[0:GDN_WIDTH, :]) + _dot(ys, wout_ref[GDN_WIDTH:, :])
    x1 = _layernorm(DN_ALPHA * rows(x_ref) + mix, g1_ref[...], b1_ref[...])
    x1b = x1.astype(BF16)
    acc = jnp.zeros(x1.shape, F32)
    for f in range(D_FF // ff_chunk):
        hcol = jnp.dot(x1b, wff1_ref[:, f * ff_chunk:(f + 1) * ff_chunk], preferred_element_type=F32)
        hcol = jnp.square(jnp.maximum(hcol, 0.0))
        acc = acc + _dot(hcol, wff2_ref[f * ff_chunk:(f + 1) * ff_chunk, :])
    y_ref[...] = _layernorm(DN_ALPHA * x1 + acc, g2_ref[...], b2_ref[...]).reshape(y_ref.shape)


def _post_call(grid, tok_specs, out_spec, out_shape, args):
    weights = args[4:]

    def resident(a):
        return pl.BlockSpec(a.shape, lambda *_: (0,) * a.ndim, pipeline_mode=pl.Buffered(1))

    return pl.pallas_call(
        functools.partial(_post_kernel, ff_chunk=1024),
        grid=grid,
        in_specs=list(tok_specs) + [resident(a) for a in weights],
        out_specs=out_spec,
        out_shape=out_shape,
        compiler_params=_compiler_params(("arbitrary",) * len(grid)),
        name="post",
    )(*args)


def _post(x, o, ys, u, *weights, tile):
    n = x.shape[0]
    tok = lambda w: pl.BlockSpec((tile, w), lambda i: (i, 0))
    return _post_call((n // tile,), [tok(D_MODEL), tok(GDN_WIDTH), tok(S5_WIDTH), tok(S5_WIDTH)], tok(D_MODEL),
                      jax.ShapeDtypeStruct((n, D_MODEL), F32), (x, o, ys, u) + weights)


def kernel(x_prompt, x_sample, state_gdn, state_conv, state_ssm_re, state_ssm_im, w_in, w_conv, gdn_a_log,
           gdn_dt_bias, gdn_norm_w, s5_a_re, s5_a_im, s5_b_re, s5_b_im, s5_c_re, s5_c_im, s5_d, s5_log_dt,
           w_glu, b_glu, w_out, ln1_g, ln1_b, w_ff1, w_ff2, ln2_g, ln2_b):
    bsz, seq, _ = x_prompt.shape
    ns = x_sample.shape[0]
    nchunk = seq // S5_CHUNK
    l = 0

    w = w_in[l]
    i_gate = QKVZ_W
    i_u = QKVZ_W + 2 * GDN_HEADS
    wm = w[:, :QKVZ_W].astype(BF16)
    wgate = w[:, i_gate:i_u]
    wg = jnp.pad(wgate, ((0, 0), (0, GATE_LANES - 2 * GDN_HEADS))).astype(BF16)
    wgt = jnp.pad(wgate.T, ((0, GATE_ROWS - 2 * GDN_HEADS), (0, 0))).astype(BF16)
    wu = w[:, i_u:].astype(BF16)
    pad_h = jnp.zeros((GDN_HEADS,), F32)
    alog8 = jnp.concatenate([pad_h, gdn_a_log[l]])
    dtb8 = jnp.concatenate([pad_h, gdn_dt_bias[l]])
    alog_row = jnp.pad(alog8, (0, GATE_LANES - 8)).reshape(1, GATE_LANES)
    dtb_row = jnp.pad(dtb8, (0, GATE_LANES - 8)).reshape(1, GATE_LANES)
    alog_col = jnp.pad(alog8, (0, GATE_ROWS - 8)).reshape(GATE_ROWS, 1)
    dtb_col = jnp.pad(dtb8, (0, GATE_ROWS - 8)).reshape(GATE_ROWS, 1)
    normw = gdn_norm_w[l].reshape(1, GDN_HEAD_DIM)
    wconv = w_conv[l]
    row = lambda a: a.reshape(1, -1)
    post_w = (row(s5_d[l]), w_glu[l].astype(BF16), row(b_glu[l]), w_out[l].astype(BF16), row(ln1_g[l]), row(ln1_b[l]),
              w_ff1[l].astype(BF16), w_ff2[l].astype(BF16), row(ln2_g[l]), row(ln2_b[l]))

    xp = x_prompt.reshape(bsz * seq, D_MODEL)
    xs = x_sample.reshape(ns, D_MODEL)
    qkvz_p, gates_p, gates_t_p, u_p = _proj(xp, wm, wg, wgt, wu, tile=512)
    qkvz_s, gates_s, _, u_s = _proj(xs, wm, wg, wgt, wu, tile=ns)

    o_p, gdn_p = _gdn_prompt(qkvz_p, gates_p, gates_t_p, wconv, alog_row, dtb_row, alog_col, dtb_col, normw, bsz, seq)
    conv_p = qkvz_p.reshape(bsz, seq, QKVZ_W)[:, seq - (CONV_W - 1):, :QKV_W]

    conv_t = jnp.transpose(state_conv[l], (1, 0, 2))
    newconv_t, ops_s, sc = _sample_pre(qkvz_s, gates_s, conv_t, wconv, alog_row, dtb_row)
    gdn_s, o_s = _sample_state(state_gdn[l], ops_s, sc, qkvz_s[:, QKV_W:], normw)
    conv_s = jnp.transpose(newconv_t, (1, 0, 2))

    us_t = u_s.reshape(ns, S5_GROUPS, S5_GROUP).transpose(1, 0, 2)
    h0r_t = state_ssm_re[l].transpose(1, 0, 2)
    h0i_t = state_ssm_im[l].transpose(1, 0, 2)
    mt, gt, gts, wf, coef, ys_t, hsr, hsi = _s5_params(us_t, h0r_t, h0i_t, s5_a_re[l], s5_a_im[l], s5_log_dt[l],
                                                       s5_b_re[l], s5_b_im[l], s5_c_re[l], s5_c_im[l])
    coef_rows = coef.transpose(1, 0, 2).reshape(8, S5_GROUPS * LANE_TILE)
    ys_p, h_last = _s5_seq(u_p, mt, gt, gts, wf, coef_rows, bsz, seq)
    h_last = h_last.reshape(bsz, S5_GROUPS, 2 * S5_STATE)
    ys_s = ys_t.transpose(1, 0, 2).reshape(ns, S5_WIDTH)

    y_p = _post(xp, o_p, ys_p, u_p, *post_w, tile=512)
    y_s = _post(xs, o_s, ys_s, u_s, *post_w, tile=ns)

    t3 = lambda a: a.transpose(1, 0, 2)[None]
    return (y_p.reshape(bsz, seq, D_MODEL), y_s.reshape(ns, 1, D_MODEL),
            gdn_p[None], conv_p[None], h_last[None, :, :, :S5_STATE], h_last[None, :, :, S5_STATE:],
            gdn_s[None], conv_s[None], t3(hsr), t3(hsi))
```

```python
import functools

import jax
import jax.numpy as jnp
from jax import lax
from jax.experimental import pallas as pl
from jax.experimental.pallas import tpu as pltpu

F32 = jnp.float32
BF16 = jnp.bfloat16

D_MODEL = 1024
GDN_HEADS = 4
GDN_HEAD_DIM = 128
GDN_WIDTH = GDN_HEADS * GDN_HEAD_DIM
CONV_W = 4
GDN_CHUNK = 64
S5_WIDTH = D_MODEL - GDN_WIDTH
S5_GROUP = 16
S5_GROUPS = S5_WIDTH // S5_GROUP
S5_STATE = 64
D_FF = 4 * D_MODEL
DEPTH = 1
DN_ALPHA = (2.0 * DEPTH) ** 0.25
NORM_EPS = 1e-6
QKV_W = 3 * GDN_WIDTH
QKVZ_W = QKV_W + GDN_WIDTH
GATE_LANES = 128
GATE_ROWS = 16
S5_CHUNK = 16
S5_FLAT = S5_CHUNK * S5_GROUP
GDN_STEP = 4 * GDN_CHUNK
VMEM_LIMIT = 56 * 1024 * 1024


def _dot(a, b):
    return jnp.dot(a.astype(BF16), b.astype(BF16), preferred_element_type=F32)


def _dot_nt(a, b):
    return lax.dot_general(a.astype(BF16), b.astype(BF16), (((1,), (1,)), ((), ())), preferred_element_type=F32)


def _dot_tn(a, b):
    return lax.dot_general(a.astype(BF16), b.astype(BF16), (((0,), (0,)), ((), ())), preferred_element_type=F32)


def _split(x):
    hi = x.astype(BF16)
    lo = (x - hi.astype(F32)).astype(BF16)
    return hi, lo


def _dot_hi(a, b, dims=(((1,), (0,)), ((), ()))):
    ah, al = _split(a)
    bh, bl = _split(b)
    d = functools.partial(lax.dot_general, dimension_numbers=dims, preferred_element_type=F32)
    return d(ah, bh) + (d(al, bh) + d(ah, bl))


def _dot_x3(a, b_exact):
    a1 = a.astype(BF16)
    r1 = a - a1.astype(F32)
    a2 = r1.astype(BF16)
    a3 = (r1 - a2.astype(F32)).astype(BF16)
    return _dot(a1, b_exact) + (_dot(a2, b_exact) + _dot(a3, b_exact))


def _dot_x3_left(a_exact, b):
    b1 = b.astype(BF16)
    r1 = b - b1.astype(F32)
    b2 = r1.astype(BF16)
    b3 = (r1 - b2.astype(F32)).astype(BF16)
    return _dot(a_exact, b1) + (_dot(a_exact, b2) + _dot(a_exact, b3))


def _silu(x):
    return x * jax.nn.sigmoid(x)


def _layernorm(x, g, b):
    mu = jnp.mean(x, axis=-1, keepdims=True)
    xc = x - mu
    var = jnp.mean(xc * xc, axis=-1, keepdims=True)
    return xc * lax.rsqrt(var + NORM_EPS) * g + b


def _l2norm(a):
    return a * lax.rsqrt(jnp.sum(a * a, axis=-1, keepdims=True) + NORM_EPS)


def _compiler_params(semantics):
    return pltpu.CompilerParams(dimension_semantics=semantics, vmem_limit_bytes=VMEM_LIMIT)


def _whole(shape):
    n = len(shape)
    return pl.BlockSpec(shape, lambda *_: (0,) * n)


def _proj_kernel(x_ref, wm_ref, wg_ref, wgt_ref, wu_ref, qkvz_ref, gates_ref, gates_t_ref, u_ref):
    xb = x_ref[...].astype(BF16)
    qkvz_ref[...] = jnp.dot(xb, wm_ref[...], preferred_element_type=F32)
    gates_ref[...] = jnp.dot(xb, wg_ref[...], preferred_element_type=F32)
    gates_t_ref[...] = lax.dot_general(wgt_ref[...], xb, (((1,), (1,)), ((), ())), preferred_element_type=F32)
    u_ref[...] = jnp.dot(xb, wu_ref[...], preferred_element_type=F32)


def _proj(x, wm, wg, wgt, wu, tile):
    n = x.shape[0]
    return pl.pallas_call(
        _proj_kernel,
        grid=(n // tile,),
        in_specs=[pl.BlockSpec((tile, D_MODEL), lambda i: (i, 0)),
                  _whole(wm.shape), _whole(wg.shape), _whole(wgt.shape), _whole(wu.shape)],
        out_specs=[pl.BlockSpec((tile, QKVZ_W), lambda i: (i, 0)),
                   pl.BlockSpec((tile, GATE_LANES), lambda i: (i, 0)),
                   pl.BlockSpec((GATE_ROWS, tile), lambda i: (0, i)),
                   pl.BlockSpec((tile, S5_WIDTH), lambda i: (i, 0))],
        out_shape=[jax.ShapeDtypeStruct((n, QKVZ_W), F32),
                   jax.ShapeDtypeStruct((n, GATE_LANES), F32),
                   jax.ShapeDtypeStruct((GATE_ROWS, n), F32),
                   jax.ShapeDtypeStruct((n, S5_WIDTH), F32)],
        compiler_params=_compiler_params(("arbitrary",)),
        name="proj",
    )(x, wm, wg, wgt, wu)


def _gate_values(logits, a_log, dt_bias):
    beta = jax.nn.sigmoid(logits)
    g = -jnp.exp(a_log) * jax.nn.softplus(logits + dt_bias)
    return beta, g


def _gdn_kernel(qkvz_ref, gates_ref, gates_t_ref, wconv_ref, alog_row_ref, dtb_row_ref, alog_col_ref, dtb_col_ref,
                normw_ref, o_ref, s_ref, xpad_ref):
    step = pl.program_id(1)
    tt = GDN_STEP
    c = GDN_CHUNK
    dk = GDN_HEAD_DIM

    @pl.when(step == 0)
    def _():
        s_ref[...] = jnp.zeros_like(s_ref)
        xpad_ref[0:8, :] = jnp.zeros((8, QKV_W), F32)

    xpad_ref[8:8 + tt, :] = qkvz_ref[:, 0:QKV_W]
    wc = wconv_ref[...]

    beta_c, g_c = _gate_values(gates_ref[...], alog_row_ref[...], dtb_row_ref[...])
    _, g_r = _gate_values(gates_t_ref[...], alog_col_ref[...], dtb_col_ref[...])

    ri = lax.broadcasted_iota(jnp.int32, (tt, tt), 0)
    ci = lax.broadcasted_iota(jnp.int32, (tt, tt), 1)
    same_chunk = (ri // c) == (ci // c)
    tril_bd = jnp.where(same_chunk & (ci <= ri), 1.0, 0.0).astype(BF16)
    triu_bd = jnp.where(same_chunk & (ri <= ci), 1.0, 0.0).astype(BF16)
    gc_c = _dot_x3_left(tril_bd, g_c)
    gc_r = _dot_x3(g_r, triu_bd)

    r64 = lax.broadcasted_iota(jnp.int32, (c, c), 0)
    c64 = lax.broadcasted_iota(jnp.int32, (c, c), 1)
    tri = c64 <= r64
    strict = c64 < r64
    eye = jnp.where(r64 == c64, 1.0, 0.0).astype(F32)
    scale = GDN_HEAD_DIM ** -0.5
    normw = normw_ref[...]

    heads = range(GDN_HEADS)
    nch = tt // c

    def prep(ch):
        r0 = ch * c
        conv = xpad_ref[8 + r0:8 + r0 + c, :] * wc[3:4, :]
        for j in range(1, CONV_W):
            conv = conv + xpad_ref[8 + r0 - j:8 + r0 - j + c, :] * wc[3 - j:4 - j, :]
        qkv = _silu(conv)
        out = []
        for h in heads:
            q = _l2norm(qkv[:, h * dk:(h + 1) * dk]) * scale
            k = _l2norm(qkv[:, GDN_WIDTH + h * dk:GDN_WIDTH + (h + 1) * dk])
            v = qkv[:, 2 * GDN_WIDTH + h * dk:2 * GDN_WIDTH + (h + 1) * dk]
            beta = beta_c[r0:r0 + c, h:h + 1]
            gcol = gc_c[r0:r0 + c, 4 + h:5 + h]
            grow = gc_r[4 + h:5 + h, r0:r0 + c]
            decay = jnp.where(tri, jnp.exp(jnp.where(tri, gcol - grow, 0.0)), 0.0)
            eg = jnp.exp(gcol)
            g_last = gcol[c - 1:c, :]
            kb = k * beta
            a1 = _dot_nt(jnp.concatenate([kb, q], axis=0), k)
            out.append(dict(
                qg=q * eg, kd=k * jnp.exp(g_last - gcol), eg_last=jnp.exp(g_last),
                qk=jnp.where(tri, a1[c:2 * c] * decay, 0.0),
                rhs=jnp.concatenate([v * beta, kb * eg], axis=1),
                neg_l=-jnp.where(strict, a1[0:c] * decay, 0.0)))
        return out

    def solve(pre):
        p = [u["neg_l"] for u in pre]
        tinv = [eye + a for a in p]
        yield
        for _ in range(5):
            p = [_dot_hi(a, a) for a in p]
            tinv = [t + _dot_hi(t, a) for t, a in zip(tinv, p)]
            yield
        for u, t in zip(pre, tinv):
            u["sol"] = _dot_hi(t, u["rhs"])
        yield

    def advance(ch, pre):
        r0 = ch * c
        s_old = [s_ref[0, h] for h in heads]
        m1 = [_dot(jnp.concatenate([u["sol"][:, dk:2 * dk], u["qg"]], axis=0), s) for u, s in zip(pre, s_old)]
        yield
        v_new = [u["sol"][:, 0:dk] - m[0:c] for u, m in zip(pre, m1)]
        for h in heads:
            s_ref[0, h] = s_old[h] * pre[h]["eg_last"] + _dot_tn(pre[h]["kd"], v_new[h])
        yield
        for h in heads:
            o = m1[h][c:2 * c] + _dot(pre[h]["qk"], v_new[h])
            z = qkvz_ref[r0:r0 + c, QKV_W + h * dk:QKV_W + (h + 1) * dk]
            o = o * lax.rsqrt(jnp.mean(o * o, axis=-1, keepdims=True) + NORM_EPS) * normw * _silu(z)
            o_ref[r0:r0 + c, h * dk:(h + 1) * dk] = o
        yield

    def interleave(*gens):
        live = list(gens)
        while live:
            for gen in list(live):
                if next(gen, "done") == "done":
                    live.remove(gen)

    pre = {0: prep(0)}
    interleave(solve(pre[0]))
    for ch in range(nch):
        if ch + 1 < nch:
            pre[ch + 1] = prep(ch + 1)
            interleave(solve(pre[ch + 1]), advance(ch, pre[ch]))
        else:
            interleave(advance(ch, pre[ch]))
        del pre[ch]
    xpad_ref[0:8, :] = xpad_ref[tt:tt + 8, :]


def _gdn_prompt(qkvz, gates, gates_t, wconv, alog_row, dtb_row, alog_col, dtb_col, normw, bsz, seq):
    steps = seq // GDN_STEP
    return pl.pallas_call(
        _gdn_kernel,
        grid=(bsz, steps),
        in_specs=[pl.BlockSpec((GDN_STEP, QKVZ_W), lambda b, i: (b * steps + i, 0)),
                  pl.BlockSpec((GDN_STEP, GATE_LANES), lambda b, i: (b * steps + i, 0)),
                  pl.BlockSpec((GATE_ROWS, GDN_STEP), lambda b, i: (0, b * steps + i)),
                  _whole(wconv.shape), _whole(alog_row.shape), _whole(dtb_row.shape),
                  _whole(alog_col.shape), _whole(dtb_col.shape), _whole(normw.shape)],
        out_specs=[pl.BlockSpec((GDN_STEP, GDN_WIDTH), lambda b, i: (b * steps + i, 0)),
                   pl.BlockSpec((1, GDN_HEADS, GDN_HEAD_DIM, GDN_HEAD_DIM), lambda b, i: (b, 0, 0, 0))],
        out_shape=[jax.ShapeDtypeStruct((bsz * seq, GDN_WIDTH), F32),
                   jax.ShapeDtypeStruct((bsz, GDN_HEADS, GDN_HEAD_DIM, GDN_HEAD_DIM), F32)],
        scratch_shapes=[pltpu.VMEM((8 + GDN_STEP, QKV_W), F32)],
        compiler_params=_compiler_params(("arbitrary", "arbitrary")),
        name="gdn_prompt",
    )(qkvz, gates, gates_t, wconv, alog_row, dtb_row, alog_col, dtb_col, normw)


GDN_SEQS = 4


def _gdn_wide_kernel(qkvz_ref, gates_ref, wconv_ref, alog_row_ref, dtb_row_ref, normw_ref, o_ref, s_ref, xpad_ref):
    step = pl.program_id(1)
    ns = qkvz_ref.shape[0]
    c = GDN_CHUNK
    dk = GDN_HEAD_DIM
    rows = ns * c

    @pl.when(step == 0)
    def _():
        s_ref[...] = jnp.zeros_like(s_ref)
        xpad_ref[:, 0:8, :] = jnp.zeros((ns, 8, QKV_W), F32)

    wc = wconv_ref[...]
    beta_c, g_c = _gate_values(gates_ref[...].reshape(rows, GATE_LANES), alog_row_ref[...], dtb_row_ref[...])

    ri = lax.broadcasted_iota(jnp.int32, (rows, rows), 0)
    ci = lax.broadcasted_iota(jnp.int32, (rows, rows), 1)
    tril_bd = jnp.where(((ri // c) == (ci // c)) & (ci <= ri), 1.0, 0.0).astype(BF16)
    gc_c = _dot_x3_left(tril_bd, g_c)
    gc_t = [gc_c[i * GATE_LANES:(i + 1) * GATE_LANES, :].T for i in range(rows // GATE_LANES)]

    r64 = lax.broadcasted_iota(jnp.int32, (c, c), 0)
    c64 = lax.broadcasted_iota(jnp.int32, (c, c), 1)
    tri = c64 <= r64
    strict = c64 < r64
    eye = jnp.where(r64 == c64, 1.0, 0.0).astype(F32)
    sub_block = {b: (r64 // (2 * b) == c64 // (2 * b)) & ((r64 // b) % 2 == 1) & ((c64 // b) % 2 == 0)
                 for b in (1, 2, 4, 8, 16, 32)}
    scale = GDN_HEAD_DIM ** -0.5
    normw = normw_ref[...]

    units = []
    for s in range(ns):
        xpad_ref[s, 8:8 + c, :] = qkvz_ref[s, :, 0:QKV_W]
        conv = xpad_ref[s, 8:8 + c, :] * wc[3:4, :]
        for j in range(1, CONV_W):
            conv = conv + xpad_ref[s, 8 - j:8 - j + c, :] * wc[3 - j:4 - j, :]
        xpad_ref[s, 0:8, :] = xpad_ref[s, c:c + 8, :]
        qkv = _silu(conv)
        r0 = s * c
        for h in range(GDN_HEADS):
            q = _l2norm(qkv[:, h * dk:(h + 1) * dk]) * scale
            k = _l2norm(qkv[:, GDN_WIDTH + h * dk:GDN_WIDTH + (h + 1) * dk])
            v = qkv[:, 2 * GDN_WIDTH + h * dk:2 * GDN_WIDTH + (h + 1) * dk]
            beta = beta_c[r0:r0 + c, h:h + 1]
            gcol = gc_c[r0:r0 + c, 4 + h:5 + h]
            lane0 = r0 % GATE_LANES
            grow = gc_t[r0 // GATE_LANES][4 + h:5 + h, lane0:lane0 + c]
            decay = jnp.where(tri, jnp.exp(jnp.where(tri, gcol - grow, 0.0)), 0.0)
            eg = jnp.exp(gcol)
            g_last = gcol[c - 1:c, :]
            kb = k * beta
            a1 = _dot_nt(jnp.concatenate([kb, q], axis=0), k)
            units.append(dict(
                s=s, h=h, qg=q * eg, kd=k * jnp.exp(g_last - gcol), eg_last=jnp.exp(g_last),
                qk=jnp.where(tri, a1[c:2 * c] * decay, 0.0),
                rhs=jnp.concatenate([v * beta, kb * eg], axis=1),
                lmat=jnp.where(strict, a1[0:c] * decay, 0.0)))

    for u in units:
        u["tinv"] = eye - jnp.where(sub_block[1], u["lmat"], 0.0)
    for b in (2, 4, 8, 16, 32):
        for u in units:
            u["y"] = _dot_hi(jnp.where(sub_block[b], u["lmat"], 0.0), u["tinv"])
        for u in units:
            u["tinv"] = u["tinv"] - _dot_hi(u["tinv"], u["y"])
    for u in units:
        u["sol"] = _dot_hi(u["tinv"], u["rhs"])

    for u in units:
        u["s_old"] = s_ref[u["s"], u["h"]]
        u["m1"] = _dot(jnp.concatenate([u["sol"][:, dk:2 * dk], u["qg"]], axis=0), u["s_old"])
    for u in units:
        u["v_new"] = u["sol"][:, 0:dk] - u["m1"][0:c]
        s_ref[u["s"], u["h"]] = u["s_old"] * u["eg_last"] + _dot_tn(u["kd"], u["v_new"])
    for u in units:
        s, h = u["s"], u["h"]
        o = u["m1"][c:2 * c] + _dot(u["qk"], u["v_new"])
        z = qkvz_ref[s, :, QKV_W + h * dk:QKV_W + (h + 1) * dk]
        o = o * lax.rsqrt(jnp.mean(o * o, axis=-1, keepdims=True) + NORM_EPS) * normw * _silu(z)
        o_ref[s, :, h * dk:(h + 1) * dk] = o


def _gdn_wide(qkvz3, gates3, wconv, alog_row, dtb_row, normw):
    bsz, seq, _ = qkvz3.shape
    ns, c = min(GDN_SEQS, bsz), GDN_CHUNK
    hd = (GDN_HEADS, GDN_HEAD_DIM, GDN_HEAD_DIM)
    return pl.pallas_call(
        _gdn_wide_kernel,
        grid=(bsz // ns, seq // c),
        in_specs=[pl.BlockSpec((ns, c, QKVZ_W), lambda b, i: (b, i, 0)),
                  pl.BlockSpec((ns, c, GATE_LANES), lambda b, i: (b, i, 0)),
                  _whole(wconv.shape), _whole(alog_row.shape), _whole(dtb_row.shape), _whole(normw.shape)],
        out_specs=[pl.BlockSpec((ns, c, GDN_WIDTH), lambda b, i: (b, i, 0)),
                   pl.BlockSpec((ns,) + hd, lambda b, i: (b, 0, 0, 0))],
        out_shape=[jax.ShapeDtypeStruct((bsz, seq, GDN_WIDTH), F32),
                   jax.ShapeDtypeStruct((bsz,) + hd, F32)],
        scratch_shapes=[pltpu.VMEM((ns, 8 + c, QKV_W), F32)],
        compiler_params=_compiler_params(("arbitrary", "arbitrary")),
        name="gdn_prompt",
    )(qkvz3, gates3, wconv, alog_row, dtb_row, normw)


def _s5_params_kernel(us_ref, h0r_ref, h0i_ref, ar_row_ref, ai_row_ref, ar_col_ref, ai_col_ref, ldt_ref,
                      br_ref, bi_ref, cr_ref, ci_ref,
                      mt_ref, gt_ref, gts_ref, wf_ref, coef_ref, ys_ref, hsr_ref, hsi_ref):
    L = S5_CHUNK
    P = S5_STATE
    dt = jnp.exp(ldt_ref[0])
    ar_row = ar_row_ref[0] * dt
    ai_row = ai_row_ref[0] * dt
    ar_col = ar_col_ref[0] * dt
    ai_col = ai_col_ref[0] * dt
    first = lax.broadcasted_iota(jnp.int32, (1, 2 * P), 1) < P

    ea = jnp.exp(ar_col)
    lbr = ea * jnp.cos(ai_col)
    lbi = ea * jnp.sin(ai_col)
    lam_r = ar_col_ref[0]
    lam_i = ai_col_ref[0]
    den = lam_r * lam_r + lam_i * lam_i
    fr = ((lbr - 1.0) * lam_r + lbi * lam_i) / den
    fi = (lbi * lam_r - (lbr - 1.0) * lam_i) / den
    b_r = br_ref[0]
    b_i = bi_ref[0]
    bbr = fr * b_r - fi * b_i
    bbi = fr * b_i + fi * b_r

    rt = lax.broadcasted_iota(jnp.int32, (S5_GROUP, S5_FLAT), 0)
    lt = lax.broadcasted_iota(jnp.int32, (S5_GROUP, S5_FLAT), 1)
    tile_mat = jnp.where(lt % S5_GROUP == rt, 1.0, 0.0).astype(BF16)
    bwr = _dot_x3(bbr, tile_mat)
    bwi = _dot_x3(bbi, tile_mat)

    s_lane = (lax.broadcasted_iota(jnp.int32, (1, S5_FLAT), 1) // S5_GROUP).astype(F32)
    tau_g = (L - 1.0) - s_lane
    mag = jnp.exp(ar_col * tau_g)
    pgr = mag * jnp.cos(ai_col * tau_g)
    pgi = mag * jnp.sin(ai_col * tau_g)
    gtr = pgr * bwr - pgi * bwi
    gti = pgr * bwi + pgi * bwr
    gt_ref[0] = jnp.concatenate([gtr, gti], axis=0).astype(BF16)
    gts_ref[0] = jnp.concatenate([gti, gtr], axis=0).astype(BF16)

    nt = L + 1
    tau_w = (lax.broadcasted_iota(jnp.int32, (nt * S5_GROUP, 1), 0) // S5_GROUP).astype(F32)
    magw = jnp.exp(tau_w * ar_row)
    pwr = magw * jnp.cos(tau_w * ai_row)
    pwi = magw * jnp.sin(tau_w * ai_row)
    c_r = jnp.concatenate([cr_ref[0]] * nt, axis=0)
    c_i = jnp.concatenate([ci_ref[0]] * nt, axis=0)
    wall = jnp.where(first, c_r * pwr - c_i * pwi, -(c_r * pwi + c_i * pwr))
    wf_ref[0] = wall[S5_GROUP:].astype(BF16)

    zw = _dot_hi(wall[0:S5_FLAT], jnp.concatenate([bwr, bwi], axis=0))
    s_of_lane = lax.broadcasted_iota(jnp.int32, (S5_FLAT, S5_FLAT), 1) // S5_GROUP
    mt = zw
    for j in range(4):
        sh = S5_GROUP << j
        shifted = jnp.concatenate([jnp.zeros((sh, S5_FLAT), F32), mt[0:S5_FLAT - sh]], axis=0)
        mt = jnp.where(((s_of_lane >> j) & 1) == 1, shifted, mt)
    mt_ref[0] = mt.astype(BF16)

    m16 = jnp.exp(ar_row * float(L))
    p16r = m16 * jnp.cos(ai_row * float(L))
    p16i = m16 * jnp.sin(ai_row * float(L))
    coef_b = jnp.where(first, -p16i, p16i)
    srow = lax.broadcasted_iota(jnp.int32, (8, 2 * P), 0)
    coef_ref[0] = jnp.where(srow == 0, p16r, jnp.where(srow == 1, coef_b, jnp.where(srow == 2, -coef_b, 0.0)))

    e1 = jnp.exp(ar_row[:, 0:P])
    l1r = e1 * jnp.cos(ai_row[:, 0:P])
    l1i = e1 * jnp.sin(ai_row[:, 0:P])
    us = us_ref[0]
    nt_dims = (((1,), (1,)), ((), ()))
    bur = _dot_hi(us, bbr, nt_dims)
    bui = _dot_hi(us, bbi, nt_dims)
    h0r = h0r_ref[0]
    h0i = h0i_ref[0]
    hsr = l1r * h0r - l1i * h0i + bur
    hsi = l1r * h0i + l1i * h0r + bui
    hsr_ref[0] = hsr
    hsi_ref[0] = hsi
    ys_ref[0] = _dot_nt(hsr, cr_ref[0][:, 0:P]) - _dot_nt(hsi, ci_ref[0][:, 0:P])


def _s5_params(us_t, h0r_t, h0i_t, ar, ai, ldt, b_r, b_i, c_r, c_i):
    g = S5_GROUPS
    ns = us_t.shape[1]
    twice = lambda a: jnp.concatenate([a, a], axis=-1)
    ar_row = twice(ar).reshape(g, 1, 2 * S5_STATE)
    ai_row = twice(ai).reshape(g, 1, 2 * S5_STATE)
    ar_col = ar.reshape(g, S5_STATE, 1)
    ai_col = ai.reshape(g, S5_STATE, 1)
    ldt3 = ldt.reshape(g, 1, 1)

    def blk(shape):
        return pl.BlockSpec((1,) + shape, lambda i: (i,) + (0,) * len(shape))

    return pl.pallas_call(
        _s5_params_kernel,
        grid=(g,),
        in_specs=[blk((ns, S5_GROUP)), blk((ns, S5_STATE)), blk((ns, S5_STATE)),
                  blk((1, 2 * S5_STATE)), blk((1, 2 * S5_STATE)), blk((S5_STATE, 1)), blk((S5_STATE, 1)), blk((1, 1)),
                  blk((S5_STATE, S5_GROUP)), blk((S5_STATE, S5_GROUP)),
                  blk((S5_GROUP, 2 * S5_STATE)), blk((S5_GROUP, 2 * S5_STATE))],
        out_specs=[blk((S5_FLAT, S5_FLAT)), blk((2 * S5_STATE, S5_FLAT)), blk((2 * S5_STATE, S5_FLAT)),
                   blk((S5_FLAT, 2 * S5_STATE)), blk((8, 2 * S5_STATE)),
                   blk((ns, S5_GROUP)), blk((ns, S5_STATE)), blk((ns, S5_STATE))],
        out_shape=[jax.ShapeDtypeStruct((g, S5_FLAT, S5_FLAT), BF16),
                   jax.ShapeDtypeStruct((g, 2 * S5_STATE, S5_FLAT), BF16),
                   jax.ShapeDtypeStruct((g, 2 * S5_STATE, S5_FLAT), BF16),
                   jax.ShapeDtypeStruct((g, S5_FLAT, 2 * S5_STATE), BF16),
                   jax.ShapeDtypeStruct((g, 8, 2 * S5_STATE), F32),
                   jax.ShapeDtypeStruct((g, ns, S5_GROUP), F32),
                   jax.ShapeDtypeStruct((g, ns, S5_STATE), F32),
                   jax.ShapeDtypeStruct((g, ns, S5_STATE), F32)],
        compiler_params=_compiler_params(("arbitrary",)),
        name="s5_params",
    )(us_t, h0r_t, h0i_t, ar_row, ai_row, ar_col, ai_col, ldt3, b_r, b_i, twice(c_r), twice(c_i))


LANE_TILE = 128
PIECES = LANE_TILE // S5_GROUP
RELAYOUT_ROWS = S5_CHUNK * S5_CHUNK


def _lane_gather(srcs, src_piece):
    blk = lax.broadcasted_iota(jnp.int32, srcs[0].shape, 1) // S5_GROUP
    acc = None
    for k, src in enumerate(srcs):
        shift = (S5_GROUP * (k - src_piece)) % LANE_TILE
        r = pltpu.roll(src, shift, axis=1) if shift else src
        acc = r if acc is None else jnp.where(blk == k, r, acc)
    return acc


def _s5_seq_kernel(u_ref, mt_ref, gt_ref, gts_ref, wf_ref, coef_ref, ys_ref, hl_ref,
                   uflat_ref, e_ref, es_ref, hin_ref, yflat_ref, xp_ref, *, nchunk):
    rr = RELAYOUT_ROWS
    ro = lax.broadcasted_iota(jnp.int32, (rr, rr), 0)
    ri = lax.broadcasted_iota(jnp.int32, (rr, rr), 1)
    perm = jnp.where((ro // S5_CHUNK == ri % S5_CHUNK) & (ro % S5_CHUNK == ri // S5_CHUNK), 1.0, 0.0).astype(BF16)

    def gather_in(m, carry):
        rows = pl.ds(pl.multiple_of(m * rr, rr), rr)
        xp_ref[...] = jnp.dot(perm, u_ref[rows, :].astype(BF16), preferred_element_type=F32)
        crow = pl.ds(pl.multiple_of(m * S5_CHUNK, S5_CHUNK), S5_CHUNK)
        for j in range(S5_FLAT // LANE_TILE):
            for g in range(S5_GROUPS):
                col = (g // PIECES) * LANE_TILE
                srcs = [xp_ref[(PIECES * j + k) * S5_CHUNK:(PIECES * j + k + 1) * S5_CHUNK, col:col + LANE_TILE]
                        for k in range(PIECES)]
                uflat_ref[g, crow, j * LANE_TILE:(j + 1) * LANE_TILE] = _lane_gather(srcs, g % PIECES).astype(BF16)
        return carry

    lax.fori_loop(0, nchunk // S5_CHUNK, gather_in, 0)

    for g in range(S5_GROUPS):
        lanes = slice(g * LANE_TILE, (g + 1) * LANE_TILE)
        ug = uflat_ref[g]
        e_ref[:, lanes] = _dot_nt(ug, gt_ref[g])
        es_ref[:, lanes] = _dot_nt(ug, gts_ref[g])

    ca = coef_ref[0:1, :]
    cb = coef_ref[1:2, :]
    cbs = coef_ref[2:3, :]

    def scan(c, carry):
        h, hs = carry
        row = pl.ds(c, 1)
        hin_ref[row, :] = h
        return ca * h + cb * hs + e_ref[row, :], ca * hs + cbs * h + es_ref[row, :]

    zero = jnp.zeros((1, S5_GROUPS * LANE_TILE), F32)
    h_last, _ = lax.fori_loop(0, nchunk, scan, (zero, zero))
    hl_ref[0] = h_last

    for g in range(S5_GROUPS):
        lanes = slice(g * LANE_TILE, (g + 1) * LANE_TILE)
        yflat_ref[g] = _dot_nt(uflat_ref[g], mt_ref[g]) + _dot_nt(hin_ref[:, lanes], wf_ref[g])

    def gather_out(m, carry):
        crow = pl.ds(pl.multiple_of(m * S5_CHUNK, S5_CHUNK), S5_CHUNK)
        for t in range(S5_CHUNK):
            j = (t // PIECES) * LANE_TILE
            for col in range(S5_WIDTH // LANE_TILE):
                srcs = [yflat_ref[PIECES * col + k, crow, j:j + LANE_TILE] for k in range(PIECES)]
                xp_ref[t * S5_CHUNK:(t + 1) * S5_CHUNK, col * LANE_TILE:(col + 1) * LANE_TILE] = (
                    _lane_gather(srcs, t % PIECES))
        rows = pl.ds(pl.multiple_of(m * rr, rr), rr)
        ys_ref[rows, :] = _dot_x3_left(perm, xp_ref[...])
        return carry

    lax.fori_loop(0, nchunk // S5_CHUNK, gather_out, 0)


def _s5_seq(u, mt, gt, gts, wf, coef, bsz, seq):
    nchunk = seq // S5_CHUNK
    width = S5_GROUPS * LANE_TILE

    def resident(a):
        return pl.BlockSpec(a.shape, lambda b: (0,) * a.ndim, pipeline_mode=pl.Buffered(1))

    return pl.pallas_call(
        functools.partial(_s5_seq_kernel, nchunk=nchunk),
        grid=(bsz,),
        in_specs=[pl.BlockSpec((seq, S5_WIDTH), lambda b: (b, 0))] + [resident(a) for a in (mt, gt, gts, wf, coef)],
        out_specs=[pl.BlockSpec((seq, S5_WIDTH), lambda b: (b, 0)),
                   pl.BlockSpec((1, 1, width), lambda b: (b, 0, 0))],
        out_shape=[jax.ShapeDtypeStruct((bsz * seq, S5_WIDTH), F32),
                   jax.ShapeDtypeStruct((bsz, 1, width), F32)],
        scratch_shapes=[pltpu.VMEM((S5_GROUPS, nchunk, S5_FLAT), BF16),
                        pltpu.VMEM((nchunk, width), F32), pltpu.VMEM((nchunk, width), F32),
                        pltpu.VMEM((nchunk, width), F32),
                        pltpu.VMEM((S5_GROUPS, nchunk, S5_FLAT), F32),
                        pltpu.VMEM((RELAYOUT_ROWS, S5_WIDTH), F32)],
        compiler_params=_compiler_params(("arbitrary",)),
        name="s5_seq",
    )(u, mt, gt, gts, wf, coef)


def _sample_pre_kernel(qkvz_ref, gates_ref, conv_ref, wconv_ref, alog_row_ref, dtb_row_ref,
                       newconv_ref, ops_ref, sc_ref):
    dk = GDN_HEAD_DIM
    x_new = qkvz_ref[:, 0:QKV_W]
    wc = wconv_ref[...]
    conv = x_new * wc[3:4, :]
    for j in range(CONV_W - 1):
        conv = conv + conv_ref[j] * wc[j:j + 1, :]
    qkv = _silu(conv)
    newconv_ref[0] = conv_ref[1]
    newconv_ref[1] = conv_ref[2]
    newconv_ref[2] = x_new

    beta_c, g_c = _gate_values(gates_ref[...], alog_row_ref[...], dtb_row_ref[...])
    scale = GDN_HEAD_DIM ** -0.5
    lane = lax.broadcasted_iota(jnp.int32, sc_ref.shape, 1)
    sc = jnp.zeros(sc_ref.shape, F32)
    for h in range(GDN_HEADS):
        q = _l2norm(qkv[:, h * dk:(h + 1) * dk]) * scale
        k = _l2norm(qkv[:, GDN_WIDTH + h * dk:GDN_WIDTH + (h + 1) * dk])
        v = qkv[:, 2 * GDN_WIDTH + h * dk:2 * GDN_WIDTH + (h + 1) * dk]
        beta = beta_c[:, h:h + 1]
        eg = jnp.exp(g_c[:, 4 + h:5 + h])
        cols = slice(h * dk, (h + 1) * dk)
        ops_ref[0, :, cols] = (beta * eg) * k
        ops_ref[1, :, cols] = q * eg
        ops_ref[2, :, cols] = k
        ops_ref[3, :, cols] = beta * v
        sc = jnp.where(lane == h, jnp.sum(q * k, axis=-1, keepdims=True), sc)
        sc = jnp.where(lane == 4 + h, eg, sc)
    sc_ref[...] = sc


def _sample_pre(qkvz_s, gates_s, conv_t, wconv, alog_row, dtb_row):
    ns = qkvz_s.shape[0]
    return pl.pallas_call(
        _sample_pre_kernel,
        out_shape=[jax.ShapeDtypeStruct((CONV_W - 1, ns, QKV_W), F32),
                   jax.ShapeDtypeStruct((4, ns, GDN_WIDTH), F32),
                   jax.ShapeDtypeStruct((ns, GATE_LANES), F32)],
        compiler_params=pltpu.CompilerParams(vmem_limit_bytes=VMEM_LIMIT),
        name="sample_pre",
    )(qkvz_s, gates_s, conv_t, wconv, alog_row, dtb_row)


def _sample_state_kernel(s_ref, ops_ref, sc_ref, z_ref, normw_ref, snew_ref, o_ref, *, nb):
    dk = GDN_HEAD_DIM
    row = lax.broadcasted_iota(jnp.int32, (8, dk), 0)
    for j in range(nb):
        for h in range(GDN_HEADS):
            cols = slice(h * dk, (h + 1) * dk)
            w = ops_ref[0, j:j + 1, cols]
            qg = ops_ref[1, j:j + 1, cols]
            k = ops_ref[2, j:j + 1, cols]
            u = ops_ref[3, j:j + 1, cols]
            lhs = jnp.where(row == 0, jnp.broadcast_to(w, (8, dk)), jnp.broadcast_to(qg, (8, dk)))
            s_old = s_ref[j, h]
            m1 = _dot(lhs, s_old)
            v_new = u - m1[0:1, :]
            qk = sc_ref[j:j + 1, h:h + 1]
            eg = sc_ref[j:j + 1, 4 + h:5 + h]
            o_ref[j:j + 1, cols] = m1[1:2, :] + qk * v_new
            k8 = jnp.where(row == 0, jnp.broadcast_to(k, (8, dk)), 0.0)
            snew_ref[j, h] = s_old * eg + _dot_tn(k8, jnp.broadcast_to(v_new, (8, dk)))
    normw = normw_ref[...]
    for h in range(GDN_HEADS):
        o = o_ref[:, h * dk:(h + 1) * dk]
        z = z_ref[:, h * dk:(h + 1) * dk]
        o_ref[:, h * dk:(h + 1) * dk] = (o * lax.rsqrt(jnp.mean(o * o, axis=-1, keepdims=True) + NORM_EPS)
                                         * normw * _silu(z))


def _sample_state(state, ops, sc, z, normw, nb=8):
    ns = state.shape[0]
    hd = (GDN_HEADS, GDN_HEAD_DIM, GDN_HEAD_DIM)
    return pl.pallas_call(
        functools.partial(_sample_state_kernel, nb=nb),
        grid=(ns // nb,),
        in_specs=[pl.BlockSpec((nb,) + hd, lambda i: (i, 0, 0, 0)),
                  pl.BlockSpec((4, nb, GDN_WIDTH), lambda i: (0, i, 0)),
                  pl.BlockSpec((nb, GATE_LANES), lambda i: (i, 0)),
                  pl.BlockSpec((nb, GDN_WIDTH), lambda i: (i, 0)),
                  _whole(normw.shape)],
        out_specs=[pl.BlockSpec((nb,) + hd, lambda i: (i, 0, 0, 0)),
                   pl.BlockSpec((nb, GDN_WIDTH), lambda i: (i, 0))],
        out_shape=[jax.ShapeDtypeStruct(state.shape, F32),
                   jax.ShapeDtypeStruct((ns, GDN_WIDTH), F32)],
        compiler_params=_compiler_params(("arbitrary",)),
        name="sample_state",
    )(state, ops, sc, z, normw)


def _post_kernel(x_ref, o_ref, ys_ref, u_ref, d_ref, wglu_ref, bglu_ref, wout_ref, g1_ref, b1_ref,
                 wff1_ref, wff2_ref, g2_ref, b2_ref, y_ref, *, ff_chunk):
    rows = lambda ref: ref[...].reshape(-1, ref.shape[-1])
    ys = jax.nn.gelu(rows(ys_ref) + d_ref[...] * rows(u_ref))
    ys = ys * jax.nn.sigmoid(_dot(ys, wglu_ref[...]) + bglu_ref[...])
    mix = _dot(rows(o_ref), wout_ref[0:GDN_WIDTH, :]) + _dot(ys, wout_ref[GDN_WIDTH:, :])
    x1 = _layernorm(DN_ALPHA * rows(x_ref) + mix, g1_ref[...], b1_ref[...])
    x1b = x1.astype(BF16)
    acc = jnp.zeros(x1.shape, F32)
    for f in range(D_FF // ff_chunk):
        hcol = jnp.dot(x1b, wff1_ref[:, f * ff_chunk:(f + 1) * ff_chunk], preferred_element_type=F32)
        hcol = jnp.square(jnp.maximum(hcol, 0.0))
        acc = acc + _dot(hcol, wff2_ref[f * ff_chunk:(f + 1) * ff_chunk, :])
    y_ref[...] = _layernorm(DN_ALPHA * x1 + acc, g2_ref[...], b2_ref[...]).reshape(y_ref.shape)


def _post_call(grid, tok_specs, out_spec, out_shape, args):
    weights = args[4:]

    def resident(a):
        return pl.BlockSpec(a.shape, lambda *_: (0,) * a.ndim, pipeline_mode=pl.Buffered(1))

    return pl.pallas_call(
        functools.partial(_post_kernel, ff_chunk=1024),
        grid=grid,
        in_specs=list(tok_specs) + [resident(a) for a in weights],
        out_specs=out_spec,
        out_shape=out_shape,
        compiler_params=_compiler_params(("arbitrary",) * len(grid)),
        name="post",
    )(*args)


def _post(x, o, ys, u, *weights, tile):
    n = x.shape[0]
    tok = lambda w: pl.BlockSpec((tile, w), lambda i: (i, 0))
    return _post_call((n // tile,), [tok(D_MODEL), tok(GDN_WIDTH), tok(S5_WIDTH), tok(S5_WIDTH)], tok(D_MODEL),
                      jax.ShapeDtypeStruct((n, D_MODEL), F32), (x, o, ys, u) + weights)


def kernel(x_prompt, x_sample, state_gdn, state_conv, state_ssm_re, state_ssm_im, w_in, w_conv, gdn_a_log,
           gdn_dt_bias, gdn_norm_w, s5_a_re, s5_a_im, s5_b_re, s5_b_im, s5_c_re, s5_c_im, s5_d, s5_log_dt,
           w_glu, b_glu, w_out, ln1_g, ln1_b, w_ff1, w_ff2, ln2_g, ln2_b):
    bsz, seq, _ = x_prompt.shape
    ns = x_sample.shape[0]
    nchunk = seq // S5_CHUNK
    l = 0

    w = w_in[l]
    i_gate = QKVZ_W
    i_u = QKVZ_W + 2 * GDN_HEADS
    wm = w[:, :QKVZ_W].astype(BF16)
    wgate = w[:, i_gate:i_u]
    wg = jnp.pad(wgate, ((0, 0), (0, GATE_LANES - 2 * GDN_HEADS))).astype(BF16)
    wgt = jnp.pad(wgate.T, ((0, GATE_ROWS - 2 * GDN_HEADS), (0, 0))).astype(BF16)
    wu = w[:, i_u:].astype(BF16)
    pad_h = jnp.zeros((GDN_HEADS,), F32)
    alog8 = jnp.concatenate([pad_h, gdn_a_log[l]])
    dtb8 = jnp.concatenate([pad_h, gdn_dt_bias[l]])
    alog_row = jnp.pad(alog8, (0, GATE_LANES - 8)).reshape(1, GATE_LANES)
    dtb_row = jnp.pad(dtb8, (0, GATE_LANES - 8)).reshape(1, GATE_LANES)
    alog_col = jnp.pad(alog8, (0, GATE_ROWS - 8)).reshape(GATE_ROWS, 1)
    dtb_col = jnp.pad(dtb8, (0, GATE_ROWS - 8)).reshape(GATE_ROWS, 1)
    normw = gdn_norm_w[l].reshape(1, GDN_HEAD_DIM)
    wconv = w_conv[l]
    row = lambda a: a.reshape(1, -1)
    post_w = (row(s5_d[l]), w_glu[l].astype(BF16), row(b_glu[l]), w_out[l].astype(BF16), row(ln1_g[l]), row(ln1_b[l]),
              w_ff1[l].astype(BF16), w_ff2[l].astype(BF16), row(ln2_g[l]), row(ln2_b[l]))

    xp = x_prompt.reshape(bsz * seq, D_MODEL)
    xs = x_sample.reshape(ns, D_MODEL)
    qkvz_p, gates_p, gates_t_p, u_p = _proj(xp, wm, wg, wgt, wu, tile=512)
    qkvz_s, gates_s, _, u_s = _proj(xs, wm, wg, wgt, wu, tile=ns)

    o_p, gdn_p = _gdn_wide(qkvz_p.reshape(bsz, seq, QKVZ_W), gates_p.reshape(bsz, seq, GATE_LANES), wconv,
                           alog_row, dtb_row, normw)
    o_p = o_p.reshape(bsz * seq, GDN_WIDTH)
    conv_p = qkvz_p.reshape(bsz, seq, QKVZ_W)[:, seq - (CONV_W - 1):, :QKV_W]

    conv_t = jnp.transpose(state_conv[l], (1, 0, 2))
    newconv_t, ops_s, sc = _sample_pre(qkvz_s, gates_s, conv_t, wconv, alog_row, dtb_row)
    gdn_s, o_s = _sample_state(state_gdn[l], ops_s, sc, qkvz_s[:, QKV_W:], normw)
    conv_s = jnp.transpose(newconv_t, (1, 0, 2))

    us_t = u_s.reshape(ns, S5_GROUPS, S5_GROUP).transpose(1, 0, 2)
    h0r_t = state_ssm_re[l].transpose(1, 0, 2)
    h0i_t = state_ssm_im[l].transpose(1, 0, 2)
    mt, gt, gts, wf, coef, ys_t, hsr, hsi = _s5_params(us_t, h0r_t, h0i_t, s5_a_re[l], s5_a_im[l], s5_log_dt[l],
                                                       s5_b_re[l], s5_b_im[l], s5_c_re[l], s5_c_im[l])
    coef_rows = coef.transpose(1, 0, 2).reshape(8, S5_GROUPS * LANE_TILE)
    ys_p, h_last = _s5_seq(u_p, mt, gt, gts, wf, coef_rows, bsz, seq)
    h_last = h_last.reshape(bsz, S5_GROUPS, 2 * S5_STATE)
    ys_s = ys_t.transpose(1, 0, 2).reshape(ns, S5_WIDTH)

    y_p = _post(xp, o_p, ys_p, u_p, *post_w, tile=512)
    y_s = _post(xs, o_s, ys_s, u_s, *post_w, tile=ns)

    t3 = lambda a: a.transpose(1, 0, 2)[None]
    return (y_p.reshape(bsz, seq, D_MODEL), y_s.reshape(ns, 1, D_MODEL),
            gdn_p[None], conv_p[None], h_last[None, :, :, :S5_STATE], h_last[None, :, :, S5_STATE:],
            gdn_s[None], conv_s[None], t3(hsr), t3(hsi))
```

```python
import functools

import jax
import jax.numpy as jnp
from jax import lax
from jax.experimental import pallas as pl
from jax.experimental.pallas import tpu as pltpu

F32 = jnp.float32
BF16 = jnp.bfloat16

D_MODEL = 1024
GDN_HEADS = 4
GDN_HEAD_DIM = 128
GDN_WIDTH = GDN_HEADS * GDN_HEAD_DIM
CONV_W = 4
GDN_CHUNK = 64
S5_WIDTH = D_MODEL - GDN_WIDTH
S5_GROUP = 16
S5_GROUPS = S5_WIDTH // S5_GROUP
S5_STATE = 64
D_FF = 4 * D_MODEL
DEPTH = 1
DN_ALPHA = (2.0 * DEPTH) ** 0.25
NORM_EPS = 1e-6
QKV_W = 3 * GDN_WIDTH
QKVZ_W = QKV_W + GDN_WIDTH
GATE_LANES = 128
GATE_ROWS = 16
S5_CHUNK = 16
S5_FLAT = S5_CHUNK * S5_GROUP
GDN_STEP = 4 * GDN_CHUNK
VMEM_LIMIT = 56 * 1024 * 1024


def _dot(a, b):
    return jnp.dot(a.astype(BF16), b.astype(BF16), preferred_element_type=F32)


def _dot_nt(a, b):
    return lax.dot_general(a.astype(BF16), b.astype(BF16), (((1,), (1,)), ((), ())), preferred_element_type=F32)


def _dot_tn(a, b):
    return lax.dot_general(a.astype(BF16), b.astype(BF16), (((0,), (0,)), ((), ())), preferred_element_type=F32)


def _split(x):
    hi = x.astype(BF16)
    lo = (x - hi.astype(F32)).astype(BF16)
    return hi, lo


def _dot_hi(a, b, dims=(((1,), (0,)), ((), ()))):
    ah, al = _split(a)
    bh, bl = _split(b)
    d = functools.partial(lax.dot_general, dimension_numbers=dims, preferred_element_type=F32)
    return d(ah, bh) + (d(al, bh) + d(ah, bl))


def _dot_x3(a, b_exact):
    a1 = a.astype(BF16)
    r1 = a - a1.astype(F32)
    a2 = r1.astype(BF16)
    a3 = (r1 - a2.astype(F32)).astype(BF16)
    return _dot(a1, b_exact) + (_dot(a2, b_exact) + _dot(a3, b_exact))


def _dot_x3_left(a_exact, b):
    b1 = b.astype(BF16)
    r1 = b - b1.astype(F32)
    b2 = r1.astype(BF16)
    b3 = (r1 - b2.astype(F32)).astype(BF16)
    return _dot(a_exact, b1) + (_dot(a_exact, b2) + _dot(a_exact, b3))


def _complex_powers(base_r, base_i, exponent, nbits):
    shape = exponent.shape
    pr = jnp.ones(shape, F32)
    pi = jnp.zeros(shape, F32)
    br, bi = base_r, base_i
    for j in range(nbits):
        bit = ((exponent >> j) & 1) == 1
        pr, pi = jnp.where(bit, pr * br - pi * bi, pr), jnp.where(bit, pr * bi + pi * br, pi)
        br, bi = br * br - bi * bi, 2.0 * br * bi
    return pr, pi


def _silu(x):
    return x * jax.nn.sigmoid(x)


def _layernorm(x, g, b):
    mu = jnp.mean(x, axis=-1, keepdims=True)
    xc = x - mu
    var = jnp.mean(xc * xc, axis=-1, keepdims=True)
    return xc * lax.rsqrt(var + NORM_EPS) * g + b


def _l2norm(a):
    return a * lax.rsqrt(jnp.sum(a * a, axis=-1, keepdims=True) + NORM_EPS)


def _compiler_params(semantics):
    return pltpu.CompilerParams(dimension_semantics=semantics, vmem_limit_bytes=VMEM_LIMIT)


def _whole(shape):
    n = len(shape)
    return pl.BlockSpec(shape, lambda *_: (0,) * n)


def _proj_kernel(x_ref, wm_ref, wg_ref, wgt_ref, wu_ref, qkvz_ref, gates_ref, gates_t_ref, u_ref):
    xb = x_ref[...].astype(BF16)
    qkvz_ref[...] = jnp.dot(xb, wm_ref[...], preferred_element_type=F32)
    gates_ref[...] = jnp.dot(xb, wg_ref[...], preferred_element_type=F32)
    gates_t_ref[...] = lax.dot_general(wgt_ref[...], xb, (((1,), (1,)), ((), ())), preferred_element_type=F32)
    u_ref[...] = jnp.dot(xb, wu_ref[...], preferred_element_type=F32)


def _proj(x, wm, wg, wgt, wu, tile):
    n = x.shape[0]
    return pl.pallas_call(
        _proj_kernel,
        grid=(n // tile,),
        in_specs=[pl.BlockSpec((tile, D_MODEL), lambda i: (i, 0)),
                  _whole(wm.shape), _whole(wg.shape), _whole(wgt.shape), _whole(wu.shape)],
        out_specs=[pl.BlockSpec((tile, QKVZ_W), lambda i: (i, 0)),
                   pl.BlockSpec((tile, GATE_LANES), lambda i: (i, 0)),
                   pl.BlockSpec((GATE_ROWS, tile), lambda i: (0, i)),
                   pl.BlockSpec((tile, S5_WIDTH), lambda i: (i, 0))],
        out_shape=[jax.ShapeDtypeStruct((n, QKVZ_W), F32),
                   jax.ShapeDtypeStruct((n, GATE_LANES), F32),
                   jax.ShapeDtypeStruct((GATE_ROWS, n), F32),
                   jax.ShapeDtypeStruct((n, S5_WIDTH), F32)],
        compiler_params=_compiler_params(("arbitrary",)),
        name="proj",
    )(x, wm, wg, wgt, wu)


def _gate_values(logits, a_log, dt_bias):
    beta = jax.nn.sigmoid(logits)
    g = -jnp.exp(a_log) * jax.nn.softplus(logits + dt_bias)
    return beta, g


def _gdn_kernel(qkvz_ref, gates_ref, gates_t_ref, wconv_ref, alog_row_ref, dtb_row_ref, alog_col_ref, dtb_col_ref,
                normw_ref, o_ref, s_ref, xpad_ref):
    step = pl.program_id(1)
    tt = GDN_STEP
    c = GDN_CHUNK
    dk = GDN_HEAD_DIM

    @pl.when(step == 0)
    def _():
        s_ref[...] = jnp.zeros_like(s_ref)
        xpad_ref[0:8, :] = jnp.zeros((8, QKV_W), F32)

    xpad_ref[8:8 + tt, :] = qkvz_ref[:, 0:QKV_W]
    wc = wconv_ref[...]

    beta_c, g_c = _gate_values(gates_ref[...], alog_row_ref[...], dtb_row_ref[...])
    _, g_r = _gate_values(gates_t_ref[...], alog_col_ref[...], dtb_col_ref[...])

    ri = lax.broadcasted_iota(jnp.int32, (tt, tt), 0)
    ci = lax.broadcasted_iota(jnp.int32, (tt, tt), 1)
    same_chunk = (ri // c) == (ci // c)
    tril_bd = jnp.where(same_chunk & (ci <= ri), 1.0, 0.0).astype(BF16)
    triu_bd = jnp.where(same_chunk & (ri <= ci), 1.0, 0.0).astype(BF16)
    gc_c = _dot_x3_left(tril_bd, g_c)
    gc_r = _dot_x3(g_r, triu_bd)

    r64 = lax.broadcasted_iota(jnp.int32, (c, c), 0)
    c64 = lax.broadcasted_iota(jnp.int32, (c, c), 1)
    tri = c64 <= r64
    strict = c64 < r64
    eye = jnp.where(r64 == c64, 1.0, 0.0).astype(F32)
    scale = GDN_HEAD_DIM ** -0.5
    normw = normw_ref[...]

    heads = range(GDN_HEADS)
    nch = tt // c

    def prep(ch):
        r0 = ch * c
        conv = xpad_ref[8 + r0:8 + r0 + c, :] * wc[3:4, :]
        for j in range(1, CONV_W):
            conv = conv + xpad_ref[8 + r0 - j:8 + r0 - j + c, :] * wc[3 - j:4 - j, :]
        qkv = _silu(conv)
        out = []
        for h in heads:
            q = _l2norm(qkv[:, h * dk:(h + 1) * dk]) * scale
            k = _l2norm(qkv[:, GDN_WIDTH + h * dk:GDN_WIDTH + (h + 1) * dk])
            v = qkv[:, 2 * GDN_WIDTH + h * dk:2 * GDN_WIDTH + (h + 1) * dk]
            beta = beta_c[r0:r0 + c, h:h + 1]
            gcol = gc_c[r0:r0 + c, 4 + h:5 + h]
            grow = gc_r[4 + h:5 + h, r0:r0 + c]
            decay = jnp.where(tri, jnp.exp(jnp.where(tri, gcol - grow, 0.0)), 0.0)
            eg = jnp.exp(gcol)
            g_last = gcol[c - 1:c, :]
            kb = k * beta
            a1 = _dot_nt(jnp.concatenate([kb, q], axis=0), k)
            out.append(dict(
                qg=q * eg, kd=k * jnp.exp(g_last - gcol), eg_last=jnp.exp(g_last),
                qk=jnp.where(tri, a1[c:2 * c] * decay, 0.0),
                rhs=jnp.concatenate([v * beta, kb * eg], axis=1),
                neg_l=-jnp.where(strict, a1[0:c] * decay, 0.0)))
        return out

    def solve(pre):
        p = [u["neg_l"] for u in pre]
        tinv = [eye + a for a in p]
        yield
        for _ in range(5):
            p = [_dot_hi(a, a) for a in p]
            tinv = [t + _dot_hi(t, a) for t, a in zip(tinv, p)]
            yield
        for u, t in zip(pre, tinv):
            u["sol"] = _dot_hi(t, u["rhs"])
        yield

    def advance(ch, pre):
        r0 = ch * c
        s_old = [s_ref[0, h] for h in heads]
        m1 = [_dot(jnp.concatenate([u["sol"][:, dk:2 * dk], u["qg"]], axis=0), s) for u, s in zip(pre, s_old)]
        yield
        v_new = [u["sol"][:, 0:dk] - m[0:c] for u, m in zip(pre, m1)]
        for h in heads:
            s_ref[0, h] = s_old[h] * pre[h]["eg_last"] + _dot_tn(pre[h]["kd"], v_new[h])
        yield
        for h in heads:
            o = m1[h][c:2 * c] + _dot(pre[h]["qk"], v_new[h])
            z = qkvz_ref[r0:r0 + c, QKV_W + h * dk:QKV_W + (h + 1) * dk]
            o = o * lax.rsqrt(jnp.mean(o * o, axis=-1, keepdims=True) + NORM_EPS) * normw * _silu(z)
            o_ref[r0:r0 + c, h * dk:(h + 1) * dk] = o
        yield

    def interleave(*gens):
        live = list(gens)
        while live:
            for gen in list(live):
                if next(gen, "done") == "done":
                    live.remove(gen)

    pre = {0: prep(0)}
    interleave(solve(pre[0]))
    for ch in range(nch):
        if ch + 1 < nch:
            pre[ch + 1] = prep(ch + 1)
            interleave(solve(pre[ch + 1]), advance(ch, pre[ch]))
        else:
            interleave(advance(ch, pre[ch]))
        del pre[ch]
    xpad_ref[0:8, :] = xpad_ref[tt:tt + 8, :]


def _gdn_prompt(qkvz, gates, gates_t, wconv, alog_row, dtb_row, alog_col, dtb_col, normw, bsz, seq):
    steps = seq // GDN_STEP
    return pl.pallas_call(
        _gdn_kernel,
        grid=(bsz, steps),
        in_specs=[pl.BlockSpec((GDN_STEP, QKVZ_W), lambda b, i: (b * steps + i, 0)),
                  pl.BlockSpec((GDN_STEP, GATE_LANES), lambda b, i: (b * steps + i, 0)),
                  pl.BlockSpec((GATE_ROWS, GDN_STEP), lambda b, i: (0, b * steps + i)),
                  _whole(wconv.shape), _whole(alog_row.shape), _whole(dtb_row.shape),
                  _whole(alog_col.shape), _whole(dtb_col.shape), _whole(normw.shape)],
        out_specs=[pl.BlockSpec((GDN_STEP, GDN_WIDTH), lambda b, i: (b * steps + i, 0)),
                   pl.BlockSpec((1, GDN_HEADS, GDN_HEAD_DIM, GDN_HEAD_DIM), lambda b, i: (b, 0, 0, 0))],
        out_shape=[jax.ShapeDtypeStruct((bsz * seq, GDN_WIDTH), F32),
                   jax.ShapeDtypeStruct((bsz, GDN_HEADS, GDN_HEAD_DIM, GDN_HEAD_DIM), F32)],
        scratch_shapes=[pltpu.VMEM((8 + GDN_STEP, QKV_W), F32)],
        compiler_params=_compiler_params(("arbitrary", "arbitrary")),
        name="gdn_prompt",
    )(qkvz, gates, gates_t, wconv, alog_row, dtb_row, alog_col, dtb_col, normw)


GDN_SEQS = 8


def _gdn_wide_kernel(qkvz_ref, gates_ref, wconv_ref, alog_row_ref, dtb_row_ref, normw_ref, o_ref, s_ref, xpad_ref):
    step = pl.program_id(1)
    ns = qkvz_ref.shape[0]
    c = GDN_CHUNK
    dk = GDN_HEAD_DIM
    rows = ns * c

    @pl.when(step == 0)
    def _():
        s_ref[...] = jnp.zeros_like(s_ref)
        xpad_ref[:, 0:8, :] = jnp.zeros((ns, 8, QKV_W), F32)

    wc = wconv_ref[...]
    beta_c, g_c = _gate_values(gates_ref[...].reshape(rows, GATE_LANES), alog_row_ref[...], dtb_row_ref[...])

    ri = lax.broadcasted_iota(jnp.int32, (rows, rows), 0)
    ci = lax.broadcasted_iota(jnp.int32, (rows, rows), 1)
    tril_bd = jnp.where(((ri // c) == (ci // c)) & (ci <= ri), 1.0, 0.0).astype(BF16)
    gc_c = _dot_x3_left(tril_bd, g_c)
    gc_t = [gc_c[i * GATE_LANES:(i + 1) * GATE_LANES, :].T for i in range(rows // GATE_LANES)]

    r64 = lax.broadcasted_iota(jnp.int32, (c, c), 0)
    c64 = lax.broadcasted_iota(jnp.int32, (c, c), 1)
    tri = c64 <= r64
    strict = c64 < r64
    eye = jnp.where(r64 == c64, 1.0, 0.0).astype(F32)
    sub_block = {b: (r64 // (2 * b) == c64 // (2 * b)) & ((r64 // b) % 2 == 1) & ((c64 // b) % 2 == 0)
                 for b in (1, 2, 4, 8, 16, 32)}
    scale = GDN_HEAD_DIM ** -0.5
    normw = normw_ref[...]

    def prep(seqs, units):
        for s in seqs:
            xpad_ref[s, 8:8 + c, :] = qkvz_ref[s, :, 0:QKV_W]
            conv = xpad_ref[s, 8:8 + c, :] * wc[3:4, :]
            for j in range(1, CONV_W):
                conv = conv + xpad_ref[s, 8 - j:8 - j + c, :] * wc[3 - j:4 - j, :]
            xpad_ref[s, 0:8, :] = xpad_ref[s, c:c + 8, :]
            qkv = _silu(conv)
            r0 = s * c
            for h in range(GDN_HEADS):
                q = _l2norm(qkv[:, h * dk:(h + 1) * dk]) * scale
                k = _l2norm(qkv[:, GDN_WIDTH + h * dk:GDN_WIDTH + (h + 1) * dk])
                v = qkv[:, 2 * GDN_WIDTH + h * dk:2 * GDN_WIDTH + (h + 1) * dk]
                beta = beta_c[r0:r0 + c, h:h + 1]
                gcol = gc_c[r0:r0 + c, 4 + h:5 + h]
                lane0 = r0 % GATE_LANES
                grow = gc_t[r0 // GATE_LANES][4 + h:5 + h, lane0:lane0 + c]
                decay = jnp.where(tri, jnp.exp(jnp.where(tri, gcol - grow, 0.0)), 0.0)
                eg = jnp.exp(gcol)
                g_last = gcol[c - 1:c, :]
                kb = k * beta
                a1 = _dot_nt(jnp.concatenate([kb, q], axis=0), k)
                units.append(dict(
                    s=s, h=h, qg=q * eg, kd=k * jnp.exp(g_last - gcol), eg_last=jnp.exp(g_last),
                    qk=jnp.where(tri, a1[c:2 * c] * decay, 0.0),
                    rhs=jnp.concatenate([v * beta, kb * eg], axis=1),
                    lmat=jnp.where(strict, a1[0:c] * decay, 0.0)))
                yield

    def solve(units):
        for u in units:
            u["tinv"] = eye - jnp.where(sub_block[1], u["lmat"], 0.0)
        for b in (2, 4, 8, 16, 32):
            for u in units:
                u["y"] = _dot_hi(jnp.where(sub_block[b], u["lmat"], 0.0), u["tinv"])
            yield
            for u in units:
                u["tinv"] = u["tinv"] - _dot_hi(u["tinv"], u["y"])
            yield
        for u in units:
            u["sol"] = _dot_hi(u["tinv"], u["rhs"])
        yield

    def advance(units):
        for u in units:
            u["s_old"] = s_ref[u["s"], u["h"]]
            u["m1"] = _dot(jnp.concatenate([u["sol"][:, dk:2 * dk], u["qg"]], axis=0), u["s_old"])
        yield
        for u in units:
            u["v_new"] = u["sol"][:, 0:dk] - u["m1"][0:c]
            s_ref[u["s"], u["h"]] = u["s_old"] * u["eg_last"] + _dot_tn(u["kd"], u["v_new"])
        yield
        for u in units:
            s, h = u["s"], u["h"]
            o = u["m1"][c:2 * c] + _dot(u["qk"], u["v_new"])
            z = qkvz_ref[s, :, QKV_W + h * dk:QKV_W + (h + 1) * dk]
            o = o * lax.rsqrt(jnp.mean(o * o, axis=-1, keepdims=True) + NORM_EPS) * normw * _silu(z)
            o_ref[s, :, h * dk:(h + 1) * dk] = o
        yield

    def interleave(*gens):
        live = list(gens)
        while live:
            for gen in list(live):
                if next(gen, "done") == "done":
                    live.remove(gen)

    half = max(ns // 2, 1)
    wave_a, wave_b = [], []
    interleave(prep(range(0, half), wave_a))
    interleave(solve(wave_a), prep(range(half, ns), wave_b))
    interleave(solve(wave_b), advance(wave_a))
    interleave(advance(wave_b))


def _gdn_wide(qkvz3, gates3, wconv, alog_row, dtb_row, normw):
    bsz, seq, _ = qkvz3.shape
    ns, c = min(GDN_SEQS, bsz), GDN_CHUNK
    hd = (GDN_HEADS, GDN_HEAD_DIM, GDN_HEAD_DIM)
    return pl.pallas_call(
        _gdn_wide_kernel,
        grid=(bsz // ns, seq // c),
        in_specs=[pl.BlockSpec((ns, c, QKVZ_W), lambda b, i: (b, i, 0)),
                  pl.BlockSpec((ns, c, GATE_LANES), lambda b, i: (b, i, 0)),
                  _whole(wconv.shape), _whole(alog_row.shape), _whole(dtb_row.shape), _whole(normw.shape)],
        out_specs=[pl.BlockSpec((ns, c, GDN_WIDTH), lambda b, i: (b, i, 0)),
                   pl.BlockSpec((ns,) + hd, lambda b, i: (b, 0, 0, 0))],
        out_shape=[jax.ShapeDtypeStruct((bsz, seq, GDN_WIDTH), F32),
                   jax.ShapeDtypeStruct((bsz,) + hd, F32)],
        scratch_shapes=[pltpu.VMEM((ns, 8 + c, QKV_W), F32)],
        compiler_params=_compiler_params(("arbitrary", "arbitrary")),
        name="gdn_prompt",
    )(qkvz3, gates3, wconv, alog_row, dtb_row, normw)


def _s5_params_kernel(us_ref, h0r_ref, h0i_ref, ar_row_ref, ai_row_ref, ar_col_ref, ai_col_ref, ldt_ref,
                      br_ref, bi_ref, cr_ref, ci_ref,
                      mt_ref, gt_ref, gts_ref, wf_ref, coef_ref, ys_ref, hsr_ref, hsi_ref):
    L = S5_CHUNK
    P = S5_STATE
    dt = jnp.exp(ldt_ref[0])
    ar_row = ar_row_ref[0] * dt
    ai_row = ai_row_ref[0] * dt
    ar_col = ar_col_ref[0] * dt
    ai_col = ai_col_ref[0] * dt
    first = lax.broadcasted_iota(jnp.int32, (1, 2 * P), 1) < P

    ea = jnp.exp(ar_col)
    lbr = ea * jnp.cos(ai_col)
    lbi = ea * jnp.sin(ai_col)
    lam_r = ar_col_ref[0]
    lam_i = ai_col_ref[0]
    den = lam_r * lam_r + lam_i * lam_i
    fr = ((lbr - 1.0) * lam_r + lbi * lam_i) / den
    fi = (lbi * lam_r - (lbr - 1.0) * lam_i) / den
    b_r = br_ref[0]
    b_i = bi_ref[0]
    bbr = fr * b_r - fi * b_i
    bbi = fr * b_i + fi * b_r

    rt = lax.broadcasted_iota(jnp.int32, (S5_GROUP, S5_FLAT), 0)
    lt = lax.broadcasted_iota(jnp.int32, (S5_GROUP, S5_FLAT), 1)
    tile_mat = jnp.where(lt % S5_GROUP == rt, 1.0, 0.0).astype(BF16)
    bwr = _dot_x3(bbr, tile_mat)
    bwi = _dot_x3(bbi, tile_mat)

    tau_g = (L - 1) - lax.broadcasted_iota(jnp.int32, (P, S5_FLAT), 1) // S5_GROUP
    pgr, pgi = _complex_powers(lbr, lbi, tau_g, 4)
    gtr = pgr * bwr - pgi * bwi
    gti = pgr * bwi + pgi * bwr
    gt_ref[0] = jnp.concatenate([gtr, gti], axis=0).astype(BF16)
    gts_ref[0] = jnp.concatenate([gti, gtr], axis=0).astype(BF16)

    nt = L + 1
    e1 = jnp.exp(ar_row)
    l1r = e1 * jnp.cos(ai_row)
    l1i = e1 * jnp.sin(ai_row)
    tau_w = lax.broadcasted_iota(jnp.int32, (nt * S5_GROUP, 2 * P), 0) // S5_GROUP
    pwr, pwi = _complex_powers(l1r, l1i, tau_w, 5)
    c_r = jnp.concatenate([cr_ref[0]] * nt, axis=0)
    c_i = jnp.concatenate([ci_ref[0]] * nt, axis=0)
    wall = jnp.where(first, c_r * pwr - c_i * pwi, -(c_r * pwi + c_i * pwr))
    wf_ref[0] = wall[S5_GROUP:].astype(BF16)

    zw = _dot_hi(wall[0:S5_FLAT], jnp.concatenate([bwr, bwi], axis=0))
    s_of_lane = lax.broadcasted_iota(jnp.int32, (S5_FLAT, S5_FLAT), 1) // S5_GROUP
    mt = zw
    for j in range(4):
        sh = S5_GROUP << j
        shifted = jnp.concatenate([jnp.zeros((sh, S5_FLAT), F32), mt[0:S5_FLAT - sh]], axis=0)
        mt = jnp.where(((s_of_lane >> j) & 1) == 1, shifted, mt)
    mt_ref[0] = mt.astype(BF16)

    p16r = pwr[L * S5_GROUP:L * S5_GROUP + 1, :]
    p16i = pwi[L * S5_GROUP:L * S5_GROUP + 1, :]
    coef_b = jnp.where(first, -p16i, p16i)
    srow = lax.broadcasted_iota(jnp.int32, (8, 2 * P), 0)
    coef_ref[0] = jnp.where(srow == 0, p16r, jnp.where(srow == 1, coef_b, jnp.where(srow == 2, -coef_b, 0.0)))

    l1r = l1r[:, 0:P]
    l1i = l1i[:, 0:P]
    us = us_ref[0]
    nt_dims = (((1,), (1,)), ((), ()))
    bur = _dot_hi(us, bbr, nt_dims)
    bui = _dot_hi(us, bbi, nt_dims)
    h0r = h0r_ref[0]
    h0i = h0i_ref[0]
    hsr = l1r * h0r - l1i * h0i + bur
    hsi = l1r * h0i + l1i * h0r + bui
    hsr_ref[0] = hsr
    hsi_ref[0] = hsi
    ys_ref[0] = _dot_nt(hsr, cr_ref[0][:, 0:P]) - _dot_nt(hsi, ci_ref[0][:, 0:P])


def _s5_params(us_t, h0r_t, h0i_t, ar, ai, ldt, b_r, b_i, c_r, c_i):
    g = S5_GROUPS
    ns = us_t.shape[1]
    twice = lambda a: jnp.concatenate([a, a], axis=-1)
    ar_row = twice(ar).reshape(g, 1, 2 * S5_STATE)
    ai_row = twice(ai).reshape(g, 1, 2 * S5_STATE)
    ar_col = ar.reshape(g, S5_STATE, 1)
    ai_col = ai.reshape(g, S5_STATE, 1)
    ldt3 = ldt.reshape(g, 1, 1)

    def blk(shape):
        return pl.BlockSpec((1,) + shape, lambda i: (i,) + (0,) * len(shape))

    return pl.pallas_call(
        _s5_params_kernel,
        grid=(g,),
        in_specs=[blk((ns, S5_GROUP)), blk((ns, S5_STATE)), blk((ns, S5_STATE)),
                  blk((1, 2 * S5_STATE)), blk((1, 2 * S5_STATE)), blk((S5_STATE, 1)), blk((S5_STATE, 1)), blk((1, 1)),
                  blk((S5_STATE, S5_GROUP)), blk((S5_STATE, S5_GROUP)),
                  blk((S5_GROUP, 2 * S5_STATE)), blk((S5_GROUP, 2 * S5_STATE))],
        out_specs=[blk((S5_FLAT, S5_FLAT)), blk((2 * S5_STATE, S5_FLAT)), blk((2 * S5_STATE, S5_FLAT)),
                   blk((S5_FLAT, 2 * S5_STATE)), blk((8, 2 * S5_STATE)),
                   blk((ns, S5_GROUP)), blk((ns, S5_STATE)), blk((ns, S5_STATE))],
        out_shape=[jax.ShapeDtypeStruct((g, S5_FLAT, S5_FLAT), BF16),
                   jax.ShapeDtypeStruct((g, 2 * S5_STATE, S5_FLAT), BF16),
                   jax.ShapeDtypeStruct((g, 2 * S5_STATE, S5_FLAT), BF16),
                   jax.ShapeDtypeStruct((g, S5_FLAT, 2 * S5_STATE), BF16),
                   jax.ShapeDtypeStruct((g, 8, 2 * S5_STATE), F32),
                   jax.ShapeDtypeStruct((g, ns, S5_GROUP), F32),
                   jax.ShapeDtypeStruct((g, ns, S5_STATE), F32),
                   jax.ShapeDtypeStruct((g, ns, S5_STATE), F32)],
        compiler_params=_compiler_params(("arbitrary",)),
        name="s5_params",
    )(us_t, h0r_t, h0i_t, ar_row, ai_row, ar_col, ai_col, ldt3, b_r, b_i, twice(c_r), twice(c_i))


LANE_TILE = 128
PIECES = LANE_TILE // S5_GROUP
RELAYOUT_ROWS = S5_CHUNK * S5_CHUNK


def _piece_transpose(tiles):
    tiles = list(tiles)
    piece = lax.broadcasted_iota(jnp.int32, tiles[0].shape, 1) // S5_GROUP
    d = PIECES // 2
    while d:
        keep_low = (piece & d) == 0
        for k in range(PIECES):
            if k & d:
                continue
            a, b = tiles[k], tiles[k + d]
            tiles[k] = jnp.where(keep_low, a, pltpu.roll(b, S5_GROUP * d, axis=1))
            tiles[k + d] = jnp.where(keep_low, pltpu.roll(a, LANE_TILE - S5_GROUP * d, axis=1), b)
        d //= 2
    return tiles


def _s5_seq_kernel(u_ref, mt_ref, gt_ref, gts_ref, wf_ref, coef_ref, ys_ref, hl_ref,
                   uflat_ref, e_ref, es_ref, hin_ref, yflat_ref, xp_ref, *, nchunk):
    rr = RELAYOUT_ROWS
    ro = lax.broadcasted_iota(jnp.int32, (rr, rr), 0)
    ri = lax.broadcasted_iota(jnp.int32, (rr, rr), 1)
    perm = jnp.where((ro // S5_CHUNK == ri % S5_CHUNK) & (ro % S5_CHUNK == ri // S5_CHUNK), 1.0, 0.0).astype(BF16)

    def gather_in(m, carry):
        rows = pl.ds(pl.multiple_of(m * rr, rr), rr)
        xp_ref[...] = jnp.dot(perm, u_ref[rows, :].astype(BF16), preferred_element_type=F32)
        crow = pl.ds(pl.multiple_of(m * S5_CHUNK, S5_CHUNK), S5_CHUNK)
        for j in range(S5_FLAT // LANE_TILE):
            for cb in range(S5_WIDTH // LANE_TILE):
                tiles = [xp_ref[(PIECES * j + k) * S5_CHUNK:(PIECES * j + k + 1) * S5_CHUNK,
                                cb * LANE_TILE:(cb + 1) * LANE_TILE] for k in range(PIECES)]
                for p, tile in enumerate(_piece_transpose(tiles)):
                    uflat_ref[PIECES * cb + p, crow, j * LANE_TILE:(j + 1) * LANE_TILE] = tile.astype(BF16)
        return carry

    lax.fori_loop(0, nchunk // S5_CHUNK, gather_in, 0)

    for g in range(S5_GROUPS):
        lanes = slice(g * LANE_TILE, (g + 1) * LANE_TILE)
        ug = uflat_ref[g]
        e_ref[:, lanes] = _dot_nt(ug, gt_ref[g])
        es_ref[:, lanes] = _dot_nt(ug, gts_ref[g])

    ca = coef_ref[0:1, :]
    cb = coef_ref[1:2, :]
    cbs = coef_ref[2:3, :]

    def scan(c, carry):
        h, hs = carry
        row = pl.ds(c, 1)
        hin_ref[row, :] = h
        return ca * h + cb * hs + e_ref[row, :], ca * hs + cbs * h + es_ref[row, :]

    zero = jnp.zeros((1, S5_GROUPS * LANE_TILE), F32)
    h_last, _ = lax.fori_loop(0, nchunk, scan, (zero, zero))
    hl_ref[0] = h_last

    for g in range(S5_GROUPS):
        lanes = slice(g * LANE_TILE, (g + 1) * LANE_TILE)
        yflat_ref[g] = _dot_nt(uflat_ref[g], mt_ref[g]) + _dot_nt(hin_ref[:, lanes], wf_ref[g])

    def gather_out(m, carry):
        crow = pl.ds(pl.multiple_of(m * S5_CHUNK, S5_CHUNK), S5_CHUNK)
        for j in range(S5_FLAT // LANE_TILE):
            for cb in range(S5_WIDTH // LANE_TILE):
                tiles = [yflat_ref[PIECES * cb + k, crow, j * LANE_TILE:(j + 1) * LANE_TILE] for k in range(PIECES)]
                for p, tile in enumerate(_piece_transpose(tiles)):
                    t = PIECES * j + p
                    xp_ref[t * S5_CHUNK:(t + 1) * S5_CHUNK, cb * LANE_TILE:(cb + 1) * LANE_TILE] = tile
        rows = pl.ds(pl.multiple_of(m * rr, rr), rr)
        ys_ref[rows, :] = _dot_x3_left(perm, xp_ref[...])
        return carry

    lax.fori_loop(0, nchunk // S5_CHUNK, gather_out, 0)


def _s5_seq(u, mt, gt, gts, wf, coef, bsz, seq):
    nchunk = seq // S5_CHUNK
    width = S5_GROUPS * LANE_TILE

    def resident(a):
        return pl.BlockSpec(a.shape, lambda b: (0,) * a.ndim, pipeline_mode=pl.Buffered(1))

    return pl.pallas_call(
        functools.partial(_s5_seq_kernel, nchunk=nchunk),
        grid=(bsz,),
        in_specs=[pl.BlockSpec((seq, S5_WIDTH), lambda b: (b, 0))] + [resident(a) for a in (mt, gt, gts, wf, coef)],
        out_specs=[pl.BlockSpec((seq, S5_WIDTH), lambda b: (b, 0)),
                   pl.BlockSpec((1, 1, width), lambda b: (b, 0, 0))],
        out_shape=[jax.ShapeDtypeStruct((bsz * seq, S5_WIDTH), F32),
                   jax.ShapeDtypeStruct((bsz, 1, width), F32)],
        scratch_shapes=[pltpu.VMEM((S5_GROUPS, nchunk, S5_FLAT), BF16),
                        pltpu.VMEM((nchunk, width), F32), pltpu.VMEM((nchunk, width), F32),
                        pltpu.VMEM((nchunk, width), F32),
                        pltpu.VMEM((S5_GROUPS, nchunk, S5_FLAT), F32),
                        pltpu.VMEM((RELAYOUT_ROWS, S5_WIDTH), F32)],
        compiler_params=_compiler_params(("arbitrary",)),
        name="s5_seq",
    )(u, mt, gt, gts, wf, coef)


def _sample_pre_kernel(qkvz_ref, gates_ref, conv_ref, wconv_ref, alog_row_ref, dtb_row_ref,
                       newconv_ref, ops_ref, sc_ref):
    dk = GDN_HEAD_DIM
    x_new = qkvz_ref[:, 0:QKV_W]
    wc = wconv_ref[...]
    conv = x_new * wc[3:4, :]
    for j in range(CONV_W - 1):
        conv = conv + conv_ref[j] * wc[j:j + 1, :]
    qkv = _silu(conv)
    newconv_ref[0] = conv_ref[1]
    newconv_ref[1] = conv_ref[2]
    newconv_ref[2] = x_new

    beta_c, g_c = _gate_values(gates_ref[...], alog_row_ref[...], dtb_row_ref[...])
    scale = GDN_HEAD_DIM ** -0.5
    lane = lax.broadcasted_iota(jnp.int32, sc_ref.shape, 1)
    sc = jnp.zeros(sc_ref.shape, F32)
    for h in range(GDN_HEADS):
        q = _l2norm(qkv[:, h * dk:(h + 1) * dk]) * scale
        k = _l2norm(qkv[:, GDN_WIDTH + h * dk:GDN_WIDTH + (h + 1) * dk])
        v = qkv[:, 2 * GDN_WIDTH + h * dk:2 * GDN_WIDTH + (h + 1) * dk]
        beta = beta_c[:, h:h + 1]
        eg = jnp.exp(g_c[:, 4 + h:5 + h])
        cols = slice(h * dk, (h + 1) * dk)
        ops_ref[0, :, cols] = (beta * eg) * k
        ops_ref[1, :, cols] = q * eg
        ops_ref[2, :, cols] = k
        ops_ref[3, :, cols] = beta * v
        sc = jnp.where(lane == h, jnp.sum(q * k, axis=-1, keepdims=True), sc)
        sc = jnp.where(lane == 4 + h, eg, sc)
    sc_ref[...] = sc


def _sample_pre(qkvz_s, gates_s, conv_t, wconv, alog_row, dtb_row):
    ns = qkvz_s.shape[0]
    return pl.pallas_call(
        _sample_pre_kernel,
        out_shape=[jax.ShapeDtypeStruct((CONV_W - 1, ns, QKV_W), F32),
                   jax.ShapeDtypeStruct((4, ns, GDN_WIDTH), F32),
                   jax.ShapeDtypeStruct((ns, GATE_LANES), F32)],
        compiler_params=pltpu.CompilerParams(vmem_limit_bytes=VMEM_LIMIT),
        name="sample_pre",
    )(qkvz_s, gates_s, conv_t, wconv, alog_row, dtb_row)


def _sample_state_kernel(s_ref, ops_ref, sc_ref, z_ref, normw_ref, snew_ref, o_ref, *, nb):
    dk = GDN_HEAD_DIM
    row = lax.broadcasted_iota(jnp.int32, (8, dk), 0)
    units = [(j, h) for j in range(nb) for h in range(GDN_HEADS)]
    m1 = {}
    for j, h in units:
        cols = slice(h * dk, (h + 1) * dk)
        w = ops_ref[0, j:j + 1, cols]
        qg = ops_ref[1, j:j + 1, cols]
        lhs = jnp.where(row == 0, jnp.broadcast_to(w, (8, dk)), jnp.broadcast_to(qg, (8, dk)))
        m1[j, h] = _dot(lhs, s_ref[j, h])
    for j, h in units:
        cols = slice(h * dk, (h + 1) * dk)
        k = ops_ref[2, j:j + 1, cols]
        u = ops_ref[3, j:j + 1, cols]
        v_new = u - m1[j, h][0:1, :]
        qk = sc_ref[j:j + 1, h:h + 1]
        eg = sc_ref[j:j + 1, 4 + h:5 + h]
        o_ref[j:j + 1, cols] = m1[j, h][1:2, :] + qk * v_new
        k8 = jnp.where(row == 0, jnp.broadcast_to(k, (8, dk)), 0.0)
        snew_ref[j, h] = s_ref[j, h] * eg + _dot_tn(k8, jnp.broadcast_to(v_new, (8, dk)))
    normw = normw_ref[...]
    for h in range(GDN_HEADS):
        o = o_ref[:, h * dk:(h + 1) * dk]
        z = z_ref[:, h * dk:(h + 1) * dk]
        o_ref[:, h * dk:(h + 1) * dk] = (o * lax.rsqrt(jnp.mean(o * o, axis=-1, keepdims=True) + NORM_EPS)
                                         * normw * _silu(z))


def _sample_state(state, ops, sc, z, normw, nb=8):
    ns = state.shape[0]
    hd = (GDN_HEADS, GDN_HEAD_DIM, GDN_HEAD_DIM)
    return pl.pallas_call(
        functools.partial(_sample_state_kernel, nb=nb),
        grid=(ns // nb,),
        in_specs=[pl.BlockSpec((nb,) + hd, lambda i: (i, 0, 0, 0)),
                  pl.BlockSpec((4, nb, GDN_WIDTH), lambda i: (0, i, 0)),
                  pl.BlockSpec((nb, GATE_LANES), lambda i: (i, 0)),
                  pl.BlockSpec((nb, GDN_WIDTH), lambda i: (i, 0)),
                  _whole(normw.shape)],
        out_specs=[pl.BlockSpec((nb,) + hd, lambda i: (i, 0, 0, 0)),
                   pl.BlockSpec((nb, GDN_WIDTH), lambda i: (i, 0))],
        out_shape=[jax.ShapeDtypeStruct(state.shape, F32),
                   jax.ShapeDtypeStruct((ns, GDN_WIDTH), F32)],
        compiler_params=_compiler_params(("arbitrary",)),
        name="sample_state",
    )(state, ops, sc, z, normw)


def _post_kernel(x_ref, o_ref, ys_ref, u_ref, d_ref, wglu_ref, bglu_ref, wout_ref, g1_ref, b1_ref,
                 wff1_ref, wff2_ref, g2_ref, b2_ref, y_ref, *, ff_chunk):
    rows = lambda ref: ref[...].reshape(-1, ref.shape[-1])
    ys = jax.nn.gelu(rows(ys_ref) + d_ref[...] * rows(u_ref))
    ys = ys * jax.nn.sigmoid(_dot(ys, wglu_ref[...]) + bglu_ref[...])
    mix = _dot(rows(o_ref), wout_ref[0:GDN_WIDTH, :]) + _dot(ys, wout_ref[GDN_WIDTH:, :])
    x1 = _layernorm(DN_ALPHA * rows(x_ref) + mix, g1_ref[...], b1_ref[...])
    x1b = x1.astype(BF16)
    acc = jnp.zeros(x1.shape, F32)
    for f in range(D_FF // ff_chunk):
        hcol = jnp.dot(x1b, wff1_ref[:, f * ff_chunk:(f + 1) * ff_chunk], preferred_element_type=F32)
        hcol = jnp.square(jnp.maximum(hcol, 0.0))
        acc = acc + _dot(hcol, wff2_ref[f * ff_chunk:(f + 1) * ff_chunk, :])
    y_ref[...] = _layernorm(DN_ALPHA * x1 + acc, g2_ref[...], b2_ref[...]).reshape(y_ref.shape)


def _post_call(grid, tok_specs, out_spec, out_shape, args):
    weights = args[4:]

    def resident(a):
        return pl.BlockSpec(a.shape, lambda *_: (0,) * a.ndim, pipeline_mode=pl.Buffered(1))

    return pl.pallas_call(
        functools.partial(_post_kernel, ff_chunk=1024),
        grid=grid,
        in_specs=list(tok_specs) + [resident(a) for a in weights],
        out_specs=out_spec,
        out_shape=out_shape,
        compiler_params=_compiler_params(("arbitrary",) * len(grid)),
        name="post",
    )(*args)


def _post(x, o, ys, u, *weights, tile):
    n = x.shape[0]
    tok = lambda w: pl.BlockSpec((tile, w), lambda i: (i, 0))
    return _post_call((n // tile,), [tok(D_MODEL), tok(GDN_WIDTH), tok(S5_WIDTH), tok(S5_WIDTH)], tok(D_MODEL),
                      jax.ShapeDtypeStruct((n, D_MODEL), F32), (x, o, ys, u) + weights)


def kernel(x_prompt, x_sample, state_gdn, state_conv, state_ssm_re, state_ssm_im, w_in, w_conv, gdn_a_log,
           gdn_dt_bias, gdn_norm_w, s5_a_re, s5_a_im, s5_b_re, s5_b_im, s5_c_re, s5_c_im, s5_d, s5_log_dt,
           w_glu, b_glu, w_out, ln1_g, ln1_b, w_ff1, w_ff2, ln2_g, ln2_b):
    bsz, seq, _ = x_prompt.shape
    ns = x_sample.shape[0]
    nchunk = seq // S5_CHUNK
    l = 0

    w = w_in[l]
    i_gate = QKVZ_W
    i_u = QKVZ_W + 2 * GDN_HEADS
    wm = w[:, :QKVZ_W].astype(BF16)
    wgate = w[:, i_gate:i_u]
    wg = jnp.pad(wgate, ((0, 0), (0, GATE_LANES - 2 * GDN_HEADS))).astype(BF16)
    wgt = jnp.pad(wgate.T, ((0, GATE_ROWS - 2 * GDN_HEADS), (0, 0))).astype(BF16)
    wu = w[:, i_u:].astype(BF16)
    pad_h = jnp.zeros((GDN_HEADS,), F32)
    alog8 = jnp.concatenate([pad_h, gdn_a_log[l]])
    dtb8 = jnp.concatenate([pad_h, gdn_dt_bias[l]])
    alog_row = jnp.pad(alog8, (0, GATE_LANES - 8)).reshape(1, GATE_LANES)
    dtb_row = jnp.pad(dtb8, (0, GATE_LANES - 8)).reshape(1, GATE_LANES)
    alog_col = jnp.pad(alog8, (0, GATE_ROWS - 8)).reshape(GATE_ROWS, 1)
    dtb_col = jnp.pad(dtb8, (0, GATE_ROWS - 8)).reshape(GATE_ROWS, 1)
    normw = gdn_norm_w[l].reshape(1, GDN_HEAD_DIM)
    wconv = w_conv[l]
    row = lambda a: a.reshape(1, -1)
    post_w = (row(s5_d[l]), w_glu[l].astype(BF16), row(b_glu[l]), w_out[l].astype(BF16), row(ln1_g[l]), row(ln1_b[l]),
              w_ff1[l].astype(BF16), w_ff2[l].astype(BF16), row(ln2_g[l]), row(ln2_b[l]))

    xp = x_prompt.reshape(bsz * seq, D_MODEL)
    xs = x_sample.reshape(ns, D_MODEL)
    qkvz_p, gates_p, gates_t_p, u_p = _proj(xp, wm, wg, wgt, wu, tile=512)
    qkvz_s, gates_s, _, u_s = _proj(xs, wm, wg, wgt, wu, tile=ns)

    o_p, gdn_p = _gdn_wide(qkvz_p.reshape(bsz, seq, QKVZ_W), gates_p.reshape(bsz, seq, GATE_LANES), wconv,
                           alog_row, dtb_row, normw)
    o_p = o_p.reshape(bsz * seq, GDN_WIDTH)
    conv_p = qkvz_p.reshape(bsz, seq, QKVZ_W)[:, seq - (CONV_W - 1):, :QKV_W]

    conv_t = jnp.transpose(state_conv[l], (1, 0, 2))
    newconv_t, ops_s, sc = _sample_pre(qkvz_s, gates_s, conv_t, wconv, alog_row, dtb_row)
    gdn_s, o_s = _sample_state(state_gdn[l], ops_s, sc, qkvz_s[:, QKV_W:], normw)
    conv_s = jnp.transpose(newconv_t, (1, 0, 2))

    us_t = u_s.reshape(ns, S5_GROUPS, S5_GROUP).transpose(1, 0, 2)
    h0r_t = state_ssm_re[l].transpose(1, 0, 2)
    h0i_t = state_ssm_im[l].transpose(1, 0, 2)
    mt, gt, gts, wf, coef, ys_t, hsr, hsi = _s5_params(us_t, h0r_t, h0i_t, s5_a_re[l], s5_a_im[l], s5_log_dt[l],
                                                       s5_b_re[l], s5_b_im[l], s5_c_re[l], s5_c_im[l])
    coef_rows = coef.transpose(1, 0, 2).reshape(8, S5_GROUPS * LANE_TILE)
    ys_p, h_last = _s5_seq(u_p, mt, gt, gts, wf, coef_rows, bsz, seq)
    h_last = h_last.reshape(bsz, S5_GROUPS, 2 * S5_STATE)
    ys_s = ys_t.transpose(1, 0, 2).reshape(ns, S5_WIDTH)

    y_p = _post(xp, o_p, ys_p, u_p, *post_w, tile=512)
    y_s = _post(xs, o_s, ys_s, u_s, *post_w, tile=ns)

    t3 = lambda a: a.transpose(1, 0, 2)[None]
    return (y_p.reshape(bsz, seq, D_MODEL), y_s.reshape(ns, 1, D_MODEL),
            gdn_p[None], conv_p[None], h_last[None, :, :, :S5_STATE], h_last[None, :, :, S5_STATE:],
            gdn_s[None], conv_s[None], t3(hsr), t3(hsi))
```

```python
import functools

import jax
import jax.numpy as jnp
from jax import lax
from jax.experimental import pallas as pl
from jax.experimental.pallas import tpu as pltpu

F32 = jnp.float32
BF16 = jnp.bfloat16

D_MODEL = 1024
GDN_HEADS = 4
GDN_HEAD_DIM = 128
GDN_WIDTH = GDN_HEADS * GDN_HEAD_DIM
CONV_W = 4
GDN_CHUNK = 64
S5_WIDTH = D_MODEL - GDN_WIDTH
S5_GROUP = 16
S5_GROUPS = S5_WIDTH // S5_GROUP
S5_STATE = 64
D_FF = 4 * D_MODEL
DEPTH = 1
DN_ALPHA = (2.0 * DEPTH) ** 0.25
NORM_EPS = 1e-6
QKV_W = 3 * GDN_WIDTH
QKVZ_W = QKV_W + GDN_WIDTH
GATE_LANES = 128
GATE_ROWS = 16
S5_CHUNK = 16
S5_FLAT = S5_CHUNK * S5_GROUP
GDN_STEP = 4 * GDN_CHUNK
VMEM_LIMIT = 56 * 1024 * 1024


def _dot(a, b):
    return jnp.dot(a.astype(BF16), b.astype(BF16), preferred_element_type=F32)


def _dot_nt(a, b):
    return lax.dot_general(a.astype(BF16), b.astype(BF16), (((1,), (1,)), ((), ())), preferred_element_type=F32)


def _dot_tn(a, b):
    return lax.dot_general(a.astype(BF16), b.astype(BF16), (((0,), (0,)), ((), ())), preferred_element_type=F32)


def _split(x):
    hi = x.astype(BF16)
    lo = (x - hi.astype(F32)).astype(BF16)
    return hi, lo


def _dot_hi(a, b, dims=(((1,), (0,)), ((), ()))):
    ah, al = _split(a)
    bh, bl = _split(b)
    d = functools.partial(lax.dot_general, dimension_numbers=dims, preferred_element_type=F32)
    return d(ah, bh) + (d(al, bh) + d(ah, bl))


def _dot_x3(a, b_exact):
    a1 = a.astype(BF16)
    r1 = a - a1.astype(F32)
    a2 = r1.astype(BF16)
    a3 = (r1 - a2.astype(F32)).astype(BF16)
    return _dot(a1, b_exact) + (_dot(a2, b_exact) + _dot(a3, b_exact))


def _dot_x3_left(a_exact, b):
    b1 = b.astype(BF16)
    r1 = b - b1.astype(F32)
    b2 = r1.astype(BF16)
    b3 = (r1 - b2.astype(F32)).astype(BF16)
    return _dot(a_exact, b1) + (_dot(a_exact, b2) + _dot(a_exact, b3))


def _complex_powers(base_r, base_i, exponent, nbits):
    shape = exponent.shape
    pr = jnp.ones(shape, F32)
    pi = jnp.zeros(shape, F32)
    br, bi = base_r, base_i
    for j in range(nbits):
        bit = ((exponent >> j) & 1) == 1
        pr, pi = jnp.where(bit, pr * br - pi * bi, pr), jnp.where(bit, pr * bi + pi * br, pi)
        br, bi = br * br - bi * bi, 2.0 * br * bi
    return pr, pi


def _silu(x):
    return x * jax.nn.sigmoid(x)


def _layernorm(x, g, b):
    mu = jnp.mean(x, axis=-1, keepdims=True)
    xc = x - mu
    var = jnp.mean(xc * xc, axis=-1, keepdims=True)
    return xc * lax.rsqrt(var + NORM_EPS) * g + b


def _l2norm(a):
    return a * lax.rsqrt(jnp.sum(a * a, axis=-1, keepdims=True) + NORM_EPS)


def _compiler_params(semantics):
    return pltpu.CompilerParams(dimension_semantics=semantics, vmem_limit_bytes=VMEM_LIMIT)


def _whole(shape):
    n = len(shape)
    return pl.BlockSpec(shape, lambda *_: (0,) * n)


PROJ_W = QKVZ_W + S5_WIDTH + GATE_LANES


def _proj_kernel(x_ref, w_ref, qkvz_ref, gates_ref, u_ref):
    r = jnp.dot(x_ref[...].astype(BF16), w_ref[...], preferred_element_type=F32)
    qkvz_ref[...] = r[:, 0:QKVZ_W]
    u_ref[...] = r[:, QKVZ_W:QKVZ_W + S5_WIDTH]
    gates_ref[...] = r[:, QKVZ_W + S5_WIDTH:]


def _proj(x, w, tile):
    n = x.shape[0]
    return pl.pallas_call(
        _proj_kernel,
        grid=(n // tile,),
        in_specs=[pl.BlockSpec((tile, D_MODEL), lambda i: (i, 0)),
                  pl.BlockSpec(w.shape, lambda i: (0, 0), pipeline_mode=pl.Buffered(1))],
        out_specs=[pl.BlockSpec((tile, QKVZ_W), lambda i: (i, 0)),
                   pl.BlockSpec((tile, GATE_LANES), lambda i: (i, 0)),
                   pl.BlockSpec((tile, S5_WIDTH), lambda i: (i, 0))],
        out_shape=[jax.ShapeDtypeStruct((n, QKVZ_W), F32),
                   jax.ShapeDtypeStruct((n, GATE_LANES), F32),
                   jax.ShapeDtypeStruct((n, S5_WIDTH), F32)],
        compiler_params=_compiler_params(("arbitrary",)),
        name="proj",
    )(x, w)


def _gate_values(logits, a_log, dt_bias):
    beta = jax.nn.sigmoid(logits)
    g = -jnp.exp(a_log) * jax.nn.softplus(logits + dt_bias)
    return beta, g


def _gdn_kernel(qkvz_ref, gates_ref, gates_t_ref, wconv_ref, alog_row_ref, dtb_row_ref, alog_col_ref, dtb_col_ref,
                normw_ref, o_ref, s_ref, xpad_ref):
    step = pl.program_id(1)
    tt = GDN_STEP
    c = GDN_CHUNK
    dk = GDN_HEAD_DIM

    @pl.when(step == 0)
    def _():
        s_ref[...] = jnp.zeros_like(s_ref)
        xpad_ref[0:8, :] = jnp.zeros((8, QKV_W), F32)

    xpad_ref[8:8 + tt, :] = qkvz_ref[:, 0:QKV_W]
    wc = wconv_ref[...]

    beta_c, g_c = _gate_values(gates_ref[...], alog_row_ref[...], dtb_row_ref[...])
    _, g_r = _gate_values(gates_t_ref[...], alog_col_ref[...], dtb_col_ref[...])

    ri = lax.broadcasted_iota(jnp.int32, (tt, tt), 0)
    ci = lax.broadcasted_iota(jnp.int32, (tt, tt), 1)
    same_chunk = (ri // c) == (ci // c)
    tril_bd = jnp.where(same_chunk & (ci <= ri), 1.0, 0.0).astype(BF16)
    triu_bd = jnp.where(same_chunk & (ri <= ci), 1.0, 0.0).astype(BF16)
    gc_c = _dot_x3_left(tril_bd, g_c)
    gc_r = _dot_x3(g_r, triu_bd)

    r64 = lax.broadcasted_iota(jnp.int32, (c, c), 0)
    c64 = lax.broadcasted_iota(jnp.int32, (c, c), 1)
    tri = c64 <= r64
    strict = c64 < r64
    eye = jnp.where(r64 == c64, 1.0, 0.0).astype(F32)
    scale = GDN_HEAD_DIM ** -0.5
    normw = normw_ref[...]

    heads = range(GDN_HEADS)
    nch = tt // c

    def prep(ch):
        r0 = ch * c
        conv = xpad_ref[8 + r0:8 + r0 + c, :] * wc[3:4, :]
        for j in range(1, CONV_W):
            conv = conv + xpad_ref[8 + r0 - j:8 + r0 - j + c, :] * wc[3 - j:4 - j, :]
        qkv = _silu(conv)
        out = []
        for h in heads:
            q = _l2norm(qkv[:, h * dk:(h + 1) * dk]) * scale
            k = _l2norm(qkv[:, GDN_WIDTH + h * dk:GDN_WIDTH + (h + 1) * dk])
            v = qkv[:, 2 * GDN_WIDTH + h * dk:2 * GDN_WIDTH + (h + 1) * dk]
            beta = beta_c[r0:r0 + c, h:h + 1]
            gcol = gc_c[r0:r0 + c, 4 + h:5 + h]
            grow = gc_r[4 + h:5 + h, r0:r0 + c]
            decay = jnp.where(tri, jnp.exp(jnp.where(tri, gcol - grow, 0.0)), 0.0)
            eg = jnp.exp(gcol)
            g_last = gcol[c - 1:c, :]
            kb = k * beta
            a1 = _dot_nt(jnp.concatenate([kb, q], axis=0), k)
            out.append(dict(
                qg=q * eg, kd=k * jnp.exp(g_last - gcol), eg_last=jnp.exp(g_last),
                qk=jnp.where(tri, a1[c:2 * c] * decay, 0.0),
                rhs=jnp.concatenate([v * beta, kb * eg], axis=1),
                neg_l=-jnp.where(strict, a1[0:c] * decay, 0.0)))
        return out

    def solve(pre):
        p = [u["neg_l"] for u in pre]
        tinv = [eye + a for a in p]
        yield
        for _ in range(5):
            p = [_dot_hi(a, a) for a in p]
            tinv = [t + _dot_hi(t, a) for t, a in zip(tinv, p)]
            yield
        for u, t in zip(pre, tinv):
            u["sol"] = _dot_hi(t, u["rhs"])
        yield

    def advance(ch, pre):
        r0 = ch * c
        s_old = [s_ref[0, h] for h in heads]
        m1 = [_dot(jnp.concatenate([u["sol"][:, dk:2 * dk], u["qg"]], axis=0), s) for u, s in zip(pre, s_old)]
        yield
        v_new = [u["sol"][:, 0:dk] - m[0:c] for u, m in zip(pre, m1)]
        for h in heads:
            s_ref[0, h] = s_old[h] * pre[h]["eg_last"] + _dot_tn(pre[h]["kd"], v_new[h])
        yield
        for h in heads:
            o = m1[h][c:2 * c] + _dot(pre[h]["qk"], v_new[h])
            z = qkvz_ref[r0:r0 + c, QKV_W + h * dk:QKV_W + (h + 1) * dk]
            o = o * lax.rsqrt(jnp.mean(o * o, axis=-1, keepdims=True) + NORM_EPS) * normw * _silu(z)
            o_ref[r0:r0 + c, h * dk:(h + 1) * dk] = o
        yield

    def interleave(*gens):
        live = list(gens)
        while live:
            for gen in list(live):
                if next(gen, "done") == "done":
                    live.remove(gen)

    pre = {0: prep(0)}
    interleave(solve(pre[0]))
    for ch in range(nch):
        if ch + 1 < nch:
            pre[ch + 1] = prep(ch + 1)
            interleave(solve(pre[ch + 1]), advance(ch, pre[ch]))
        else:
            interleave(advance(ch, pre[ch]))
        del pre[ch]
    xpad_ref[0:8, :] = xpad_ref[tt:tt + 8, :]


def _gdn_prompt(qkvz, gates, gates_t, wconv, alog_row, dtb_row, alog_col, dtb_col, normw, bsz, seq):
    steps = seq // GDN_STEP
    return pl.pallas_call(
        _gdn_kernel,
        grid=(bsz, steps),
        in_specs=[pl.BlockSpec((GDN_STEP, QKVZ_W), lambda b, i: (b * steps + i, 0)),
                  pl.BlockSpec((GDN_STEP, GATE_LANES), lambda b, i: (b * steps + i, 0)),
                  pl.BlockSpec((GATE_ROWS, GDN_STEP), lambda b, i: (0, b * steps + i)),
                  _whole(wconv.shape), _whole(alog_row.shape), _whole(dtb_row.shape),
                  _whole(alog_col.shape), _whole(dtb_col.shape), _whole(normw.shape)],
        out_specs=[pl.BlockSpec((GDN_STEP, GDN_WIDTH), lambda b, i: (b * steps + i, 0)),
                   pl.BlockSpec((1, GDN_HEADS, GDN_HEAD_DIM, GDN_HEAD_DIM), lambda b, i: (b, 0, 0, 0))],
        out_shape=[jax.ShapeDtypeStruct((bsz * seq, GDN_WIDTH), F32),
                   jax.ShapeDtypeStruct((bsz, GDN_HEADS, GDN_HEAD_DIM, GDN_HEAD_DIM), F32)],
        scratch_shapes=[pltpu.VMEM((8 + GDN_STEP, QKV_W), F32)],
        compiler_params=_compiler_params(("arbitrary", "arbitrary")),
        name="gdn_prompt",
    )(qkvz, gates, gates_t, wconv, alog_row, dtb_row, alog_col, dtb_col, normw)


GDN_SEQS = 8


def _gdn_wide_kernel(qkvz_ref, gates_ref, wconv_ref, alog_row_ref, dtb_row_ref, normw_ref, o_ref, s_ref, xpad_ref):
    step = pl.program_id(1)
    ns = qkvz_ref.shape[0]
    c = GDN_CHUNK
    dk = GDN_HEAD_DIM
    rows = ns * c

    @pl.when(step == 0)
    def _():
        s_ref[...] = jnp.zeros_like(s_ref)
        xpad_ref[:, 0:8, :] = jnp.zeros((ns, 8, QKV_W), F32)

    wc = wconv_ref[...]
    beta_c, g_c = _gate_values(gates_ref[...].reshape(rows, GATE_LANES), alog_row_ref[...], dtb_row_ref[...])

    ri = lax.broadcasted_iota(jnp.int32, (rows, rows), 0)
    ci = lax.broadcasted_iota(jnp.int32, (rows, rows), 1)
    tril_bd = jnp.where(((ri // c) == (ci // c)) & (ci <= ri), 1.0, 0.0).astype(BF16)
    gc_c = _dot_x3_left(tril_bd, g_c)
    gc_t = [gc_c[i * GATE_LANES:(i + 1) * GATE_LANES, :].T for i in range(rows // GATE_LANES)]

    r64 = lax.broadcasted_iota(jnp.int32, (c, c), 0)
    c64 = lax.broadcasted_iota(jnp.int32, (c, c), 1)
    tri = c64 <= r64
    strict = c64 < r64
    eye = jnp.where(r64 == c64, 1.0, 0.0).astype(F32)
    sub_block = {b: (r64 // (2 * b) == c64 // (2 * b)) & ((r64 // b) % 2 == 1) & ((c64 // b) % 2 == 0)
                 for b in (1, 2, 4, 8, 16, 32)}
    scale = GDN_HEAD_DIM ** -0.5
    normw = normw_ref[...]

    def prep(seqs, units):
        for s in seqs:
            xpad_ref[s, 8:8 + c, :] = qkvz_ref[s, :, 0:QKV_W]
            conv = xpad_ref[s, 8:8 + c, :] * wc[3:4, :]
            for j in range(1, CONV_W):
                conv = conv + xpad_ref[s, 8 - j:8 - j + c, :] * wc[3 - j:4 - j, :]
            xpad_ref[s, 0:8, :] = xpad_ref[s, c:c + 8, :]
            qkv = _silu(conv)
            r0 = s * c
            for h in range(GDN_HEADS):
                q = _l2norm(qkv[:, h * dk:(h + 1) * dk]) * scale
                k = _l2norm(qkv[:, GDN_WIDTH + h * dk:GDN_WIDTH + (h + 1) * dk])
                v = qkv[:, 2 * GDN_WIDTH + h * dk:2 * GDN_WIDTH + (h + 1) * dk]
                beta = beta_c[r0:r0 + c, h:h + 1]
                gcol = gc_c[r0:r0 + c, 4 + h:5 + h]
                lane0 = r0 % GATE_LANES
                grow = gc_t[r0 // GATE_LANES][4 + h:5 + h, lane0:lane0 + c]
                decay = jnp.where(tri, jnp.exp(jnp.where(tri, gcol - grow, 0.0)), 0.0)
                eg = jnp.exp(gcol)
                g_last = gcol[c - 1:c, :]
                kb = k * beta
                a1 = _dot_nt(jnp.concatenate([kb, q], axis=0), k)
                units.append(dict(
                    s=s, h=h, qg=q * eg, kd=k * jnp.exp(g_last - gcol), eg_last=jnp.exp(g_last),
                    qk=jnp.where(tri, a1[c:2 * c] * decay, 0.0),
                    rhs=jnp.concatenate([v * beta, kb * eg], axis=1),
                    lmat=jnp.where(strict, a1[0:c] * decay, 0.0)))
                yield

    def solve(units):
        for u in units:
            u["tinv"] = eye - jnp.where(sub_block[1], u["lmat"], 0.0)
        for b in (2, 4, 8, 16, 32):
            for u in units:
                u["y"] = _dot_hi(jnp.where(sub_block[b], u["lmat"], 0.0), u["tinv"])
            yield
            for u in units:
                u["tinv"] = u["tinv"] - _dot_hi(u["tinv"], u["y"])
            yield
        for u in units:
            u["sol"] = _dot_hi(u["tinv"], u["rhs"])
        yield

    def advance(units):
        for u in units:
            u["s_old"] = s_ref[u["s"], u["h"]]
            u["m1"] = _dot(jnp.concatenate([u["sol"][:, dk:2 * dk], u["qg"]], axis=0), u["s_old"])
        yield
        for u in units:
            u["v_new"] = u["sol"][:, 0:dk] - u["m1"][0:c]
            s_ref[u["s"], u["h"]] = u["s_old"] * u["eg_last"] + _dot_tn(u["kd"], u["v_new"])
        yield
        for u in units:
            s, h = u["s"], u["h"]
            o = u["m1"][c:2 * c] + _dot(u["qk"], u["v_new"])
            z = qkvz_ref[s, :, QKV_W + h * dk:QKV_W + (h + 1) * dk]
            o = o * lax.rsqrt(jnp.mean(o * o, axis=-1, keepdims=True) + NORM_EPS) * normw * _silu(z)
            o_ref[s, :, h * dk:(h + 1) * dk] = o
        yield

    def interleave(*gens):
        live = list(gens)
        while live:
            for gen in list(live):
                if next(gen, "done") == "done":
                    live.remove(gen)

    half = max(ns // 2, 1)
    wave_a, wave_b = [], []
    interleave(prep(range(0, half), wave_a))
    interleave(solve(wave_a), prep(range(half, ns), wave_b))
    interleave(solve(wave_b), advance(wave_a))
    interleave(advance(wave_b))


def _gdn_wide(qkvz3, gates3, wconv, alog_row, dtb_row, normw):
    bsz, seq, _ = qkvz3.shape
    ns, c = min(GDN_SEQS, bsz), GDN_CHUNK
    hd = (GDN_HEADS, GDN_HEAD_DIM, GDN_HEAD_DIM)
    return pl.pallas_call(
        _gdn_wide_kernel,
        grid=(bsz // ns, seq // c),
        in_specs=[pl.BlockSpec((ns, c, QKVZ_W), lambda b, i: (b, i, 0)),
                  pl.BlockSpec((ns, c, GATE_LANES), lambda b, i: (b, i, 0)),
                  _whole(wconv.shape), _whole(alog_row.shape), _whole(dtb_row.shape), _whole(normw.shape)],
        out_specs=[pl.BlockSpec((ns, c, GDN_WIDTH), lambda b, i: (b, i, 0)),
                   pl.BlockSpec((ns,) + hd, lambda b, i: (b, 0, 0, 0))],
        out_shape=[jax.ShapeDtypeStruct((bsz, seq, GDN_WIDTH), F32),
                   jax.ShapeDtypeStruct((bsz,) + hd, F32)],
        scratch_shapes=[pltpu.VMEM((ns, 8 + c, QKV_W), F32)],
        compiler_params=_compiler_params(("arbitrary", "arbitrary")),
        name="gdn_prompt",
    )(qkvz3, gates3, wconv, alog_row, dtb_row, normw)


def _gdn_pipe_kernel(qkvz_ref, gates_ref, wconv_ref, alog_row_ref, dtb_row_ref, normw_ref, o_ref, s_ref,
                     xpad_ref, qg_ref, kd_ref, rhs_ref, qk_ref, lm_ref, egl_ref, gz_ref):
    j = pl.program_id(1)
    ns = qkvz_ref.shape[0]
    c = GDN_CHUNK
    dk = GDN_HEAD_DIM
    rows = ns * c
    cur = j % 2
    prev = 1 - cur

    @pl.when(j == 0)
    def _():
        s_ref[...] = jnp.zeros_like(s_ref)
        xpad_ref[:, 0:8, :] = jnp.zeros((ns, 8, QKV_W), F32)
        for ref in (qg_ref, kd_ref, rhs_ref, qk_ref, lm_ref, egl_ref, gz_ref):
            ref[1] = jnp.zeros(ref.shape[1:], F32)

    r64 = lax.broadcasted_iota(jnp.int32, (c, c), 0)
    c64 = lax.broadcasted_iota(jnp.int32, (c, c), 1)
    tri = c64 <= r64
    strict = c64 < r64
    eye = jnp.where(r64 == c64, 1.0, 0.0).astype(F32)
    sub_block = {b: (r64 // (2 * b) == c64 // (2 * b)) & ((r64 // b) % 2 == 1) & ((c64 // b) % 2 == 0)
                 for b in (1, 2, 4, 8, 16, 32)}
    scale = GDN_HEAD_DIM ** -0.5
    idx = lambda s, h: s * GDN_HEADS + h

    def prep():
        wc = wconv_ref[...]
        beta_c, g_c = _gate_values(gates_ref[...].reshape(rows, GATE_LANES), alog_row_ref[...], dtb_row_ref[...])
        ri = lax.broadcasted_iota(jnp.int32, (rows, rows), 0)
        ci = lax.broadcasted_iota(jnp.int32, (rows, rows), 1)
        tril_bd = jnp.where(((ri // c) == (ci // c)) & (ci <= ri), 1.0, 0.0).astype(BF16)
        gc_c = _dot_x3_left(tril_bd, g_c)
        gc_t = [gc_c[i * GATE_LANES:(i + 1) * GATE_LANES, :].T for i in range(rows // GATE_LANES)]
        normw4 = jnp.concatenate([normw_ref[...]] * GDN_HEADS, axis=1)
        yield
        for s in range(ns):
            xpad_ref[s, 8:8 + c, :] = qkvz_ref[s, :, 0:QKV_W]
            conv = xpad_ref[s, 8:8 + c, :] * wc[3:4, :]
            for t in range(1, CONV_W):
                conv = conv + xpad_ref[s, 8 - t:8 - t + c, :] * wc[3 - t:4 - t, :]
            xpad_ref[s, 0:8, :] = xpad_ref[s, c:c + 8, :]
            qkv = _silu(conv)
            gz_ref[cur, s] = _silu(qkvz_ref[s, :, QKV_W:]) * normw4
            r0 = s * c
            for h in range(GDN_HEADS):
                q = _l2norm(qkv[:, h * dk:(h + 1) * dk]) * scale
                k = _l2norm(qkv[:, GDN_WIDTH + h * dk:GDN_WIDTH + (h + 1) * dk])
                v = qkv[:, 2 * GDN_WIDTH + h * dk:2 * GDN_WIDTH + (h + 1) * dk]
                beta = beta_c[r0:r0 + c, h:h + 1]
                gcol = gc_c[r0:r0 + c, 4 + h:5 + h]
                lane0 = r0 % GATE_LANES
                grow = gc_t[r0 // GATE_LANES][4 + h:5 + h, lane0:lane0 + c]
                decay = jnp.where(tri, jnp.exp(jnp.where(tri, gcol - grow, 0.0)), 0.0)
                eg = jnp.exp(gcol)
                g_last = gcol[c - 1:c, :]
                kb = k * beta
                a1 = _dot_nt(jnp.concatenate([kb, q], axis=0), k)
                i = idx(s, h)
                qg_ref[cur, i] = q * eg
                kd_ref[cur, i] = k * jnp.exp(g_last - gcol)
                egl_ref[cur, i] = jnp.broadcast_to(jnp.exp(g_last), egl_ref.shape[2:])
                qk_ref[cur, i] = jnp.where(tri, a1[c:2 * c] * decay, 0.0)
                rhs_ref[cur, i] = jnp.concatenate([v * beta, kb * eg], axis=1)
                lm_ref[cur, i] = jnp.where(strict, a1[0:c] * decay, 0.0)
                if h % 2:
                    yield

    def finish():
        units = [dict(s=s, h=h, i=idx(s, h)) for s in range(ns) for h in range(GDN_HEADS)]
        for u in units:
            u["lmat"] = lm_ref[prev, u["i"]]
            u["tinv"] = eye - jnp.where(sub_block[1], u["lmat"], 0.0)
        yield
        for b in (2, 4, 8, 16, 32):
            for u in units:
                u["y"] = _dot_hi(jnp.where(sub_block[b], u["lmat"], 0.0), u["tinv"])
            yield
            for u in units:
                u["tinv"] = u["tinv"] - _dot_hi(u["tinv"], u["y"])
            yield
        for u in units:
            u["sol"] = _dot_hi(u["tinv"], rhs_ref[prev, u["i"]])
        yield
        for u in units:
            u["s_old"] = s_ref[u["s"], u["h"]]
            lhs = jnp.concatenate([u["sol"][:, dk:2 * dk], qg_ref[prev, u["i"]]], axis=0)
            u["m1"] = _dot(lhs, u["s_old"])
        yield
        for u in units:
            u["v_new"] = u["sol"][:, 0:dk] - u["m1"][0:c]
            s_ref[u["s"], u["h"]] = (u["s_old"] * egl_ref[prev, u["i"]][0:1, 0:1]
                                     + _dot_tn(kd_ref[prev, u["i"]], u["v_new"]))
        yield
        for u in units:
            s, h = u["s"], u["h"]
            o = u["m1"][c:2 * c] + _dot(qk_ref[prev, u["i"]], u["v_new"])
            o = o * lax.rsqrt(jnp.mean(o * o, axis=-1, keepdims=True) + NORM_EPS) * gz_ref[prev, s, :, h * dk:(h + 1) * dk]
            o_ref[s, :, h * dk:(h + 1) * dk] = o
        yield

    live = [finish(), prep()]
    while live:
        for gen in list(live):
            if next(gen, "done") == "done":
                live.remove(gen)


def _gdn_pipe(qkvz3, gates3, wconv, alog_row, dtb_row, normw):
    bsz, seq, _ = qkvz3.shape
    ns, c = min(GDN_SEQS, bsz), GDN_CHUNK
    nchunk = seq // c
    units = ns * GDN_HEADS
    hd = (GDN_HEADS, GDN_HEAD_DIM, GDN_HEAD_DIM)
    load = lambda b, j: (b, jnp.minimum(j, nchunk - 1), 0)
    store = lambda b, j: (b, jnp.maximum(j - 1, 0), 0)
    slots = lambda *shape: pltpu.VMEM((2,) + shape, F32)
    return pl.pallas_call(
        _gdn_pipe_kernel,
        grid=(bsz // ns, nchunk + 1),
        in_specs=[pl.BlockSpec((ns, c, QKVZ_W), load),
                  pl.BlockSpec((ns, c, GATE_LANES), load),
                  _whole(wconv.shape), _whole(alog_row.shape), _whole(dtb_row.shape), _whole(normw.shape)],
        out_specs=[pl.BlockSpec((ns, c, GDN_WIDTH), store),
                   pl.BlockSpec((ns,) + hd, lambda b, j: (b, 0, 0, 0))],
        out_shape=[jax.ShapeDtypeStruct((bsz, seq, GDN_WIDTH), F32),
                   jax.ShapeDtypeStruct((bsz,) + hd, F32)],
        scratch_shapes=[pltpu.VMEM((ns, 8 + c, QKV_W), F32),
                        slots(units, c, GDN_HEAD_DIM), slots(units, c, GDN_HEAD_DIM), slots(units, c, 2 * GDN_HEAD_DIM),
                        slots(units, c, c), slots(units, c, c), slots(units, 8, GDN_HEAD_DIM),
                        slots(ns, c, GDN_WIDTH)],
        compiler_params=_compiler_params(("arbitrary", "arbitrary")),
        name="gdn_prompt",
    )(qkvz3, gates3, wconv, alog_row, dtb_row, normw)


def _s5_params_kernel(us_ref, h0r_ref, h0i_ref, ar_row_ref, ai_row_ref, ar_col_ref, ai_col_ref, ldt_ref,
                      br_ref, bi_ref, cr_ref, ci_ref,
                      mt_ref, gt_ref, gts_ref, wf_ref, coef_ref, ys_ref, hsr_ref, hsi_ref):
    L = S5_CHUNK
    P = S5_STATE
    dt = jnp.exp(ldt_ref[0])
    ar_row = ar_row_ref[0] * dt
    ai_row = ai_row_ref[0] * dt
    ar_col = ar_col_ref[0] * dt
    ai_col = ai_col_ref[0] * dt
    first = lax.broadcasted_iota(jnp.int32, (1, 2 * P), 1) < P

    ea = jnp.exp(ar_col)
    lbr = ea * jnp.cos(ai_col)
    lbi = ea * jnp.sin(ai_col)
    lam_r = ar_col_ref[0]
    lam_i = ai_col_ref[0]
    den = lam_r * lam_r + lam_i * lam_i
    fr = ((lbr - 1.0) * lam_r + lbi * lam_i) / den
    fi = (lbi * lam_r - (lbr - 1.0) * lam_i) / den
    b_r = br_ref[0]
    b_i = bi_ref[0]
    bbr = fr * b_r - fi * b_i
    bbi = fr * b_i + fi * b_r

    rt = lax.broadcasted_iota(jnp.int32, (S5_GROUP, S5_FLAT), 0)
    lt = lax.broadcasted_iota(jnp.int32, (S5_GROUP, S5_FLAT), 1)
    tile_mat = jnp.where(lt % S5_GROUP == rt, 1.0, 0.0).astype(BF16)
    bwr = _dot_x3(bbr, tile_mat)
    bwi = _dot_x3(bbi, tile_mat)

    tau_g = (L - 1) - lax.broadcasted_iota(jnp.int32, (P, S5_FLAT), 1) // S5_GROUP
    pgr, pgi = _complex_powers(lbr, lbi, tau_g, 4)
    gtr = pgr * bwr - pgi * bwi
    gti = pgr * bwi + pgi * bwr
    gt_ref[0] = jnp.concatenate([gtr, gti], axis=0).astype(BF16)
    gts_ref[0] = jnp.concatenate([gti, gtr], axis=0).astype(BF16)

    nt = L + 1
    e1 = jnp.exp(ar_row)
    l1r = e1 * jnp.cos(ai_row)
    l1i = e1 * jnp.sin(ai_row)
    tau_w = lax.broadcasted_iota(jnp.int32, (nt * S5_GROUP, 2 * P), 0) // S5_GROUP
    pwr, pwi = _complex_powers(l1r, l1i, tau_w, 5)
    c_r = jnp.concatenate([cr_ref[0]] * nt, axis=0)
    c_i = jnp.concatenate([ci_ref[0]] * nt, axis=0)
    wall = jnp.where(first, c_r * pwr - c_i * pwi, -(c_r * pwi + c_i * pwr))
    wf_ref[0] = wall[S5_GROUP:].astype(BF16)

    zw = _dot_hi(wall[0:S5_FLAT], jnp.concatenate([bwr, bwi], axis=0))
    s_of_lane = lax.broadcasted_iota(jnp.int32, (S5_FLAT, S5_FLAT), 1) // S5_GROUP
    mt = zw
    for j in range(4):
        sh = S5_GROUP << j
        shifted = jnp.concatenate([jnp.zeros((sh, S5_FLAT), F32), mt[0:S5_FLAT - sh]], axis=0)
        mt = jnp.where(((s_of_lane >> j) & 1) == 1, shifted, mt)
    mt_ref[0] = mt.astype(BF16)

    p16r = pwr[L * S5_GROUP:L * S5_GROUP + 1, :]
    p16i = pwi[L * S5_GROUP:L * S5_GROUP + 1, :]
    coef_b = jnp.where(first, -p16i, p16i)
    srow = lax.broadcasted_iota(jnp.int32, (8, 2 * P), 0)
    coef_ref[0] = jnp.where(srow == 0, p16r, jnp.where(srow == 1, coef_b, jnp.where(srow == 2, -coef_b, 0.0)))

    l1r = l1r[:, 0:P]
    l1i = l1i[:, 0:P]
    us = us_ref[0]
    nt_dims = (((1,), (1,)), ((), ()))
    bur = _dot_hi(us, bbr, nt_dims)
    bui = _dot_hi(us, bbi, nt_dims)
    h0r = h0r_ref[0]
    h0i = h0i_ref[0]
    hsr = l1r * h0r - l1i * h0i + bur
    hsi = l1r * h0i + l1i * h0r + bui
    hsr_ref[0] = hsr
    hsi_ref[0] = hsi
    ys_ref[0] = _dot_nt(hsr, cr_ref[0][:, 0:P]) - _dot_nt(hsi, ci_ref[0][:, 0:P])


def _s5_params(us_t, h0r_t, h0i_t, ar, ai, ldt, b_r, b_i, c_r, c_i):
    g = S5_GROUPS
    ns = us_t.shape[1]
    twice = lambda a: jnp.concatenate([a, a], axis=-1)
    ar_row = twice(ar).reshape(g, 1, 2 * S5_STATE)
    ai_row = twice(ai).reshape(g, 1, 2 * S5_STATE)
    ar_col = ar.reshape(g, S5_STATE, 1)
    ai_col = ai.reshape(g, S5_STATE, 1)
    ldt3 = ldt.reshape(g, 1, 1)

    def blk(shape):
        return pl.BlockSpec((1,) + shape, lambda i: (i,) + (0,) * len(shape))

    return pl.pallas_call(
        _s5_params_kernel,
        grid=(g,),
        in_specs=[blk((ns, S5_GROUP)), blk((ns, S5_STATE)), blk((ns, S5_STATE)),
                  blk((1, 2 * S5_STATE)), blk((1, 2 * S5_STATE)), blk((S5_STATE, 1)), blk((S5_STATE, 1)), blk((1, 1)),
                  blk((S5_STATE, S5_GROUP)), blk((S5_STATE, S5_GROUP)),
                  blk((S5_GROUP, 2 * S5_STATE)), blk((S5_GROUP, 2 * S5_STATE))],
        out_specs=[blk((S5_FLAT, S5_FLAT)), blk((2 * S5_STATE, S5_FLAT)), blk((2 * S5_STATE, S5_FLAT)),
                   blk((S5_FLAT, 2 * S5_STATE)), blk((8, 2 * S5_STATE)),
                   blk((ns, S5_GROUP)), blk((ns, S5_STATE)), blk((ns, S5_STATE))],
        out_shape=[jax.ShapeDtypeStruct((g, S5_FLAT, S5_FLAT), BF16),
                   jax.ShapeDtypeStruct((g, 2 * S5_STATE, S5_FLAT), BF16),
                   jax.ShapeDtypeStruct((g, 2 * S5_STATE, S5_FLAT), BF16),
                   jax.ShapeDtypeStruct((g, S5_FLAT, 2 * S5_STATE), BF16),
                   jax.ShapeDtypeStruct((g, 8, 2 * S5_STATE), F32),
                   jax.ShapeDtypeStruct((g, ns, S5_GROUP), F32),
                   jax.ShapeDtypeStruct((g, ns, S5_STATE), F32),
                   jax.ShapeDtypeStruct((g, ns, S5_STATE), F32)],
        compiler_params=_compiler_params(("arbitrary",)),
        name="s5_params",
    )(us_t, h0r_t, h0i_t, ar_row, ai_row, ar_col, ai_col, ldt3, b_r, b_i, twice(c_r), twice(c_i))


LANE_TILE = 128
PIECES = LANE_TILE // S5_GROUP
RELAYOUT_ROWS = S5_CHUNK * S5_CHUNK


def _piece_transpose(tiles):
    tiles = list(tiles)
    piece = lax.broadcasted_iota(jnp.int32, tiles[0].shape, 1) // S5_GROUP
    d = PIECES // 2
    while d:
        keep_low = (piece & d) == 0
        for k in range(PIECES):
            if k & d:
                continue
            a, b = tiles[k], tiles[k + d]
            tiles[k] = jnp.where(keep_low, a, pltpu.roll(b, S5_GROUP * d, axis=1))
            tiles[k + d] = jnp.where(keep_low, pltpu.roll(a, LANE_TILE - S5_GROUP * d, axis=1), b)
        d //= 2
    return tiles


def _s5_seq_kernel(u_ref, mt_ref, gt_ref, gts_ref, wf_ref, coef_ref, ys_ref, hl_ref,
                   uflat_ref, e_ref, es_ref, hin_ref, yflat_ref, xp_ref, *, nchunk):
    rr = RELAYOUT_ROWS
    ro = lax.broadcasted_iota(jnp.int32, (rr, rr), 0)
    ri = lax.broadcasted_iota(jnp.int32, (rr, rr), 1)
    perm = jnp.where((ro // S5_CHUNK == ri % S5_CHUNK) & (ro % S5_CHUNK == ri // S5_CHUNK), 1.0, 0.0).astype(BF16)

    def gather_in(m, carry):
        rows = pl.ds(pl.multiple_of(m * rr, rr), rr)
        xp = jnp.dot(perm, u_ref[rows, :].astype(BF16), preferred_element_type=F32)
        crow = pl.ds(pl.multiple_of(m * S5_CHUNK, S5_CHUNK), S5_CHUNK)
        for j in range(S5_FLAT // LANE_TILE):
            for cb in range(S5_WIDTH // LANE_TILE):
                tiles = [xp[(PIECES * j + k) * S5_CHUNK:(PIECES * j + k + 1) * S5_CHUNK,
                            cb * LANE_TILE:(cb + 1) * LANE_TILE] for k in range(PIECES)]
                for p, tile in enumerate(_piece_transpose(tiles)):
                    uflat_ref[PIECES * cb + p, crow, j * LANE_TILE:(j + 1) * LANE_TILE] = tile.astype(BF16)
        return carry

    lax.fori_loop(0, nchunk // S5_CHUNK, gather_in, 0, unroll=2)

    for g in range(S5_GROUPS):
        lanes = slice(g * LANE_TILE, (g + 1) * LANE_TILE)
        ug = uflat_ref[g]
        e_ref[:, lanes] = _dot_nt(ug, gt_ref[g])
        es_ref[:, lanes] = _dot_nt(ug, gts_ref[g])

    ca = coef_ref[0:1, :]
    cb = coef_ref[1:2, :]
    cbs = coef_ref[2:3, :]

    def scan(c, carry):
        h, hs = carry
        row = pl.ds(c, 1)
        hin_ref[row, :] = h
        return ca * h + cb * hs + e_ref[row, :], ca * hs + cbs * h + es_ref[row, :]

    zero = jnp.zeros((1, S5_GROUPS * LANE_TILE), F32)
    h_last, _ = lax.fori_loop(0, nchunk, scan, (zero, zero))
    hl_ref[0] = h_last

    for g in range(S5_GROUPS):
        lanes = slice(g * LANE_TILE, (g + 1) * LANE_TILE)
        yflat_ref[g] = _dot_nt(uflat_ref[g], mt_ref[g]) + _dot_nt(hin_ref[:, lanes], wf_ref[g])

    def gather_out(m, carry):
        crow = pl.ds(pl.multiple_of(m * S5_CHUNK, S5_CHUNK), S5_CHUNK)
        by_time = [[None] * (S5_WIDTH // LANE_TILE) for _ in range(S5_CHUNK)]
        for j in range(S5_FLAT // LANE_TILE):
            for cb in range(S5_WIDTH // LANE_TILE):
                tiles = [yflat_ref[PIECES * cb + k, crow, j * LANE_TILE:(j + 1) * LANE_TILE] for k in range(PIECES)]
                for p, tile in enumerate(_piece_transpose(tiles)):
                    by_time[PIECES * j + p][cb] = tile
        z = jnp.concatenate([jnp.concatenate(row, axis=1) for row in by_time], axis=0)
        rows = pl.ds(pl.multiple_of(m * rr, rr), rr)
        ys_ref[rows, :] = _dot_x3_left(perm, z)
        return carry

    lax.fori_loop(0, nchunk // S5_CHUNK, gather_out, 0, unroll=2)


def _s5_seq(u, mt, gt, gts, wf, coef, bsz, seq):
    nchunk = seq // S5_CHUNK
    width = S5_GROUPS * LANE_TILE

    def resident(a):
        return pl.BlockSpec(a.shape, lambda b: (0,) * a.ndim, pipeline_mode=pl.Buffered(1))

    return pl.pallas_call(
        functools.partial(_s5_seq_kernel, nchunk=nchunk),
        grid=(bsz,),
        in_specs=[pl.BlockSpec((seq, S5_WIDTH), lambda b: (b, 0))] + [resident(a) for a in (mt, gt, gts, wf, coef)],
        out_specs=[pl.BlockSpec((seq, S5_WIDTH), lambda b: (b, 0)),
                   pl.BlockSpec((1, 1, width), lambda b: (b, 0, 0))],
        out_shape=[jax.ShapeDtypeStruct((bsz * seq, S5_WIDTH), F32),
                   jax.ShapeDtypeStruct((bsz, 1, width), F32)],
        scratch_shapes=[pltpu.VMEM((S5_GROUPS, nchunk, S5_FLAT), BF16),
                        pltpu.VMEM((nchunk, width), F32), pltpu.VMEM((nchunk, width), F32),
                        pltpu.VMEM((nchunk, width), F32),
                        pltpu.VMEM((S5_GROUPS, nchunk, S5_FLAT), F32),
                        pltpu.VMEM((RELAYOUT_ROWS, S5_WIDTH), F32)],
        compiler_params=_compiler_params(("arbitrary",)),
        name="s5_seq",
    )(u, mt, gt, gts, wf, coef)


def _sample_pre_kernel(qkvz_ref, gates_ref, conv_ref, wconv_ref, alog_row_ref, dtb_row_ref,
                       newconv_ref, ops_ref, sc_ref):
    dk = GDN_HEAD_DIM
    x_new = qkvz_ref[:, 0:QKV_W]
    wc = wconv_ref[...]
    conv = x_new * wc[3:4, :]
    for j in range(CONV_W - 1):
        conv = conv + conv_ref[j] * wc[j:j + 1, :]
    qkv = _silu(conv)
    newconv_ref[0] = conv_ref[1]
    newconv_ref[1] = conv_ref[2]
    newconv_ref[2] = x_new

    beta_c, g_c = _gate_values(gates_ref[...], alog_row_ref[...], dtb_row_ref[...])
    scale = GDN_HEAD_DIM ** -0.5
    lane = lax.broadcasted_iota(jnp.int32, sc_ref.shape, 1)
    sc = jnp.zeros(sc_ref.shape, F32)
    for h in range(GDN_HEADS):
        q = _l2norm(qkv[:, h * dk:(h + 1) * dk]) * scale
        k = _l2norm(qkv[:, GDN_WIDTH + h * dk:GDN_WIDTH + (h + 1) * dk])
        v = qkv[:, 2 * GDN_WIDTH + h * dk:2 * GDN_WIDTH + (h + 1) * dk]
        beta = beta_c[:, h:h + 1]
        eg = jnp.exp(g_c[:, 4 + h:5 + h])
        cols = slice(h * dk, (h + 1) * dk)
        ops_ref[0, :, cols] = (beta * eg) * k
        ops_ref[1, :, cols] = q * eg
        ops_ref[2, :, cols] = k
        ops_ref[3, :, cols] = beta * v
        sc = jnp.where(lane == h, jnp.sum(q * k, axis=-1, keepdims=True), sc)
        sc = jnp.where(lane == 4 + h, eg, sc)
    sc_ref[...] = sc


def _sample_pre(qkvz_s, gates_s, conv_t, wconv, alog_row, dtb_row):
    ns = qkvz_s.shape[0]
    return pl.pallas_call(
        _sample_pre_kernel,
        out_shape=[jax.ShapeDtypeStruct((CONV_W - 1, ns, QKV_W), F32),
                   jax.ShapeDtypeStruct((4, ns, GDN_WIDTH), F32),
                   jax.ShapeDtypeStruct((ns, GATE_LANES), F32)],
        compiler_params=pltpu.CompilerParams(vmem_limit_bytes=VMEM_LIMIT),
        name="sample_pre",
    )(qkvz_s, gates_s, conv_t, wconv, alog_row, dtb_row)


def _sample_state_kernel(s_ref, ops_ref, sc_ref, z_ref, normw_ref, snew_ref, o_ref, *, nb):
    dk = GDN_HEAD_DIM
    row = lax.broadcasted_iota(jnp.int32, (8, dk), 0)
    units = [(j, h) for j in range(nb) for h in range(GDN_HEADS)]
    m1 = {}
    for j, h in units:
        cols = slice(h * dk, (h + 1) * dk)
        w = ops_ref[0, j:j + 1, cols]
        qg = ops_ref[1, j:j + 1, cols]
        lhs = jnp.where(row == 0, jnp.broadcast_to(w, (8, dk)), jnp.broadcast_to(qg, (8, dk)))
        m1[j, h] = _dot(lhs, s_ref[j, h])
    for j, h in units:
        cols = slice(h * dk, (h + 1) * dk)
        k = ops_ref[2, j:j + 1, cols]
        u = ops_ref[3, j:j + 1, cols]
        v_new = u - m1[j, h][0:1, :]
        qk = sc_ref[j:j + 1, h:h + 1]
        eg = sc_ref[j:j + 1, 4 + h:5 + h]
        o_ref[j:j + 1, cols] = m1[j, h][1:2, :] + qk * v_new
        k8 = jnp.where(row == 0, jnp.broadcast_to(k, (8, dk)), 0.0)
        snew_ref[j, h] = s_ref[j, h] * eg + _dot_tn(k8, jnp.broadcast_to(v_new, (8, dk)))
    normw = normw_ref[...]
    for h in range(GDN_HEADS):
        o = o_ref[:, h * dk:(h + 1) * dk]
        z = z_ref[:, h * dk:(h + 1) * dk]
        o_ref[:, h * dk:(h + 1) * dk] = (o * lax.rsqrt(jnp.mean(o * o, axis=-1, keepdims=True) + NORM_EPS)
                                         * normw * _silu(z))


def _sample_state(state, ops, sc, z, normw, nb=8):
    ns = state.shape[0]
    hd = (GDN_HEADS, GDN_HEAD_DIM, GDN_HEAD_DIM)
    return pl.pallas_call(
        functools.partial(_sample_state_kernel, nb=nb),
        grid=(ns // nb,),
        in_specs=[pl.BlockSpec((nb,) + hd, lambda i: (i, 0, 0, 0)),
                  pl.BlockSpec((4, nb, GDN_WIDTH), lambda i: (0, i, 0)),
                  pl.BlockSpec((nb, GATE_LANES), lambda i: (i, 0)),
                  pl.BlockSpec((nb, GDN_WIDTH), lambda i: (i, 0)),
                  _whole(normw.shape)],
        out_specs=[pl.BlockSpec((nb,) + hd, lambda i: (i, 0, 0, 0)),
                   pl.BlockSpec((nb, GDN_WIDTH), lambda i: (i, 0))],
        out_shape=[jax.ShapeDtypeStruct(state.shape, F32),
                   jax.ShapeDtypeStruct((ns, GDN_WIDTH), F32)],
        compiler_params=_compiler_params(("arbitrary",)),
        name="sample_state",
    )(state, ops, sc, z, normw)


def _post_kernel(x_ref, o_ref, ys_ref, u_ref, d_ref, wglu_ref, bglu_ref, wout_ref, g1_ref, b1_ref,
                 wff1_ref, wff2_ref, g2_ref, b2_ref, y_ref, *, ff_chunk):
    n = x_ref.shape[0]
    halves = [slice(0, n // 2), slice(n // 2, n)] if n % 16 == 0 else [slice(0, n)]
    st = [dict(rows=r) for r in halves]

    def head(s):
        r = s["rows"]
        ys = jax.nn.gelu(ys_ref[r, :] + d_ref[...] * u_ref[r, :])
        ys = ys * jax.nn.sigmoid(_dot(ys, wglu_ref[...]) + bglu_ref[...])
        yield
        mix = _dot(o_ref[r, :], wout_ref[0:GDN_WIDTH, :]) + _dot(ys, wout_ref[GDN_WIDTH:, :])
        yield
        s["x1"] = _layernorm(DN_ALPHA * x_ref[r, :] + mix, g1_ref[...], b1_ref[...])
        s["x1b"] = s["x1"].astype(BF16)
        yield

    def mlp(s):
        acc = jnp.zeros(s["x1"].shape, F32)
        for f in range(D_FF // ff_chunk):
            hcol = jnp.dot(s["x1b"], wff1_ref[:, f * ff_chunk:(f + 1) * ff_chunk], preferred_element_type=F32)
            hcol = jnp.square(jnp.maximum(hcol, 0.0))
            yield
            acc = acc + _dot(hcol, wff2_ref[f * ff_chunk:(f + 1) * ff_chunk, :])
            yield
        s["acc"] = acc

    def tail(s):
        y_ref[s["rows"], :] = _layernorm(DN_ALPHA * s["x1"] + s["acc"], g2_ref[...], b2_ref[...])
        yield

    def interleave(*gens):
        live = list(gens)
        while live:
            for gen in list(live):
                if next(gen, "done") == "done":
                    live.remove(gen)

    interleave(head(st[0]))
    for i, s in enumerate(st):
        others = [head(st[i + 1])] if i + 1 < len(st) else []
        if i > 0:
            others.append(tail(st[i - 1]))
        interleave(mlp(s), *others)
    interleave(tail(st[-1]))


def _post_call(grid, tok_specs, out_spec, out_shape, args):
    weights = args[4:]

    def resident(a):
        return pl.BlockSpec(a.shape, lambda *_: (0,) * a.ndim, pipeline_mode=pl.Buffered(1))

    return pl.pallas_call(
        functools.partial(_post_kernel, ff_chunk=1024),
        grid=grid,
        in_specs=list(tok_specs) + [resident(a) for a in weights],
        out_specs=out_spec,
        out_shape=out_shape,
        compiler_params=_compiler_params(("arbitrary",) * len(grid)),
        name="post",
    )(*args)


def _post(x, o, ys, u, *weights, tile):
    n = x.shape[0]
    tok = lambda w: pl.BlockSpec((tile, w), lambda i: (i, 0))
    return _post_call((n // tile,), [tok(D_MODEL), tok(GDN_WIDTH), tok(S5_WIDTH), tok(S5_WIDTH)], tok(D_MODEL),
                      jax.ShapeDtypeStruct((n, D_MODEL), F32), (x, o, ys, u) + weights)


def kernel(x_prompt, x_sample, state_gdn, state_conv, state_ssm_re, state_ssm_im, w_in, w_conv, gdn_a_log,
           gdn_dt_bias, gdn_norm_w, s5_a_re, s5_a_im, s5_b_re, s5_b_im, s5_c_re, s5_c_im, s5_d, s5_log_dt,
           w_glu, b_glu, w_out, ln1_g, ln1_b, w_ff1, w_ff2, ln2_g, ln2_b):
    bsz, seq, _ = x_prompt.shape
    ns = x_sample.shape[0]
    nchunk = seq // S5_CHUNK
    l = 0

    w = w_in[l]
    i_u = QKVZ_W + 2 * GDN_HEADS
    w_cat = jnp.concatenate([w[:, :QKVZ_W], w[:, i_u:], w[:, QKVZ_W:i_u],
                             jnp.zeros((D_MODEL, GATE_LANES - 2 * GDN_HEADS), F32)], axis=1).astype(BF16)
    lane_pad = (GDN_HEADS, GATE_LANES - 2 * GDN_HEADS)
    alog_row = jnp.pad(gdn_a_log[l], lane_pad).reshape(1, GATE_LANES)
    dtb_row = jnp.pad(gdn_dt_bias[l], lane_pad).reshape(1, GATE_LANES)
    normw = gdn_norm_w[l].reshape(1, GDN_HEAD_DIM)
    wconv = w_conv[l]
    row = lambda a: a.reshape(1, -1)
    post_w = (row(s5_d[l]), w_glu[l].astype(BF16), row(b_glu[l]), w_out[l].astype(BF16), row(ln1_g[l]), row(ln1_b[l]),
              w_ff1[l].astype(BF16), w_ff2[l].astype(BF16), row(ln2_g[l]), row(ln2_b[l]))

    xp = x_prompt.reshape(bsz * seq, D_MODEL)
    xs = x_sample.reshape(ns, D_MODEL)
    qkvz_p, gates_p, u_p = _proj(xp, w_cat, tile=512)
    qkvz_s, gates_s, u_s = _proj(xs, w_cat, tile=ns)

    o_p, gdn_p = _gdn_wide(qkvz_p.reshape(bsz, seq, QKVZ_W), gates_p.reshape(bsz, seq, GATE_LANES), wconv,
                           alog_row, dtb_row, normw)
    o_p = o_p.reshape(bsz * seq, GDN_WIDTH)
    conv_p = qkvz_p.reshape(bsz, seq, QKVZ_W)[:, seq - (CONV_W - 1):, :QKV_W]

    conv_t = jnp.transpose(state_conv[l], (1, 0, 2))
    newconv_t, ops_s, sc = _sample_pre(qkvz_s, gates_s, conv_t, wconv, alog_row, dtb_row)
    gdn_s, o_s = _sample_state(state_gdn[l], ops_s, sc, qkvz_s[:, QKV_W:], normw)
    conv_s = jnp.transpose(newconv_t, (1, 0, 2))

    us_t = u_s.reshape(ns, S5_GROUPS, S5_GROUP).transpose(1, 0, 2)
    h0r_t = state_ssm_re[l].transpose(1, 0, 2)
    h0i_t = state_ssm_im[l].transpose(1, 0, 2)
    mt, gt, gts, wf, coef, ys_t, hsr, hsi = _s5_params(us_t, h0r_t, h0i_t, s5_a_re[l], s5_a_im[l], s5_log_dt[l],
                                                       s5_b_re[l], s5_b_im[l], s5_c_re[l], s5_c_im[l])
    coef_rows = coef.transpose(1, 0, 2).reshape(8, S5_GROUPS * LANE_TILE)
    ys_p, h_last = _s5_seq(u_p, mt, gt, gts, wf, coef_rows, bsz, seq)
    h_last = h_last.reshape(bsz, S5_GROUPS, 2 * S5_STATE)
    ys_s = ys_t.transpose(1, 0, 2).reshape(ns, S5_WIDTH)

    y_p = _post(xp, o_p, ys_p, u_p, *post_w, tile=512)
    y_s = _post(xs, o_s, ys_s, u_s, *post_w, tile=ns)

    t3 = lambda a: a.transpose(1, 0, 2)[None]
    return (y_p.reshape(bsz, seq, D_MODEL), y_s.reshape(ns, 1, D_MODEL),
            gdn_p[None], conv_p[None], h_last[None, :, :, :S5_STATE], h_last[None, :, :, S5_STATE:],
            gdn_s[None], conv_s[None], t3(hsr), t3(hsi))
```

```python
import functools

import jax
import jax.numpy as jnp
import numpy as np
from jax import lax
from jax.experimental import pallas as pl
from jax.experimental.pallas import tpu as pltpu

F32 = jnp.float32
BF16 = jnp.bfloat16

D_MODEL = 1024
GDN_HEADS = 4
GDN_HEAD_DIM = 128
GDN_WIDTH = GDN_HEADS * GDN_HEAD_DIM
CONV_W = 4
GDN_CHUNK = 64
S5_WIDTH = D_MODEL - GDN_WIDTH
S5_GROUP = 16
S5_GROUPS = S5_WIDTH // S5_GROUP
S5_STATE = 64
D_FF = 4 * D_MODEL
DEPTH = 1
DN_ALPHA = (2.0 * DEPTH) ** 0.25
NORM_EPS = 1e-6
QKV_W = 3 * GDN_WIDTH
QKVZ_W = QKV_W + GDN_WIDTH
GATE_LANES = 128
GATE_ROWS = 16
S5_CHUNK = 16
S5_FLAT = S5_CHUNK * S5_GROUP
GDN_STEP = 4 * GDN_CHUNK
VMEM_LIMIT = 56 * 1024 * 1024


def _dot(a, b):
    return jnp.dot(a.astype(BF16), b.astype(BF16), preferred_element_type=F32)


def _dot_nt(a, b):
    return lax.dot_general(a.astype(BF16), b.astype(BF16), (((1,), (1,)), ((), ())), preferred_element_type=F32)


def _dot_tn(a, b):
    return lax.dot_general(a.astype(BF16), b.astype(BF16), (((0,), (0,)), ((), ())), preferred_element_type=F32)


def _split(x):
    hi = x.astype(BF16)
    lo = (x - hi.astype(F32)).astype(BF16)
    return hi, lo


def _dot_hi(a, b, dims=(((1,), (0,)), ((), ()))):
    ah, al = _split(a)
    bh, bl = _split(b)
    d = functools.partial(lax.dot_general, dimension_numbers=dims, preferred_element_type=F32)
    return d(ah, bh) + (d(al, bh) + d(ah, bl))


def _dot_x3(a, b_exact):
    a1 = a.astype(BF16)
    r1 = a - a1.astype(F32)
    a2 = r1.astype(BF16)
    a3 = (r1 - a2.astype(F32)).astype(BF16)
    return _dot(a1, b_exact) + (_dot(a2, b_exact) + _dot(a3, b_exact))


def _dot_x3_left(a_exact, b):
    b1 = b.astype(BF16)
    r1 = b - b1.astype(F32)
    b2 = r1.astype(BF16)
    b3 = (r1 - b2.astype(F32)).astype(BF16)
    return _dot(a_exact, b1) + (_dot(a_exact, b2) + _dot(a_exact, b3))


def _complex_powers(base_r, base_i, exponent, nbits):
    shape = exponent.shape
    pr = jnp.ones(shape, F32)
    pi = jnp.zeros(shape, F32)
    br, bi = base_r, base_i
    for j in range(nbits):
        bit = ((exponent >> j) & 1) == 1
        pr, pi = jnp.where(bit, pr * br - pi * bi, pr), jnp.where(bit, pr * bi + pi * br, pi)
        br, bi = br * br - bi * bi, 2.0 * br * bi
    return pr, pi


def _silu(x):
    return x * jax.nn.sigmoid(x)


def _layernorm(x, g, b):
    mu = jnp.mean(x, axis=-1, keepdims=True)
    xc = x - mu
    var = jnp.mean(xc * xc, axis=-1, keepdims=True)
    return xc * lax.rsqrt(var + NORM_EPS) * g + b


def _l2norm(a):
    return a * lax.rsqrt(jnp.sum(a * a, axis=-1, keepdims=True) + NORM_EPS)


def _compiler_params(semantics):
    return pltpu.CompilerParams(dimension_semantics=semantics, vmem_limit_bytes=VMEM_LIMIT)


def _whole(shape):
    n = len(shape)
    return pl.BlockSpec(shape, lambda *_: (0,) * n)


PROJ_W = QKVZ_W + S5_WIDTH + GATE_LANES


def _proj_kernel(x_ref, w_ref, qkvz_ref, gates_ref, u_ref):
    r = jnp.dot(x_ref[...].astype(BF16), w_ref[...], preferred_element_type=F32)
    qkvz_ref[...] = r[:, 0:QKVZ_W]
    u_ref[...] = r[:, QKVZ_W:QKVZ_W + S5_WIDTH]
    gates_ref[...] = r[:, QKVZ_W + S5_WIDTH:]


def _proj(x, w, tile):
    n = x.shape[0]
    return pl.pallas_call(
        _proj_kernel,
        grid=(n // tile,),
        in_specs=[pl.BlockSpec((tile, D_MODEL), lambda i: (i, 0)),
                  pl.BlockSpec(w.shape, lambda i: (0, 0), pipeline_mode=pl.Buffered(1))],
        out_specs=[pl.BlockSpec((tile, QKVZ_W), lambda i: (i, 0)),
                   pl.BlockSpec((tile, GATE_LANES), lambda i: (i, 0)),
                   pl.BlockSpec((tile, S5_WIDTH), lambda i: (i, 0))],
        out_shape=[jax.ShapeDtypeStruct((n, QKVZ_W), F32),
                   jax.ShapeDtypeStruct((n, GATE_LANES), F32),
                   jax.ShapeDtypeStruct((n, S5_WIDTH), F32)],
        compiler_params=_compiler_params(("arbitrary",)),
        name="proj",
    )(x, w)


def _gate_values(logits, a_log, dt_bias):
    beta = jax.nn.sigmoid(logits)
    g = -jnp.exp(a_log) * jax.nn.softplus(logits + dt_bias)
    return beta, g


def _gdn_kernel(qkvz_ref, gates_ref, gates_t_ref, wconv_ref, alog_row_ref, dtb_row_ref, alog_col_ref, dtb_col_ref,
                normw_ref, o_ref, s_ref, xpad_ref):
    step = pl.program_id(1)
    tt = GDN_STEP
    c = GDN_CHUNK
    dk = GDN_HEAD_DIM

    @pl.when(step == 0)
    def _():
        s_ref[...] = jnp.zeros_like(s_ref)
        xpad_ref[0:8, :] = jnp.zeros((8, QKV_W), F32)

    xpad_ref[8:8 + tt, :] = qkvz_ref[:, 0:QKV_W]
    wc = wconv_ref[...]

    beta_c, g_c = _gate_values(gates_ref[...], alog_row_ref[...], dtb_row_ref[...])
    _, g_r = _gate_values(gates_t_ref[...], alog_col_ref[...], dtb_col_ref[...])

    ri = lax.broadcasted_iota(jnp.int32, (tt, tt), 0)
    ci = lax.broadcasted_iota(jnp.int32, (tt, tt), 1)
    same_chunk = (ri // c) == (ci // c)
    tril_bd = jnp.where(same_chunk & (ci <= ri), 1.0, 0.0).astype(BF16)
    triu_bd = jnp.where(same_chunk & (ri <= ci), 1.0, 0.0).astype(BF16)
    gc_c = _dot_x3_left(tril_bd, g_c)
    gc_r = _dot_x3(g_r, triu_bd)

    r64 = lax.broadcasted_iota(jnp.int32, (c, c), 0)
    c64 = lax.broadcasted_iota(jnp.int32, (c, c), 1)
    tri = c64 <= r64
    strict = c64 < r64
    eye = jnp.where(r64 == c64, 1.0, 0.0).astype(F32)
    scale = GDN_HEAD_DIM ** -0.5
    normw = normw_ref[...]

    heads = range(GDN_HEADS)
    nch = tt // c

    def prep(ch):
        r0 = ch * c
        conv = xpad_ref[8 + r0:8 + r0 + c, :] * wc[3:4, :]
        for j in range(1, CONV_W):
            conv = conv + xpad_ref[8 + r0 - j:8 + r0 - j + c, :] * wc[3 - j:4 - j, :]
        qkv = _silu(conv)
        out = []
        for h in heads:
            q = _l2norm(qkv[:, h * dk:(h + 1) * dk]) * scale
            k = _l2norm(qkv[:, GDN_WIDTH + h * dk:GDN_WIDTH + (h + 1) * dk])
            v = qkv[:, 2 * GDN_WIDTH + h * dk:2 * GDN_WIDTH + (h + 1) * dk]
            beta = beta_c[r0:r0 + c, h:h + 1]
            gcol = gc_c[r0:r0 + c, 4 + h:5 + h]
            grow = gc_r[4 + h:5 + h, r0:r0 + c]
            decay = jnp.where(tri, jnp.exp(jnp.where(tri, gcol - grow, 0.0)), 0.0)
            eg = jnp.exp(gcol)
            g_last = gcol[c - 1:c, :]
            kb = k * beta
            a1 = _dot_nt(jnp.concatenate([kb, q], axis=0), k)
            out.append(dict(
                qg=q * eg, kd=k * jnp.exp(g_last - gcol), eg_last=jnp.exp(g_last),
                qk=jnp.where(tri, a1[c:2 * c] * decay, 0.0),
                rhs=jnp.concatenate([v * beta, kb * eg], axis=1),
                neg_l=-jnp.where(strict, a1[0:c] * decay, 0.0)))
        return out

    def solve(pre):
        p = [u["neg_l"] for u in pre]
        tinv = [eye + a for a in p]
        yield
        for _ in range(5):
            p = [_dot_hi(a, a) for a in p]
            tinv = [t + _dot_hi(t, a) for t, a in zip(tinv, p)]
            yield
        for u, t in zip(pre, tinv):
            u["sol"] = _dot_hi(t, u["rhs"])
        yield

    def advance(ch, pre):
        r0 = ch * c
        s_old = [s_ref[0, h] for h in heads]
        m1 = [_dot(jnp.concatenate([u["sol"][:, dk:2 * dk], u["qg"]], axis=0), s) for u, s in zip(pre, s_old)]
        yield
        v_new = [u["sol"][:, 0:dk] - m[0:c] for u, m in zip(pre, m1)]
        for h in heads:
            s_ref[0, h] = s_old[h] * pre[h]["eg_last"] + _dot_tn(pre[h]["kd"], v_new[h])
        yield
        for h in heads:
            o = m1[h][c:2 * c] + _dot(pre[h]["qk"], v_new[h])
            z = qkvz_ref[r0:r0 + c, QKV_W + h * dk:QKV_W + (h + 1) * dk]
            o = o * lax.rsqrt(jnp.mean(o * o, axis=-1, keepdims=True) + NORM_EPS) * normw * _silu(z)
            o_ref[r0:r0 + c, h * dk:(h + 1) * dk] = o
        yield

    def interleave(*gens):
        live = list(gens)
        while live:
            for gen in list(live):
                if next(gen, "done") == "done":
                    live.remove(gen)

    pre = {0: prep(0)}
    interleave(solve(pre[0]))
    for ch in range(nch):
        if ch + 1 < nch:
            pre[ch + 1] = prep(ch + 1)
            interleave(solve(pre[ch + 1]), advance(ch, pre[ch]))
        else:
            interleave(advance(ch, pre[ch]))
        del pre[ch]
    xpad_ref[0:8, :] = xpad_ref[tt:tt + 8, :]


def _gdn_prompt(qkvz, gates, gates_t, wconv, alog_row, dtb_row, alog_col, dtb_col, normw, bsz, seq):
    steps = seq // GDN_STEP
    return pl.pallas_call(
        _gdn_kernel,
        grid=(bsz, steps),
        in_specs=[pl.BlockSpec((GDN_STEP, QKVZ_W), lambda b, i: (b * steps + i, 0)),
                  pl.BlockSpec((GDN_STEP, GATE_LANES), lambda b, i: (b * steps + i, 0)),
                  pl.BlockSpec((GATE_ROWS, GDN_STEP), lambda b, i: (0, b * steps + i)),
                  _whole(wconv.shape), _whole(alog_row.shape), _whole(dtb_row.shape),
                  _whole(alog_col.shape), _whole(dtb_col.shape), _whole(normw.shape)],
        out_specs=[pl.BlockSpec((GDN_STEP, GDN_WIDTH), lambda b, i: (b * steps + i, 0)),
                   pl.BlockSpec((1, GDN_HEADS, GDN_HEAD_DIM, GDN_HEAD_DIM), lambda b, i: (b, 0, 0, 0))],
        out_shape=[jax.ShapeDtypeStruct((bsz * seq, GDN_WIDTH), F32),
                   jax.ShapeDtypeStruct((bsz, GDN_HEADS, GDN_HEAD_DIM, GDN_HEAD_DIM), F32)],
        scratch_shapes=[pltpu.VMEM((8 + GDN_STEP, QKV_W), F32)],
        compiler_params=_compiler_params(("arbitrary", "arbitrary")),
        name="gdn_prompt",
    )(qkvz, gates, gates_t, wconv, alog_row, dtb_row, alog_col, dtb_col, normw)


GDN_SEQS = 8


def _gdn_wide_kernel(qkvz_ref, gates_ref, wconv_ref, alog_row_ref, dtb_row_ref, normw_ref, o_ref, s_ref, xpad_ref):
    step = pl.program_id(1)
    ns = qkvz_ref.shape[0]
    c = GDN_CHUNK
    dk = GDN_HEAD_DIM
    rows = ns * c

    @pl.when(step == 0)
    def _():
        s_ref[...] = jnp.zeros_like(s_ref)
        xpad_ref[:, 0:8, :] = jnp.zeros((ns, 8, QKV_W), F32)

    wc = wconv_ref[...]
    beta_c, g_c = _gate_values(gates_ref[...].reshape(rows, GATE_LANES), alog_row_ref[...], dtb_row_ref[...])

    ri = lax.broadcasted_iota(jnp.int32, (rows, rows), 0)
    ci = lax.broadcasted_iota(jnp.int32, (rows, rows), 1)
    tril_bd = jnp.where(((ri // c) == (ci // c)) & (ci <= ri), 1.0, 0.0).astype(BF16)
    gc_c = _dot_x3_left(tril_bd, g_c)
    gc_t = [gc_c[i * GATE_LANES:(i + 1) * GATE_LANES, :].T for i in range(rows // GATE_LANES)]

    r64 = lax.broadcasted_iota(jnp.int32, (c, c), 0)
    c64 = lax.broadcasted_iota(jnp.int32, (c, c), 1)
    tri = c64 <= r64
    strict = c64 < r64
    eye = jnp.where(r64 == c64, 1.0, 0.0).astype(F32)
    sub_block = {b: (r64 // (2 * b) == c64 // (2 * b)) & ((r64 // b) % 2 == 1) & ((c64 // b) % 2 == 0)
                 for b in (1, 2, 4, 8, 16, 32)}
    scale = GDN_HEAD_DIM ** -0.5
    normw = normw_ref[...]

    def prep(seqs, units):
        for s in seqs:
            xpad_ref[s, 8:8 + c, :] = qkvz_ref[s, :, 0:QKV_W]
            conv = xpad_ref[s, 8:8 + c, :] * wc[3:4, :]
            for j in range(1, CONV_W):
                conv = conv + xpad_ref[s, 8 - j:8 - j + c, :] * wc[3 - j:4 - j, :]
            xpad_ref[s, 0:8, :] = xpad_ref[s, c:c + 8, :]
            qkv = _silu(conv)
            r0 = s * c
            for h in range(GDN_HEADS):
                q = _l2norm(qkv[:, h * dk:(h + 1) * dk]) * scale
                k = _l2norm(qkv[:, GDN_WIDTH + h * dk:GDN_WIDTH + (h + 1) * dk])
                v = qkv[:, 2 * GDN_WIDTH + h * dk:2 * GDN_WIDTH + (h + 1) * dk]
                beta = beta_c[r0:r0 + c, h:h + 1]
                gcol = gc_c[r0:r0 + c, 4 + h:5 + h]
                lane0 = r0 % GATE_LANES
                grow = gc_t[r0 // GATE_LANES][4 + h:5 + h, lane0:lane0 + c]
                decay = jnp.where(tri, jnp.exp(jnp.where(tri, gcol - grow, 0.0)), 0.0)
                eg = jnp.exp(gcol)
                g_last = gcol[c - 1:c, :]
                kb = k * beta
                a1 = _dot_nt(jnp.concatenate([kb, q], axis=0), k)
                units.append(dict(
                    s=s, h=h, qg=q * eg, kd=k * jnp.exp(g_last - gcol), eg_last=jnp.exp(g_last),
                    qk=jnp.where(tri, a1[c:2 * c] * decay, 0.0),
                    rhs=jnp.concatenate([v * beta, kb * eg], axis=1),
                    lmat=jnp.where(strict, a1[0:c] * decay, 0.0)))
                yield

    def solve(units):
        for u in units:
            u["tinv"] = eye - jnp.where(sub_block[1], u["lmat"], 0.0)
        for b in (2, 4, 8, 16, 32):
            for u in units:
                u["y"] = _dot_hi(jnp.where(sub_block[b], u["lmat"], 0.0), u["tinv"])
            yield
            for u in units:
                u["tinv"] = u["tinv"] - _dot_hi(u["tinv"], u["y"])
            yield
        for u in units:
            u["sol"] = _dot_hi(u["tinv"], u["rhs"])
        yield

    def advance(units):
        for u in units:
            u["s_old"] = s_ref[u["s"], u["h"]]
            u["m1"] = _dot(jnp.concatenate([u["sol"][:, dk:2 * dk], u["qg"]], axis=0), u["s_old"])
        yield
        for u in units:
            u["v_new"] = u["sol"][:, 0:dk] - u["m1"][0:c]
            s_ref[u["s"], u["h"]] = u["s_old"] * u["eg_last"] + _dot_tn(u["kd"], u["v_new"])
        yield
        for u in units:
            s, h = u["s"], u["h"]
            o = u["m1"][c:2 * c] + _dot(u["qk"], u["v_new"])
            z = qkvz_ref[s, :, QKV_W + h * dk:QKV_W + (h + 1) * dk]
            o = o * lax.rsqrt(jnp.mean(o * o, axis=-1, keepdims=True) + NORM_EPS) * normw * _silu(z)
            o_ref[s, :, h * dk:(h + 1) * dk] = o
        yield

    def interleave(*gens):
        live = list(gens)
        while live:
            for gen in list(live):
                if next(gen, "done") == "done":
                    live.remove(gen)

    half = max(ns // 2, 1)
    wave_a, wave_b = [], []
    interleave(prep(range(0, half), wave_a))
    interleave(solve(wave_a), prep(range(half, ns), wave_b))
    interleave(solve(wave_b), advance(wave_a))
    interleave(advance(wave_b))


def _gdn_wide(qkvz3, gates3, wconv, alog_row, dtb_row, normw):
    bsz, seq, _ = qkvz3.shape
    ns, c = min(GDN_SEQS, bsz), GDN_CHUNK
    hd = (GDN_HEADS, GDN_HEAD_DIM, GDN_HEAD_DIM)
    return pl.pallas_call(
        _gdn_wide_kernel,
        grid=(bsz // ns, seq // c),
        in_specs=[pl.BlockSpec((ns, c, QKVZ_W), lambda b, i: (b, i, 0)),
                  pl.BlockSpec((ns, c, GATE_LANES), lambda b, i: (b, i, 0)),
                  _whole(wconv.shape), _whole(alog_row.shape), _whole(dtb_row.shape), _whole(normw.shape)],
        out_specs=[pl.BlockSpec((ns, c, GDN_WIDTH), lambda b, i: (b, i, 0)),
                   pl.BlockSpec((ns,) + hd, lambda b, i: (b, 0, 0, 0))],
        out_shape=[jax.ShapeDtypeStruct((bsz, seq, GDN_WIDTH), F32),
                   jax.ShapeDtypeStruct((bsz,) + hd, F32)],
        scratch_shapes=[pltpu.VMEM((ns, 8 + c, QKV_W), F32)],
        compiler_params=_compiler_params(("arbitrary", "arbitrary")),
        name="gdn_prompt",
    )(qkvz3, gates3, wconv, alog_row, dtb_row, normw)


GDN_SUB_BLOCKS = (1, 2, 4, 8, 16, 32)
MASK_TRI, MASK_STRICT, MASK_EYE = len(GDN_SUB_BLOCKS), len(GDN_SUB_BLOCKS) + 1, len(GDN_SUB_BLOCKS) + 2


def _gdn_masks():
    c = GDN_CHUNK
    r = np.arange(c)[:, None]
    col = np.arange(2 * c)[None, :] % c
    sub = [(r // (2 * b) == col // (2 * b)) & ((r // b) % 2 == 1) & ((col // b) % 2 == 0) for b in GDN_SUB_BLOCKS]
    masks = np.stack(sub + [col <= r, col < r, col == r]).astype(np.float32)
    lane = np.broadcast_to(np.arange(2 * c)[None, :], (c, 2 * c))
    halves = np.stack([lane < c, lane >= c]).astype(np.float32)
    return jnp.asarray(masks), jnp.asarray(halves, dtype=BF16)


def _gdn_pair_kernel(qkvz_ref, gates_ref, wconv_ref, alog_row_ref, dtb_row_ref, normw_ref, masks_ref, halves_ref,
                     o_ref, s_ref, xpad_ref):
    step = pl.program_id(1)
    ns = qkvz_ref.shape[0]
    c = GDN_CHUNK
    dk = GDN_HEAD_DIM
    rows = ns * c

    @pl.when(step == 0)
    def _():
        s_ref[...] = jnp.zeros_like(s_ref)
        xpad_ref[:, 0:8, :] = jnp.zeros((ns, 8, QKV_W), F32)

    wc = wconv_ref[...]
    beta_c, g_c = _gate_values(gates_ref[...].reshape(rows, GATE_LANES), alog_row_ref[...], dtb_row_ref[...])

    ri = lax.broadcasted_iota(jnp.int32, (rows, rows), 0)
    ci = lax.broadcasted_iota(jnp.int32, (rows, rows), 1)
    tril_bd = jnp.where(((ri // c) == (ci // c)) & (ci <= ri), 1.0, 0.0).astype(BF16)
    gc_c = _dot_x3_left(tril_bd, g_c)
    gc_t = [gc_c[i * GATE_LANES:(i + 1) * GATE_LANES, :].T for i in range(rows // GATE_LANES)]
    gc_t_rolled = [pltpu.roll(t, c, axis=1) for t in gc_t]

    mask = lambda i: masks_ref[i] > 0.5
    low_half = lax.broadcasted_iota(jnp.int32, (1, 2 * c), 1) < c
    scale = GDN_HEAD_DIM ** -0.5
    normw = normw_ref[...]
    zeros_k = jnp.zeros((c, dk), F32)

    def split(x):
        hi = x.astype(BF16)
        return hi, (x - hi.astype(F32)).astype(BF16)

    def block_diag(x):
        return jnp.concatenate([x * halves_ref[0], x * halves_ref[1]], axis=0)

    def pair_product(ph, pl_, yh, yl):
        bdh = block_diag(yh)
        first = jnp.dot(jnp.concatenate([ph, pl_], axis=1), jnp.concatenate([bdh, bdh], axis=0),
                        preferred_element_type=F32)
        return first + jnp.dot(ph, block_diag(yl), preferred_element_type=F32)

    def prep(seqs, pairs):
        for s in seqs:
            xpad_ref[s, 8:8 + c, :] = qkvz_ref[s, :, 0:QKV_W]
            xp = xpad_ref[s]
            acc = xp * wc[0:1, :]
            for j in range(1, CONV_W):
                acc = pltpu.roll(acc, 1, axis=0) + xp * wc[j:j + 1, :]
            xpad_ref[s, 0:8, :] = xp[c:c + 8, :]
            qkv = _silu(acc[8:8 + c, :])
            r0 = s * c
            lane0 = r0 % GATE_LANES
            blk = r0 // GATE_LANES
            in_place, moved = (gc_t[blk], gc_t_rolled[blk])
            for h0 in range(0, GDN_HEADS, 2):
                heads, a1 = [], None
                for side, h in enumerate((h0, h0 + 1)):
                    q = _l2norm(qkv[:, h * dk:(h + 1) * dk]) * scale
                    k = _l2norm(qkv[:, GDN_WIDTH + h * dk:GDN_WIDTH + (h + 1) * dk])
                    v = qkv[:, 2 * GDN_WIDTH + h * dk:2 * GDN_WIDTH + (h + 1) * dk]
                    beta = beta_c[r0:r0 + c, h:h + 1]
                    gcol = gc_c[r0:r0 + c, 4 + h:5 + h]
                    eg = jnp.exp(gcol)
                    g_last = gcol[c - 1:c, :]
                    kb = k * beta
                    k_pad = jnp.concatenate([k, zeros_k] if side == 0 else [zeros_k, k], axis=0)
                    part = _dot_nt(jnp.concatenate([kb, q], axis=0), k_pad)
                    a1 = part if a1 is None else a1 + part
                    src = in_place if (lane0 == 0) == (side == 0) else moved
                    heads.append(dict(s=s, h=h, gcol=gcol, grow=src[4 + h:5 + h, :],
                                      qg=q * eg, kd=k * jnp.exp(g_last - gcol), eg_last=jnp.exp(g_last),
                                      rhs=jnp.concatenate([v * beta, kb * eg], axis=1)))
                ha, hb = heads
                diff = jnp.where(low_half, ha["gcol"], hb["gcol"]) - jnp.where(low_half, ha["grow"], hb["grow"])
                decay = jnp.where(mask(MASK_TRI), jnp.exp(jnp.where(mask(MASK_TRI), diff, 0.0)), 0.0)
                pairs.append(dict(heads=heads,
                                  qk=jnp.where(mask(MASK_TRI), a1[c:2 * c] * decay, 0.0),
                                  lmat=jnp.where(mask(MASK_STRICT), a1[0:c] * decay, 0.0)))
                yield

    def solve(pairs):
        for p in pairs:
            p["d"] = masks_ref[MASK_EYE] - p["lmat"] * masks_ref[0]
        for lvl in range(1, len(GDN_SUB_BLOCKS)):
            for p in pairs:
                p["dh"], p["dl"] = split(p["d"])
                oh, ol = split(p["lmat"] * masks_ref[lvl])
                p["y"] = pair_product(oh, ol, p["dh"], p["dl"])
            yield
            for p in pairs:
                yh, yl = split(p["y"])
                p["d"] = p["d"] - pair_product(p["dh"], p["dl"], yh, yl)
            yield
        for p in pairs:
            dh, dl = split(p["d"])
            (ra_h, ra_l), (rb_h, rb_l) = (split(hd["rhs"]) for hd in p["heads"])
            zr = jnp.zeros((c, 2 * dk), BF16)
            bd_h = jnp.concatenate([jnp.concatenate([ra_h, zr], axis=1), jnp.concatenate([zr, rb_h], axis=1)], axis=0)
            bd_l = jnp.concatenate([jnp.concatenate([ra_l, zr], axis=1), jnp.concatenate([zr, rb_l], axis=1)], axis=0)
            sol = (jnp.dot(jnp.concatenate([dh, dl], axis=1), jnp.concatenate([bd_h, bd_h], axis=0),
                           preferred_element_type=F32)
                   + jnp.dot(dh, bd_l, preferred_element_type=F32))
            for side, hd in enumerate(p["heads"]):
                hd["sol"] = sol[:, side * 2 * dk:(side + 1) * 2 * dk]
        yield

    def advance(pairs):
        for p in pairs:
            for hd in p["heads"]:
                hd["s_old"] = s_ref[hd["s"], hd["h"]]
                hd["m1"] = _dot(jnp.concatenate([hd["sol"][:, dk:2 * dk], hd["qg"]], axis=0), hd["s_old"])
        yield
        for p in pairs:
            for hd in p["heads"]:
                hd["v_new"] = hd["sol"][:, 0:dk] - hd["m1"][0:c]
                s_ref[hd["s"], hd["h"]] = hd["s_old"] * hd["eg_last"] + _dot_tn(hd["kd"], hd["v_new"])
        yield
        zv = jnp.zeros((c, dk), F32)
        for p in pairs:
            ha, hb = p["heads"]
            v_bd = jnp.concatenate([jnp.concatenate([ha["v_new"], zv], axis=1),
                                    jnp.concatenate([zv, hb["v_new"]], axis=1)], axis=0)
            o_pair = _dot(p["qk"], v_bd)
            for side, hd in enumerate(p["heads"]):
                s, h = hd["s"], hd["h"]
                o = hd["m1"][c:2 * c] + o_pair[:, side * dk:(side + 1) * dk]
                z = qkvz_ref[s, :, QKV_W + h * dk:QKV_W + (h + 1) * dk]
                o = o * lax.rsqrt(jnp.mean(o * o, axis=-1, keepdims=True) + NORM_EPS) * normw * _silu(z)
                o_ref[s, :, h * dk:(h + 1) * dk] = o
        yield

    def interleave(*gens):
        live = list(gens)
        while live:
            for gen in list(live):
                if next(gen, "done") == "done":
                    live.remove(gen)

    half = max(ns // 2, 1)
    wave_a, wave_b = [], []
    interleave(prep(range(0, half), wave_a))
    interleave(solve(wave_a), prep(range(half, ns), wave_b))
    interleave(solve(wave_b), advance(wave_a))
    interleave(advance(wave_b))


def _gdn_pair(qkvz3, gates3, wconv, alog_row, dtb_row, normw):
    bsz, seq, _ = qkvz3.shape
    ns, c = min(GDN_SEQS, bsz), GDN_CHUNK
    hd = (GDN_HEADS, GDN_HEAD_DIM, GDN_HEAD_DIM)
    masks, halves = _gdn_masks()
    return pl.pallas_call(
        _gdn_pair_kernel,
        grid=(bsz // ns, seq // c),
        in_specs=[pl.BlockSpec((ns, c, QKVZ_W), lambda b, i: (b, i, 0)),
                  pl.BlockSpec((ns, c, GATE_LANES), lambda b, i: (b, i, 0)),
                  _whole(wconv.shape), _whole(alog_row.shape), _whole(dtb_row.shape), _whole(normw.shape),
                  _whole(masks.shape), _whole(halves.shape)],
        out_specs=[pl.BlockSpec((ns, c, GDN_WIDTH), lambda b, i: (b, i, 0)),
                   pl.BlockSpec((ns,) + hd, lambda b, i: (b, 0, 0, 0))],
        out_shape=[jax.ShapeDtypeStruct((bsz, seq, GDN_WIDTH), F32),
                   jax.ShapeDtypeStruct((bsz,) + hd, F32)],
        scratch_shapes=[pltpu.VMEM((ns, 8 + c, QKV_W), F32)],
        compiler_params=_compiler_params(("arbitrary", "arbitrary")),
        name="gdn_prompt",
    )(qkvz3, gates3, wconv, alog_row, dtb_row, normw, masks, halves)


def _gdn_pipe_kernel(qkvz_ref, gates_ref, wconv_ref, alog_row_ref, dtb_row_ref, normw_ref, o_ref, s_ref,
                     xpad_ref, qg_ref, kd_ref, rhs_ref, qk_ref, lm_ref, egl_ref, gz_ref):
    j = pl.program_id(1)
    ns = qkvz_ref.shape[0]
    c = GDN_CHUNK
    dk = GDN_HEAD_DIM
    rows = ns * c
    cur = j % 2
    prev = 1 - cur

    @pl.when(j == 0)
    def _():
        s_ref[...] = jnp.zeros_like(s_ref)
        xpad_ref[:, 0:8, :] = jnp.zeros((ns, 8, QKV_W), F32)
        for ref in (qg_ref, kd_ref, rhs_ref, qk_ref, lm_ref, egl_ref, gz_ref):
            ref[1] = jnp.zeros(ref.shape[1:], F32)

    r64 = lax.broadcasted_iota(jnp.int32, (c, c), 0)
    c64 = lax.broadcasted_iota(jnp.int32, (c, c), 1)
    tri = c64 <= r64
    strict = c64 < r64
    eye = jnp.where(r64 == c64, 1.0, 0.0).astype(F32)
    sub_block = {b: (r64 // (2 * b) == c64 // (2 * b)) & ((r64 // b) % 2 == 1) & ((c64 // b) % 2 == 0)
                 for b in (1, 2, 4, 8, 16, 32)}
    scale = GDN_HEAD_DIM ** -0.5
    idx = lambda s, h: s * GDN_HEADS + h

    def prep():
        wc = wconv_ref[...]
        beta_c, g_c = _gate_values(gates_ref[...].reshape(rows, GATE_LANES), alog_row_ref[...], dtb_row_ref[...])
        ri = lax.broadcasted_iota(jnp.int32, (rows, rows), 0)
        ci = lax.broadcasted_iota(jnp.int32, (rows, rows), 1)
        tril_bd = jnp.where(((ri // c) == (ci // c)) & (ci <= ri), 1.0, 0.0).astype(BF16)
        gc_c = _dot_x3_left(tril_bd, g_c)
        gc_t = [gc_c[i * GATE_LANES:(i + 1) * GATE_LANES, :].T for i in range(rows // GATE_LANES)]
        normw4 = jnp.concatenate([normw_ref[...]] * GDN_HEADS, axis=1)
        yield
        for s in range(ns):
            xpad_ref[s, 8:8 + c, :] = qkvz_ref[s, :, 0:QKV_W]
            conv = xpad_ref[s, 8:8 + c, :] * wc[3:4, :]
            for t in range(1, CONV_W):
                conv = conv + xpad_ref[s, 8 - t:8 - t + c, :] * wc[3 - t:4 - t, :]
            xpad_ref[s, 0:8, :] = xpad_ref[s, c:c + 8, :]
            qkv = _silu(conv)
            gz_ref[cur, s] = _silu(qkvz_ref[s, :, QKV_W:]) * normw4
            r0 = s * c
            for h in range(GDN_HEADS):
                q = _l2norm(qkv[:, h * dk:(h + 1) * dk]) * scale
                k = _l2norm(qkv[:, GDN_WIDTH + h * dk:GDN_WIDTH + (h + 1) * dk])
                v = qkv[:, 2 * GDN_WIDTH + h * dk:2 * GDN_WIDTH + (h + 1) * dk]
                beta = beta_c[r0:r0 + c, h:h + 1]
                gcol = gc_c[r0:r0 + c, 4 + h:5 + h]
                lane0 = r0 % GATE_LANES
                grow = gc_t[r0 // GATE_LANES][4 + h:5 + h, lane0:lane0 + c]
                decay = jnp.where(tri, jnp.exp(jnp.where(tri, gcol - grow, 0.0)), 0.0)
                eg = jnp.exp(gcol)
                g_last = gcol[c - 1:c, :]
                kb = k * beta
                a1 = _dot_nt(jnp.concatenate([kb, q], axis=0), k)
                i = idx(s, h)
                qg_ref[cur, i] = q * eg
                kd_ref[cur, i] = k * jnp.exp(g_last - gcol)
                egl_ref[cur, i] = jnp.broadcast_to(jnp.exp(g_last), egl_ref.shape[2:])
                qk_ref[cur, i] = jnp.where(tri, a1[c:2 * c] * decay, 0.0)
                rhs_ref[cur, i] = jnp.concatenate([v * beta, kb * eg], axis=1)
                lm_ref[cur, i] = jnp.where(strict, a1[0:c] * decay, 0.0)
                if h % 2:
                    yield

    def finish():
        units = [dict(s=s, h=h, i=idx(s, h)) for s in range(ns) for h in range(GDN_HEADS)]
        for u in units:
            u["lmat"] = lm_ref[prev, u["i"]]
            u["tinv"] = eye - jnp.where(sub_block[1], u["lmat"], 0.0)
        yield
        for b in (2, 4, 8, 16, 32):
            for u in units:
                u["y"] = _dot_hi(jnp.where(sub_block[b], u["lmat"], 0.0), u["tinv"])
            yield
            for u in units:
                u["tinv"] = u["tinv"] - _dot_hi(u["tinv"], u["y"])
            yield
        for u in units:
            u["sol"] = _dot_hi(u["tinv"], rhs_ref[prev, u["i"]])
        yield
        for u in units:
            u["s_old"] = s_ref[u["s"], u["h"]]
            lhs = jnp.concatenate([u["sol"][:, dk:2 * dk], qg_ref[prev, u["i"]]], axis=0)
            u["m1"] = _dot(lhs, u["s_old"])
        yield
        for u in units:
            u["v_new"] = u["sol"][:, 0:dk] - u["m1"][0:c]
            s_ref[u["s"], u["h"]] = (u["s_old"] * egl_ref[prev, u["i"]][0:1, 0:1]
                                     + _dot_tn(kd_ref[prev, u["i"]], u["v_new"]))
        yield
        for u in units:
            s, h = u["s"], u["h"]
            o = u["m1"][c:2 * c] + _dot(qk_ref[prev, u["i"]], u["v_new"])
            o = o * lax.rsqrt(jnp.mean(o * o, axis=-1, keepdims=True) + NORM_EPS) * gz_ref[prev, s, :, h * dk:(h + 1) * dk]
            o_ref[s, :, h * dk:(h + 1) * dk] = o
        yield

    live = [finish(), prep()]
    while live:
        for gen in list(live):
            if next(gen, "done") == "done":
                live.remove(gen)


def _gdn_pipe(qkvz3, gates3, wconv, alog_row, dtb_row, normw):
    bsz, seq, _ = qkvz3.shape
    ns, c = min(GDN_SEQS, bsz), GDN_CHUNK
    nchunk = seq // c
    units = ns * GDN_HEADS
    hd = (GDN_HEADS, GDN_HEAD_DIM, GDN_HEAD_DIM)
    load = lambda b, j: (b, jnp.minimum(j, nchunk - 1), 0)
    store = lambda b, j: (b, jnp.maximum(j - 1, 0), 0)
    slots = lambda *shape: pltpu.VMEM((2,) + shape, F32)
    return pl.pallas_call(
        _gdn_pipe_kernel,
        grid=(bsz // ns, nchunk + 1),
        in_specs=[pl.BlockSpec((ns, c, QKVZ_W), load),
                  pl.BlockSpec((ns, c, GATE_LANES), load),
                  _whole(wconv.shape), _whole(alog_row.shape), _whole(dtb_row.shape), _whole(normw.shape)],
        out_specs=[pl.BlockSpec((ns, c, GDN_WIDTH), store),
                   pl.BlockSpec((ns,) + hd, lambda b, j: (b, 0, 0, 0))],
        out_shape=[jax.ShapeDtypeStruct((bsz, seq, GDN_WIDTH), F32),
                   jax.ShapeDtypeStruct((bsz,) + hd, F32)],
        scratch_shapes=[pltpu.VMEM((ns, 8 + c, QKV_W), F32),
                        slots(units, c, GDN_HEAD_DIM), slots(units, c, GDN_HEAD_DIM), slots(units, c, 2 * GDN_HEAD_DIM),
                        slots(units, c, c), slots(units, c, c), slots(units, 8, GDN_HEAD_DIM),
                        slots(ns, c, GDN_WIDTH)],
        compiler_params=_compiler_params(("arbitrary", "arbitrary")),
        name="gdn_prompt",
    )(qkvz3, gates3, wconv, alog_row, dtb_row, normw)


def _s5_params_kernel(us_ref, h0r_ref, h0i_ref, ar_row_ref, ai_row_ref, ar_col_ref, ai_col_ref, ldt_ref,
                      br_ref, bi_ref, cr_ref, ci_ref,
                      mt_ref, gt_ref, gts_ref, wf_ref, coef_ref, ys_ref, hsr_ref, hsi_ref):
    L = S5_CHUNK
    P = S5_STATE
    dt = jnp.exp(ldt_ref[0])
    ar_row = ar_row_ref[0] * dt
    ai_row = ai_row_ref[0] * dt
    ar_col = ar_col_ref[0] * dt
    ai_col = ai_col_ref[0] * dt
    first = lax.broadcasted_iota(jnp.int32, (1, 2 * P), 1) < P

    ea = jnp.exp(ar_col)
    lbr = ea * jnp.cos(ai_col)
    lbi = ea * jnp.sin(ai_col)
    lam_r = ar_col_ref[0]
    lam_i = ai_col_ref[0]
    den = lam_r * lam_r + lam_i * lam_i
    fr = ((lbr - 1.0) * lam_r + lbi * lam_i) / den
    fi = (lbi * lam_r - (lbr - 1.0) * lam_i) / den
    b_r = br_ref[0]
    b_i = bi_ref[0]
    bbr = fr * b_r - fi * b_i
    bbi = fr * b_i + fi * b_r

    rt = lax.broadcasted_iota(jnp.int32, (S5_GROUP, S5_FLAT), 0)
    lt = lax.broadcasted_iota(jnp.int32, (S5_GROUP, S5_FLAT), 1)
    tile_mat = jnp.where(lt % S5_GROUP == rt, 1.0, 0.0).astype(BF16)
    bwr = _dot_x3(bbr, tile_mat)
    bwi = _dot_x3(bbi, tile_mat)

    tau_g = (L - 1) - lax.broadcasted_iota(jnp.int32, (P, S5_FLAT), 1) // S5_GROUP
    pgr, pgi = _complex_powers(lbr, lbi, tau_g, 4)
    gtr = pgr * bwr - pgi * bwi
    gti = pgr * bwi + pgi * bwr
    gt_ref[0] = jnp.concatenate([gtr, gti], axis=0).astype(BF16)
    gts_ref[0] = jnp.concatenate([gti, gtr], axis=0).astype(BF16)

    nt = L + 1
    e1 = jnp.exp(ar_row)
    l1r = e1 * jnp.cos(ai_row)
    l1i = e1 * jnp.sin(ai_row)
    tau_w = lax.broadcasted_iota(jnp.int32, (nt * S5_GROUP, 2 * P), 0) // S5_GROUP
    pwr, pwi = _complex_powers(l1r, l1i, tau_w, 5)
    c_r = jnp.concatenate([cr_ref[0]] * nt, axis=0)
    c_i = jnp.concatenate([ci_ref[0]] * nt, axis=0)
    wall = jnp.where(first, c_r * pwr - c_i * pwi, -(c_r * pwi + c_i * pwr))
    wf_ref[0] = wall[S5_GROUP:].astype(BF16)

    zw = _dot_hi(wall[0:S5_FLAT], jnp.concatenate([bwr, bwi], axis=0))
    s_of_lane = lax.broadcasted_iota(jnp.int32, (S5_FLAT, S5_FLAT), 1) // S5_GROUP
    mt = zw
    for j in range(4):
        sh = S5_GROUP << j
        shifted = jnp.concatenate([jnp.zeros((sh, S5_FLAT), F32), mt[0:S5_FLAT - sh]], axis=0)
        mt = jnp.where(((s_of_lane >> j) & 1) == 1, shifted, mt)
    mt_ref[0] = mt.astype(BF16)

    p16r = pwr[L * S5_GROUP:L * S5_GROUP + 1, :]
    p16i = pwi[L * S5_GROUP:L * S5_GROUP + 1, :]
    coef_b = jnp.where(first, -p16i, p16i)
    srow = lax.broadcasted_iota(jnp.int32, (8, 2 * P), 0)
    coef_ref[0] = jnp.where(srow == 0, p16r, jnp.where(srow == 1, coef_b, jnp.where(srow == 2, -coef_b, 0.0)))

    l1r = l1r[:, 0:P]
    l1i = l1i[:, 0:P]
    us = us_ref[0]
    nt_dims = (((1,), (1,)), ((), ()))
    bur = _dot_hi(us, bbr, nt_dims)
    bui = _dot_hi(us, bbi, nt_dims)
    h0r = h0r_ref[0]
    h0i = h0i_ref[0]
    hsr = l1r * h0r - l1i * h0i + bur
    hsi = l1r * h0i + l1i * h0r + bui
    hsr_ref[0] = hsr
    hsi_ref[0] = hsi
    ys_ref[0] = _dot_nt(hsr, cr_ref[0][:, 0:P]) - _dot_nt(hsi, ci_ref[0][:, 0:P])


def _s5_params(us_t, h0r_t, h0i_t, ar, ai, ldt, b_r, b_i, c_r, c_i):
    g = S5_GROUPS
    ns = us_t.shape[1]
    twice = lambda a: jnp.concatenate([a, a], axis=-1)
    ar_row = twice(ar).reshape(g, 1, 2 * S5_STATE)
    ai_row = twice(ai).reshape(g, 1, 2 * S5_STATE)
    ar_col = ar.reshape(g, S5_STATE, 1)
    ai_col = ai.reshape(g, S5_STATE, 1)
    ldt3 = ldt.reshape(g, 1, 1)

    def blk(shape):
        return pl.BlockSpec((1,) + shape, lambda i: (i,) + (0,) * len(shape))

    return pl.pallas_call(
        _s5_params_kernel,
        grid=(g,),
        in_specs=[blk((ns, S5_GROUP)), blk((ns, S5_STATE)), blk((ns, S5_STATE)),
                  blk((1, 2 * S5_STATE)), blk((1, 2 * S5_STATE)), blk((S5_STATE, 1)), blk((S5_STATE, 1)), blk((1, 1)),
                  blk((S5_STATE, S5_GROUP)), blk((S5_STATE, S5_GROUP)),
                  blk((S5_GROUP, 2 * S5_STATE)), blk((S5_GROUP, 2 * S5_STATE))],
        out_specs=[blk((S5_FLAT, S5_FLAT)), blk((2 * S5_STATE, S5_FLAT)), blk((2 * S5_STATE, S5_FLAT)),
                   blk((S5_FLAT, 2 * S5_STATE)), blk((8, 2 * S5_STATE)),
                   blk((ns, S5_GROUP)), blk((ns, S5_STATE)), blk((ns, S5_STATE))],
        out_shape=[jax.ShapeDtypeStruct((g, S5_FLAT, S5_FLAT), BF16),
                   jax.ShapeDtypeStruct((g, 2 * S5_STATE, S5_FLAT), BF16),
                   jax.ShapeDtypeStruct((g, 2 * S5_STATE, S5_FLAT), BF16),
                   jax.ShapeDtypeStruct((g, S5_FLAT, 2 * S5_STATE), BF16),
                   jax.ShapeDtypeStruct((g, 8, 2 * S5_STATE), F32),
                   jax.ShapeDtypeStruct((g, ns, S5_GROUP), F32),
                   jax.ShapeDtypeStruct((g, ns, S5_STATE), F32),
                   jax.ShapeDtypeStruct((g, ns, S5_STATE), F32)],
        compiler_params=_compiler_params(("arbitrary",)),
        name="s5_params",
    )(us_t, h0r_t, h0i_t, ar_row, ai_row, ar_col, ai_col, ldt3, b_r, b_i, twice(c_r), twice(c_i))


LANE_TILE = 128
PIECES = LANE_TILE // S5_GROUP
RELAYOUT_ROWS = S5_CHUNK * S5_CHUNK


def _piece_transpose(tiles):
    tiles = list(tiles)
    piece = lax.broadcasted_iota(jnp.int32, tiles[0].shape, 1) // S5_GROUP
    d = PIECES // 2
    while d:
        keep_low = (piece & d) == 0
        for k in range(PIECES):
            if k & d:
                continue
            a, b = tiles[k], tiles[k + d]
            tiles[k] = jnp.where(keep_low, a, pltpu.roll(b, S5_GROUP * d, axis=1))
            tiles[k + d] = jnp.where(keep_low, pltpu.roll(a, LANE_TILE - S5_GROUP * d, axis=1), b)
        d //= 2
    return tiles


def _s5_seq_kernel(u_ref, mt_ref, gt_ref, gts_ref, wf_ref, coef_ref, ys_ref, hl_ref,
                   uflat_ref, e_ref, es_ref, hin_ref, yflat_ref, xp_ref, *, nchunk):
    rr = RELAYOUT_ROWS
    ro = lax.broadcasted_iota(jnp.int32, (rr, rr), 0)
    ri = lax.broadcasted_iota(jnp.int32, (rr, rr), 1)
    perm = jnp.where((ro // S5_CHUNK == ri % S5_CHUNK) & (ro % S5_CHUNK == ri // S5_CHUNK), 1.0, 0.0).astype(BF16)

    def gather_in(m, carry):
        rows = pl.ds(pl.multiple_of(m * rr, rr), rr)
        xp = jnp.dot(perm, u_ref[rows, :].astype(BF16), preferred_element_type=F32)
        crow = pl.ds(pl.multiple_of(m * S5_CHUNK, S5_CHUNK), S5_CHUNK)
        for j in range(S5_FLAT // LANE_TILE):
            for cb in range(S5_WIDTH // LANE_TILE):
                tiles = [xp[(PIECES * j + k) * S5_CHUNK:(PIECES * j + k + 1) * S5_CHUNK,
                            cb * LANE_TILE:(cb + 1) * LANE_TILE] for k in range(PIECES)]
                for p, tile in enumerate(_piece_transpose(tiles)):
                    uflat_ref[PIECES * cb + p, crow, j * LANE_TILE:(j + 1) * LANE_TILE] = tile.astype(BF16)
        return carry

    lax.fori_loop(0, nchunk // S5_CHUNK, gather_in, 0, unroll=2)

    for g in range(S5_GROUPS):
        lanes = slice(g * LANE_TILE, (g + 1) * LANE_TILE)
        ug = uflat_ref[g]
        e_ref[:, lanes] = _dot_nt(ug, gt_ref[g])
        es_ref[:, lanes] = _dot_nt(ug, gts_ref[g])

    ca = coef_ref[0:1, :]
    cb = coef_ref[1:2, :]
    cbs = coef_ref[2:3, :]

    def scan(c, carry):
        h, hs = carry
        row = pl.ds(c, 1)
        hin_ref[row, :] = h
        return ca * h + cb * hs + e_ref[row, :], ca * hs + cbs * h + es_ref[row, :]

    zero = jnp.zeros((1, S5_GROUPS * LANE_TILE), F32)
    h_last, _ = lax.fori_loop(0, nchunk, scan, (zero, zero))
    hl_ref[0] = h_last

    for g in range(S5_GROUPS):
        lanes = slice(g * LANE_TILE, (g + 1) * LANE_TILE)
        yflat_ref[g] = _dot_nt(uflat_ref[g], mt_ref[g]) + _dot_nt(hin_ref[:, lanes], wf_ref[g])

    def gather_out(m, carry):
        crow = pl.ds(pl.multiple_of(m * S5_CHUNK, S5_CHUNK), S5_CHUNK)
        by_time = [[None] * (S5_WIDTH // LANE_TILE) for _ in range(S5_CHUNK)]
        for j in range(S5_FLAT // LANE_TILE):
            for cb in range(S5_WIDTH // LANE_TILE):
                tiles = [yflat_ref[PIECES * cb + k, crow, j * LANE_TILE:(j + 1) * LANE_TILE] for k in range(PIECES)]
                for p, tile in enumerate(_piece_transpose(tiles)):
                    by_time[PIECES * j + p][cb] = tile
        z = jnp.concatenate([jnp.concatenate(row, axis=1) for row in by_time], axis=0)
        rows = pl.ds(pl.multiple_of(m * rr, rr), rr)
        ys_ref[rows, :] = _dot_x3_left(perm, z)
        return carry

    lax.fori_loop(0, nchunk // S5_CHUNK, gather_out, 0, unroll=2)


def _s5_seq(u, mt, gt, gts, wf, coef, bsz, seq):
    nchunk = seq // S5_CHUNK
    width = S5_GROUPS * LANE_TILE

    def resident(a):
        return pl.BlockSpec(a.shape, lambda b: (0,) * a.ndim, pipeline_mode=pl.Buffered(1))

    return pl.pallas_call(
        functools.partial(_s5_seq_kernel, nchunk=nchunk),
        grid=(bsz,),
        in_specs=[pl.BlockSpec((seq, S5_WIDTH), lambda b: (b, 0))] + [resident(a) for a in (mt, gt, gts, wf, coef)],
        out_specs=[pl.BlockSpec((seq, S5_WIDTH), lambda b: (b, 0)),
                   pl.BlockSpec((1, 1, width), lambda b: (b, 0, 0))],
        out_shape=[jax.ShapeDtypeStruct((bsz * seq, S5_WIDTH), F32),
                   jax.ShapeDtypeStruct((bsz, 1, width), F32)],
        scratch_shapes=[pltpu.VMEM((S5_GROUPS, nchunk, S5_FLAT), BF16),
                        pltpu.VMEM((nchunk, width), F32), pltpu.VMEM((nchunk, width), F32),
                        pltpu.VMEM((nchunk, width), F32),
                        pltpu.VMEM((S5_GROUPS, nchunk, S5_FLAT), F32),
                        pltpu.VMEM((RELAYOUT_ROWS, S5_WIDTH), F32)],
        compiler_params=_compiler_params(("arbitrary",)),
        name="s5_seq",
    )(u, mt, gt, gts, wf, coef)


def _sample_pre_kernel(qkvz_ref, gates_ref, conv_ref, wconv_ref, alog_row_ref, dtb_row_ref,
                       newconv_ref, ops_ref, sc_ref):
    dk = GDN_HEAD_DIM
    x_new = qkvz_ref[:, 0:QKV_W]
    wc = wconv_ref[...]
    conv = x_new * wc[3:4, :]
    for j in range(CONV_W - 1):
        conv = conv + conv_ref[j] * wc[j:j + 1, :]
    qkv = _silu(conv)
    newconv_ref[0] = conv_ref[1]
    newconv_ref[1] = conv_ref[2]
    newconv_ref[2] = x_new

    beta_c, g_c = _gate_values(gates_ref[...], alog_row_ref[...], dtb_row_ref[...])
    scale = GDN_HEAD_DIM ** -0.5
    lane = lax.broadcasted_iota(jnp.int32, sc_ref.shape, 1)
    sc = jnp.zeros(sc_ref.shape, F32)
    for h in range(GDN_HEADS):
        q = _l2norm(qkv[:, h * dk:(h + 1) * dk]) * scale
        k = _l2norm(qkv[:, GDN_WIDTH + h * dk:GDN_WIDTH + (h + 1) * dk])
        v = qkv[:, 2 * GDN_WIDTH + h * dk:2 * GDN_WIDTH + (h + 1) * dk]
        beta = beta_c[:, h:h + 1]
        eg = jnp.exp(g_c[:, 4 + h:5 + h])
        cols = slice(h * dk, (h + 1) * dk)
        ops_ref[0, :, cols] = (beta * eg) * k
        ops_ref[1, :, cols] = q * eg
        ops_ref[2, :, cols] = k
        ops_ref[3, :, cols] = beta * v
        sc = jnp.where(lane == h, jnp.sum(q * k, axis=-1, keepdims=True), sc)
        sc = jnp.where(lane == 4 + h, eg, sc)
    sc_ref[...] = sc


def _sample_pre(qkvz_s, gates_s, conv_t, wconv, alog_row, dtb_row):
    ns = qkvz_s.shape[0]
    return pl.pallas_call(
        _sample_pre_kernel,
        out_shape=[jax.ShapeDtypeStruct((CONV_W - 1, ns, QKV_W), F32),
                   jax.ShapeDtypeStruct((4, ns, GDN_WIDTH), F32),
                   jax.ShapeDtypeStruct((ns, GATE_LANES), F32)],
        compiler_params=pltpu.CompilerParams(vmem_limit_bytes=VMEM_LIMIT),
        name="sample_pre",
    )(qkvz_s, gates_s, conv_t, wconv, alog_row, dtb_row)


def _sample_state_kernel(s_ref, ops_ref, sc_ref, z_ref, normw_ref, snew_ref, o_ref, *, nb):
    dk = GDN_HEAD_DIM
    row = lax.broadcasted_iota(jnp.int32, (8, dk), 0)
    units = [(j, h) for j in range(nb) for h in range(GDN_HEADS)]
    m1 = {}
    for j, h in units:
        cols = slice(h * dk, (h + 1) * dk)
        w = ops_ref[0, j:j + 1, cols]
        qg = ops_ref[1, j:j + 1, cols]
        lhs = jnp.where(row == 0, jnp.broadcast_to(w, (8, dk)), jnp.broadcast_to(qg, (8, dk)))
        m1[j, h] = _dot(lhs, s_ref[j, h])
    for j, h in units:
        cols = slice(h * dk, (h + 1) * dk)
        k = ops_ref[2, j:j + 1, cols]
        u = ops_ref[3, j:j + 1, cols]
        v_new = u - m1[j, h][0:1, :]
        qk = sc_ref[j:j + 1, h:h + 1]
        eg = sc_ref[j:j + 1, 4 + h:5 + h]
        o_ref[j:j + 1, cols] = m1[j, h][1:2, :] + qk * v_new
        k8 = jnp.where(row == 0, jnp.broadcast_to(k, (8, dk)), 0.0)
        snew_ref[j, h] = s_ref[j, h] * eg + _dot_tn(k8, jnp.broadcast_to(v_new, (8, dk)))
    normw = normw_ref[...]
    for h in range(GDN_HEADS):
        o = o_ref[:, h * dk:(h + 1) * dk]
        z = z_ref[:, h * dk:(h + 1) * dk]
        o_ref[:, h * dk:(h + 1) * dk] = (o * lax.rsqrt(jnp.mean(o * o, axis=-1, keepdims=True) + NORM_EPS)
                                         * normw * _silu(z))


def _sample_state(state, ops, sc, z, normw, nb=8):
    ns = state.shape[0]
    hd = (GDN_HEADS, GDN_HEAD_DIM, GDN_HEAD_DIM)
    return pl.pallas_call(
        functools.partial(_sample_state_kernel, nb=nb),
        grid=(ns // nb,),
        in_specs=[pl.BlockSpec((nb,) + hd, lambda i: (i, 0, 0, 0)),
                  pl.BlockSpec((4, nb, GDN_WIDTH), lambda i: (0, i, 0)),
                  pl.BlockSpec((nb, GATE_LANES), lambda i: (i, 0)),
                  pl.BlockSpec((nb, GDN_WIDTH), lambda i: (i, 0)),
                  _whole(normw.shape)],
        out_specs=[pl.BlockSpec((nb,) + hd, lambda i: (i, 0, 0, 0)),
                   pl.BlockSpec((nb, GDN_WIDTH), lambda i: (i, 0))],
        out_shape=[jax.ShapeDtypeStruct(state.shape, F32),
                   jax.ShapeDtypeStruct((ns, GDN_WIDTH), F32)],
        compiler_params=_compiler_params(("arbitrary",)),
        name="sample_state",
    )(state, ops, sc, z, normw)


def _post_kernel(x_ref, o_ref, ys_ref, u_ref, d_ref, wglu_ref, bglu_ref, wout_ref, g1_ref, b1_ref,
                 wff1_ref, wff2_ref, g2_ref, b2_ref, y_ref, *, ff_chunk):
    n = x_ref.shape[0]
    halves = [slice(0, n // 2), slice(n // 2, n)] if n % 16 == 0 else [slice(0, n)]
    st = [dict(rows=r) for r in halves]

    def head(s):
        r = s["rows"]
        ys = jax.nn.gelu(ys_ref[r, :] + d_ref[...] * u_ref[r, :])
        ys = ys * jax.nn.sigmoid(_dot(ys, wglu_ref[...]) + bglu_ref[...])
        yield
        mix = _dot(o_ref[r, :], wout_ref[0:GDN_WIDTH, :]) + _dot(ys, wout_ref[GDN_WIDTH:, :])
        yield
        s["x1"] = _layernorm(DN_ALPHA * x_ref[r, :] + mix, g1_ref[...], b1_ref[...])
        s["x1b"] = s["x1"].astype(BF16)
        yield

    def mlp(s):
        acc = jnp.zeros(s["x1"].shape, F32)
        for f in range(D_FF // ff_chunk):
            hcol = jnp.dot(s["x1b"], wff1_ref[:, f * ff_chunk:(f + 1) * ff_chunk], preferred_element_type=F32)
            hcol = jnp.square(jnp.maximum(hcol, 0.0))
            yield
            acc = acc + _dot(hcol, wff2_ref[f * ff_chunk:(f + 1) * ff_chunk, :])
            yield
        s["acc"] = acc

    def tail(s):
        y_ref[s["rows"], :] = _layernorm(DN_ALPHA * s["x1"] + s["acc"], g2_ref[...], b2_ref[...])
        yield

    def interleave(*gens):
        live = list(gens)
        while live:
            for gen in list(live):
                if next(gen, "done") == "done":
                    live.remove(gen)

    interleave(head(st[0]))
    for i, s in enumerate(st):
        others = [head(st[i + 1])] if i + 1 < len(st) else []
        if i > 0:
            others.append(tail(st[i - 1]))
        interleave(mlp(s), *others)
    interleave(tail(st[-1]))


def _post_call(grid, tok_specs, out_spec, out_shape, args):
    weights = args[4:]

    def resident(a):
        return pl.BlockSpec(a.shape, lambda *_: (0,) * a.ndim, pipeline_mode=pl.Buffered(1))

    return pl.pallas_call(
        functools.partial(_post_kernel, ff_chunk=1024),
        grid=grid,
        in_specs=list(tok_specs) + [resident(a) for a in weights],
        out_specs=out_spec,
        out_shape=out_shape,
        compiler_params=_compiler_params(("arbitrary",) * len(grid)),
        name="post",
    )(*args)


def _post(x, o, ys, u, *weights, tile):
    n = x.shape[0]
    tok = lambda w: pl.BlockSpec((tile, w), lambda i: (i, 0))
    return _post_call((n // tile,), [tok(D_MODEL), tok(GDN_WIDTH), tok(S5_WIDTH), tok(S5_WIDTH)], tok(D_MODEL),
                      jax.ShapeDtypeStruct((n, D_MODEL), F32), (x, o, ys, u) + weights)


def kernel(x_prompt, x_sample, state_gdn, state_conv, state_ssm_re, state_ssm_im, w_in, w_conv, gdn_a_log,
           gdn_dt_bias, gdn_norm_w, s5_a_re, s5_a_im, s5_b_re, s5_b_im, s5_c_re, s5_c_im, s5_d, s5_log_dt,
           w_glu, b_glu, w_out, ln1_g, ln1_b, w_ff1, w_ff2, ln2_g, ln2_b):
    bsz, seq, _ = x_prompt.shape
    ns = x_sample.shape[0]
    nchunk = seq // S5_CHUNK
    l = 0

    w = w_in[l]
    i_u = QKVZ_W + 2 * GDN_HEADS
    w_cat = jnp.concatenate([w[:, :QKVZ_W], w[:, i_u:], w[:, QKVZ_W:i_u],
                             jnp.zeros((D_MODEL, GATE_LANES - 2 * GDN_HEADS), F32)], axis=1).astype(BF16)
    lane_pad = (GDN_HEADS, GATE_LANES - 2 * GDN_HEADS)
    alog_row = jnp.pad(gdn_a_log[l], lane_pad).reshape(1, GATE_LANES)
    dtb_row = jnp.pad(gdn_dt_bias[l], lane_pad).reshape(1, GATE_LANES)
    normw = gdn_norm_w[l].reshape(1, GDN_HEAD_DIM)
    wconv = w_conv[l]
    row = lambda a: a.reshape(1, -1)
    post_w = (row(s5_d[l]), w_glu[l].astype(BF16), row(b_glu[l]), w_out[l].astype(BF16), row(ln1_g[l]), row(ln1_b[l]),
              w_ff1[l].astype(BF16), w_ff2[l].astype(BF16), row(ln2_g[l]), row(ln2_b[l]))

    xp = x_prompt.reshape(bsz * seq, D_MODEL)
    xs = x_sample.reshape(ns, D_MODEL)
    qkvz_p, gates_p, u_p = _proj(xp, w_cat, tile=512)
    qkvz_s, gates_s, u_s = _proj(xs, w_cat, tile=ns)

    o_p, gdn_p = _gdn_pair(qkvz_p.reshape(bsz, seq, QKVZ_W), gates_p.reshape(bsz, seq, GATE_LANES), wconv,
                           alog_row, dtb_row, normw)
    o_p = o_p.reshape(bsz * seq, GDN_WIDTH)
    conv_p = qkvz_p.reshape(bsz, seq, QKVZ_W)[:, seq - (CONV_W - 1):, :QKV_W]

    conv_t = jnp.transpose(state_conv[l], (1, 0, 2))
    newconv_t, ops_s, sc = _sample_pre(qkvz_s, gates_s, conv_t, wconv, alog_row, dtb_row)
    gdn_s, o_s = _sample_state(state_gdn[l], ops_s, sc, qkvz_s[:, QKV_W:], normw)
    conv_s = jnp.transpose(newconv_t, (1, 0, 2))

    us_t = u_s.reshape(ns, S5_GROUPS, S5_GROUP).transpose(1, 0, 2)
    h0r_t = state_ssm_re[l].transpose(1, 0, 2)
    h0i_t = state_ssm_im[l].transpose(1, 0, 2)
    mt, gt, gts, wf, coef, ys_t, hsr, hsi = _s5_params(us_t, h0r_t, h0i_t, s5_a_re[l], s5_a_im[l], s5_log_dt[l],
                                                       s5_b_re[l], s5_b_im[l], s5_c_re[l], s5_c_im[l])
    coef_rows = coef.transpose(1, 0, 2).reshape(8, S5_GROUPS * LANE_TILE)
    ys_p, h_last = _s5_seq(u_p, mt, gt, gts, wf, coef_rows, bsz, seq)
    h_last = h_last.reshape(bsz, S5_GROUPS, 2 * S5_STATE)
    ys_s = ys_t.transpose(1, 0, 2).reshape(ns, S5_WIDTH)

    y_p = _post(xp, o_p, ys_p, u_p, *post_w, tile=512)
    y_s = _post(xs, o_s, ys_s, u_s, *post_w, tile=ns)

    t3 = lambda a: a.transpose(1, 0, 2)[None]
    return (y_p.reshape(bsz, seq, D_MODEL), y_s.reshape(ns, 1, D_MODEL),
            gdn_p[None], conv_p[None], h_last[None, :, :, :S5_STATE], h_last[None, :, :, S5_STATE:],
            gdn_s[None], conv_s[None], t3(hsr), t3(hsi))
```

```python
import functools

import jax
import jax.numpy as jnp
import numpy as np
from jax import lax
from jax.experimental import pallas as pl
from jax.experimental.pallas import tpu as pltpu

F32 = jnp.float32
BF16 = jnp.bfloat16

D_MODEL = 1024
GDN_HEADS = 4
GDN_HEAD_DIM = 128
GDN_WIDTH = GDN_HEADS * GDN_HEAD_DIM
CONV_W = 4
GDN_CHUNK = 64
S5_WIDTH = D_MODEL - GDN_WIDTH
S5_GROUP = 16
S5_GROUPS = S5_WIDTH // S5_GROUP
S5_STATE = 64
D_FF = 4 * D_MODEL
DEPTH = 1
DN_ALPHA = (2.0 * DEPTH) ** 0.25
NORM_EPS = 1e-6
QKV_W = 3 * GDN_WIDTH
QKVZ_W = QKV_W + GDN_WIDTH
GATE_LANES = 128
S5_CHUNK = 16
S5_FLAT = S5_CHUNK * S5_GROUP
VMEM_LIMIT = 56 * 1024 * 1024


def _dot(a, b):
    return jnp.dot(a.astype(BF16), b.astype(BF16), preferred_element_type=F32)


def _dot_nt(a, b):
    return lax.dot_general(a.astype(BF16), b.astype(BF16), (((1,), (1,)), ((), ())), preferred_element_type=F32)


def _dot_tn(a, b):
    return lax.dot_general(a.astype(BF16), b.astype(BF16), (((0,), (0,)), ((), ())), preferred_element_type=F32)


def _split(x):
    hi = x.astype(BF16)
    lo = (x - hi.astype(F32)).astype(BF16)
    return hi, lo


def _dot_hi(a, b, dims=(((1,), (0,)), ((), ()))):
    ah, al = _split(a)
    bh, bl = _split(b)
    d = functools.partial(lax.dot_general, dimension_numbers=dims, preferred_element_type=F32)
    return d(ah, bh) + (d(al, bh) + d(ah, bl))


def _dot_x3(a, b_exact):
    a1 = a.astype(BF16)
    r1 = a - a1.astype(F32)
    a2 = r1.astype(BF16)
    a3 = (r1 - a2.astype(F32)).astype(BF16)
    return _dot(a1, b_exact) + (_dot(a2, b_exact) + _dot(a3, b_exact))


def _dot_x3_left(a_exact, b):
    b1 = b.astype(BF16)
    r1 = b - b1.astype(F32)
    b2 = r1.astype(BF16)
    b3 = (r1 - b2.astype(F32)).astype(BF16)
    return _dot(a_exact, b1) + (_dot(a_exact, b2) + _dot(a_exact, b3))


def _complex_powers(base_r, base_i, exponent, nbits):
    shape = exponent.shape
    pr = jnp.ones(shape, F32)
    pi = jnp.zeros(shape, F32)
    br, bi = base_r, base_i
    for j in range(nbits):
        bit = ((exponent >> j) & 1) == 1
        pr, pi = jnp.where(bit, pr * br - pi * bi, pr), jnp.where(bit, pr * bi + pi * br, pi)
        br, bi = br * br - bi * bi, 2.0 * br * bi
    return pr, pi


def _silu(x):
    return x * jax.nn.sigmoid(x)


def _layernorm(x, g, b):
    mu = jnp.mean(x, axis=-1, keepdims=True)
    xc = x - mu
    var = jnp.mean(xc * xc, axis=-1, keepdims=True)
    return xc * lax.rsqrt(var + NORM_EPS) * g + b


def _l2norm(a):
    return a * lax.rsqrt(jnp.sum(a * a, axis=-1, keepdims=True) + NORM_EPS)


def _compiler_params(semantics):
    return pltpu.CompilerParams(dimension_semantics=semantics, vmem_limit_bytes=VMEM_LIMIT)


def _whole(shape):
    n = len(shape)
    return pl.BlockSpec(shape, lambda *_: (0,) * n)


def _interleave(*gens):
    live = list(gens)
    while live:
        for gen in list(live):
            if next(gen, "done") == "done":
                live.remove(gen)


def _proj_kernel(x_ref, w_ref, qkvz_ref, gates_ref, u_ref):
    r = jnp.dot(x_ref[...].astype(BF16), w_ref[...], preferred_element_type=F32)
    qkvz_ref[...] = r[:, 0:QKVZ_W]
    u_ref[...] = r[:, QKVZ_W:QKVZ_W + S5_WIDTH]
    gates_ref[...] = r[:, QKVZ_W + S5_WIDTH:]


def _proj(x, w, tile):
    n = x.shape[0]
    return pl.pallas_call(
        _proj_kernel,
        grid=(n // tile,),
        in_specs=[pl.BlockSpec((tile, D_MODEL), lambda i: (i, 0)),
                  pl.BlockSpec(w.shape, lambda i: (0, 0), pipeline_mode=pl.Buffered(1))],
        out_specs=[pl.BlockSpec((tile, QKVZ_W), lambda i: (i, 0)),
                   pl.BlockSpec((tile, GATE_LANES), lambda i: (i, 0)),
                   pl.BlockSpec((tile, S5_WIDTH), lambda i: (i, 0))],
        out_shape=[jax.ShapeDtypeStruct((n, QKVZ_W), F32),
                   jax.ShapeDtypeStruct((n, GATE_LANES), F32),
                   jax.ShapeDtypeStruct((n, S5_WIDTH), F32)],
        compiler_params=_compiler_params(("arbitrary",)),
        name="proj",
    )(x, w)


GDN_SEQS = 8
GDN_SUB_BLOCKS = (1, 2, 4, 8, 16, 32)
MASK_TRI, MASK_STRICT, MASK_EYE = len(GDN_SUB_BLOCKS), len(GDN_SUB_BLOCKS) + 1, len(GDN_SUB_BLOCKS) + 2
SEL_LOW, SEL_HIGH, SEL_SUB = 0, 1, 2


def _gate_values(logits, a_log, dt_bias):
    beta = jax.nn.sigmoid(logits)
    g = -jnp.exp(a_log) * jax.nn.softplus(logits + dt_bias)
    return beta, g


def _gdn_masks():
    c = GDN_CHUNK
    r = np.arange(c)[:, None]
    col = np.arange(2 * c)[None, :] % c
    sub = [(r // (2 * b) == col // (2 * b)) & ((r // b) % 2 == 1) & ((col // b) % 2 == 0) for b in GDN_SUB_BLOCKS]
    masks = np.stack(sub + [col <= r, col < r, col == r]).astype(np.float32)
    lane = np.broadcast_to(np.arange(2 * c)[None, :], (c, 2 * c))
    selectors = np.stack([lane < c, lane >= c] + sub).astype(np.float32)
    return jnp.asarray(masks), jnp.asarray(selectors, dtype=BF16)


def _gdn_pair_kernel(qkvz_ref, gates_ref, wconv_ref, alog_row_ref, dtb_row_ref, normw_ref, masks_ref, sel_ref,
                     o_ref, s_ref, xpad_ref):
    step = pl.program_id(1)
    ns = qkvz_ref.shape[0]
    c = GDN_CHUNK
    dk = GDN_HEAD_DIM
    rows = ns * c

    @pl.when(step == 0)
    def _():
        s_ref[...] = jnp.zeros_like(s_ref)
        xpad_ref[:, 0:8, :] = jnp.zeros((ns, 8, QKV_W), F32)

    wc = wconv_ref[...]
    beta_c, g_c = _gate_values(gates_ref[...].reshape(rows, GATE_LANES), alog_row_ref[...], dtb_row_ref[...])

    ri = lax.broadcasted_iota(jnp.int32, (rows, rows), 0)
    ci = lax.broadcasted_iota(jnp.int32, (rows, rows), 1)
    tril_bd = jnp.where(((ri // c) == (ci // c)) & (ci <= ri), 1.0, 0.0).astype(BF16)
    gc_c = _dot_x3_left(tril_bd, g_c)
    gc_t = [gc_c[i * GATE_LANES:(i + 1) * GATE_LANES, :].T for i in range(rows // GATE_LANES)]
    gc_t_rolled = [pltpu.roll(t, c, axis=1) for t in gc_t]

    mask = lambda i: masks_ref[i] > 0.5
    low_half = lax.broadcasted_iota(jnp.int32, (1, 2 * c), 1) < c
    scale = GDN_HEAD_DIM ** -0.5
    normw = normw_ref[...]
    zeros_k = jnp.zeros((c, dk), F32)

    def block_diag(x):
        return jnp.concatenate([x * sel_ref[SEL_LOW], x * sel_ref[SEL_HIGH]], axis=0)

    def pair_product(ph, pl_, yh, yl):
        bdh = block_diag(yh)
        first = jnp.dot(jnp.concatenate([ph, pl_], axis=1), jnp.concatenate([bdh, bdh], axis=0),
                        preferred_element_type=F32)
        return first + jnp.dot(ph, block_diag(yl), preferred_element_type=F32)

    def prep(seqs, pairs):
        for s in seqs:
            xpad_ref[s, 8:8 + c, :] = qkvz_ref[s, :, 0:QKV_W]
            xp = xpad_ref[s]
            acc = xp * wc[0:1, :]
            for j in range(1, CONV_W):
                acc = pltpu.roll(acc, 1, axis=0) + xp * wc[j:j + 1, :]
            xpad_ref[s, 0:8, :] = xp[c:c + 8, :]
            qkv = _silu(acc[8:8 + c, :])
            r0 = s * c
            lane0 = r0 % GATE_LANES
            blk = r0 // GATE_LANES
            in_place, moved = (gc_t[blk], gc_t_rolled[blk])
            for h0 in range(0, GDN_HEADS, 2):
                heads, a1 = [], None
                for side, h in enumerate((h0, h0 + 1)):
                    q = _l2norm(qkv[:, h * dk:(h + 1) * dk]) * scale
                    k = _l2norm(qkv[:, GDN_WIDTH + h * dk:GDN_WIDTH + (h + 1) * dk])
                    v = qkv[:, 2 * GDN_WIDTH + h * dk:2 * GDN_WIDTH + (h + 1) * dk]
                    beta = beta_c[r0:r0 + c, h:h + 1]
                    gcol = gc_c[r0:r0 + c, 4 + h:5 + h]
                    eg = jnp.exp(gcol)
                    g_last = gcol[c - 1:c, :]
                    kb = k * beta
                    k_pad = jnp.concatenate([k, zeros_k] if side == 0 else [zeros_k, k], axis=0)
                    part = _dot_nt(jnp.concatenate([kb, q], axis=0), k_pad)
                    a1 = part if a1 is None else a1 + part
                    src = in_place if (lane0 == 0) == (side == 0) else moved
                    heads.append(dict(s=s, h=h, gcol=gcol, grow=src[4 + h:5 + h, :],
                                      qg=q * eg, kd=k * jnp.exp(g_last - gcol), eg_last=jnp.exp(g_last),
                                      rhs=jnp.concatenate([v * beta, kb * eg], axis=1)))
                ha, hb = heads
                diff = jnp.where(low_half, ha["gcol"], hb["gcol"]) - jnp.where(low_half, ha["grow"], hb["grow"])
                decay = jnp.where(mask(MASK_TRI), jnp.exp(jnp.where(mask(MASK_TRI), diff, 0.0)), 0.0)
                pairs.append(dict(heads=heads,
                                  qk=jnp.where(mask(MASK_TRI), a1[c:2 * c] * decay, 0.0),
                                  lmat=jnp.where(mask(MASK_STRICT), a1[0:c] * decay, 0.0)))
                yield

    def solve(pairs):
        for p in pairs:
            p["d"] = masks_ref[MASK_EYE] - p["lmat"] * masks_ref[0]
            p["lh"], p["ll"] = _split(p["lmat"])
        for lvl in range(1, len(GDN_SUB_BLOCKS)):
            off = sel_ref[SEL_SUB + lvl]
            for p in pairs:
                p["dh"], p["dl"] = _split(p["d"])
                p["y"] = pair_product(p["lh"] * off, p["ll"] * off, p["dh"], p["dl"])
            yield
            for p in pairs:
                yh, yl = _split(p["y"])
                p["d"] = p["d"] - pair_product(p["dh"], p["dl"], yh, yl)
            yield
        for p in pairs:
            dh, dl = _split(p["d"])
            (ra_h, ra_l), (rb_h, rb_l) = (_split(hd["rhs"]) for hd in p["heads"])
            zr = jnp.zeros((c, 2 * dk), BF16)
            bd_h = jnp.concatenate([jnp.concatenate([ra_h, zr], axis=1), jnp.concatenate([zr, rb_h], axis=1)], axis=0)
            bd_l = jnp.concatenate([jnp.concatenate([ra_l, zr], axis=1), jnp.concatenate([zr, rb_l], axis=1)], axis=0)
            sol = (jnp.dot(jnp.concatenate([dh, dl], axis=1), jnp.concatenate([bd_h, bd_h], axis=0),
                           preferred_element_type=F32)
                   + jnp.dot(dh, bd_l, preferred_element_type=F32))
            for side, hd in enumerate(p["heads"]):
                hd["sol"] = sol[:, side * 2 * dk:(side + 1) * 2 * dk]
        yield

    def advance(pairs):
        for p in pairs:
            for hd in p["heads"]:
                hd["s_old"] = s_ref[hd["s"], hd["h"]]
                hd["m1"] = _dot(jnp.concatenate([hd["sol"][:, dk:2 * dk], hd["qg"]], axis=0), hd["s_old"])
        yield
        for p in pairs:
            for hd in p["heads"]:
                hd["v_new"] = hd["sol"][:, 0:dk] - hd["m1"][0:c]
                s_ref[hd["s"], hd["h"]] = hd["s_old"] * hd["eg_last"] + _dot_tn(hd["kd"], hd["v_new"])
        yield
        zv = jnp.zeros((c, dk), F32)
        for p in pairs:
            ha, hb = p["heads"]
            v_bd = jnp.concatenate([jnp.concatenate([ha["v_new"], zv], axis=1),
                                    jnp.concatenate([zv, hb["v_new"]], axis=1)], axis=0)
            o_pair = _dot(p["qk"], v_bd)
            for side, hd in enumerate(p["heads"]):
                s, h = hd["s"], hd["h"]
                o = hd["m1"][c:2 * c] + o_pair[:, side * dk:(side + 1) * dk]
                z = qkvz_ref[s, :, QKV_W + h * dk:QKV_W + (h + 1) * dk]
                o = o * lax.rsqrt(jnp.mean(o * o, axis=-1, keepdims=True) + NORM_EPS) * normw * _silu(z)
                o_ref[s, :, h * dk:(h + 1) * dk] = o
        yield

    half = max(ns // 2, 1)
    wave_a, wave_b = [], []
    _interleave(prep(range(0, half), wave_a))
    _interleave(solve(wave_a), prep(range(half, ns), wave_b))
    _interleave(solve(wave_b), advance(wave_a))
    _interleave(advance(wave_b))


def _gdn_pair(qkvz3, gates3, wconv, alog_row, dtb_row, normw):
    bsz, seq, _ = qkvz3.shape
    ns, c = min(GDN_SEQS, bsz), GDN_CHUNK
    hd = (GDN_HEADS, GDN_HEAD_DIM, GDN_HEAD_DIM)
    masks, selectors = _gdn_masks()
    return pl.pallas_call(
        _gdn_pair_kernel,
        grid=(bsz // ns, seq // c),
        in_specs=[pl.BlockSpec((ns, c, QKVZ_W), lambda b, i: (b, i, 0)),
                  pl.BlockSpec((ns, c, GATE_LANES), lambda b, i: (b, i, 0)),
                  _whole(wconv.shape), _whole(alog_row.shape), _whole(dtb_row.shape), _whole(normw.shape),
                  _whole(masks.shape), _whole(selectors.shape)],
        out_specs=[pl.BlockSpec((ns, c, GDN_WIDTH), lambda b, i: (b, i, 0)),
                   pl.BlockSpec((ns,) + hd, lambda b, i: (b, 0, 0, 0))],
        out_shape=[jax.ShapeDtypeStruct((bsz, seq, GDN_WIDTH), F32),
                   jax.ShapeDtypeStruct((bsz,) + hd, F32)],
        scratch_shapes=[pltpu.VMEM((ns, 8 + c, QKV_W), F32)],
        compiler_params=_compiler_params(("arbitrary", "arbitrary")),
        name="gdn_prompt",
    )(qkvz3, gates3, wconv, alog_row, dtb_row, normw, masks, selectors)


def _s5_params_kernel(us_ref, h0r_ref, h0i_ref, ar_row_ref, ai_row_ref, ar_col_ref, ai_col_ref, ldt_ref,
                      br_ref, bi_ref, cr_ref, ci_ref,
                      mt_ref, gt_ref, gts_ref, wf_ref, coef_ref, ys_ref, hsr_ref, hsi_ref):
    L = S5_CHUNK
    P = S5_STATE
    dt = jnp.exp(ldt_ref[0])
    ar_row = ar_row_ref[0] * dt
    ai_row = ai_row_ref[0] * dt
    ar_col = ar_col_ref[0] * dt
    ai_col = ai_col_ref[0] * dt
    first = lax.broadcasted_iota(jnp.int32, (1, 2 * P), 1) < P

    ea = jnp.exp(ar_col)
    lbr = ea * jnp.cos(ai_col)
    lbi = ea * jnp.sin(ai_col)
    lam_r = ar_col_ref[0]
    lam_i = ai_col_ref[0]
    den = lam_r * lam_r + lam_i * lam_i
    fr = ((lbr - 1.0) * lam_r + lbi * lam_i) / den
    fi = (lbi * lam_r - (lbr - 1.0) * lam_i) / den
    b_r = br_ref[0]
    b_i = bi_ref[0]
    bbr = fr * b_r - fi * b_i
    bbi = fr * b_i + fi * b_r

    rt = lax.broadcasted_iota(jnp.int32, (S5_GROUP, S5_FLAT), 0)
    lt = lax.broadcasted_iota(jnp.int32, (S5_GROUP, S5_FLAT), 1)
    tile_mat = jnp.where(lt % S5_GROUP == rt, 1.0, 0.0).astype(BF16)
    bwr = _dot_x3(bbr, tile_mat)
    bwi = _dot_x3(bbi, tile_mat)

    tau_g = (L - 1) - lax.broadcasted_iota(jnp.int32, (P, S5_FLAT), 1) // S5_GROUP
    pgr, pgi = _complex_powers(lbr, lbi, tau_g, 4)
    gtr = pgr * bwr - pgi * bwi
    gti = pgr * bwi + pgi * bwr
    gt_ref[0] = jnp.concatenate([gtr, gti], axis=0).astype(BF16)
    gts_ref[0] = jnp.concatenate([gti, gtr], axis=0).astype(BF16)

    nt = L + 1
    e1 = jnp.exp(ar_row)
    l1r = e1 * jnp.cos(ai_row)
    l1i = e1 * jnp.sin(ai_row)
    tau_w = lax.broadcasted_iota(jnp.int32, (nt * S5_GROUP, 2 * P), 0) // S5_GROUP
    pwr, pwi = _complex_powers(l1r, l1i, tau_w, 5)
    c_r = jnp.concatenate([cr_ref[0]] * nt, axis=0)
    c_i = jnp.concatenate([ci_ref[0]] * nt, axis=0)
    wall = jnp.where(first, c_r * pwr - c_i * pwi, -(c_r * pwi + c_i * pwr))
    wf_ref[0] = wall[S5_GROUP:].astype(BF16)

    zw = _dot_hi(wall[0:S5_FLAT], jnp.concatenate([bwr, bwi], axis=0))
    s_of_lane = lax.broadcasted_iota(jnp.int32, (S5_FLAT, S5_FLAT), 1) // S5_GROUP
    mt = zw
    for j in range(4):
        sh = S5_GROUP << j
        shifted = jnp.concatenate([jnp.zeros((sh, S5_FLAT), F32), mt[0:S5_FLAT - sh]], axis=0)
        mt = jnp.where(((s_of_lane >> j) & 1) == 1, shifted, mt)
    mt_ref[0] = mt.astype(BF16)

    p16r = pwr[L * S5_GROUP:L * S5_GROUP + 1, :]
    p16i = pwi[L * S5_GROUP:L * S5_GROUP + 1, :]
    coef_b = jnp.where(first, -p16i, p16i)
    srow = lax.broadcasted_iota(jnp.int32, (8, 2 * P), 0)
    coef_ref[0] = jnp.where(srow == 0, p16r, jnp.where(srow == 1, coef_b, jnp.where(srow == 2, -coef_b, 0.0)))

    l1r = l1r[:, 0:P]
    l1i = l1i[:, 0:P]
    grp = pl.ds(pl.program_id(0), 1)
    ns = us_ref.shape[0]
    us = us_ref[:, grp, :].reshape(ns, S5_GROUP)
    nt_dims = (((1,), (1,)), ((), ()))
    bur = _dot_hi(us, bbr, nt_dims)
    bui = _dot_hi(us, bbi, nt_dims)
    h0r = h0r_ref[:, grp, :].reshape(ns, P)
    h0i = h0i_ref[:, grp, :].reshape(ns, P)
    hsr = l1r * h0r - l1i * h0i + bur
    hsi = l1r * h0i + l1i * h0r + bui
    hsr_ref[:, grp, :] = hsr.reshape(ns, 1, P)
    hsi_ref[:, grp, :] = hsi.reshape(ns, 1, P)
    ys = _dot_nt(hsr, cr_ref[0][:, 0:P]) - _dot_nt(hsi, ci_ref[0][:, 0:P])
    ys_ref[:, grp, :] = ys.reshape(ns, 1, S5_GROUP)


def _s5_params(us3, h0r, h0i, ar, ai, ldt, b_r, b_i, c_r, c_i):
    g = S5_GROUPS
    ns = us3.shape[0]
    twice = lambda a: jnp.concatenate([a, a], axis=-1)
    ar_row = twice(ar).reshape(g, 1, 2 * S5_STATE)
    ai_row = twice(ai).reshape(g, 1, 2 * S5_STATE)
    ar_col = ar.reshape(g, S5_STATE, 1)
    ai_col = ai.reshape(g, S5_STATE, 1)
    ldt3 = ldt.reshape(g, 1, 1)

    def blk(shape):
        return pl.BlockSpec((1,) + shape, lambda i: (i,) + (0,) * len(shape))

    return pl.pallas_call(
        _s5_params_kernel,
        grid=(g,),
        in_specs=[_whole(us3.shape), _whole(h0r.shape), _whole(h0i.shape),
                  blk((1, 2 * S5_STATE)), blk((1, 2 * S5_STATE)), blk((S5_STATE, 1)), blk((S5_STATE, 1)), blk((1, 1)),
                  blk((S5_STATE, S5_GROUP)), blk((S5_STATE, S5_GROUP)),
                  blk((S5_GROUP, 2 * S5_STATE)), blk((S5_GROUP, 2 * S5_STATE))],
        out_specs=[blk((S5_FLAT, S5_FLAT)), blk((2 * S5_STATE, S5_FLAT)), blk((2 * S5_STATE, S5_FLAT)),
                   blk((S5_FLAT, 2 * S5_STATE)), blk((8, 2 * S5_STATE)),
                   _whole(us3.shape), _whole(h0r.shape), _whole(h0i.shape)],
        out_shape=[jax.ShapeDtypeStruct((g, S5_FLAT, S5_FLAT), BF16),
                   jax.ShapeDtypeStruct((g, 2 * S5_STATE, S5_FLAT), BF16),
                   jax.ShapeDtypeStruct((g, 2 * S5_STATE, S5_FLAT), BF16),
                   jax.ShapeDtypeStruct((g, S5_FLAT, 2 * S5_STATE), BF16),
                   jax.ShapeDtypeStruct((g, 8, 2 * S5_STATE), F32),
                   jax.ShapeDtypeStruct(us3.shape, F32),
                   jax.ShapeDtypeStruct(h0r.shape, F32),
                   jax.ShapeDtypeStruct(h0i.shape, F32)],
        compiler_params=_compiler_params(("arbitrary",)),
        name="s5_params",
    )(us3, h0r, h0i, ar_row, ai_row, ar_col, ai_col, ldt3, b_r, b_i, twice(c_r), twice(c_i))


LANE_TILE = 128
PIECES = LANE_TILE // S5_GROUP
RELAYOUT_ROWS = S5_CHUNK * S5_CHUNK


def _piece_transpose(tiles):
    tiles = list(tiles)
    piece = lax.broadcasted_iota(jnp.int32, tiles[0].shape, 1) // S5_GROUP
    d = PIECES // 2
    while d:
        keep_low = (piece & d) == 0
        for k in range(PIECES):
            if k & d:
                continue
            a, b = tiles[k], tiles[k + d]
            tiles[k] = jnp.where(keep_low, a, pltpu.roll(b, S5_GROUP * d, axis=1))
            tiles[k + d] = jnp.where(keep_low, pltpu.roll(a, LANE_TILE - S5_GROUP * d, axis=1), b)
        d //= 2
    return tiles


def _s5_seq_kernel(u_ref, mt_ref, gt_ref, gts_ref, wf_ref, coef_ref, ys_ref, hl_ref,
                   uflat_ref, e_ref, es_ref, hin_ref, yflat_ref, *, nchunk):
    rr = RELAYOUT_ROWS
    ro = lax.broadcasted_iota(jnp.int32, (rr, rr), 0)
    ri = lax.broadcasted_iota(jnp.int32, (rr, rr), 1)
    perm = jnp.where((ro // S5_CHUNK == ri % S5_CHUNK) & (ro % S5_CHUNK == ri // S5_CHUNK), 1.0, 0.0).astype(BF16)

    def gather_in(m, carry):
        rows = pl.ds(pl.multiple_of(m * rr, rr), rr)
        xp = jnp.dot(perm, u_ref[rows, :].astype(BF16), preferred_element_type=F32)
        crow = pl.ds(pl.multiple_of(m * S5_CHUNK, S5_CHUNK), S5_CHUNK)
        for j in range(S5_FLAT // LANE_TILE):
            for cb in range(S5_WIDTH // LANE_TILE):
                tiles = [xp[(PIECES * j + k) * S5_CHUNK:(PIECES * j + k + 1) * S5_CHUNK,
                            cb * LANE_TILE:(cb + 1) * LANE_TILE] for k in range(PIECES)]
                for p, tile in enumerate(_piece_transpose(tiles)):
                    uflat_ref[PIECES * cb + p, crow, j * LANE_TILE:(j + 1) * LANE_TILE] = tile.astype(BF16)
        return carry

    lax.fori_loop(0, nchunk // S5_CHUNK, gather_in, 0, unroll=2)

    for g in range(S5_GROUPS):
        lanes = slice(g * LANE_TILE, (g + 1) * LANE_TILE)
        ug = uflat_ref[g]
        e_ref[:, lanes] = _dot_nt(ug, gt_ref[g])
        es_ref[:, lanes] = _dot_nt(ug, gts_ref[g])

    ca = coef_ref[0:1, :]
    cb = coef_ref[1:2, :]
    cbs = coef_ref[2:3, :]

    def scan(c, carry):
        h, hs = carry
        row = pl.ds(c, 1)
        hin_ref[row, :] = h
        return ca * h + cb * hs + e_ref[row, :], ca * hs + cbs * h + es_ref[row, :]

    zero = jnp.zeros((1, S5_GROUPS * LANE_TILE), F32)
    h_last, _ = lax.fori_loop(0, nchunk, scan, (zero, zero))
    hl_ref[0] = h_last

    for g in range(S5_GROUPS):
        lanes = slice(g * LANE_TILE, (g + 1) * LANE_TILE)
        yflat_ref[g] = _dot_nt(uflat_ref[g], mt_ref[g]) + _dot_nt(hin_ref[:, lanes], wf_ref[g])

    def gather_out(m, carry):
        crow = pl.ds(pl.multiple_of(m * S5_CHUNK, S5_CHUNK), S5_CHUNK)
        by_time = [[None] * (S5_WIDTH // LANE_TILE) for _ in range(S5_CHUNK)]
        for j in range(S5_FLAT // LANE_TILE):
            for cb in range(S5_WIDTH // LANE_TILE):
                tiles = [yflat_ref[PIECES * cb + k, crow, j * LANE_TILE:(j + 1) * LANE_TILE] for k in range(PIECES)]
                for p, tile in enumerate(_piece_transpose(tiles)):
                    by_time[PIECES * j + p][cb] = tile
        z = jnp.concatenate([jnp.concatenate(row, axis=1) for row in by_time], axis=0)
        rows = pl.ds(pl.multiple_of(m * rr, rr), rr)
        ys_ref[rows, :] = _dot_x3_left(perm, z)
        return carry

    lax.fori_loop(0, nchunk // S5_CHUNK, gather_out, 0, unroll=2)


def _s5_seq(u, mt, gt, gts, wf, coef, bsz, seq):
    nchunk = seq // S5_CHUNK
    width = S5_GROUPS * LANE_TILE

    def resident(a):
        return pl.BlockSpec(a.shape, lambda b: (0,) * a.ndim, pipeline_mode=pl.Buffered(1))

    return pl.pallas_call(
        functools.partial(_s5_seq_kernel, nchunk=nchunk),
        grid=(bsz,),
        in_specs=[pl.BlockSpec((seq, S5_WIDTH), lambda b: (b, 0))] + [resident(a) for a in (mt, gt, gts, wf, coef)],
        out_specs=[pl.BlockSpec((seq, S5_WIDTH), lambda b: (b, 0)),
                   pl.BlockSpec((1, 1, width), lambda b: (b, 0, 0))],
        out_shape=[jax.ShapeDtypeStruct((bsz * seq, S5_WIDTH), F32),
                   jax.ShapeDtypeStruct((bsz, 1, width), F32)],
        scratch_shapes=[pltpu.VMEM((S5_GROUPS, nchunk, S5_FLAT), BF16),
                        pltpu.VMEM((nchunk, width), F32), pltpu.VMEM((nchunk, width), F32),
                        pltpu.VMEM((nchunk, width), F32),
                        pltpu.VMEM((S5_GROUPS, nchunk, S5_FLAT), F32)],
        compiler_params=_compiler_params(("arbitrary",)),
        name="s5_seq",
    )(u, mt, gt, gts, wf, coef)


def _sample_pre_kernel(qkvz_ref, gates_ref, conv_ref, wconv_ref, alog_row_ref, dtb_row_ref,
                       newconv_ref, ops_ref, sc_ref):
    dk = GDN_HEAD_DIM
    x_new = qkvz_ref[:, 0:QKV_W]
    wc = wconv_ref[...]
    conv = x_new * wc[3:4, :]
    for j in range(CONV_W - 1):
        conv = conv + conv_ref[:, j, :] * wc[j:j + 1, :]
    qkv = _silu(conv)
    newconv_ref[:, 0, :] = conv_ref[:, 1, :]
    newconv_ref[:, 1, :] = conv_ref[:, 2, :]
    newconv_ref[:, 2, :] = x_new

    beta_c, g_c = _gate_values(gates_ref[...], alog_row_ref[...], dtb_row_ref[...])
    scale = GDN_HEAD_DIM ** -0.5
    lane = lax.broadcasted_iota(jnp.int32, sc_ref.shape, 1)
    sc = jnp.zeros(sc_ref.shape, F32)
    for h in range(GDN_HEADS):
        q = _l2norm(qkv[:, h * dk:(h + 1) * dk]) * scale
        k = _l2norm(qkv[:, GDN_WIDTH + h * dk:GDN_WIDTH + (h + 1) * dk])
        v = qkv[:, 2 * GDN_WIDTH + h * dk:2 * GDN_WIDTH + (h + 1) * dk]
        beta = beta_c[:, h:h + 1]
        eg = jnp.exp(g_c[:, 4 + h:5 + h])
        cols = slice(h * dk, (h + 1) * dk)
        ops_ref[0, :, cols] = (beta * eg) * k
        ops_ref[1, :, cols] = q * eg
        ops_ref[2, :, cols] = k
        ops_ref[3, :, cols] = beta * v
        sc = jnp.where(lane == h, jnp.sum(q * k, axis=-1, keepdims=True), sc)
        sc = jnp.where(lane == 4 + h, eg, sc)
    sc_ref[...] = sc


def _sample_pre(qkvz_s, gates_s, conv_state, wconv, alog_row, dtb_row):
    ns = qkvz_s.shape[0]
    return pl.pallas_call(
        _sample_pre_kernel,
        out_shape=[jax.ShapeDtypeStruct((ns, CONV_W - 1, QKV_W), F32),
                   jax.ShapeDtypeStruct((4, ns, GDN_WIDTH), F32),
                   jax.ShapeDtypeStruct((ns, GATE_LANES), F32)],
        compiler_params=pltpu.CompilerParams(vmem_limit_bytes=VMEM_LIMIT),
        name="sample_pre",
    )(qkvz_s, gates_s, conv_state, wconv, alog_row, dtb_row)


def _sample_state_kernel(s_ref, ops_ref, sc_ref, z_ref, normw_ref, snew_ref, o_ref, *, nb):
    dk = GDN_HEAD_DIM
    row = lax.broadcasted_iota(jnp.int32, (8, dk), 0)
    units = [(j, h) for j in range(nb) for h in range(GDN_HEADS)]
    m1 = {}
    for j, h in units:
        cols = slice(h * dk, (h + 1) * dk)
        w = ops_ref[0, j:j + 1, cols]
        qg = ops_ref[1, j:j + 1, cols]
        lhs = jnp.where(row == 0, jnp.broadcast_to(w, (8, dk)), jnp.broadcast_to(qg, (8, dk)))
        m1[j, h] = _dot(lhs, s_ref[j, h])
    for j, h in units:
        cols = slice(h * dk, (h + 1) * dk)
        k = ops_ref[2, j:j + 1, cols]
        u = ops_ref[3, j:j + 1, cols]
        v_new = u - m1[j, h][0:1, :]
        qk = sc_ref[j:j + 1, h:h + 1]
        eg = sc_ref[j:j + 1, 4 + h:5 + h]
        o_ref[j:j + 1, cols] = m1[j, h][1:2, :] + qk * v_new
        k8 = jnp.where(row == 0, jnp.broadcast_to(k, (8, dk)), 0.0)
        snew_ref[j, h] = s_ref[j, h] * eg + _dot_tn(k8, jnp.broadcast_to(v_new, (8, dk)))
    normw = normw_ref[...]
    for h in range(GDN_HEADS):
        o = o_ref[:, h * dk:(h + 1) * dk]
        z = z_ref[:, h * dk:(h + 1) * dk]
        o_ref[:, h * dk:(h + 1) * dk] = (o * lax.rsqrt(jnp.mean(o * o, axis=-1, keepdims=True) + NORM_EPS)
                                         * normw * _silu(z))


def _sample_state(state, ops, sc, z, normw, nb=8):
    ns = state.shape[0]
    hd = (GDN_HEADS, GDN_HEAD_DIM, GDN_HEAD_DIM)
    return pl.pallas_call(
        functools.partial(_sample_state_kernel, nb=nb),
        grid=(ns // nb,),
        in_specs=[pl.BlockSpec((nb,) + hd, lambda i: (i, 0, 0, 0)),
                  pl.BlockSpec((4, nb, GDN_WIDTH), lambda i: (0, i, 0)),
                  pl.BlockSpec((nb, GATE_LANES), lambda i: (i, 0)),
                  pl.BlockSpec((nb, GDN_WIDTH), lambda i: (i, 0)),
                  _whole(normw.shape)],
        out_specs=[pl.BlockSpec((nb,) + hd, lambda i: (i, 0, 0, 0)),
                   pl.BlockSpec((nb, GDN_WIDTH), lambda i: (i, 0))],
        out_shape=[jax.ShapeDtypeStruct(state.shape, F32),
                   jax.ShapeDtypeStruct((ns, GDN_WIDTH), F32)],
        compiler_params=_compiler_params(("arbitrary",)),
        name="sample_state",
    )(state, ops, sc, z, normw)


def _post_kernel(x_ref, o_ref, ys_ref, u_ref, d_ref, wglu_ref, bglu_ref, wout_ref, g1_ref, b1_ref,
                 wff1_ref, wff2_ref, g2_ref, b2_ref, y_ref, *, ff_chunk):
    n = x_ref.shape[0]
    halves = [slice(0, n // 2), slice(n // 2, n)] if n >= 256 else [slice(0, n)]
    st = [dict(rows=r) for r in halves]

    def head(s):
        r = s["rows"]
        ys = jax.nn.gelu(ys_ref[r, :] + d_ref[...] * u_ref[r, :])
        ys = ys * jax.nn.sigmoid(_dot(ys, wglu_ref[...]) + bglu_ref[...])
        yield
        mix = _dot(o_ref[r, :], wout_ref[0:GDN_WIDTH, :]) + _dot(ys, wout_ref[GDN_WIDTH:, :])
        yield
        s["x1"] = _layernorm(DN_ALPHA * x_ref[r, :] + mix, g1_ref[...], b1_ref[...])
        s["x1b"] = s["x1"].astype(BF16)
        yield

    def mlp(s):
        acc = jnp.zeros(s["x1"].shape, F32)
        for f in range(D_FF // ff_chunk):
            hcol = jnp.dot(s["x1b"], wff1_ref[:, f * ff_chunk:(f + 1) * ff_chunk], preferred_element_type=F32)
            hcol = jnp.square(jnp.maximum(hcol, 0.0))
            yield
            acc = acc + _dot(hcol, wff2_ref[f * ff_chunk:(f + 1) * ff_chunk, :])
            yield
        s["acc"] = acc

    def tail(s):
        y_ref[s["rows"], :] = _layernorm(DN_ALPHA * s["x1"] + s["acc"], g2_ref[...], b2_ref[...])
        yield

    _interleave(head(st[0]))
    for i, s in enumerate(st):
        others = [head(st[i + 1])] if i + 1 < len(st) else []
        if i > 0:
            others.append(tail(st[i - 1]))
        _interleave(mlp(s), *others)
    _interleave(tail(st[-1]))


def _post(x, o, ys, u, *weights, tile):
    n = x.shape[0]
    tok = lambda w: pl.BlockSpec((tile, w), lambda i: (i, 0))

    def resident(a):
        return pl.BlockSpec(a.shape, lambda i: (0,) * a.ndim, pipeline_mode=pl.Buffered(1))

    return pl.pallas_call(
        functools.partial(_post_kernel, ff_chunk=1024),
        grid=(n // tile,),
        in_specs=[tok(D_MODEL), tok(GDN_WIDTH), tok(S5_WIDTH), tok(S5_WIDTH)] + [resident(a) for a in weights],
        out_specs=tok(D_MODEL),
        out_shape=jax.ShapeDtypeStruct((n, D_MODEL), F32),
        compiler_params=_compiler_params(("arbitrary",)),
        name="post",
    )(x, o, ys, u, *weights)


def kernel(x_prompt, x_sample, state_gdn, state_conv, state_ssm_re, state_ssm_im, w_in, w_conv, gdn_a_log,
           gdn_dt_bias, gdn_norm_w, s5_a_re, s5_a_im, s5_b_re, s5_b_im, s5_c_re, s5_c_im, s5_d, s5_log_dt,
           w_glu, b_glu, w_out, ln1_g, ln1_b, w_ff1, w_ff2, ln2_g, ln2_b):
    bsz, seq, _ = x_prompt.shape
    ns = x_sample.shape[0]
    l = 0

    w = w_in[l]
    i_u = QKVZ_W + 2 * GDN_HEADS
    w_cat = jnp.concatenate([w[:, :QKVZ_W], w[:, i_u:], w[:, QKVZ_W:i_u],
                             jnp.zeros((D_MODEL, GATE_LANES - 2 * GDN_HEADS), F32)], axis=1).astype(BF16)
    lane_pad = (GDN_HEADS, GATE_LANES - 2 * GDN_HEADS)
    alog_row = jnp.pad(gdn_a_log[l], lane_pad).reshape(1, GATE_LANES)
    dtb_row = jnp.pad(gdn_dt_bias[l], lane_pad).reshape(1, GATE_LANES)
    normw = gdn_norm_w[l].reshape(1, GDN_HEAD_DIM)
    wconv = w_conv[l]
    row = lambda a: a.reshape(1, -1)
    post_w = (row(s5_d[l]), w_glu[l].astype(BF16), row(b_glu[l]), w_out[l].astype(BF16), row(ln1_g[l]), row(ln1_b[l]),
              w_ff1[l].astype(BF16), w_ff2[l].astype(BF16), row(ln2_g[l]), row(ln2_b[l]))

    xp = x_prompt.reshape(bsz * seq, D_MODEL)
    xs = x_sample.reshape(ns, D_MODEL)
    qkvz_p, gates_p, u_p = _proj(xp, w_cat, tile=512)
    qkvz_s, gates_s, u_s = _proj(xs, w_cat, tile=ns)

    o_p, gdn_p = _gdn_pair(qkvz_p.reshape(bsz, seq, QKVZ_W), gates_p.reshape(bsz, seq, GATE_LANES), wconv,
                           alog_row, dtb_row, normw)
    o_p = o_p.reshape(bsz * seq, GDN_WIDTH)
    conv_p = qkvz_p.reshape(bsz, seq, QKVZ_W)[:, seq - (CONV_W - 1):, :QKV_W]

    conv_s, ops_s, sc = _sample_pre(qkvz_s, gates_s, state_conv[l], wconv, alog_row, dtb_row)
    gdn_s, o_s = _sample_state(state_gdn[l], ops_s, sc, qkvz_s[:, QKV_W:], normw)

    mt, gt, gts, wf, coef, ys3, hsr, hsi = _s5_params(u_s.reshape(ns, S5_GROUPS, S5_GROUP), state_ssm_re[l],
                                                      state_ssm_im[l], s5_a_re[l], s5_a_im[l], s5_log_dt[l],
                                                      s5_b_re[l], s5_b_im[l], s5_c_re[l], s5_c_im[l])
    coef_rows = coef.transpose(1, 0, 2).reshape(8, S5_GROUPS * LANE_TILE)
    ys_p, h_last = _s5_seq(u_p, mt, gt, gts, wf, coef_rows, bsz, seq)
    h_last = h_last.reshape(bsz, S5_GROUPS, 2 * S5_STATE)
    ys_s = ys3.reshape(ns, S5_WIDTH)

    y_p = _post(xp, o_p, ys_p, u_p, *post_w, tile=512)
    y_s = _post(xs, o_s, ys_s, u_s, *post_w, tile=ns)

    return (y_p.reshape(bsz, seq, D_MODEL), y_s.reshape(ns, 1, D_MODEL),
            gdn_p[None], conv_p[None], h_last[None, :, :, :S5_STATE], h_last[None, :, :, S5_STATE:],
            gdn_s[None], conv_s[None], hsr[None], hsi[None])
```

```python
import functools

import jax
import jax.numpy as jnp
import numpy as np
from jax import lax
from jax.experimental import pallas as pl
from jax.experimental.pallas import tpu as pltpu

F32 = jnp.float32
BF16 = jnp.bfloat16

D_MODEL = 1024
GDN_HEADS = 4
GDN_HEAD_DIM = 128
GDN_WIDTH = GDN_HEADS * GDN_HEAD_DIM
CONV_W = 4
GDN_CHUNK = 64
S5_WIDTH = D_MODEL - GDN_WIDTH
S5_GROUP = 16
S5_GROUPS = S5_WIDTH // S5_GROUP
S5_STATE = 64
D_FF = 4 * D_MODEL
DEPTH = 1
DN_ALPHA = (2.0 * DEPTH) ** 0.25
NORM_EPS = 1e-6
QKV_W = 3 * GDN_WIDTH
QKVZ_W = QKV_W + GDN_WIDTH
LANE_TILE = 128
GATE_LANES = LANE_TILE
S5_CHUNK = 16
S5_FLAT = S5_CHUNK * S5_GROUP
VMEM_LIMIT = 56 * 1024 * 1024


def _dot(a, b):
    return jnp.dot(a.astype(BF16), b.astype(BF16), preferred_element_type=F32)


def _dot_nt(a, b):
    return lax.dot_general(a.astype(BF16), b.astype(BF16), (((1,), (1,)), ((), ())), preferred_element_type=F32)


def _dot_tn(a, b):
    return lax.dot_general(a.astype(BF16), b.astype(BF16), (((0,), (0,)), ((), ())), preferred_element_type=F32)


def _split(x):
    hi = x.astype(BF16)
    lo = (x - hi.astype(F32)).astype(BF16)
    return hi, lo


def _dot_hi(a, b, dims=(((1,), (0,)), ((), ()))):
    ah, al = _split(a)
    bh, bl = _split(b)
    d = functools.partial(lax.dot_general, dimension_numbers=dims, preferred_element_type=F32)
    return d(ah, bh) + (d(al, bh) + d(ah, bl))


def _dot_x3(a, b_exact):
    a1 = a.astype(BF16)
    r1 = a - a1.astype(F32)
    a2 = r1.astype(BF16)
    a3 = (r1 - a2.astype(F32)).astype(BF16)
    return _dot(a1, b_exact) + (_dot(a2, b_exact) + _dot(a3, b_exact))


def _dot_x3_left(a_exact, b):
    b1 = b.astype(BF16)
    r1 = b - b1.astype(F32)
    b2 = r1.astype(BF16)
    b3 = (r1 - b2.astype(F32)).astype(BF16)
    return _dot(a_exact, b1) + (_dot(a_exact, b2) + _dot(a_exact, b3))


def _complex_powers(base_r, base_i, exponent, nbits):
    shape = exponent.shape
    pr = jnp.ones(shape, F32)
    pi = jnp.zeros(shape, F32)
    br, bi = base_r, base_i
    for j in range(nbits):
        bit = ((exponent >> j) & 1) == 1
        pr, pi = jnp.where(bit, pr * br - pi * bi, pr), jnp.where(bit, pr * bi + pi * br, pi)
        br, bi = br * br - bi * bi, 2.0 * br * bi
    return pr, pi


def _silu(x):
    return x * jax.nn.sigmoid(x)


def _layernorm(x, g, b):
    mu = jnp.mean(x, axis=-1, keepdims=True)
    xc = x - mu
    var = jnp.mean(xc * xc, axis=-1, keepdims=True)
    return xc * lax.rsqrt(var + NORM_EPS) * g + b


def _l2norm(a):
    return a * lax.rsqrt(jnp.sum(a * a, axis=-1, keepdims=True) + NORM_EPS)


def _compiler_params(semantics):
    return pltpu.CompilerParams(dimension_semantics=semantics, vmem_limit_bytes=VMEM_LIMIT)


def _whole(shape):
    n = len(shape)
    return pl.BlockSpec(shape, lambda *_: (0,) * n)


def _interleave(*gens):
    live = list(gens)
    while live:
        for gen in list(live):
            if next(gen, "done") == "done":
                live.remove(gen)


def _proj_kernel(x_ref, w_ref, qkvz_ref, gates_ref, u_ref):
    r = jnp.dot(x_ref[...].astype(BF16), w_ref[...], preferred_element_type=F32)
    qkvz_ref[...] = r[:, 0:QKVZ_W]
    u_ref[...] = r[:, QKVZ_W:QKVZ_W + S5_WIDTH]
    gates_ref[...] = r[:, QKVZ_W + S5_WIDTH:]


def _proj_conv_kernel(x_ref, w_ref, wconv_ref, qkvz_ref, gates_ref, u_ref, tail_ref, xpad_ref, *, tiles_per_seq):
    tile = x_ref.shape[0]
    dk = GDN_HEAD_DIM

    @pl.when(pl.program_id(0) % tiles_per_seq == 0)
    def _():
        xpad_ref[0:8, :] = jnp.zeros((8, QKV_W), F32)

    wc = wconv_ref[...]
    scale = GDN_HEAD_DIM ** -0.5
    half = tile // 2

    def project(r0):
        rows = slice(r0, r0 + half)
        xb = x_ref[rows, :].astype(BF16)
        step = 2 * LANE_TILE
        for c0 in range(0, w_ref.shape[1], step):
            c1 = min(c0 + step, w_ref.shape[1])
            r = jnp.dot(xb, w_ref[:, c0:c1], preferred_element_type=F32)
            if c1 <= QKV_W:
                xpad_ref[8 + r0:8 + r0 + half, c0:c1] = r
            elif c1 <= QKVZ_W:
                qkvz_ref[rows, c0:c1] = r
            elif c1 <= QKVZ_W + S5_WIDTH:
                u_ref[rows, c0 - QKVZ_W:c1 - QKVZ_W] = r
            else:
                gates_ref[rows, :] = r
            yield

    def activate(r0):
        for lt in range(QKV_W // dk):
            cols = slice(lt * dk, (lt + 1) * dk)
            conv = xpad_ref[8 + r0:8 + r0 + half, cols] * wc[3:4, cols]
            for j in range(1, CONV_W):
                conv = conv + xpad_ref[8 + r0 - j:8 + r0 - j + half, cols] * wc[3 - j:4 - j, cols]
            act = _silu(conv)
            if lt < GDN_HEADS:
                act = _l2norm(act) * scale
            elif lt < 2 * GDN_HEADS:
                act = _l2norm(act)
            qkvz_ref[r0:r0 + half, cols] = act
            yield

    _interleave(project(0))
    _interleave(project(half), activate(0))
    _interleave(activate(half))
    tail_ref[0] = xpad_ref[tile:tile + 8, :]
    xpad_ref[0:8, :] = xpad_ref[tile:tile + 8, :]


def _proj(x, w, tile):
    n = x.shape[0]
    return pl.pallas_call(
        _proj_kernel,
        grid=(n // tile,),
        in_specs=[pl.BlockSpec((tile, D_MODEL), lambda i: (i, 0)),
                  pl.BlockSpec(w.shape, lambda i: (0, 0), pipeline_mode=pl.Buffered(1))],
        out_specs=[pl.BlockSpec((tile, QKVZ_W), lambda i: (i, 0)),
                   pl.BlockSpec((tile, GATE_LANES), lambda i: (i, 0)),
                   pl.BlockSpec((tile, S5_WIDTH), lambda i: (i, 0))],
        out_shape=[jax.ShapeDtypeStruct((n, QKVZ_W), F32),
                   jax.ShapeDtypeStruct((n, GATE_LANES), F32),
                   jax.ShapeDtypeStruct((n, S5_WIDTH), F32)],
        compiler_params=_compiler_params(("arbitrary",)),
        name="proj",
    )(x, w)


def _proj_conv(x, w, wconv, seq, tile):
    n = x.shape[0]
    tiles_per_seq = seq // tile
    return pl.pallas_call(
        functools.partial(_proj_conv_kernel, tiles_per_seq=tiles_per_seq),
        grid=(n // tile,),
        in_specs=[pl.BlockSpec((tile, D_MODEL), lambda i: (i, 0)),
                  pl.BlockSpec(w.shape, lambda i: (0, 0), pipeline_mode=pl.Buffered(1)),
                  _whole(wconv.shape)],
        out_specs=[pl.BlockSpec((tile, QKVZ_W), lambda i: (i, 0)),
                   pl.BlockSpec((tile, GATE_LANES), lambda i: (i, 0)),
                   pl.BlockSpec((tile, S5_WIDTH), lambda i: (i, 0)),
                   pl.BlockSpec((1, 8, QKV_W), lambda i: (i // tiles_per_seq, 0, 0))],
        out_shape=[jax.ShapeDtypeStruct((n, QKVZ_W), F32),
                   jax.ShapeDtypeStruct((n, GATE_LANES), F32),
                   jax.ShapeDtypeStruct((n, S5_WIDTH), F32),
                   jax.ShapeDtypeStruct((n // seq, 8, QKV_W), F32)],
        scratch_shapes=[pltpu.VMEM((8 + tile, QKV_W), F32)],
        compiler_params=_compiler_params(("arbitrary",)),
        name="proj_conv",
    )(x, w, wconv)


GDN_SEQS = 8
GDN_SUB_BLOCKS = (1, 2, 4, 8, 16, 32)
MASK_TRI, MASK_STRICT, MASK_EYE = len(GDN_SUB_BLOCKS), len(GDN_SUB_BLOCKS) + 1, len(GDN_SUB_BLOCKS) + 2
SEL_LOW, SEL_HIGH, SEL_SUB = 0, 1, 2


def _gate_values(logits, a_log, dt_bias):
    beta = jax.nn.sigmoid(logits)
    g = -jnp.exp(a_log) * jax.nn.softplus(logits + dt_bias)
    return beta, g


def _gdn_masks():
    c = GDN_CHUNK
    r = np.arange(c)[:, None]
    col = np.arange(2 * c)[None, :] % c
    sub = [(r // (2 * b) == col // (2 * b)) & ((r // b) % 2 == 1) & ((col // b) % 2 == 0) for b in GDN_SUB_BLOCKS]
    masks = np.stack(sub + [col <= r, col < r, col == r]).astype(np.float32)
    lane = np.broadcast_to(np.arange(2 * c)[None, :], (c, 2 * c))
    selectors = np.stack([lane < c, lane >= c] + sub).astype(np.float32)
    return jnp.asarray(masks), jnp.asarray(selectors, dtype=BF16)


def _gdn_pair_kernel(qkvz_ref, gates_ref, alog_row_ref, dtb_row_ref, normw_ref, masks_ref, sel_ref, o_ref, s_ref):
    step = pl.program_id(1)
    ns = qkvz_ref.shape[0]
    c = GDN_CHUNK
    dk = GDN_HEAD_DIM
    rows = ns * c

    @pl.when(step == 0)
    def _():
        s_ref[...] = jnp.zeros_like(s_ref)

    beta_c, g_c = _gate_values(gates_ref[...].reshape(rows, GATE_LANES), alog_row_ref[...], dtb_row_ref[...])

    ri = lax.broadcasted_iota(jnp.int32, (rows, rows), 0)
    ci = lax.broadcasted_iota(jnp.int32, (rows, rows), 1)
    tril_bd = jnp.where(((ri // c) == (ci // c)) & (ci <= ri), 1.0, 0.0).astype(BF16)
    gc_c = _dot_x3_left(tril_bd, g_c)
    gc_t = [gc_c[i * GATE_LANES:(i + 1) * GATE_LANES, :].T for i in range(rows // GATE_LANES)]
    gc_t_rolled = [pltpu.roll(t, c, axis=1) for t in gc_t]

    mask = lambda i: masks_ref[i] > 0.5
    low_half = lax.broadcasted_iota(jnp.int32, (1, 2 * c), 1) < c
    normw = normw_ref[...]
    zeros_k = jnp.zeros((c, dk), F32)

    def block_diag(x):
        return jnp.concatenate([x * sel_ref[SEL_LOW], x * sel_ref[SEL_HIGH]], axis=0)

    def pair_product(ph, pl_, yh, yl):
        bdh = block_diag(yh)
        first = jnp.dot(jnp.concatenate([ph, pl_], axis=1), jnp.concatenate([bdh, bdh], axis=0),
                        preferred_element_type=F32)
        return first + jnp.dot(ph, block_diag(yl), preferred_element_type=F32)

    def prep(seqs, pairs):
        for s in seqs:
            r0 = s * c
            lane0 = r0 % GATE_LANES
            blk = r0 // GATE_LANES
            in_place, moved = (gc_t[blk], gc_t_rolled[blk])
            for h0 in range(0, GDN_HEADS, 2):
                heads, a1 = [], None
                for side, h in enumerate((h0, h0 + 1)):
                    q = qkvz_ref[s, :, h * dk:(h + 1) * dk]
                    k = qkvz_ref[s, :, GDN_WIDTH + h * dk:GDN_WIDTH + (h + 1) * dk]
                    v = qkvz_ref[s, :, 2 * GDN_WIDTH + h * dk:2 * GDN_WIDTH + (h + 1) * dk]
                    beta = beta_c[r0:r0 + c, h:h + 1]
                    gcol = gc_c[r0:r0 + c, 4 + h:5 + h]
                    eg = jnp.exp(gcol)
                    g_last = gcol[c - 1:c, :]
                    kb = k * beta
                    k_pad = jnp.concatenate([k, zeros_k] if side == 0 else [zeros_k, k], axis=0)
                    part = _dot_nt(jnp.concatenate([kb, q], axis=0), k_pad)
                    a1 = part if a1 is None else a1 + part
                    src = in_place if (lane0 == 0) == (side == 0) else moved
                    heads.append(dict(s=s, h=h, gcol=gcol, grow=src[4 + h:5 + h, :],
                                      qg=q * eg, kd=k * jnp.exp(g_last - gcol), eg_last=jnp.exp(g_last),
                                      rhs=jnp.concatenate([v * beta, kb * eg], axis=1)))
                ha, hb = heads
                diff = jnp.where(low_half, ha["gcol"], hb["gcol"]) - jnp.where(low_half, ha["grow"], hb["grow"])
                decay = jnp.where(mask(MASK_TRI), jnp.exp(jnp.where(mask(MASK_TRI), diff, 0.0)), 0.0)
                pairs.append(dict(heads=heads,
                                  qk=jnp.where(mask(MASK_TRI), a1[c:2 * c] * decay, 0.0),
                                  lmat=jnp.where(mask(MASK_STRICT), a1[0:c] * decay, 0.0)))
                yield

    def solve(pairs):
        for p in pairs:
            p["d"] = masks_ref[MASK_EYE] - p["lmat"] * masks_ref[0]
            p["lh"], p["ll"] = _split(p["lmat"])
        for lvl in range(1, len(GDN_SUB_BLOCKS)):
            off = sel_ref[SEL_SUB + lvl]
            for p in pairs:
                p["dh"], p["dl"] = _split(p["d"])
                p["y"] = pair_product(p["lh"] * off, p["ll"] * off, p["dh"], p["dl"])
            yield
            for p in pairs:
                yh, yl = _split(p["y"])
                p["d"] = p["d"] - pair_product(p["dh"], p["dl"], yh, yl)
            yield
        for p in pairs:
            dh, dl = _split(p["d"])
            (ra_h, ra_l), (rb_h, rb_l) = (_split(hd["rhs"]) for hd in p["heads"])
            zr = jnp.zeros((c, 2 * dk), BF16)
            bd_h = jnp.concatenate([jnp.concatenate([ra_h, zr], axis=1), jnp.concatenate([zr, rb_h], axis=1)], axis=0)
            bd_l = jnp.concatenate([jnp.concatenate([ra_l, zr], axis=1), jnp.concatenate([zr, rb_l], axis=1)], axis=0)
            sol = (jnp.dot(jnp.concatenate([dh, dl], axis=1), jnp.concatenate([bd_h, bd_h], axis=0),
                           preferred_element_type=F32)
                   + jnp.dot(dh, bd_l, preferred_element_type=F32))
            for side, hd in enumerate(p["heads"]):
                hd["sol"] = sol[:, side * 2 * dk:(side + 1) * 2 * dk]
        yield

    def advance(pairs):
        for p in pairs:
            for hd in p["heads"]:
                hd["s_old"] = s_ref[hd["s"], hd["h"]]
                hd["m1"] = _dot(jnp.concatenate([hd["sol"][:, dk:2 * dk], hd["qg"]], axis=0), hd["s_old"])
        yield
        for p in pairs:
            for hd in p["heads"]:
                hd["v_new"] = hd["sol"][:, 0:dk] - hd["m1"][0:c]
                s_ref[hd["s"], hd["h"]] = hd["s_old"] * hd["eg_last"] + _dot_tn(hd["kd"], hd["v_new"])
        yield
        zv = jnp.zeros((c, dk), F32)
        for p in pairs:
            ha, hb = p["heads"]
            v_bd = jnp.concatenate([jnp.concatenate([ha["v_new"], zv], axis=1),
                                    jnp.concatenate([zv, hb["v_new"]], axis=1)], axis=0)
            o_pair = _dot(p["qk"], v_bd)
            for side, hd in enumerate(p["heads"]):
                s, h = hd["s"], hd["h"]
                o = hd["m1"][c:2 * c] + o_pair[:, side * dk:(side + 1) * dk]
                z = qkvz_ref[s, :, QKV_W + h * dk:QKV_W + (h + 1) * dk]
                o = o * lax.rsqrt(jnp.mean(o * o, axis=-1, keepdims=True) + NORM_EPS) * normw * _silu(z)
                o_ref[s, :, h * dk:(h + 1) * dk] = o
        yield

    half = max(ns // 2, 1)
    wave_a, wave_b = [], []
    _interleave(prep(range(0, half), wave_a))
    _interleave(solve(wave_a), prep(range(half, ns), wave_b))
    _interleave(solve(wave_b), advance(wave_a))
    _interleave(advance(wave_b))


def _gdn_pair(qkvz3, gates3, alog_row, dtb_row, normw):
    bsz, seq, _ = qkvz3.shape
    ns, c = min(GDN_SEQS, bsz), GDN_CHUNK
    hd = (GDN_HEADS, GDN_HEAD_DIM, GDN_HEAD_DIM)
    masks, selectors = _gdn_masks()
    return pl.pallas_call(
        _gdn_pair_kernel,
        grid=(bsz // ns, seq // c),
        in_specs=[pl.BlockSpec((ns, c, QKVZ_W), lambda b, i: (b, i, 0)),
                  pl.BlockSpec((ns, c, GATE_LANES), lambda b, i: (b, i, 0)),
                  _whole(alog_row.shape), _whole(dtb_row.shape), _whole(normw.shape),
                  _whole(masks.shape), _whole(selectors.shape)],
        out_specs=[pl.BlockSpec((ns, c, GDN_WIDTH), lambda b, i: (b, i, 0)),
                   pl.BlockSpec((ns,) + hd, lambda b, i: (b, 0, 0, 0))],
        out_shape=[jax.ShapeDtypeStruct((bsz, seq, GDN_WIDTH), F32),
                   jax.ShapeDtypeStruct((bsz,) + hd, F32)],
        compiler_params=_compiler_params(("arbitrary", "arbitrary")),
        name="gdn_prompt",
    )(qkvz3, gates3, alog_row, dtb_row, normw, masks, selectors)


def _s5_params_kernel(us_ref, h0r_ref, h0i_ref, ar_row_ref, ai_row_ref, ar_col_ref, ai_col_ref, ldt_ref,
                      br_ref, bi_ref, cr_ref, ci_ref,
                      mt_ref, gt_ref, gts_ref, wf_ref, coef_ref, ys_ref, hsr_ref, hsi_ref):
    L = S5_CHUNK
    P = S5_STATE
    dt = jnp.exp(ldt_ref[0])
    ar_row = ar_row_ref[0] * dt
    ai_row = ai_row_ref[0] * dt
    ar_col = ar_col_ref[0] * dt
    ai_col = ai_col_ref[0] * dt
    first = lax.broadcasted_iota(jnp.int32, (1, 2 * P), 1) < P

    ea = jnp.exp(ar_col)
    lbr = ea * jnp.cos(ai_col)
    lbi = ea * jnp.sin(ai_col)
    lam_r = ar_col_ref[0]
    lam_i = ai_col_ref[0]
    den = lam_r * lam_r + lam_i * lam_i
    fr = ((lbr - 1.0) * lam_r + lbi * lam_i) / den
    fi = (lbi * lam_r - (lbr - 1.0) * lam_i) / den
    b_r = br_ref[0]
    b_i = bi_ref[0]
    bbr = fr * b_r - fi * b_i
    bbi = fr * b_i + fi * b_r

    rt = lax.broadcasted_iota(jnp.int32, (S5_GROUP, S5_FLAT), 0)
    lt = lax.broadcasted_iota(jnp.int32, (S5_GROUP, S5_FLAT), 1)
    tile_mat = jnp.where(lt % S5_GROUP == rt, 1.0, 0.0).astype(BF16)
    bwr = _dot_x3(bbr, tile_mat)
    bwi = _dot_x3(bbi, tile_mat)

    tau_g = (L - 1) - lax.broadcasted_iota(jnp.int32, (P, S5_FLAT), 1) // S5_GROUP
    pgr, pgi = _complex_powers(lbr, lbi, tau_g, 4)
    gtr = pgr * bwr - pgi * bwi
    gti = pgr * bwi + pgi * bwr
    gt_ref[0] = jnp.concatenate([gtr, gti], axis=0).astype(BF16)
    gts_ref[0] = jnp.concatenate([gti, gtr], axis=0).astype(BF16)

    nt = L + 1
    e1 = jnp.exp(ar_row)
    l1r = e1 * jnp.cos(ai_row)
    l1i = e1 * jnp.sin(ai_row)
    tau_w = lax.broadcasted_iota(jnp.int32, (nt * S5_GROUP, 2 * P), 0) // S5_GROUP
    pwr, pwi = _complex_powers(l1r, l1i, tau_w, 5)
    c_r = jnp.concatenate([cr_ref[0]] * nt, axis=0)
    c_i = jnp.concatenate([ci_ref[0]] * nt, axis=0)
    wall = jnp.where(first, c_r * pwr - c_i * pwi, -(c_r * pwi + c_i * pwr))
    wf_ref[0] = wall[S5_GROUP:].astype(BF16)

    zw = _dot_hi(wall[0:S5_FLAT], jnp.concatenate([bwr, bwi], axis=0))
    s_of_lane = lax.broadcasted_iota(jnp.int32, (S5_FLAT, S5_FLAT), 1) // S5_GROUP
    mt = zw
    for j in range(4):
        sh = S5_GROUP << j
        shifted = jnp.concatenate([jnp.zeros((sh, S5_FLAT), F32), mt[0:S5_FLAT - sh]], axis=0)
        mt = jnp.where(((s_of_lane >> j) & 1) == 1, shifted, mt)
    mt_ref[0] = mt.astype(BF16)

    p16r = pwr[L * S5_GROUP:L * S5_GROUP + 1, :]
    p16i = pwi[L * S5_GROUP:L * S5_GROUP + 1, :]
    coef_b = jnp.where(first, -p16i, p16i)
    srow = lax.broadcasted_iota(jnp.int32, (8, 2 * P), 0)
    coef_ref[0] = jnp.where(srow == 0, p16r, jnp.where(srow == 1, coef_b, jnp.where(srow == 2, -coef_b, 0.0)))

    l1r = l1r[:, 0:P]
    l1i = l1i[:, 0:P]
    us = us_ref[0]
    nt_dims = (((1,), (1,)), ((), ()))
    bur = _dot_hi(us, bbr, nt_dims)
    bui = _dot_hi(us, bbi, nt_dims)
    h0r = h0r_ref[0]
    h0i = h0i_ref[0]
    hsr = l1r * h0r - l1i * h0i + bur
    hsi = l1r * h0i + l1i * h0r + bui
    hsr_ref[0] = hsr
    hsi_ref[0] = hsi
    ys_ref[0] = _dot_nt(hsr, cr_ref[0][:, 0:P]) - _dot_nt(hsi, ci_ref[0][:, 0:P])


def _s5_params(us_t, h0r_t, h0i_t, ar, ai, ldt, b_r, b_i, c_r, c_i):
    g = S5_GROUPS
    ns = us_t.shape[1]
    twice = lambda a: jnp.concatenate([a, a], axis=-1)
    ar_row = twice(ar).reshape(g, 1, 2 * S5_STATE)
    ai_row = twice(ai).reshape(g, 1, 2 * S5_STATE)
    ar_col = ar.reshape(g, S5_STATE, 1)
    ai_col = ai.reshape(g, S5_STATE, 1)
    ldt3 = ldt.reshape(g, 1, 1)

    def blk(shape):
        return pl.BlockSpec((1,) + shape, lambda i: (i,) + (0,) * len(shape))

    return pl.pallas_call(
        _s5_params_kernel,
        grid=(g,),
        in_specs=[blk((ns, S5_GROUP)), blk((ns, S5_STATE)), blk((ns, S5_STATE)),
                  blk((1, 2 * S5_STATE)), blk((1, 2 * S5_STATE)), blk((S5_STATE, 1)), blk((S5_STATE, 1)), blk((1, 1)),
                  blk((S5_STATE, S5_GROUP)), blk((S5_STATE, S5_GROUP)),
                  blk((S5_GROUP, 2 * S5_STATE)), blk((S5_GROUP, 2 * S5_STATE))],
        out_specs=[blk((S5_FLAT, S5_FLAT)), blk((2 * S5_STATE, S5_FLAT)), blk((2 * S5_STATE, S5_FLAT)),
                   blk((S5_FLAT, 2 * S5_STATE)), blk((8, 2 * S5_STATE)),
                   blk((ns, S5_GROUP)), blk((ns, S5_STATE)), blk((ns, S5_STATE))],
        out_shape=[jax.ShapeDtypeStruct((g, S5_FLAT, S5_FLAT), BF16),
                   jax.ShapeDtypeStruct((g, 2 * S5_STATE, S5_FLAT), BF16),
                   jax.ShapeDtypeStruct((g, 2 * S5_STATE, S5_FLAT), BF16),
                   jax.ShapeDtypeStruct((g, S5_FLAT, 2 * S5_STATE), BF16),
                   jax.ShapeDtypeStruct((g, 8, 2 * S5_STATE), F32),
                   jax.ShapeDtypeStruct((g, ns, S5_GROUP), F32),
                   jax.ShapeDtypeStruct((g, ns, S5_STATE), F32),
                   jax.ShapeDtypeStruct((g, ns, S5_STATE), F32)],
        compiler_params=_compiler_params(("arbitrary",)),
        name="s5_params",
    )(us_t, h0r_t, h0i_t, ar_row, ai_row, ar_col, ai_col, ldt3, b_r, b_i, twice(c_r), twice(c_i))


PIECES = LANE_TILE // S5_GROUP
RELAYOUT_ROWS = S5_CHUNK * S5_CHUNK


def _piece_transpose(tiles):
    tiles = list(tiles)
    piece = lax.broadcasted_iota(jnp.int32, tiles[0].shape, 1) // S5_GROUP
    d = PIECES // 2
    while d:
        keep_low = (piece & d) == 0
        for k in range(PIECES):
            if k & d:
                continue
            a, b = tiles[k], tiles[k + d]
            tiles[k] = jnp.where(keep_low, a, pltpu.roll(b, S5_GROUP * d, axis=1))
            tiles[k + d] = jnp.where(keep_low, pltpu.roll(a, LANE_TILE - S5_GROUP * d, axis=1), b)
        d //= 2
    return tiles


def _s5_seq_kernel(u_ref, mt_ref, gt_ref, gts_ref, wf_ref, coef_ref, ys_ref, hl_ref,
                   uflat_ref, e_ref, es_ref, hin_ref, yflat_ref, *, nchunk):
    rr = RELAYOUT_ROWS
    ro = lax.broadcasted_iota(jnp.int32, (rr, rr), 0)
    ri = lax.broadcasted_iota(jnp.int32, (rr, rr), 1)
    perm = jnp.where((ro // S5_CHUNK == ri % S5_CHUNK) & (ro % S5_CHUNK == ri // S5_CHUNK), 1.0, 0.0).astype(BF16)

    def gather_in(m, carry):
        rows = pl.ds(pl.multiple_of(m * rr, rr), rr)
        xp = jnp.dot(perm, u_ref[rows, :].astype(BF16), preferred_element_type=F32)
        crow = pl.ds(pl.multiple_of(m * S5_CHUNK, S5_CHUNK), S5_CHUNK)
        for j in range(S5_FLAT // LANE_TILE):
            for cb in range(S5_WIDTH // LANE_TILE):
                tiles = [xp[(PIECES * j + k) * S5_CHUNK:(PIECES * j + k + 1) * S5_CHUNK,
                            cb * LANE_TILE:(cb + 1) * LANE_TILE] for k in range(PIECES)]
                for p, tile in enumerate(_piece_transpose(tiles)):
                    uflat_ref[PIECES * cb + p, crow, j * LANE_TILE:(j + 1) * LANE_TILE] = tile.astype(BF16)
        return carry

    lax.fori_loop(0, nchunk // S5_CHUNK, gather_in, 0, unroll=2)

    for g in range(S5_GROUPS):
        lanes = slice(g * LANE_TILE, (g + 1) * LANE_TILE)
        ug = uflat_ref[g]
        e_ref[:, lanes] = _dot_nt(ug, gt_ref[g])
        es_ref[:, lanes] = _dot_nt(ug, gts_ref[g])

    ca = coef_ref[0:1, :]
    cb = coef_ref[1:2, :]
    cbs = coef_ref[2:3, :]

    def scan(c, carry):
        h, hs = carry
        row = pl.ds(c, 1)
        hin_ref[row, :] = h
        return ca * h + cb * hs + e_ref[row, :], ca * hs + cbs * h + es_ref[row, :]

    zero = jnp.zeros((1, S5_GROUPS * LANE_TILE), F32)
    h_last, _ = lax.fori_loop(0, nchunk, scan, (zero, zero))
    hl_ref[0] = h_last

    for g in range(S5_GROUPS):
        lanes = slice(g * LANE_TILE, (g + 1) * LANE_TILE)
        yflat_ref[g] = _dot_nt(uflat_ref[g], mt_ref[g]) + _dot_nt(hin_ref[:, lanes], wf_ref[g])

    def gather_out(m, carry):
        crow = pl.ds(pl.multiple_of(m * S5_CHUNK, S5_CHUNK), S5_CHUNK)
        by_time = [[None] * (S5_WIDTH // LANE_TILE) for _ in range(S5_CHUNK)]
        for j in range(S5_FLAT // LANE_TILE):
            for cb in range(S5_WIDTH // LANE_TILE):
                tiles = [yflat_ref[PIECES * cb + k, crow, j * LANE_TILE:(j + 1) * LANE_TILE] for k in range(PIECES)]
                for p, tile in enumerate(_piece_transpose(tiles)):
                    by_time[PIECES * j + p][cb] = tile
        z = jnp.concatenate([jnp.concatenate(row, axis=1) for row in by_time], axis=0)
        rows = pl.ds(pl.multiple_of(m * rr, rr), rr)
        ys_ref[rows, :] = _dot_x3_left(perm, z)
        return carry

    lax.fori_loop(0, nchunk // S5_CHUNK, gather_out, 0, unroll=2)


def _s5_seq(u, mt, gt, gts, wf, coef, bsz, seq):
    nchunk = seq // S5_CHUNK
    width = S5_GROUPS * LANE_TILE

    def resident(a):
        return pl.BlockSpec(a.shape, lambda b: (0,) * a.ndim, pipeline_mode=pl.Buffered(1))

    return pl.pallas_call(
        functools.partial(_s5_seq_kernel, nchunk=nchunk),
        grid=(bsz,),
        in_specs=[pl.BlockSpec((seq, S5_WIDTH), lambda b: (b, 0))] + [resident(a) for a in (mt, gt, gts, wf, coef)],
        out_specs=[pl.BlockSpec((seq, S5_WIDTH), lambda b: (b, 0)),
                   pl.BlockSpec((1, 1, width), lambda b: (b, 0, 0))],
        out_shape=[jax.ShapeDtypeStruct((bsz * seq, S5_WIDTH), F32),
                   jax.ShapeDtypeStruct((bsz, 1, width), F32)],
        scratch_shapes=[pltpu.VMEM((S5_GROUPS, nchunk, S5_FLAT), BF16),
                        pltpu.VMEM((nchunk, width), F32), pltpu.VMEM((nchunk, width), F32),
                        pltpu.VMEM((nchunk, width), F32),
                        pltpu.VMEM((S5_GROUPS, nchunk, S5_FLAT), F32)],
        compiler_params=_compiler_params(("arbitrary",)),
        name="s5_seq",
    )(u, mt, gt, gts, wf, coef)


def _sample_pre_kernel(qkvz_ref, gates_ref, conv_ref, wconv_ref, alog_row_ref, dtb_row_ref,
                       newconv_ref, ops_ref, sc_ref):
    dk = GDN_HEAD_DIM
    x_new = qkvz_ref[:, 0:QKV_W]
    wc = wconv_ref[...]
    conv = x_new * wc[3:4, :]
    for j in range(CONV_W - 1):
        conv = conv + conv_ref[j] * wc[j:j + 1, :]
    qkv = _silu(conv)
    newconv_ref[0] = conv_ref[1]
    newconv_ref[1] = conv_ref[2]
    newconv_ref[2] = x_new

    beta_c, g_c = _gate_values(gates_ref[...], alog_row_ref[...], dtb_row_ref[...])
    scale = GDN_HEAD_DIM ** -0.5
    lane = lax.broadcasted_iota(jnp.int32, sc_ref.shape, 1)
    sc = jnp.zeros(sc_ref.shape, F32)
    for h in range(GDN_HEADS):
        q = _l2norm(qkv[:, h * dk:(h + 1) * dk]) * scale
        k = _l2norm(qkv[:, GDN_WIDTH + h * dk:GDN_WIDTH + (h + 1) * dk])
        v = qkv[:, 2 * GDN_WIDTH + h * dk:2 * GDN_WIDTH + (h + 1) * dk]
        beta = beta_c[:, h:h + 1]
        eg = jnp.exp(g_c[:, 4 + h:5 + h])
        cols = slice(h * dk, (h + 1) * dk)
        ops_ref[0, :, cols] = (beta * eg) * k
        ops_ref[1, :, cols] = q * eg
        ops_ref[2, :, cols] = k
        ops_ref[3, :, cols] = beta * v
        sc = jnp.where(lane == h, jnp.sum(q * k, axis=-1, keepdims=True), sc)
        sc = jnp.where(lane == 4 + h, eg, sc)
    sc_ref[...] = sc


def _sample_pre(qkvz_s, gates_s, conv_t, wconv, alog_row, dtb_row):
    ns = qkvz_s.shape[0]
    return pl.pallas_call(
        _sample_pre_kernel,
        out_shape=[jax.ShapeDtypeStruct((CONV_W - 1, ns, QKV_W), F32),
                   jax.ShapeDtypeStruct((4, ns, GDN_WIDTH), F32),
                   jax.ShapeDtypeStruct((ns, GATE_LANES), F32)],
        compiler_params=pltpu.CompilerParams(vmem_limit_bytes=VMEM_LIMIT),
        name="sample_pre",
    )(qkvz_s, gates_s, conv_t, wconv, alog_row, dtb_row)


def _sample_state_kernel(s_ref, ops_ref, sc_ref, z_ref, normw_ref, snew_ref, o_ref, *, nb):
    dk = GDN_HEAD_DIM
    row = lax.broadcasted_iota(jnp.int32, (8, dk), 0)
    units = [(j, h) for j in range(nb) for h in range(GDN_HEADS)]
    m1 = {}
    for j, h in units:
        cols = slice(h * dk, (h + 1) * dk)
        w = ops_ref[0, j:j + 1, cols]
        qg = ops_ref[1, j:j + 1, cols]
        lhs = jnp.where(row == 0, jnp.broadcast_to(w, (8, dk)), jnp.broadcast_to(qg, (8, dk)))
        m1[j, h] = _dot(lhs, s_ref[j, h])
    for j, h in units:
        cols = slice(h * dk, (h + 1) * dk)
        k = ops_ref[2, j:j + 1, cols]
        u = ops_ref[3, j:j + 1, cols]
        v_new = u - m1[j, h][0:1, :]
        qk = sc_ref[j:j + 1, h:h + 1]
        eg = sc_ref[j:j + 1, 4 + h:5 + h]
        o_ref[j:j + 1, cols] = m1[j, h][1:2, :] + qk * v_new
        k8 = jnp.where(row == 0, jnp.broadcast_to(k, (8, dk)), 0.0)
        snew_ref[j, h] = s_ref[j, h] * eg + _dot_tn(k8, jnp.broadcast_to(v_new, (8, dk)))
    normw = normw_ref[...]
    for h in range(GDN_HEADS):
        o = o_ref[:, h * dk:(h + 1) * dk]
        z = z_ref[:, h * dk:(h + 1) * dk]
        o_ref[:, h * dk:(h + 1) * dk] = (o * lax.rsqrt(jnp.mean(o * o, axis=-1, keepdims=True) + NORM_EPS)
                                         * normw * _silu(z))


def _sample_state(state, ops, sc, z, normw, nb=8):
    ns = state.shape[0]
    hd = (GDN_HEADS, GDN_HEAD_DIM, GDN_HEAD_DIM)
    return pl.pallas_call(
        functools.partial(_sample_state_kernel, nb=nb),
        grid=(ns // nb,),
        in_specs=[pl.BlockSpec((nb,) + hd, lambda i: (i, 0, 0, 0)),
                  pl.BlockSpec((4, nb, GDN_WIDTH), lambda i: (0, i, 0)),
                  pl.BlockSpec((nb, GATE_LANES), lambda i: (i, 0)),
                  pl.BlockSpec((nb, GDN_WIDTH), lambda i: (i, 0)),
                  _whole(normw.shape)],
        out_specs=[pl.BlockSpec((nb,) + hd, lambda i: (i, 0, 0, 0)),
                   pl.BlockSpec((nb, GDN_WIDTH), lambda i: (i, 0))],
        out_shape=[jax.ShapeDtypeStruct(state.shape, F32),
                   jax.ShapeDtypeStruct((ns, GDN_WIDTH), F32)],
        compiler_params=_compiler_params(("arbitrary",)),
        name="sample_state",
    )(state, ops, sc, z, normw)


def _post_kernel(x_ref, o_ref, ys_ref, u_ref, d_ref, wglu_ref, bglu_ref, wout_ref, g1_ref, b1_ref,
                 wff1_ref, wff2_ref, g2_ref, b2_ref, y_ref, *, ff_chunk):
    n = x_ref.shape[0]
    halves = [slice(0, n // 2), slice(n // 2, n)] if n >= 256 else [slice(0, n)]
    st = [dict(rows=r) for r in halves]

    def head(s):
        r = s["rows"]
        ys = jax.nn.gelu(ys_ref[r, :] + d_ref[...] * u_ref[r, :])
        ys = ys * jax.nn.sigmoid(_dot(ys, wglu_ref[...]) + bglu_ref[...])
        yield
        mix = _dot(o_ref[r, :], wout_ref[0:GDN_WIDTH, :]) + _dot(ys, wout_ref[GDN_WIDTH:, :])
        yield
        s["x1"] = _layernorm(DN_ALPHA * x_ref[r, :] + mix, g1_ref[...], b1_ref[...])
        s["x1b"] = s["x1"].astype(BF16)
        yield

    def mlp(s):
        acc = jnp.zeros(s["x1"].shape, F32)
        for f in range(D_FF // ff_chunk):
            hcol = jnp.dot(s["x1b"], wff1_ref[:, f * ff_chunk:(f + 1) * ff_chunk], preferred_element_type=F32)
            hcol = jnp.square(jnp.maximum(hcol, 0.0))
            yield
            acc = acc + _dot(hcol, wff2_ref[f * ff_chunk:(f + 1) * ff_chunk, :])
            yield
        s["acc"] = acc

    def tail(s):
        y_ref[s["rows"], :] = _layernorm(DN_ALPHA * s["x1"] + s["acc"], g2_ref[...], b2_ref[...])
        yield

    _interleave(head(st[0]))
    for i, s in enumerate(st):
        others = [head(st[i + 1])] if i + 1 < len(st) else []
        if i > 0:
            others.append(tail(st[i - 1]))
        _interleave(mlp(s), *others)
    _interleave(tail(st[-1]))


def _post(x, o, ys, u, *weights, tile):
    n = x.shape[0]
    tok = lambda w: pl.BlockSpec((tile, w), lambda i: (i, 0))

    def resident(a):
        return pl.BlockSpec(a.shape, lambda i: (0,) * a.ndim, pipeline_mode=pl.Buffered(1))

    return pl.pallas_call(
        functools.partial(_post_kernel, ff_chunk=1024),
        grid=(n // tile,),
        in_specs=[tok(D_MODEL), tok(GDN_WIDTH), tok(S5_WIDTH), tok(S5_WIDTH)] + [resident(a) for a in weights],
        out_specs=tok(D_MODEL),
        out_shape=jax.ShapeDtypeStruct((n, D_MODEL), F32),
        compiler_params=_compiler_params(("arbitrary",)),
        name="post",
    )(x, o, ys, u, *weights)


def kernel(x_prompt, x_sample, state_gdn, state_conv, state_ssm_re, state_ssm_im, w_in, w_conv, gdn_a_log,
           gdn_dt_bias, gdn_norm_w, s5_a_re, s5_a_im, s5_b_re, s5_b_im, s5_c_re, s5_c_im, s5_d, s5_log_dt,
           w_glu, b_glu, w_out, ln1_g, ln1_b, w_ff1, w_ff2, ln2_g, ln2_b):
    bsz, seq, _ = x_prompt.shape
    ns = x_sample.shape[0]
    l = 0

    w = w_in[l]
    i_u = QKVZ_W + 2 * GDN_HEADS
    w_cat = jnp.concatenate([w[:, :QKVZ_W], w[:, i_u:], w[:, QKVZ_W:i_u],
                             jnp.zeros((D_MODEL, GATE_LANES - 2 * GDN_HEADS), F32)], axis=1).astype(BF16)
    lane_pad = (GDN_HEADS, GATE_LANES - 2 * GDN_HEADS)
    alog_row = jnp.pad(gdn_a_log[l], lane_pad).reshape(1, GATE_LANES)
    dtb_row = jnp.pad(gdn_dt_bias[l], lane_pad).reshape(1, GATE_LANES)
    normw = gdn_norm_w[l].reshape(1, GDN_HEAD_DIM)
    wconv = w_conv[l]
    row = lambda a: a.reshape(1, -1)
    post_w = (row(s5_d[l]), w_glu[l].astype(BF16), row(b_glu[l]), w_out[l].astype(BF16), row(ln1_g[l]), row(ln1_b[l]),
              w_ff1[l].astype(BF16), w_ff2[l].astype(BF16), row(ln2_g[l]), row(ln2_b[l]))

    xp = x_prompt.reshape(bsz * seq, D_MODEL)
    xs = x_sample.reshape(ns, D_MODEL)
    qkvz_p, gates_p, u_p, tail_p = _proj_conv(xp, w_cat, wconv, seq, tile=min(512, seq))
    qkvz_s, gates_s, u_s = _proj(xs, w_cat, tile=ns)

    o_p, gdn_p = _gdn_pair(qkvz_p.reshape(bsz, seq, QKVZ_W), gates_p.reshape(bsz, seq, GATE_LANES),
                           alog_row, dtb_row, normw)
    o_p = o_p.reshape(bsz * seq, GDN_WIDTH)
    conv_p = tail_p[:, 8 - (CONV_W - 1):, :]

    conv_t = jnp.transpose(state_conv[l], (1, 0, 2))
    newconv_t, ops_s, sc = _sample_pre(qkvz_s, gates_s, conv_t, wconv, alog_row, dtb_row)
    gdn_s, o_s = _sample_state(state_gdn[l], ops_s, sc, qkvz_s[:, QKV_W:], normw)
    conv_s = jnp.transpose(newconv_t, (1, 0, 2))

    us_t = u_s.reshape(ns, S5_GROUPS, S5_GROUP).transpose(1, 0, 2)
    h0r_t = state_ssm_re[l].transpose(1, 0, 2)
    h0i_t = state_ssm_im[l].transpose(1, 0, 2)
    mt, gt, gts, wf, coef, ys_t, hsr, hsi = _s5_params(us_t, h0r_t, h0i_t, s5_a_re[l], s5_a_im[l], s5_log_dt[l],
                                                       s5_b_re[l], s5_b_im[l], s5_c_re[l], s5_c_im[l])
    coef_rows = coef.transpose(1, 0, 2).reshape(8, S5_GROUPS * LANE_TILE)
    ys_p, h_last = _s5_seq(u_p, mt, gt, gts, wf, coef_rows, bsz, seq)
    h_last = h_last.reshape(bsz, S5_GROUPS, 2 * S5_STATE)
    ys_s = ys_t.transpose(1, 0, 2).reshape(ns, S5_WIDTH)

    y_p = _post(xp, o_p, ys_p, u_p, *post_w, tile=512)
    y_s = _post(xs, o_s, ys_s, u_s, *post_w, tile=ns)

    t3 = lambda a: a.transpose(1, 0, 2)[None]
    return (y_p.reshape(bsz, seq, D_MODEL), y_s.reshape(ns, 1, D_MODEL),
            gdn_p[None], conv_p[None], h_last[None, :, :, :S5_STATE], h_last[None, :, :, S5_STATE:],
            gdn_s[None], conv_s[None], t3(hsr), t3(hsi))
```

```python
import functools

import jax
import jax.numpy as jnp
import numpy as np
from jax import lax
from jax.experimental import pallas as pl
from jax.experimental.pallas import tpu as pltpu

F32 = jnp.float32
BF16 = jnp.bfloat16

D_MODEL = 1024
GDN_HEADS = 4
GDN_HEAD_DIM = 128
GDN_WIDTH = GDN_HEADS * GDN_HEAD_DIM
CONV_W = 4
GDN_CHUNK = 64
S5_WIDTH = D_MODEL - GDN_WIDTH
S5_GROUP = 16
S5_GROUPS = S5_WIDTH // S5_GROUP
S5_STATE = 64
D_FF = 4 * D_MODEL
DEPTH = 1
DN_ALPHA = (2.0 * DEPTH) ** 0.25
NORM_EPS = 1e-6
QKV_W = 3 * GDN_WIDTH
QKVZ_W = QKV_W + GDN_WIDTH
LANE_TILE = 128
GATE_LANES = LANE_TILE
S5_CHUNK = 16
S5_FLAT = S5_CHUNK * S5_GROUP
VMEM_LIMIT = 56 * 1024 * 1024


def _dot(a, b):
    return jnp.dot(a.astype(BF16), b.astype(BF16), preferred_element_type=F32)


def _dot_nt(a, b):
    return lax.dot_general(a.astype(BF16), b.astype(BF16), (((1,), (1,)), ((), ())), preferred_element_type=F32)


def _dot_tn(a, b):
    return lax.dot_general(a.astype(BF16), b.astype(BF16), (((0,), (0,)), ((), ())), preferred_element_type=F32)


def _split(x):
    hi = x.astype(BF16)
    lo = (x - hi.astype(F32)).astype(BF16)
    return hi, lo


def _dot_hi(a, b, dims=(((1,), (0,)), ((), ()))):
    ah, al = _split(a)
    bh, bl = _split(b)
    d = functools.partial(lax.dot_general, dimension_numbers=dims, preferred_element_type=F32)
    return d(ah, bh) + (d(al, bh) + d(ah, bl))


def _dot_x3(a, b_exact):
    a1 = a.astype(BF16)
    r1 = a - a1.astype(F32)
    a2 = r1.astype(BF16)
    a3 = (r1 - a2.astype(F32)).astype(BF16)
    return _dot(a1, b_exact) + (_dot(a2, b_exact) + _dot(a3, b_exact))


def _dot_x3_left(a_exact, b):
    b1 = b.astype(BF16)
    r1 = b - b1.astype(F32)
    b2 = r1.astype(BF16)
    b3 = (r1 - b2.astype(F32)).astype(BF16)
    return _dot(a_exact, b1) + (_dot(a_exact, b2) + _dot(a_exact, b3))


def _complex_powers(base_r, base_i, exponent, nbits):
    shape = exponent.shape
    pr = jnp.ones(shape, F32)
    pi = jnp.zeros(shape, F32)
    br, bi = base_r, base_i
    for j in range(nbits):
        bit = ((exponent >> j) & 1) == 1
        pr, pi = jnp.where(bit, pr * br - pi * bi, pr), jnp.where(bit, pr * bi + pi * br, pi)
        br, bi = br * br - bi * bi, 2.0 * br * bi
    return pr, pi


def _silu(x):
    return x * jax.nn.sigmoid(x)


def _layernorm(x, g, b):
    mu = jnp.mean(x, axis=-1, keepdims=True)
    xc = x - mu
    var = jnp.mean(xc * xc, axis=-1, keepdims=True)
    return xc * lax.rsqrt(var + NORM_EPS) * g + b


def _l2norm(a):
    return a * lax.rsqrt(jnp.sum(a * a, axis=-1, keepdims=True) + NORM_EPS)


def _compiler_params(semantics):
    return pltpu.CompilerParams(dimension_semantics=semantics, vmem_limit_bytes=VMEM_LIMIT)


def _whole(shape):
    n = len(shape)
    return pl.BlockSpec(shape, lambda *_: (0,) * n)


def _interleave(*gens):
    live = list(gens)
    while live:
        for gen in list(live):
            if next(gen, "done") == "done":
                live.remove(gen)


def _proj_kernel(x_ref, w_ref, qkvz_ref, gates_ref, u_ref):
    r = jnp.dot(x_ref[...].astype(BF16), w_ref[...], preferred_element_type=F32)
    qkvz_ref[...] = r[:, 0:QKVZ_W]
    u_ref[...] = r[:, QKVZ_W:QKVZ_W + S5_WIDTH]
    gates_ref[...] = r[:, QKVZ_W + S5_WIDTH:]


PROJ_PARTS = 4


def _proj_conv_kernel(x_ref, w_ref, wconv_ref, qkvz_ref, gates_ref, u_ref, tail_ref, xpad_a_ref, xpad_b_ref, *,
                      tiles_per_seq):
    tile = x_ref.shape[0]
    dk = GDN_HEAD_DIM
    parts = PROJ_PARTS
    pr = tile // parts
    bufs = (xpad_a_ref, xpad_b_ref)
    last = bufs[(parts - 1) % 2]

    @pl.when(pl.program_id(0) % tiles_per_seq == 0)
    def _():
        last[pr:pr + 8, :] = jnp.zeros((8, QKV_W), F32)

    wc = wconv_ref[...]
    scale = GDN_HEAD_DIM ** -0.5

    def project(p):
        buf, prev = bufs[p % 2], bufs[(p - 1) % 2]
        rows = slice(p * pr, (p + 1) * pr)
        buf[0:8, :] = prev[pr:pr + 8, :]
        xb = x_ref[rows, :].astype(BF16)
        step = 2 * LANE_TILE
        for c0 in range(0, w_ref.shape[1], step):
            c1 = min(c0 + step, w_ref.shape[1])
            r = jnp.dot(xb, w_ref[:, c0:c1], preferred_element_type=F32)
            if c1 <= QKV_W:
                buf[8:8 + pr, c0:c1] = r
            elif c1 <= QKVZ_W:
                qkvz_ref[rows, c0:c1] = r
            elif c1 <= QKVZ_W + S5_WIDTH:
                u_ref[rows, c0 - QKVZ_W:c1 - QKVZ_W] = r
            else:
                gates_ref[rows, :] = r
            yield

    def activate(p):
        buf = bufs[p % 2]
        for lt in range(QKV_W // dk):
            cols = slice(lt * dk, (lt + 1) * dk)
            conv = buf[8:8 + pr, cols] * wc[3:4, cols]
            for j in range(1, CONV_W):
                conv = conv + buf[8 - j:8 - j + pr, cols] * wc[3 - j:4 - j, cols]
            act = _silu(conv)
            if lt < GDN_HEADS:
                act = _l2norm(act) * scale
            elif lt < 2 * GDN_HEADS:
                act = _l2norm(act)
            qkvz_ref[p * pr:(p + 1) * pr, cols] = act
            yield

    _interleave(project(0))
    for p in range(1, parts):
        _interleave(project(p), activate(p - 1))
    _interleave(activate(parts - 1))
    tail_ref[0] = last[pr:pr + 8, :]


def _proj(x, w, tile):
    n = x.shape[0]
    return pl.pallas_call(
        _proj_kernel,
        grid=(n // tile,),
        in_specs=[pl.BlockSpec((tile, D_MODEL), lambda i: (i, 0)),
                  pl.BlockSpec(w.shape, lambda i: (0, 0), pipeline_mode=pl.Buffered(1))],
        out_specs=[pl.BlockSpec((tile, QKVZ_W), lambda i: (i, 0)),
                   pl.BlockSpec((tile, GATE_LANES), lambda i: (i, 0)),
                   pl.BlockSpec((tile, S5_WIDTH), lambda i: (i, 0))],
        out_shape=[jax.ShapeDtypeStruct((n, QKVZ_W), F32),
                   jax.ShapeDtypeStruct((n, GATE_LANES), F32),
                   jax.ShapeDtypeStruct((n, S5_WIDTH), F32)],
        compiler_params=_compiler_params(("arbitrary",)),
        name="proj",
    )(x, w)


def _proj_conv(x, w, wconv, seq, tile):
    n = x.shape[0]
    tiles_per_seq = seq // tile
    assert PROJ_PARTS % 2 == 0 and tile % (8 * PROJ_PARTS) == 0 and seq % tile == 0
    return pl.pallas_call(
        functools.partial(_proj_conv_kernel, tiles_per_seq=tiles_per_seq),
        grid=(n // tile,),
        in_specs=[pl.BlockSpec((tile, D_MODEL), lambda i: (i, 0)),
                  pl.BlockSpec(w.shape, lambda i: (0, 0), pipeline_mode=pl.Buffered(1)),
                  _whole(wconv.shape)],
        out_specs=[pl.BlockSpec((tile, QKVZ_W), lambda i: (i, 0)),
                   pl.BlockSpec((tile, GATE_LANES), lambda i: (i, 0)),
                   pl.BlockSpec((tile, S5_WIDTH), lambda i: (i, 0)),
                   pl.BlockSpec((1, 8, QKV_W), lambda i: (i // tiles_per_seq, 0, 0))],
        out_shape=[jax.ShapeDtypeStruct((n, QKVZ_W), F32),
                   jax.ShapeDtypeStruct((n, GATE_LANES), F32),
                   jax.ShapeDtypeStruct((n, S5_WIDTH), F32),
                   jax.ShapeDtypeStruct((n // seq, 8, QKV_W), F32)],
        scratch_shapes=[pltpu.VMEM((8 + tile // PROJ_PARTS, QKV_W), F32)] * 2,
        compiler_params=_compiler_params(("arbitrary",)),
        name="proj_conv",
    )(x, w, wconv)


GDN_SEQS = 8
GDN_WAVES = 1
GDN_SUB_BLOCKS = (1, 2, 4, 8, 16, 32)
MASK_TRI, MASK_STRICT, MASK_EYE = len(GDN_SUB_BLOCKS), len(GDN_SUB_BLOCKS) + 1, len(GDN_SUB_BLOCKS) + 2
SEL_LOW, SEL_HIGH, SEL_SUB = 0, 1, 2


def _gate_values(logits, a_log, dt_bias):
    beta = jax.nn.sigmoid(logits)
    g = -jnp.exp(a_log) * jax.nn.softplus(logits + dt_bias)
    return beta, g


def _gdn_masks():
    c = GDN_CHUNK
    r = np.arange(c)[:, None]
    col = np.arange(2 * c)[None, :] % c
    sub = [(r // (2 * b) == col // (2 * b)) & ((r // b) % 2 == 1) & ((col // b) % 2 == 0) for b in GDN_SUB_BLOCKS]
    masks = np.stack(sub + [col <= r, col < r, col == r]).astype(np.float32)
    lane = np.broadcast_to(np.arange(2 * c)[None, :], (c, 2 * c))
    selectors = np.stack([lane < c, lane >= c] + sub).astype(np.float32)
    return jnp.asarray(masks), jnp.asarray(selectors, dtype=BF16)


def _gdn_pair_kernel(qkvz_ref, gates_ref, alog_row_ref, dtb_row_ref, normw_ref, masks_ref, sel_ref, o_ref, s_ref):
    step = pl.program_id(1)
    ns = qkvz_ref.shape[0]
    c = GDN_CHUNK
    dk = GDN_HEAD_DIM
    rows = ns * c

    @pl.when(step == 0)
    def _():
        s_ref[...] = jnp.zeros_like(s_ref)

    beta_c, g_c = _gate_values(gates_ref[...].reshape(rows, GATE_LANES), alog_row_ref[...], dtb_row_ref[...])

    ri = lax.broadcasted_iota(jnp.int32, (rows, rows), 0)
    ci = lax.broadcasted_iota(jnp.int32, (rows, rows), 1)
    tril_bd = jnp.where(((ri // c) == (ci // c)) & (ci <= ri), 1.0, 0.0).astype(BF16)
    gc_c = _dot_x3_left(tril_bd, g_c)
    gc_t = [gc_c[i * GATE_LANES:(i + 1) * GATE_LANES, :].T for i in range(rows // GATE_LANES)]
    gc_t_rolled = [pltpu.roll(t, c, axis=1) for t in gc_t]

    mask = lambda i: masks_ref[i] > 0.5
    low_half = lax.broadcasted_iota(jnp.int32, (1, 2 * c), 1) < c
    normw = normw_ref[...]
    zeros_k = jnp.zeros((c, dk), F32)

    def block_diag(x):
        return jnp.concatenate([x * sel_ref[SEL_LOW], x * sel_ref[SEL_HIGH]], axis=0)

    def pair_product(ph, pl_, yh, yl):
        bdh = block_diag(yh)
        first = jnp.dot(jnp.concatenate([ph, pl_], axis=1), jnp.concatenate([bdh, bdh], axis=0),
                        preferred_element_type=F32)
        return first + jnp.dot(ph, block_diag(yl), preferred_element_type=F32)

    def prep(seqs, pairs):
        for s in seqs:
            r0 = s * c
            lane0 = r0 % GATE_LANES
            blk = r0 // GATE_LANES
            in_place, moved = (gc_t[blk], gc_t_rolled[blk])
            for h0 in range(0, GDN_HEADS, 2):
                heads, a1 = [], None
                for side, h in enumerate((h0, h0 + 1)):
                    q = qkvz_ref[s, :, h * dk:(h + 1) * dk]
                    k = qkvz_ref[s, :, GDN_WIDTH + h * dk:GDN_WIDTH + (h + 1) * dk]
                    v = qkvz_ref[s, :, 2 * GDN_WIDTH + h * dk:2 * GDN_WIDTH + (h + 1) * dk]
                    beta = beta_c[r0:r0 + c, h:h + 1]
                    gcol = gc_c[r0:r0 + c, 4 + h:5 + h]
                    eg = jnp.exp(gcol)
                    g_last = gcol[c - 1:c, :]
                    kb = k * beta
                    k_pad = jnp.concatenate([k, zeros_k] if side == 0 else [zeros_k, k], axis=0)
                    part = _dot_nt(jnp.concatenate([kb, q], axis=0), k_pad)
                    a1 = part if a1 is None else a1 + part
                    src = in_place if (lane0 == 0) == (side == 0) else moved
                    heads.append(dict(s=s, h=h, gcol=gcol, grow=src[4 + h:5 + h, :],
                                      qg=q * eg, kd=k * jnp.exp(g_last - gcol), eg_last=jnp.exp(g_last),
                                      rhs=jnp.concatenate([v * beta, kb * eg], axis=1)))
                ha, hb = heads
                diff = jnp.where(low_half, ha["gcol"], hb["gcol"]) - jnp.where(low_half, ha["grow"], hb["grow"])
                decay = jnp.where(mask(MASK_TRI), jnp.exp(jnp.where(mask(MASK_TRI), diff, 0.0)), 0.0)
                pairs.append(dict(heads=heads,
                                  qk=jnp.where(mask(MASK_TRI), a1[c:2 * c] * decay, 0.0),
                                  lmat=jnp.where(mask(MASK_STRICT), a1[0:c] * decay, 0.0)))
                yield

    def solve(pairs):
        for p in pairs:
            p["d"] = masks_ref[MASK_EYE] - p["lmat"] * masks_ref[0]
            p["lh"], p["ll"] = _split(p["lmat"])
        for lvl in range(1, len(GDN_SUB_BLOCKS)):
            off = sel_ref[SEL_SUB + lvl]
            for p in pairs:
                p["dh"], p["dl"] = _split(p["d"])
                p["y"] = pair_product(p["lh"] * off, p["ll"] * off, p["dh"], p["dl"])
            yield
            for p in pairs:
                yh, yl = _split(p["y"])
                p["d"] = p["d"] - pair_product(p["dh"], p["dl"], yh, yl)
            yield
        for p in pairs:
            dh, dl = _split(p["d"])
            (ra_h, ra_l), (rb_h, rb_l) = (_split(hd["rhs"]) for hd in p["heads"])
            zr = jnp.zeros((c, 2 * dk), BF16)
            bd_h = jnp.concatenate([jnp.concatenate([ra_h, zr], axis=1), jnp.concatenate([zr, rb_h], axis=1)], axis=0)
            bd_l = jnp.concatenate([jnp.concatenate([ra_l, zr], axis=1), jnp.concatenate([zr, rb_l], axis=1)], axis=0)
            sol = (jnp.dot(jnp.concatenate([dh, dl], axis=1), jnp.concatenate([bd_h, bd_h], axis=0),
                           preferred_element_type=F32)
                   + jnp.dot(dh, bd_l, preferred_element_type=F32))
            for side, hd in enumerate(p["heads"]):
                hd["sol"] = sol[:, side * 2 * dk:(side + 1) * 2 * dk]
        yield

    def advance(pairs):
        for p in pairs:
            for hd in p["heads"]:
                hd["s_old"] = s_ref[hd["s"], hd["h"]]
                hd["m1"] = _dot(jnp.concatenate([hd["sol"][:, dk:2 * dk], hd["qg"]], axis=0), hd["s_old"])
        yield
        for p in pairs:
            for hd in p["heads"]:
                hd["v_new"] = hd["sol"][:, 0:dk] - hd["m1"][0:c]
                s_ref[hd["s"], hd["h"]] = hd["s_old"] * hd["eg_last"] + _dot_tn(hd["kd"], hd["v_new"])
        yield
        zv = jnp.zeros((c, dk), F32)
        for p in pairs:
            ha, hb = p["heads"]
            v_bd = jnp.concatenate([jnp.concatenate([ha["v_new"], zv], axis=1),
                                    jnp.concatenate([zv, hb["v_new"]], axis=1)], axis=0)
            o_pair = _dot(p["qk"], v_bd)
            for side, hd in enumerate(p["heads"]):
                s, h = hd["s"], hd["h"]
                o = hd["m1"][c:2 * c] + o_pair[:, side * dk:(side + 1) * dk]
                z = qkvz_ref[s, :, QKV_W + h * dk:QKV_W + (h + 1) * dk]
                o = o * lax.rsqrt(jnp.mean(o * o, axis=-1, keepdims=True) + NORM_EPS) * normw * _silu(z)
                o_ref[s, :, h * dk:(h + 1) * dk] = o
        yield

    half = max(ns // 2, 1) if GDN_WAVES == 2 else ns
    wave_a, wave_b = [], []
    _interleave(prep(range(0, half), wave_a))
    _interleave(solve(wave_a), prep(range(half, ns), wave_b))
    _interleave(solve(wave_b), advance(wave_a))
    _interleave(advance(wave_b))


def _gdn_pair(qkvz3, gates3, alog_row, dtb_row, normw):
    bsz, seq, _ = qkvz3.shape
    ns, c = min(GDN_SEQS, bsz), GDN_CHUNK
    hd = (GDN_HEADS, GDN_HEAD_DIM, GDN_HEAD_DIM)
    masks, selectors = _gdn_masks()
    return pl.pallas_call(
        _gdn_pair_kernel,
        grid=(bsz // ns, seq // c),
        in_specs=[pl.BlockSpec((ns, c, QKVZ_W), lambda b, i: (b, i, 0)),
                  pl.BlockSpec((ns, c, GATE_LANES), lambda b, i: (b, i, 0)),
                  _whole(alog_row.shape), _whole(dtb_row.shape), _whole(normw.shape),
                  _whole(masks.shape), _whole(selectors.shape)],
        out_specs=[pl.BlockSpec((ns, c, GDN_WIDTH), lambda b, i: (b, i, 0)),
                   pl.BlockSpec((ns,) + hd, lambda b, i: (b, 0, 0, 0))],
        out_shape=[jax.ShapeDtypeStruct((bsz, seq, GDN_WIDTH), F32),
                   jax.ShapeDtypeStruct((bsz,) + hd, F32)],
        compiler_params=_compiler_params(("arbitrary", "arbitrary")),
        name="gdn_prompt",
    )(qkvz3, gates3, alog_row, dtb_row, normw, masks, selectors)


def _s5_params_kernel(us_ref, h0r_ref, h0i_ref, ar_row_ref, ai_row_ref, ar_col_ref, ai_col_ref, ldt_ref,
                      br_ref, bi_ref, cr_ref, ci_ref,
                      mt_ref, gt_ref, gts_ref, wf_ref, coef_ref, ys_ref, hsr_ref, hsi_ref):
    L = S5_CHUNK
    P = S5_STATE
    dt = jnp.exp(ldt_ref[0])
    ar_row = ar_row_ref[0] * dt
    ai_row = ai_row_ref[0] * dt
    ar_col = ar_col_ref[0] * dt
    ai_col = ai_col_ref[0] * dt
    first = lax.broadcasted_iota(jnp.int32, (1, 2 * P), 1) < P

    ea = jnp.exp(ar_col)
    lbr = ea * jnp.cos(ai_col)
    lbi = ea * jnp.sin(ai_col)
    lam_r = ar_col_ref[0]
    lam_i = ai_col_ref[0]
    den = lam_r * lam_r + lam_i * lam_i
    fr = ((lbr - 1.0) * lam_r + lbi * lam_i) / den
    fi = (lbi * lam_r - (lbr - 1.0) * lam_i) / den
    b_r = br_ref[0]
    b_i = bi_ref[0]
    bbr = fr * b_r - fi * b_i
    bbi = fr * b_i + fi * b_r

    rt = lax.broadcasted_iota(jnp.int32, (S5_GROUP, S5_FLAT), 0)
    lt = lax.broadcasted_iota(jnp.int32, (S5_GROUP, S5_FLAT), 1)
    tile_mat = jnp.where(lt % S5_GROUP == rt, 1.0, 0.0).astype(BF16)
    bwr = _dot_x3(bbr, tile_mat)
    bwi = _dot_x3(bbi, tile_mat)

    tau_g = (L - 1) - lax.broadcasted_iota(jnp.int32, (P, S5_FLAT), 1) // S5_GROUP
    pgr, pgi = _complex_powers(lbr, lbi, tau_g, 4)
    gtr = pgr * bwr - pgi * bwi
    gti = pgr * bwi + pgi * bwr
    gt_ref[0] = jnp.concatenate([gtr, gti], axis=0).astype(BF16)
    gts_ref[0] = jnp.concatenate([gti, gtr], axis=0).astype(BF16)

    nt = L + 1
    e1 = jnp.exp(ar_row)
    l1r = e1 * jnp.cos(ai_row)
    l1i = e1 * jnp.sin(ai_row)
    tau_w = lax.broadcasted_iota(jnp.int32, (nt * S5_GROUP, 2 * P), 0) // S5_GROUP
    pwr, pwi = _complex_powers(l1r, l1i, tau_w, 5)
    c_r = jnp.concatenate([cr_ref[0]] * nt, axis=0)
    c_i = jnp.concatenate([ci_ref[0]] * nt, axis=0)
    wall = jnp.where(first, c_r * pwr - c_i * pwi, -(c_r * pwi + c_i * pwr))
    wf_ref[0] = wall[S5_GROUP:].astype(BF16)

    zw = _dot_hi(wall[0:S5_FLAT], jnp.concatenate([bwr, bwi], axis=0))
    s_of_lane = lax.broadcasted_iota(jnp.int32, (S5_FLAT, S5_FLAT), 1) // S5_GROUP
    mt = zw
    for j in range(4):
        sh = S5_GROUP << j
        shifted = jnp.concatenate([jnp.zeros((sh, S5_FLAT), F32), mt[0:S5_FLAT - sh]], axis=0)
        mt = jnp.where(((s_of_lane >> j) & 1) == 1, shifted, mt)
    mt_ref[0] = mt.astype(BF16)

    p16r = pwr[L * S5_GROUP:L * S5_GROUP + 1, :]
    p16i = pwi[L * S5_GROUP:L * S5_GROUP + 1, :]
    coef_b = jnp.where(first, -p16i, p16i)
    srow = lax.broadcasted_iota(jnp.int32, (8, 2 * P), 0)
    coef_ref[0] = jnp.where(srow == 0, p16r, jnp.where(srow == 1, coef_b, jnp.where(srow == 2, -coef_b, 0.0)))

    l1r = l1r[:, 0:P]
    l1i = l1i[:, 0:P]
    us = us_ref[0]
    nt_dims = (((1,), (1,)), ((), ()))
    bur = _dot_hi(us, bbr, nt_dims)
    bui = _dot_hi(us, bbi, nt_dims)
    h0r = h0r_ref[0]
    h0i = h0i_ref[0]
    hsr = l1r * h0r - l1i * h0i + bur
    hsi = l1r * h0i + l1i * h0r + bui
    hsr_ref[0] = hsr
    hsi_ref[0] = hsi
    ys_ref[0] = _dot_nt(hsr, cr_ref[0][:, 0:P]) - _dot_nt(hsi, ci_ref[0][:, 0:P])


def _s5_params(us_t, h0r_t, h0i_t, ar, ai, ldt, b_r, b_i, c_r, c_i):
    g = S5_GROUPS
    ns = us_t.shape[1]
    twice = lambda a: jnp.concatenate([a, a], axis=-1)
    ar_row = twice(ar).reshape(g, 1, 2 * S5_STATE)
    ai_row = twice(ai).reshape(g, 1, 2 * S5_STATE)
    ar_col = ar.reshape(g, S5_STATE, 1)
    ai_col = ai.reshape(g, S5_STATE, 1)
    ldt3 = ldt.reshape(g, 1, 1)

    def blk(shape):
        return pl.BlockSpec((1,) + shape, lambda i: (i,) + (0,) * len(shape))

    return pl.pallas_call(
        _s5_params_kernel,
        grid=(g,),
        in_specs=[blk((ns, S5_GROUP)), blk((ns, S5_STATE)), blk((ns, S5_STATE)),
                  blk((1, 2 * S5_STATE)), blk((1, 2 * S5_STATE)), blk((S5_STATE, 1)), blk((S5_STATE, 1)), blk((1, 1)),
                  blk((S5_STATE, S5_GROUP)), blk((S5_STATE, S5_GROUP)),
                  blk((S5_GROUP, 2 * S5_STATE)), blk((S5_GROUP, 2 * S5_STATE))],
        out_specs=[blk((S5_FLAT, S5_FLAT)), blk((2 * S5_STATE, S5_FLAT)), blk((2 * S5_STATE, S5_FLAT)),
                   blk((S5_FLAT, 2 * S5_STATE)), blk((8, 2 * S5_STATE)),
                   blk((ns, S5_GROUP)), blk((ns, S5_STATE)), blk((ns, S5_STATE))],
        out_shape=[jax.ShapeDtypeStruct((g, S5_FLAT, S5_FLAT), BF16),
                   jax.ShapeDtypeStruct((g, 2 * S5_STATE, S5_FLAT), BF16),
                   jax.ShapeDtypeStruct((g, 2 * S5_STATE, S5_FLAT), BF16),
                   jax.ShapeDtypeStruct((g, S5_FLAT, 2 * S5_STATE), BF16),
                   jax.ShapeDtypeStruct((g, 8, 2 * S5_STATE), F32),
                   jax.ShapeDtypeStruct((g, ns, S5_GROUP), F32),
                   jax.ShapeDtypeStruct((g, ns, S5_STATE), F32),
                   jax.ShapeDtypeStruct((g, ns, S5_STATE), F32)],
        compiler_params=_compiler_params(("arbitrary",)),
        name="s5_params",
    )(us_t, h0r_t, h0i_t, ar_row, ai_row, ar_col, ai_col, ldt3, b_r, b_i, twice(c_r), twice(c_i))


PIECES = LANE_TILE // S5_GROUP
RELAYOUT_ROWS = S5_CHUNK * S5_CHUNK


def _piece_transpose(tiles):
    tiles = list(tiles)
    piece = lax.broadcasted_iota(jnp.int32, tiles[0].shape, 1) // S5_GROUP
    d = PIECES // 2
    while d:
        keep_low = (piece & d) == 0
        for k in range(PIECES):
            if k & d:
                continue
            a, b = tiles[k], tiles[k + d]
            tiles[k] = jnp.where(keep_low, a, pltpu.roll(b, S5_GROUP * d, axis=1))
            tiles[k + d] = jnp.where(keep_low, pltpu.roll(a, LANE_TILE - S5_GROUP * d, axis=1), b)
        d //= 2
    return tiles


def _s5_seq_kernel(u_ref, mt_ref, gt_ref, gts_ref, wf_ref, coef_ref, ys_ref, hl_ref,
                   uflat_ref, e_ref, es_ref, hin_ref, yflat_ref, *, nchunk):
    rr = RELAYOUT_ROWS
    ro = lax.broadcasted_iota(jnp.int32, (rr, rr), 0)
    ri = lax.broadcasted_iota(jnp.int32, (rr, rr), 1)
    perm = jnp.where((ro // S5_CHUNK == ri % S5_CHUNK) & (ro % S5_CHUNK == ri // S5_CHUNK), 1.0, 0.0).astype(BF16)

    def gather_in(m, carry):
        rows = pl.ds(pl.multiple_of(m * rr, rr), rr)
        xp = jnp.dot(perm, u_ref[rows, :].astype(BF16), preferred_element_type=F32)
        crow = pl.ds(pl.multiple_of(m * S5_CHUNK, S5_CHUNK), S5_CHUNK)
        for j in range(S5_FLAT // LANE_TILE):
            for cb in range(S5_WIDTH // LANE_TILE):
                tiles = [xp[(PIECES * j + k) * S5_CHUNK:(PIECES * j + k + 1) * S5_CHUNK,
                            cb * LANE_TILE:(cb + 1) * LANE_TILE] for k in range(PIECES)]
                for p, tile in enumerate(_piece_transpose(tiles)):
                    uflat_ref[PIECES * cb + p, crow, j * LANE_TILE:(j + 1) * LANE_TILE] = tile.astype(BF16)
        return carry

    lax.fori_loop(0, nchunk // S5_CHUNK, gather_in, 0, unroll=2)

    for g in range(S5_GROUPS):
        lanes = slice(g * LANE_TILE, (g + 1) * LANE_TILE)
        ug = uflat_ref[g]
        e_ref[:, lanes] = _dot_nt(ug, gt_ref[g])
        es_ref[:, lanes] = _dot_nt(ug, gts_ref[g])

    ca = coef_ref[0:1, :]
    cb = coef_ref[1:2, :]
    cbs = coef_ref[2:3, :]

    def scan(c, carry):
        h, hs = carry
        row = pl.ds(c, 1)
        hin_ref[row, :] = h
        return ca * h + cb * hs + e_ref[row, :], ca * hs + cbs * h + es_ref[row, :]

    zero = jnp.zeros((1, S5_GROUPS * LANE_TILE), F32)
    h_last, _ = lax.fori_loop(0, nchunk, scan, (zero, zero))
    hl_ref[0] = h_last

    for g in range(S5_GROUPS):
        lanes = slice(g * LANE_TILE, (g + 1) * LANE_TILE)
        yflat_ref[g] = _dot_nt(uflat_ref[g], mt_ref[g]) + _dot_nt(hin_ref[:, lanes], wf_ref[g])

    def gather_out(m, carry):
        crow = pl.ds(pl.multiple_of(m * S5_CHUNK, S5_CHUNK), S5_CHUNK)
        by_time = [[None] * (S5_WIDTH // LANE_TILE) for _ in range(S5_CHUNK)]
        for j in range(S5_FLAT // LANE_TILE):
            for cb in range(S5_WIDTH // LANE_TILE):
                tiles = [yflat_ref[PIECES * cb + k, crow, j * LANE_TILE:(j + 1) * LANE_TILE] for k in range(PIECES)]
                for p, tile in enumerate(_piece_transpose(tiles)):
                    by_time[PIECES * j + p][cb] = tile
        z = jnp.concatenate([jnp.concatenate(row, axis=1) for row in by_time], axis=0)
        rows = pl.ds(pl.multiple_of(m * rr, rr), rr)
        ys_ref[rows, :] = _dot_x3_left(perm, z)
        return carry

    lax.fori_loop(0, nchunk // S5_CHUNK, gather_out, 0, unroll=2)


def _s5_seq(u, mt, gt, gts, wf, coef, bsz, seq):
    nchunk = seq // S5_CHUNK
    width = S5_GROUPS * LANE_TILE

    def resident(a):
        return pl.BlockSpec(a.shape, lambda b: (0,) * a.ndim, pipeline_mode=pl.Buffered(1))

    return pl.pallas_call(
        functools.partial(_s5_seq_kernel, nchunk=nchunk),
        grid=(bsz,),
        in_specs=[pl.BlockSpec((seq, S5_WIDTH), lambda b: (b, 0))] + [resident(a) for a in (mt, gt, gts, wf, coef)],
        out_specs=[pl.BlockSpec((seq, S5_WIDTH), lambda b: (b, 0)),
                   pl.BlockSpec((1, 1, width), lambda b: (b, 0, 0))],
        out_shape=[jax.ShapeDtypeStruct((bsz * seq, S5_WIDTH), F32),
                   jax.ShapeDtypeStruct((bsz, 1, width), F32)],
        scratch_shapes=[pltpu.VMEM((S5_GROUPS, nchunk, S5_FLAT), BF16),
                        pltpu.VMEM((nchunk, width), F32), pltpu.VMEM((nchunk, width), F32),
                        pltpu.VMEM((nchunk, width), F32),
                        pltpu.VMEM((S5_GROUPS, nchunk, S5_FLAT), F32)],
        compiler_params=_compiler_params(("arbitrary",)),
        name="s5_seq",
    )(u, mt, gt, gts, wf, coef)


def _sample_pre_kernel(qkvz_ref, gates_ref, conv_ref, wconv_ref, alog_row_ref, dtb_row_ref,
                       newconv_ref, ops_ref, sc_ref):
    dk = GDN_HEAD_DIM
    x_new = qkvz_ref[:, 0:QKV_W]
    wc = wconv_ref[...]
    conv = x_new * wc[3:4, :]
    for j in range(CONV_W - 1):
        conv = conv + conv_ref[j] * wc[j:j + 1, :]
    qkv = _silu(conv)
    newconv_ref[0] = conv_ref[1]
    newconv_ref[1] = conv_ref[2]
    newconv_ref[2] = x_new

    beta_c, g_c = _gate_values(gates_ref[...], alog_row_ref[...], dtb_row_ref[...])
    scale = GDN_HEAD_DIM ** -0.5
    lane = lax.broadcasted_iota(jnp.int32, sc_ref.shape, 1)
    sc = jnp.zeros(sc_ref.shape, F32)
    for h in range(GDN_HEADS):
        q = _l2norm(qkv[:, h * dk:(h + 1) * dk]) * scale
        k = _l2norm(qkv[:, GDN_WIDTH + h * dk:GDN_WIDTH + (h + 1) * dk])
        v = qkv[:, 2 * GDN_WIDTH + h * dk:2 * GDN_WIDTH + (h + 1) * dk]
        beta = beta_c[:, h:h + 1]
        eg = jnp.exp(g_c[:, 4 + h:5 + h])
        cols = slice(h * dk, (h + 1) * dk)
        ops_ref[0, :, cols] = (beta * eg) * k
        ops_ref[1, :, cols] = q * eg
        ops_ref[2, :, cols] = k
        ops_ref[3, :, cols] = beta * v
        sc = jnp.where(lane == h, jnp.sum(q * k, axis=-1, keepdims=True), sc)
        sc = jnp.where(lane == 4 + h, eg, sc)
    sc_ref[...] = sc


def _sample_pre(qkvz_s, gates_s, conv_t, wconv, alog_row, dtb_row):
    ns = qkvz_s.shape[0]
    return pl.pallas_call(
        _sample_pre_kernel,
        out_shape=[jax.ShapeDtypeStruct((CONV_W - 1, ns, QKV_W), F32),
                   jax.ShapeDtypeStruct((4, ns, GDN_WIDTH), F32),
                   jax.ShapeDtypeStruct((ns, GATE_LANES), F32)],
        compiler_params=pltpu.CompilerParams(vmem_limit_bytes=VMEM_LIMIT),
        name="sample_pre",
    )(qkvz_s, gates_s, conv_t, wconv, alog_row, dtb_row)


def _sample_state_kernel(s_ref, ops_ref, sc_ref, z_ref, normw_ref, snew_ref, o_ref, *, nb):
    dk = GDN_HEAD_DIM
    row = lax.broadcasted_iota(jnp.int32, (8, dk), 0)
    units = [(j, h) for j in range(nb) for h in range(GDN_HEADS)]
    m1 = {}
    for j, h in units:
        cols = slice(h * dk, (h + 1) * dk)
        w = ops_ref[0, j:j + 1, cols]
        qg = ops_ref[1, j:j + 1, cols]
        lhs = jnp.where(row == 0, jnp.broadcast_to(w, (8, dk)), jnp.broadcast_to(qg, (8, dk)))
        m1[j, h] = _dot(lhs, s_ref[j, h])
    for j, h in units:
        cols = slice(h * dk, (h + 1) * dk)
        k = ops_ref[2, j:j + 1, cols]
        u = ops_ref[3, j:j + 1, cols]
        v_new = u - m1[j, h][0:1, :]
        qk = sc_ref[j:j + 1, h:h + 1]
        eg = sc_ref[j:j + 1, 4 + h:5 + h]
        o_ref[j:j + 1, cols] = m1[j, h][1:2, :] + qk * v_new
        k8 = jnp.where(row == 0, jnp.broadcast_to(k, (8, dk)), 0.0)
        snew_ref[j, h] = s_ref[j, h] * eg + _dot_tn(k8, jnp.broadcast_to(v_new, (8, dk)))
    normw = normw_ref[...]
    for h in range(GDN_HEADS):
        o = o_ref[:, h * dk:(h + 1) * dk]
        z = z_ref[:, h * dk:(h + 1) * dk]
        o_ref[:, h * dk:(h + 1) * dk] = (o * lax.rsqrt(jnp.mean(o * o, axis=-1, keepdims=True) + NORM_EPS)
                                         * normw * _silu(z))


def _sample_state(state, ops, sc, z, normw, nb=8):
    ns = state.shape[0]
    hd = (GDN_HEADS, GDN_HEAD_DIM, GDN_HEAD_DIM)
    return pl.pallas_call(
        functools.partial(_sample_state_kernel, nb=nb),
        grid=(ns // nb,),
        in_specs=[pl.BlockSpec((nb,) + hd, lambda i: (i, 0, 0, 0)),
                  pl.BlockSpec((4, nb, GDN_WIDTH), lambda i: (0, i, 0)),
                  pl.BlockSpec((nb, GATE_LANES), lambda i: (i, 0)),
                  pl.BlockSpec((nb, GDN_WIDTH), lambda i: (i, 0)),
                  _whole(normw.shape)],
        out_specs=[pl.BlockSpec((nb,) + hd, lambda i: (i, 0, 0, 0)),
                   pl.BlockSpec((nb, GDN_WIDTH), lambda i: (i, 0))],
        out_shape=[jax.ShapeDtypeStruct(state.shape, F32),
                   jax.ShapeDtypeStruct((ns, GDN_WIDTH), F32)],
        compiler_params=_compiler_params(("arbitrary",)),
        name="sample_state",
    )(state, ops, sc, z, normw)


def _post_kernel(x_ref, o_ref, ys_ref, u_ref, d_ref, wglu_ref, bglu_ref, wout_ref, g1_ref, b1_ref,
                 wff1_ref, wff2_ref, g2_ref, b2_ref, y_ref, *, ff_chunk):
    n = x_ref.shape[0]
    halves = [slice(0, n // 2), slice(n // 2, n)] if n >= 256 else [slice(0, n)]
    st = [dict(rows=r) for r in halves]

    def head(s):
        r = s["rows"]
        ys = jax.nn.gelu(ys_ref[r, :] + d_ref[...] * u_ref[r, :])
        ys = ys * jax.nn.sigmoid(_dot(ys, wglu_ref[...]) + bglu_ref[...])
        yield
        mix = _dot(o_ref[r, :], wout_ref[0:GDN_WIDTH, :]) + _dot(ys, wout_ref[GDN_WIDTH:, :])
        yield
        s["x1"] = _layernorm(DN_ALPHA * x_ref[r, :] + mix, g1_ref[...], b1_ref[...])
        s["x1b"] = s["x1"].astype(BF16)
        yield

    def mlp(s):
        acc = jnp.zeros(s["x1"].shape, F32)
        for f in range(D_FF // ff_chunk):
            hcol = jnp.dot(s["x1b"], wff1_ref[:, f * ff_chunk:(f + 1) * ff_chunk], preferred_element_type=F32)
            hcol = jnp.square(jnp.maximum(hcol, 0.0))
            yield
            acc = acc + _dot(hcol, wff2_ref[f * ff_chunk:(f + 1) * ff_chunk, :])
            yield
        s["acc"] = acc

    def tail(s):
        y_ref[s["rows"], :] = _layernorm(DN_ALPHA * s["x1"] + s["acc"], g2_ref[...], b2_ref[...])
        yield

    _interleave(head(st[0]))
    for i, s in enumerate(st):
        others = [head(st[i + 1])] if i + 1 < len(st) else []
        if i > 0:
            others.append(tail(st[i - 1]))
        _interleave(mlp(s), *others)
    _interleave(tail(st[-1]))


def _post(x, o, ys, u, *weights, tile):
    n = x.shape[0]
    tok = lambda w: pl.BlockSpec((tile, w), lambda i: (i, 0))

    def resident(a):
        return pl.BlockSpec(a.shape, lambda i: (0,) * a.ndim, pipeline_mode=pl.Buffered(1))

    return pl.pallas_call(
        functools.partial(_post_kernel, ff_chunk=1024),
        grid=(n // tile,),
        in_specs=[tok(D_MODEL), tok(GDN_WIDTH), tok(S5_WIDTH), tok(S5_WIDTH)] + [resident(a) for a in weights],
        out_specs=tok(D_MODEL),
        out_shape=jax.ShapeDtypeStruct((n, D_MODEL), F32),
        compiler_params=_compiler_params(("arbitrary",)),
        name="post",
    )(x, o, ys, u, *weights)


def kernel(x_prompt, x_sample, state_gdn, state_conv, state_ssm_re, state_ssm_im, w_in, w_conv, gdn_a_log,
           gdn_dt_bias, gdn_norm_w, s5_a_re, s5_a_im, s5_b_re, s5_b_im, s5_c_re, s5_c_im, s5_d, s5_log_dt,
           w_glu, b_glu, w_out, ln1_g, ln1_b, w_ff1, w_ff2, ln2_g, ln2_b):
    bsz, seq, _ = x_prompt.shape
    ns = x_sample.shape[0]
    l = 0

    w = w_in[l]
    i_u = QKVZ_W + 2 * GDN_HEADS
    w_cat = jnp.concatenate([w[:, :QKVZ_W], w[:, i_u:], w[:, QKVZ_W:i_u],
                             jnp.zeros((D_MODEL, GATE_LANES - 2 * GDN_HEADS), F32)], axis=1).astype(BF16)
    lane_pad = (GDN_HEADS, GATE_LANES - 2 * GDN_HEADS)
    alog_row = jnp.pad(gdn_a_log[l], lane_pad).reshape(1, GATE_LANES)
    dtb_row = jnp.pad(gdn_dt_bias[l], lane_pad).reshape(1, GATE_LANES)
    normw = gdn_norm_w[l].reshape(1, GDN_HEAD_DIM)
    wconv = w_conv[l]
    row = lambda a: a.reshape(1, -1)
    post_w = (row(s5_d[l]), w_glu[l].astype(BF16), row(b_glu[l]), w_out[l].astype(BF16), row(ln1_g[l]), row(ln1_b[l]),
              w_ff1[l].astype(BF16), w_ff2[l].astype(BF16), row(ln2_g[l]), row(ln2_b[l]))

    xp = x_prompt.reshape(bsz * seq, D_MODEL)
    xs = x_sample.reshape(ns, D_MODEL)
    qkvz_p, gates_p, u_p, tail_p = _proj_conv(xp, w_cat, wconv, seq, tile=min(512, seq))
    qkvz_s, gates_s, u_s = _proj(xs, w_cat, tile=ns)

    o_p, gdn_p = _gdn_pair(qkvz_p.reshape(bsz, seq, QKVZ_W), gates_p.reshape(bsz, seq, GATE_LANES),
                           alog_row, dtb_row, normw)
    o_p = o_p.reshape(bsz * seq, GDN_WIDTH)
    conv_p = tail_p[:, 8 - (CONV_W - 1):, :]

    conv_t = jnp.transpose(state_conv[l], (1, 0, 2))
    newconv_t, ops_s, sc = _sample_pre(qkvz_s, gates_s, conv_t, wconv, alog_row, dtb_row)
    gdn_s, o_s = _sample_state(state_gdn[l], ops_s, sc, qkvz_s[:, QKV_W:], normw)
    conv_s = jnp.transpose(newconv_t, (1, 0, 2))

    us_t = u_s.reshape(ns, S5_GROUPS, S5_GROUP).transpose(1, 0, 2)
    h0r_t = state_ssm_re[l].transpose(1, 0, 2)
    h0i_t = state_ssm_im[l].transpose(1, 0, 2)
    mt, gt, gts, wf, coef, ys_t, hsr, hsi = _s5_params(us_t, h0r_t, h0i_t, s5_a_re[l], s5_a_im[l], s5_log_dt[l],
                                                       s5_b_re[l], s5_b_im[l], s5_c_re[l], s5_c_im[l])
    coef_rows = coef.transpose(1, 0, 2).reshape(8, S5_GROUPS * LANE_TILE)
    ys_p, h_last = _s5_seq(u_p, mt, gt, gts, wf, coef_rows, bsz, seq)
    h_last = h_last.reshape(bsz, S5_GROUPS, 2 * S5_STATE)
    ys_s = ys_t.transpose(1, 0, 2).reshape(ns, S5_WIDTH)

    y_p = _post(xp, o_p, ys_p, u_p, *post_w, tile=512)
    y_s = _post(xs, o_s, ys_s, u_s, *post_w, tile=ns)

    t3 = lambda a: a.transpose(1, 0, 2)[None]
    return (y_p.reshape(bsz, seq, D_MODEL), y_s.reshape(ns, 1, D_MODEL),
            gdn_p[None], conv_p[None], h_last[None, :, :, :S5_STATE], h_last[None, :, :, S5_STATE:],
            gdn_s[None], conv_s[None], t3(hsr), t3(hsi))
```

```python
import functools

import jax
import jax.numpy as jnp
import numpy as np
from jax import lax
from jax.experimental import pallas as pl
from jax.experimental.pallas import tpu as pltpu

F32 = jnp.float32
BF16 = jnp.bfloat16

D_MODEL = 1024
GDN_HEADS = 4
GDN_HEAD_DIM = 128
GDN_WIDTH = GDN_HEADS * GDN_HEAD_DIM
CONV_W = 4
GDN_CHUNK = 64
S5_WIDTH = D_MODEL - GDN_WIDTH
S5_GROUP = 16
S5_GROUPS = S5_WIDTH // S5_GROUP
S5_STATE = 64
D_FF = 4 * D_MODEL
DEPTH = 1
DN_ALPHA = (2.0 * DEPTH) ** 0.25
NORM_EPS = 1e-6
QKV_W = 3 * GDN_WIDTH
QKVZ_W = QKV_W + GDN_WIDTH
LANE_TILE = 128
GATE_LANES = LANE_TILE
S5_CHUNK = 16
S5_FLAT = S5_CHUNK * S5_GROUP
VMEM_LIMIT = 56 * 1024 * 1024


def _dot(a, b):
    return jnp.dot(a.astype(BF16), b.astype(BF16), preferred_element_type=F32)


def _dot_nt(a, b):
    return lax.dot_general(a.astype(BF16), b.astype(BF16), (((1,), (1,)), ((), ())), preferred_element_type=F32)


def _dot_tn(a, b):
    return lax.dot_general(a.astype(BF16), b.astype(BF16), (((0,), (0,)), ((), ())), preferred_element_type=F32)


def _split(x):
    hi = x.astype(BF16)
    lo = (x - hi.astype(F32)).astype(BF16)
    return hi, lo


def _dot_hi(a, b, dims=(((1,), (0,)), ((), ()))):
    ah, al = _split(a)
    bh, bl = _split(b)
    d = functools.partial(lax.dot_general, dimension_numbers=dims, preferred_element_type=F32)
    return d(ah, bh) + (d(al, bh) + d(ah, bl))


def _dot_x3(a, b_exact):
    a1 = a.astype(BF16)
    r1 = a - a1.astype(F32)
    a2 = r1.astype(BF16)
    a3 = (r1 - a2.astype(F32)).astype(BF16)
    return _dot(a1, b_exact) + (_dot(a2, b_exact) + _dot(a3, b_exact))


def _dot_x3_left(a_exact, b):
    b1 = b.astype(BF16)
    r1 = b - b1.astype(F32)
    b2 = r1.astype(BF16)
    b3 = (r1 - b2.astype(F32)).astype(BF16)
    return _dot(a_exact, b1) + (_dot(a_exact, b2) + _dot(a_exact, b3))


def _complex_powers(base_r, base_i, exponent, nbits):
    shape = exponent.shape
    pr = jnp.ones(shape, F32)
    pi = jnp.zeros(shape, F32)
    br, bi = base_r, base_i
    for j in range(nbits):
        bit = ((exponent >> j) & 1) == 1
        pr, pi = jnp.where(bit, pr * br - pi * bi, pr), jnp.where(bit, pr * bi + pi * br, pi)
        br, bi = br * br - bi * bi, 2.0 * br * bi
    return pr, pi


def _silu(x):
    return x * jax.nn.sigmoid(x)


def _layernorm(x, g, b):
    mu = jnp.mean(x, axis=-1, keepdims=True)
    xc = x - mu
    var = jnp.mean(xc * xc, axis=-1, keepdims=True)
    return xc * lax.rsqrt(var + NORM_EPS) * g + b


def _l2norm(a):
    return a * lax.rsqrt(jnp.sum(a * a, axis=-1, keepdims=True) + NORM_EPS)


def _compiler_params(semantics):
    return pltpu.CompilerParams(dimension_semantics=semantics, vmem_limit_bytes=VMEM_LIMIT)


def _whole(shape):
    n = len(shape)
    return pl.BlockSpec(shape, lambda *_: (0,) * n)


def _interleave(*gens):
    live = list(gens)
    while live:
        for gen in list(live):
            if next(gen, "done") == "done":
                live.remove(gen)


def _proj_kernel(x_ref, w_ref, qkvz_ref, gates_ref, u_ref):
    r = jnp.dot(x_ref[...].astype(BF16), w_ref[...], preferred_element_type=F32)
    qkvz_ref[...] = r[:, 0:QKVZ_W]
    u_ref[...] = r[:, QKVZ_W:QKVZ_W + S5_WIDTH]
    gates_ref[...] = r[:, QKVZ_W + S5_WIDTH:]


PROJ_PARTS = 4


def _proj_conv_kernel(x_ref, w_ref, wconv_ref, qkvz_ref, gates_ref, u_ref, tail_ref, xpad_a_ref, xpad_b_ref, *,
                      tiles_per_seq):
    tile = x_ref.shape[0]
    dk = GDN_HEAD_DIM
    parts = PROJ_PARTS
    pr = tile // parts
    bufs = (xpad_a_ref, xpad_b_ref)
    last = bufs[(parts - 1) % 2]

    @pl.when(pl.program_id(0) % tiles_per_seq == 0)
    def _():
        last[pr:pr + 8, :] = jnp.zeros((8, QKV_W), F32)

    wc = wconv_ref[...]
    scale = GDN_HEAD_DIM ** -0.5

    def project(p):
        buf, prev = bufs[p % 2], bufs[(p - 1) % 2]
        rows = slice(p * pr, (p + 1) * pr)
        buf[0:8, :] = prev[pr:pr + 8, :]
        xb = x_ref[rows, :].astype(BF16)
        step = 2 * LANE_TILE
        for c0 in range(0, w_ref.shape[1], step):
            c1 = min(c0 + step, w_ref.shape[1])
            r = jnp.dot(xb, w_ref[:, c0:c1], preferred_element_type=F32)
            if c1 <= QKV_W:
                buf[8:8 + pr, c0:c1] = r
            elif c1 <= QKVZ_W:
                qkvz_ref[rows, c0:c1] = r
            elif c1 <= QKVZ_W + S5_WIDTH:
                u_ref[rows, c0 - QKVZ_W:c1 - QKVZ_W] = r
            else:
                gates_ref[rows, :] = r
            yield

    def activate(p):
        buf = bufs[p % 2]
        for lt in range(QKV_W // dk):
            cols = slice(lt * dk, (lt + 1) * dk)
            conv = buf[8:8 + pr, cols] * wc[3:4, cols]
            for j in range(1, CONV_W):
                conv = conv + buf[8 - j:8 - j + pr, cols] * wc[3 - j:4 - j, cols]
            act = _silu(conv)
            if lt < GDN_HEADS:
                act = _l2norm(act) * scale
            elif lt < 2 * GDN_HEADS:
                act = _l2norm(act)
            qkvz_ref[p * pr:(p + 1) * pr, cols] = act
            yield

    _interleave(project(0))
    for p in range(1, parts):
        _interleave(project(p), activate(p - 1))
    _interleave(activate(parts - 1))
    tail_ref[0] = last[pr:pr + 8, :]


def _proj(x, w, tile):
    n = x.shape[0]
    return pl.pallas_call(
        _proj_kernel,
        grid=(n // tile,),
        in_specs=[pl.BlockSpec((tile, D_MODEL), lambda i: (i, 0)),
                  pl.BlockSpec(w.shape, lambda i: (0, 0), pipeline_mode=pl.Buffered(1))],
        out_specs=[pl.BlockSpec((tile, QKVZ_W), lambda i: (i, 0)),
                   pl.BlockSpec((tile, GATE_LANES), lambda i: (i, 0)),
                   pl.BlockSpec((tile, S5_WIDTH), lambda i: (i, 0))],
        out_shape=[jax.ShapeDtypeStruct((n, QKVZ_W), F32),
                   jax.ShapeDtypeStruct((n, GATE_LANES), F32),
                   jax.ShapeDtypeStruct((n, S5_WIDTH), F32)],
        compiler_params=_compiler_params(("arbitrary",)),
        name="proj",
    )(x, w)


def _proj_conv(x, w, wconv, seq, tile):
    n = x.shape[0]
    tiles_per_seq = seq // tile
    assert PROJ_PARTS % 2 == 0 and tile % (8 * PROJ_PARTS) == 0 and seq % tile == 0
    return pl.pallas_call(
        functools.partial(_proj_conv_kernel, tiles_per_seq=tiles_per_seq),
        grid=(n // tile,),
        in_specs=[pl.BlockSpec((tile, D_MODEL), lambda i: (i, 0)),
                  pl.BlockSpec(w.shape, lambda i: (0, 0), pipeline_mode=pl.Buffered(1)),
                  _whole(wconv.shape)],
        out_specs=[pl.BlockSpec((tile, QKVZ_W), lambda i: (i, 0)),
                   pl.BlockSpec((tile, GATE_LANES), lambda i: (i, 0)),
                   pl.BlockSpec((tile, S5_WIDTH), lambda i: (i, 0)),
                   pl.BlockSpec((1, 8, QKV_W), lambda i: (i // tiles_per_seq, 0, 0))],
        out_shape=[jax.ShapeDtypeStruct((n, QKVZ_W), F32),
                   jax.ShapeDtypeStruct((n, GATE_LANES), F32),
                   jax.ShapeDtypeStruct((n, S5_WIDTH), F32),
                   jax.ShapeDtypeStruct((n // seq, 8, QKV_W), F32)],
        scratch_shapes=[pltpu.VMEM((8 + tile // PROJ_PARTS, QKV_W), F32)] * 2,
        compiler_params=_compiler_params(("arbitrary",)),
        name="proj_conv",
    )(x, w, wconv)


GDN_SEQS = 8
GDN_SUB_BLOCKS = (1, 2, 4, 8, 16, 32)
MASK_TRI, MASK_STRICT, MASK_EYE = len(GDN_SUB_BLOCKS), len(GDN_SUB_BLOCKS) + 1, len(GDN_SUB_BLOCKS) + 2
SEL_LOW, SEL_HIGH, SEL_SUB = 0, 1, 2


def _gate_values(logits, a_log, dt_bias):
    beta = jax.nn.sigmoid(logits)
    g = -jnp.exp(a_log) * jax.nn.softplus(logits + dt_bias)
    return beta, g


def _gdn_masks():
    c = GDN_CHUNK
    r = np.arange(c)[:, None]
    col = np.arange(2 * c)[None, :] % c
    sub = [(r // (2 * b) == col // (2 * b)) & ((r // b) % 2 == 1) & ((col // b) % 2 == 0) for b in GDN_SUB_BLOCKS]
    masks = np.stack(sub + [col <= r, col < r, col == r]).astype(np.float32)
    lane = np.broadcast_to(np.arange(2 * c)[None, :], (c, 2 * c))
    selectors = np.stack([lane < c, lane >= c] + sub).astype(np.float32)
    return jnp.asarray(masks), jnp.asarray(selectors, dtype=BF16)


def _gdn_pair_kernel(qkvz_ref, gates_ref, alog_row_ref, dtb_row_ref, normw_ref, masks_ref, sel_ref, o_ref, s_ref):
    step = pl.program_id(1)
    ns = qkvz_ref.shape[0]
    c = GDN_CHUNK
    dk = GDN_HEAD_DIM
    rows = ns * c

    @pl.when(step == 0)
    def _():
        s_ref[...] = jnp.zeros_like(s_ref)

    beta_c, g_c = _gate_values(gates_ref[...].reshape(rows, GATE_LANES), alog_row_ref[...], dtb_row_ref[...])

    ri = lax.broadcasted_iota(jnp.int32, (rows, rows), 0)
    ci = lax.broadcasted_iota(jnp.int32, (rows, rows), 1)
    tril_bd = jnp.where(((ri // c) == (ci // c)) & (ci <= ri), 1.0, 0.0).astype(BF16)
    gc_c = _dot_x3_left(tril_bd, g_c)
    gc_t = [gc_c[i * GATE_LANES:(i + 1) * GATE_LANES, :].T for i in range(rows // GATE_LANES)]
    gc_t_rolled = [pltpu.roll(t, c, axis=1) for t in gc_t]

    mask = lambda i: masks_ref[i] > 0.5
    low_half = lax.broadcasted_iota(jnp.int32, (1, 2 * c), 1) < c
    normw = normw_ref[...]
    zeros_k = jnp.zeros((c, dk), F32)

    def block_diag(x):
        return jnp.concatenate([x * sel_ref[SEL_LOW], x * sel_ref[SEL_HIGH]], axis=0)

    def pair_product(ph, pl_, yh, yl):
        bdh = block_diag(yh)
        first = jnp.dot(jnp.concatenate([ph, pl_], axis=1), jnp.concatenate([bdh, bdh], axis=0),
                        preferred_element_type=F32)
        return first + jnp.dot(ph, block_diag(yl), preferred_element_type=F32)

    def prep(seqs, pairs):
        for s in seqs:
            r0 = s * c
            lane0 = r0 % GATE_LANES
            blk = r0 // GATE_LANES
            in_place, moved = (gc_t[blk], gc_t_rolled[blk])
            for h0 in range(0, GDN_HEADS, 2):
                heads, a1 = [], None
                for side, h in enumerate((h0, h0 + 1)):
                    q = qkvz_ref[s, :, h * dk:(h + 1) * dk]
                    k = qkvz_ref[s, :, GDN_WIDTH + h * dk:GDN_WIDTH + (h + 1) * dk]
                    v = qkvz_ref[s, :, 2 * GDN_WIDTH + h * dk:2 * GDN_WIDTH + (h + 1) * dk]
                    beta = beta_c[r0:r0 + c, h:h + 1]
                    gcol = gc_c[r0:r0 + c, 4 + h:5 + h]
                    eg = jnp.exp(gcol)
                    g_last = gcol[c - 1:c, :]
                    kb = k * beta
                    k_pad = jnp.concatenate([k, zeros_k] if side == 0 else [zeros_k, k], axis=0)
                    part = _dot_nt(jnp.concatenate([kb, q], axis=0), k_pad)
                    a1 = part if a1 is None else a1 + part
                    src = in_place if (lane0 == 0) == (side == 0) else moved
                    heads.append(dict(s=s, h=h, gcol=gcol, grow=src[4 + h:5 + h, :],
                                      qg=q * eg, kd=k * jnp.exp(g_last - gcol), eg_last=jnp.exp(g_last),
                                      rhs=jnp.concatenate([v * beta, kb * eg], axis=1)))
                ha, hb = heads
                diff = jnp.where(low_half, ha["gcol"], hb["gcol"]) - jnp.where(low_half, ha["grow"], hb["grow"])
                decay = jnp.where(mask(MASK_TRI), jnp.exp(jnp.where(mask(MASK_TRI), diff, 0.0)), 0.0)
                pairs.append(dict(heads=heads,
                                  qk=jnp.where(mask(MASK_TRI), a1[c:2 * c] * decay, 0.0),
                                  lmat=jnp.where(mask(MASK_STRICT), a1[0:c] * decay, 0.0)))
                yield

    def solve(pairs):
        for p in pairs:
            p["d"] = masks_ref[MASK_EYE] - p["lmat"] * masks_ref[0]
            p["lh"], p["ll"] = _split(p["lmat"])
        for lvl in range(1, len(GDN_SUB_BLOCKS)):
            off = sel_ref[SEL_SUB + lvl]
            for p in pairs:
                p["dh"], p["dl"] = _split(p["d"])
                p["y"] = pair_product(p["lh"] * off, p["ll"] * off, p["dh"], p["dl"])
            yield
            for p in pairs:
                yh, yl = _split(p["y"])
                p["d"] = p["d"] - pair_product(p["dh"], p["dl"], yh, yl)
            yield
        for p in pairs:
            dh, dl = _split(p["d"])
            (ra_h, ra_l), (rb_h, rb_l) = (_split(hd["rhs"]) for hd in p["heads"])
            zr = jnp.zeros((c, 2 * dk), BF16)
            bd_h = jnp.concatenate([jnp.concatenate([ra_h, zr], axis=1), jnp.concatenate([zr, rb_h], axis=1)], axis=0)
            bd_l = jnp.concatenate([jnp.concatenate([ra_l, zr], axis=1), jnp.concatenate([zr, rb_l], axis=1)], axis=0)
            sol = (jnp.dot(jnp.concatenate([dh, dl], axis=1), jnp.concatenate([bd_h, bd_h], axis=0),
                           preferred_element_type=F32)
                   + jnp.dot(dh, bd_l, preferred_element_type=F32))
            for side, hd in enumerate(p["heads"]):
                hd["sol"] = sol[:, side * 2 * dk:(side + 1) * 2 * dk]
        yield

    def advance(pairs):
        for p in pairs:
            for hd in p["heads"]:
                hd["s_old"] = s_ref[hd["s"], hd["h"]]
                hd["m1"] = _dot(jnp.concatenate([hd["sol"][:, dk:2 * dk], hd["qg"]], axis=0), hd["s_old"])
        yield
        for p in pairs:
            for hd in p["heads"]:
                hd["v_new"] = hd["sol"][:, 0:dk] - hd["m1"][0:c]
                s_ref[hd["s"], hd["h"]] = hd["s_old"] * hd["eg_last"] + _dot_tn(hd["kd"], hd["v_new"])
        yield
        zv = jnp.zeros((c, dk), F32)
        for p in pairs:
            ha, hb = p["heads"]
            v_bd = jnp.concatenate([jnp.concatenate([ha["v_new"], zv], axis=1),
                                    jnp.concatenate([zv, hb["v_new"]], axis=1)], axis=0)
            o_pair = _dot(p["qk"], v_bd)
            for side, hd in enumerate(p["heads"]):
                s, h = hd["s"], hd["h"]
                o = hd["m1"][c:2 * c] + o_pair[:, side * dk:(side + 1) * dk]
                z = qkvz_ref[s, :, QKV_W + h * dk:QKV_W + (h + 1) * dk]
                o = o * lax.rsqrt(jnp.mean(o * o, axis=-1, keepdims=True) + NORM_EPS) * normw * _silu(z)
                o_ref[s, :, h * dk:(h + 1) * dk] = o
        yield

    pairs = []
    _interleave(prep(range(ns), pairs))
    _interleave(solve(pairs))
    _interleave(advance(pairs))


def _gdn_pair(qkvz3, gates3, alog_row, dtb_row, normw):
    bsz, seq, _ = qkvz3.shape
    ns, c = min(GDN_SEQS, bsz), GDN_CHUNK
    hd = (GDN_HEADS, GDN_HEAD_DIM, GDN_HEAD_DIM)
    masks, selectors = _gdn_masks()
    return pl.pallas_call(
        _gdn_pair_kernel,
        grid=(bsz // ns, seq // c),
        in_specs=[pl.BlockSpec((ns, c, QKVZ_W), lambda b, i: (b, i, 0)),
                  pl.BlockSpec((ns, c, GATE_LANES), lambda b, i: (b, i, 0)),
                  _whole(alog_row.shape), _whole(dtb_row.shape), _whole(normw.shape),
                  _whole(masks.shape), _whole(selectors.shape)],
        out_specs=[pl.BlockSpec((ns, c, GDN_WIDTH), lambda b, i: (b, i, 0)),
                   pl.BlockSpec((ns,) + hd, lambda b, i: (b, 0, 0, 0))],
        out_shape=[jax.ShapeDtypeStruct((bsz, seq, GDN_WIDTH), F32),
                   jax.ShapeDtypeStruct((bsz,) + hd, F32)],
        compiler_params=_compiler_params(("arbitrary", "arbitrary")),
        name="gdn_prompt",
    )(qkvz3, gates3, alog_row, dtb_row, normw, masks, selectors)


def _s5_params_kernel(us_ref, h0r_ref, h0i_ref, ar_row_ref, ai_row_ref, ar_col_ref, ai_col_ref, ldt_ref,
                      br_ref, bi_ref, cr_ref, ci_ref,
                      mt_ref, gt_ref, gts_ref, wf_ref, coef_ref, ys_ref, hsr_ref, hsi_ref):
    L = S5_CHUNK
    P = S5_STATE
    dt = jnp.exp(ldt_ref[0])
    ar_row = ar_row_ref[0] * dt
    ai_row = ai_row_ref[0] * dt
    ar_col = ar_col_ref[0] * dt
    ai_col = ai_col_ref[0] * dt
    first = lax.broadcasted_iota(jnp.int32, (1, 2 * P), 1) < P

    ea = jnp.exp(ar_col)
    lbr = ea * jnp.cos(ai_col)
    lbi = ea * jnp.sin(ai_col)
    lam_r = ar_col_ref[0]
    lam_i = ai_col_ref[0]
    den = lam_r * lam_r + lam_i * lam_i
    fr = ((lbr - 1.0) * lam_r + lbi * lam_i) / den
    fi = (lbi * lam_r - (lbr - 1.0) * lam_i) / den
    b_r = br_ref[0]
    b_i = bi_ref[0]
    bbr = fr * b_r - fi * b_i
    bbi = fr * b_i + fi * b_r

    rt = lax.broadcasted_iota(jnp.int32, (S5_GROUP, S5_FLAT), 0)
    lt = lax.broadcasted_iota(jnp.int32, (S5_GROUP, S5_FLAT), 1)
    tile_mat = jnp.where(lt % S5_GROUP == rt, 1.0, 0.0).astype(BF16)
    bwr = _dot_x3(bbr, tile_mat)
    bwi = _dot_x3(bbi, tile_mat)

    tau_g = (L - 1) - lax.broadcasted_iota(jnp.int32, (P, S5_FLAT), 1) // S5_GROUP
    pgr, pgi = _complex_powers(lbr, lbi, tau_g, 4)
    gtr = pgr * bwr - pgi * bwi
    gti = pgr * bwi + pgi * bwr
    gt_ref[0] = jnp.concatenate([gtr, gti], axis=0).astype(BF16)
    gts_ref[0] = jnp.concatenate([gti, gtr], axis=0).astype(BF16)

    nt = L + 1
    e1 = jnp.exp(ar_row)
    l1r = e1 * jnp.cos(ai_row)
    l1i = e1 * jnp.sin(ai_row)
    tau_w = lax.broadcasted_iota(jnp.int32, (nt * S5_GROUP, 2 * P), 0) // S5_GROUP
    pwr, pwi = _complex_powers(l1r, l1i, tau_w, 5)
    c_r = jnp.concatenate([cr_ref[0]] * nt, axis=0)
    c_i = jnp.concatenate([ci_ref[0]] * nt, axis=0)
    wall = jnp.where(first, c_r * pwr - c_i * pwi, -(c_r * pwi + c_i * pwr))
    wf_ref[0] = wall[S5_GROUP:].astype(BF16)

    zw = _dot_hi(wall[0:S5_FLAT], jnp.concatenate([bwr, bwi], axis=0))
    s_of_lane = lax.broadcasted_iota(jnp.int32, (S5_FLAT, S5_FLAT), 1) // S5_GROUP
    mt = zw
    for j in range(4):
        sh = S5_GROUP << j
        shifted = jnp.concatenate([jnp.zeros((sh, S5_FLAT), F32), mt[0:S5_FLAT - sh]], axis=0)
        mt = jnp.where(((s_of_lane >> j) & 1) == 1, shifted, mt)
    mt_ref[0] = mt.astype(BF16)

    p16r = pwr[L * S5_GROUP:L * S5_GROUP + 1, :]
    p16i = pwi[L * S5_GROUP:L * S5_GROUP + 1, :]
    coef_b = jnp.where(first, -p16i, p16i)
    srow = lax.broadcasted_iota(jnp.int32, (8, 2 * P), 0)
    coef_ref[0] = jnp.where(srow == 0, p16r, jnp.where(srow == 1, coef_b, jnp.where(srow == 2, -coef_b, 0.0)))

    l1r = l1r[:, 0:P]
    l1i = l1i[:, 0:P]
    us = us_ref[0]
    nt_dims = (((1,), (1,)), ((), ()))
    bur = _dot_hi(us, bbr, nt_dims)
    bui = _dot_hi(us, bbi, nt_dims)
    h0r = h0r_ref[0]
    h0i = h0i_ref[0]
    hsr = l1r * h0r - l1i * h0i + bur
    hsi = l1r * h0i + l1i * h0r + bui
    hsr_ref[0] = hsr
    hsi_ref[0] = hsi
    ys_ref[0] = _dot_nt(hsr, cr_ref[0][:, 0:P]) - _dot_nt(hsi, ci_ref[0][:, 0:P])


def _s5_params(us_t, h0r_t, h0i_t, ar, ai, ldt, b_r, b_i, c_r, c_i):
    g = S5_GROUPS
    ns = us_t.shape[1]
    twice = lambda a: jnp.concatenate([a, a], axis=-1)
    ar_row = twice(ar).reshape(g, 1, 2 * S5_STATE)
    ai_row = twice(ai).reshape(g, 1, 2 * S5_STATE)
    ar_col = ar.reshape(g, S5_STATE, 1)
    ai_col = ai.reshape(g, S5_STATE, 1)
    ldt3 = ldt.reshape(g, 1, 1)

    def blk(shape):
        return pl.BlockSpec((1,) + shape, lambda i: (i,) + (0,) * len(shape))

    return pl.pallas_call(
        _s5_params_kernel,
        grid=(g,),
        in_specs=[blk((ns, S5_GROUP)), blk((ns, S5_STATE)), blk((ns, S5_STATE)),
                  blk((1, 2 * S5_STATE)), blk((1, 2 * S5_STATE)), blk((S5_STATE, 1)), blk((S5_STATE, 1)), blk((1, 1)),
                  blk((S5_STATE, S5_GROUP)), blk((S5_STATE, S5_GROUP)),
                  blk((S5_GROUP, 2 * S5_STATE)), blk((S5_GROUP, 2 * S5_STATE))],
        out_specs=[blk((S5_FLAT, S5_FLAT)), blk((2 * S5_STATE, S5_FLAT)), blk((2 * S5_STATE, S5_FLAT)),
                   blk((S5_FLAT, 2 * S5_STATE)), blk((8, 2 * S5_STATE)),
                   blk((ns, S5_GROUP)), blk((ns, S5_STATE)), blk((ns, S5_STATE))],
        out_shape=[jax.ShapeDtypeStruct((g, S5_FLAT, S5_FLAT), BF16),
                   jax.ShapeDtypeStruct((g, 2 * S5_STATE, S5_FLAT), BF16),
                   jax.ShapeDtypeStruct((g, 2 * S5_STATE, S5_FLAT), BF16),
                   jax.ShapeDtypeStruct((g, S5_FLAT, 2 * S5_STATE), BF16),
                   jax.ShapeDtypeStruct((g, 8, 2 * S5_STATE), F32),
                   jax.ShapeDtypeStruct((g, ns, S5_GROUP), F32),
                   jax.ShapeDtypeStruct((g, ns, S5_STATE), F32),
                   jax.ShapeDtypeStruct((g, ns, S5_STATE), F32)],
        compiler_params=_compiler_params(("arbitrary",)),
        name="s5_params",
    )(us_t, h0r_t, h0i_t, ar_row, ai_row, ar_col, ai_col, ldt3, b_r, b_i, twice(c_r), twice(c_i))


PIECES = LANE_TILE // S5_GROUP
RELAYOUT_ROWS = S5_CHUNK * S5_CHUNK


def _piece_transpose(tiles):
    tiles = list(tiles)
    piece = lax.broadcasted_iota(jnp.int32, tiles[0].shape, 1) // S5_GROUP
    d = PIECES // 2
    while d:
        keep_low = (piece & d) == 0
        for k in range(PIECES):
            if k & d:
                continue
            a, b = tiles[k], tiles[k + d]
            tiles[k] = jnp.where(keep_low, a, pltpu.roll(b, S5_GROUP * d, axis=1))
            tiles[k + d] = jnp.where(keep_low, pltpu.roll(a, LANE_TILE - S5_GROUP * d, axis=1), b)
        d //= 2
    return tiles


def _s5_seq_kernel(u_ref, mt_ref, gt_ref, gts_ref, wf_ref, coef_ref, ys_ref, hl_ref,
                   uflat_ref, e_ref, es_ref, hin_ref, yflat_ref, *, nchunk):
    rr = RELAYOUT_ROWS
    ro = lax.broadcasted_iota(jnp.int32, (rr, rr), 0)
    ri = lax.broadcasted_iota(jnp.int32, (rr, rr), 1)
    perm = jnp.where((ro // S5_CHUNK == ri % S5_CHUNK) & (ro % S5_CHUNK == ri // S5_CHUNK), 1.0, 0.0).astype(BF16)

    def gather_in(m, carry):
        rows = pl.ds(pl.multiple_of(m * rr, rr), rr)
        xp = jnp.dot(perm, u_ref[rows, :].astype(BF16), preferred_element_type=F32)
        crow = pl.ds(pl.multiple_of(m * S5_CHUNK, S5_CHUNK), S5_CHUNK)
        for j in range(S5_FLAT // LANE_TILE):
            for cb in range(S5_WIDTH // LANE_TILE):
                tiles = [xp[(PIECES * j + k) * S5_CHUNK:(PIECES * j + k + 1) * S5_CHUNK,
                            cb * LANE_TILE:(cb + 1) * LANE_TILE] for k in range(PIECES)]
                for p, tile in enumerate(_piece_transpose(tiles)):
                    uflat_ref[PIECES * cb + p, crow, j * LANE_TILE:(j + 1) * LANE_TILE] = tile.astype(BF16)
        return carry

    lax.fori_loop(0, nchunk // S5_CHUNK, gather_in, 0, unroll=2)

    for g in range(S5_GROUPS):
        lanes = slice(g * LANE_TILE, (g + 1) * LANE_TILE)
        both = _dot_nt(uflat_ref[g], jnp.concatenate([gt_ref[g], gts_ref[g]], axis=0))
        e_ref[:, lanes] = both[:, 0:LANE_TILE]
        es_ref[:, lanes] = both[:, LANE_TILE:]

    ca = coef_ref[0:1, :]
    cb = coef_ref[1:2, :]
    cbs = coef_ref[2:3, :]

    def scan(c, carry):
        h, hs = carry
        row = pl.ds(c, 1)
        hin_ref[row, :] = h
        return ca * h + cb * hs + e_ref[row, :], ca * hs + cbs * h + es_ref[row, :]

    zero = jnp.zeros((1, S5_GROUPS * LANE_TILE), F32)
    h_last, _ = lax.fori_loop(0, nchunk, scan, (zero, zero))
    hl_ref[0] = h_last

    for g in range(S5_GROUPS):
        lanes = slice(g * LANE_TILE, (g + 1) * LANE_TILE)
        yflat_ref[g] = _dot_nt(uflat_ref[g], mt_ref[g]) + _dot_nt(hin_ref[:, lanes], wf_ref[g])

    def gather_out(m, carry):
        crow = pl.ds(pl.multiple_of(m * S5_CHUNK, S5_CHUNK), S5_CHUNK)
        by_time = [[None] * (S5_WIDTH // LANE_TILE) for _ in range(S5_CHUNK)]
        for j in range(S5_FLAT // LANE_TILE):
            for cb in range(S5_WIDTH // LANE_TILE):
                tiles = [yflat_ref[PIECES * cb + k, crow, j * LANE_TILE:(j + 1) * LANE_TILE] for k in range(PIECES)]
                for p, tile in enumerate(_piece_transpose(tiles)):
                    by_time[PIECES * j + p][cb] = tile
        z = jnp.concatenate([jnp.concatenate(row, axis=1) for row in by_time], axis=0)
        rows = pl.ds(pl.multiple_of(m * rr, rr), rr)
        ys_ref[rows, :] = _dot_x3_left(perm, z)
        return carry

    lax.fori_loop(0, nchunk // S5_CHUNK, gather_out, 0, unroll=2)


def _s5_seq(u, mt, gt, gts, wf, coef, bsz, seq):
    nchunk = seq // S5_CHUNK
    width = S5_GROUPS * LANE_TILE

    def resident(a):
        return pl.BlockSpec(a.shape, lambda b: (0,) * a.ndim, pipeline_mode=pl.Buffered(1))

    return pl.pallas_call(
        functools.partial(_s5_seq_kernel, nchunk=nchunk),
        grid=(bsz,),
        in_specs=[pl.BlockSpec((seq, S5_WIDTH), lambda b: (b, 0))] + [resident(a) for a in (mt, gt, gts, wf, coef)],
        out_specs=[pl.BlockSpec((seq, S5_WIDTH), lambda b: (b, 0)),
                   pl.BlockSpec((1, 1, width), lambda b: (b, 0, 0))],
        out_shape=[jax.ShapeDtypeStruct((bsz * seq, S5_WIDTH), F32),
                   jax.ShapeDtypeStruct((bsz, 1, width), F32)],
        scratch_shapes=[pltpu.VMEM((S5_GROUPS, nchunk, S5_FLAT), BF16),
                        pltpu.VMEM((nchunk, width), F32), pltpu.VMEM((nchunk, width), F32),
                        pltpu.VMEM((nchunk, width), F32),
                        pltpu.VMEM((S5_GROUPS, nchunk, S5_FLAT), F32)],
        compiler_params=_compiler_params(("arbitrary",)),
        name="s5_seq",
    )(u, mt, gt, gts, wf, coef)


def _sample_pre_kernel(qkvz_ref, gates_ref, conv_ref, wconv_ref, alog_row_ref, dtb_row_ref,
                       newconv_ref, ops_ref, sc_ref):
    dk = GDN_HEAD_DIM
    x_new = qkvz_ref[:, 0:QKV_W]
    wc = wconv_ref[...]
    conv = x_new * wc[3:4, :]
    for j in range(CONV_W - 1):
        conv = conv + conv_ref[j] * wc[j:j + 1, :]
    qkv = _silu(conv)
    newconv_ref[0] = conv_ref[1]
    newconv_ref[1] = conv_ref[2]
    newconv_ref[2] = x_new

    beta_c, g_c = _gate_values(gates_ref[...], alog_row_ref[...], dtb_row_ref[...])
    scale = GDN_HEAD_DIM ** -0.5
    lane = lax.broadcasted_iota(jnp.int32, sc_ref.shape, 1)
    sc = jnp.zeros(sc_ref.shape, F32)
    for h in range(GDN_HEADS):
        q = _l2norm(qkv[:, h * dk:(h + 1) * dk]) * scale
        k = _l2norm(qkv[:, GDN_WIDTH + h * dk:GDN_WIDTH + (h + 1) * dk])
        v = qkv[:, 2 * GDN_WIDTH + h * dk:2 * GDN_WIDTH + (h + 1) * dk]
        beta = beta_c[:, h:h + 1]
        eg = jnp.exp(g_c[:, 4 + h:5 + h])
        cols = slice(h * dk, (h + 1) * dk)
        ops_ref[0, :, cols] = (beta * eg) * k
        ops_ref[1, :, cols] = q * eg
        ops_ref[2, :, cols] = k
        ops_ref[3, :, cols] = beta * v
        sc = jnp.where(lane == h, jnp.sum(q * k, axis=-1, keepdims=True), sc)
        sc = jnp.where(lane == 4 + h, eg, sc)
    sc_ref[...] = sc


def _sample_pre(qkvz_s, gates_s, conv_t, wconv, alog_row, dtb_row):
    ns = qkvz_s.shape[0]
    return pl.pallas_call(
        _sample_pre_kernel,
        out_shape=[jax.ShapeDtypeStruct((CONV_W - 1, ns, QKV_W), F32),
                   jax.ShapeDtypeStruct((4, ns, GDN_WIDTH), F32),
                   jax.ShapeDtypeStruct((ns, GATE_LANES), F32)],
        compiler_params=pltpu.CompilerParams(vmem_limit_bytes=VMEM_LIMIT),
        name="sample_pre",
    )(qkvz_s, gates_s, conv_t, wconv, alog_row, dtb_row)


def _sample_state_kernel(s_ref, ops_ref, sc_ref, z_ref, normw_ref, snew_ref, o_ref, *, nb):
    dk = GDN_HEAD_DIM
    row = lax.broadcasted_iota(jnp.int32, (8, dk), 0)
    units = [(j, h) for j in range(nb) for h in range(GDN_HEADS)]
    m1 = {}
    for j, h in units:
        cols = slice(h * dk, (h + 1) * dk)
        w = ops_ref[0, j:j + 1, cols]
        qg = ops_ref[1, j:j + 1, cols]
        lhs = jnp.where(row == 0, jnp.broadcast_to(w, (8, dk)), jnp.broadcast_to(qg, (8, dk)))
        m1[j, h] = _dot(lhs, s_ref[j, h])
    for j, h in units:
        cols = slice(h * dk, (h + 1) * dk)
        k = ops_ref[2, j:j + 1, cols]
        u = ops_ref[3, j:j + 1, cols]
        v_new = u - m1[j, h][0:1, :]
        qk = sc_ref[j:j + 1, h:h + 1]
        eg = sc_ref[j:j + 1, 4 + h:5 + h]
        o_ref[j:j + 1, cols] = m1[j, h][1:2, :] + qk * v_new
        k8 = jnp.where(row == 0, jnp.broadcast_to(k, (8, dk)), 0.0)
        snew_ref[j, h] = s_ref[j, h] * eg + _dot_tn(k8, jnp.broadcast_to(v_new, (8, dk)))
    normw = normw_ref[...]
    for h in range(GDN_HEADS):
        o = o_ref[:, h * dk:(h + 1) * dk]
        z = z_ref[:, h * dk:(h + 1) * dk]
        o_ref[:, h * dk:(h + 1) * dk] = (o * lax.rsqrt(jnp.mean(o * o, axis=-1, keepdims=True) + NORM_EPS)
                                         * normw * _silu(z))


def _sample_state(state, ops, sc, z, normw, nb=8):
    ns = state.shape[0]
    hd = (GDN_HEADS, GDN_HEAD_DIM, GDN_HEAD_DIM)
    return pl.pallas_call(
        functools.partial(_sample_state_kernel, nb=nb),
        grid=(ns // nb,),
        in_specs=[pl.BlockSpec((nb,) + hd, lambda i: (i, 0, 0, 0)),
                  pl.BlockSpec((4, nb, GDN_WIDTH), lambda i: (0, i, 0)),
                  pl.BlockSpec((nb, GATE_LANES), lambda i: (i, 0)),
                  pl.BlockSpec((nb, GDN_WIDTH), lambda i: (i, 0)),
                  _whole(normw.shape)],
        out_specs=[pl.BlockSpec((nb,) + hd, lambda i: (i, 0, 0, 0)),
                   pl.BlockSpec((nb, GDN_WIDTH), lambda i: (i, 0))],
        out_shape=[jax.ShapeDtypeStruct(state.shape, F32),
                   jax.ShapeDtypeStruct((ns, GDN_WIDTH), F32)],
        compiler_params=_compiler_params(("arbitrary",)),
        name="sample_state",
    )(state, ops, sc, z, normw)


def _post_kernel(x_ref, o_ref, ys_ref, u_ref, d_ref, wglu_ref, bglu_ref, wout_ref, g1_ref, b1_ref,
                 wff1_ref, wff2_ref, g2_ref, b2_ref, y_ref, *, ff_chunk):
    n = x_ref.shape[0]
    halves = [slice(0, n // 2), slice(n // 2, n)] if n >= 256 else [slice(0, n)]
    st = [dict(rows=r) for r in halves]

    def head(s):
        r = s["rows"]
        ys = jax.nn.gelu(ys_ref[r, :] + d_ref[...] * u_ref[r, :])
        ys = ys * jax.nn.sigmoid(_dot(ys, wglu_ref[...]) + bglu_ref[...])
        yield
        mix = _dot(o_ref[r, :], wout_ref[0:GDN_WIDTH, :]) + _dot(ys, wout_ref[GDN_WIDTH:, :])
        yield
        s["x1"] = _layernorm(DN_ALPHA * x_ref[r, :] + mix, g1_ref[...], b1_ref[...])
        s["x1b"] = s["x1"].astype(BF16)
        yield

    def mlp(s):
        acc = jnp.zeros(s["x1"].shape, F32)
        for f in range(D_FF // ff_chunk):
            hcol = jnp.dot(s["x1b"], wff1_ref[:, f * ff_chunk:(f + 1) * ff_chunk], preferred_element_type=F32)
            hcol = jnp.square(jnp.maximum(hcol, 0.0))
            yield
            acc = acc + _dot(hcol, wff2_ref[f * ff_chunk:(f + 1) * ff_chunk, :])
            yield
        s["acc"] = acc

    def tail(s):
        y_ref[s["rows"], :] = _layernorm(DN_ALPHA * s["x1"] + s["acc"], g2_ref[...], b2_ref[...])
        yield

    _interleave(head(st[0]))
    for i, s in enumerate(st):
        others = [head(st[i + 1])] if i + 1 < len(st) else []
        if i > 0:
            others.append(tail(st[i - 1]))
        _interleave(mlp(s), *others)
    _interleave(tail(st[-1]))


def _post(x, o, ys, u, *weights, tile):
    n = x.shape[0]
    tok = lambda w: pl.BlockSpec((tile, w), lambda i: (i, 0))

    def resident(a):
        return pl.BlockSpec(a.shape, lambda i: (0,) * a.ndim, pipeline_mode=pl.Buffered(1))

    return pl.pallas_call(
        functools.partial(_post_kernel, ff_chunk=1024),
        grid=(n // tile,),
        in_specs=[tok(D_MODEL), tok(GDN_WIDTH), tok(S5_WIDTH), tok(S5_WIDTH)] + [resident(a) for a in weights],
        out_specs=tok(D_MODEL),
        out_shape=jax.ShapeDtypeStruct((n, D_MODEL), F32),
        compiler_params=_compiler_params(("arbitrary",)),
        name="post",
    )(x, o, ys, u, *weights)


def kernel(x_prompt, x_sample, state_gdn, state_conv, state_ssm_re, state_ssm_im, w_in, w_conv, gdn_a_log,
           gdn_dt_bias, gdn_norm_w, s5_a_re, s5_a_im, s5_b_re, s5_b_im, s5_c_re, s5_c_im, s5_d, s5_log_dt,
           w_glu, b_glu, w_out, ln1_g, ln1_b, w_ff1, w_ff2, ln2_g, ln2_b):
    bsz, seq, _ = x_prompt.shape
    ns = x_sample.shape[0]
    l = 0

    w = w_in[l]
    i_u = QKVZ_W + 2 * GDN_HEADS
    w_cat = jnp.concatenate([w[:, :QKVZ_W], w[:, i_u:], w[:, QKVZ_W:i_u],
                             jnp.zeros((D_MODEL, GATE_LANES - 2 * GDN_HEADS), F32)], axis=1).astype(BF16)
    lane_pad = (GDN_HEADS, GATE_LANES - 2 * GDN_HEADS)
    alog_row = jnp.pad(gdn_a_log[l], lane_pad).reshape(1, GATE_LANES)
    dtb_row = jnp.pad(gdn_dt_bias[l], lane_pad).reshape(1, GATE_LANES)
    normw = gdn_norm_w[l].reshape(1, GDN_HEAD_DIM)
    wconv = w_conv[l]
    row = lambda a: a.reshape(1, -1)
    post_w = (row(s5_d[l]), w_glu[l].astype(BF16), row(b_glu[l]), w_out[l].astype(BF16), row(ln1_g[l]), row(ln1_b[l]),
              w_ff1[l].astype(BF16), w_ff2[l].astype(BF16), row(ln2_g[l]), row(ln2_b[l]))

    xp = x_prompt.reshape(bsz * seq, D_MODEL)
    xs = x_sample.reshape(ns, D_MODEL)
    qkvz_p, gates_p, u_p, tail_p = _proj_conv(xp, w_cat, wconv, seq, tile=min(512, seq))
    qkvz_s, gates_s, u_s = _proj(xs, w_cat, tile=ns)

    o_p, gdn_p = _gdn_pair(qkvz_p.reshape(bsz, seq, QKVZ_W), gates_p.reshape(bsz, seq, GATE_LANES),
                           alog_row, dtb_row, normw)
    o_p = o_p.reshape(bsz * seq, GDN_WIDTH)
    conv_p = tail_p[:, 8 - (CONV_W - 1):, :]

    conv_t = jnp.transpose(state_conv[l], (1, 0, 2))
    newconv_t, ops_s, sc = _sample_pre(qkvz_s, gates_s, conv_t, wconv, alog_row, dtb_row)
    gdn_s, o_s = _sample_state(state_gdn[l], ops_s, sc, qkvz_s[:, QKV_W:], normw)
    conv_s = jnp.transpose(newconv_t, (1, 0, 2))

    us_t = u_s.reshape(ns, S5_GROUPS, S5_GROUP).transpose(1, 0, 2)
    h0r_t = state_ssm_re[l].transpose(1, 0, 2)
    h0i_t = state_ssm_im[l].transpose(1, 0, 2)
    mt, gt, gts, wf, coef, ys_t, hsr, hsi = _s5_params(us_t, h0r_t, h0i_t, s5_a_re[l], s5_a_im[l], s5_log_dt[l],
                                                       s5_b_re[l], s5_b_im[l], s5_c_re[l], s5_c_im[l])
    coef_rows = coef.transpose(1, 0, 2).reshape(8, S5_GROUPS * LANE_TILE)
    ys_p, h_last = _s5_seq(u_p, mt, gt, gts, wf, coef_rows, bsz, seq)
    h_last = h_last.reshape(bsz, S5_GROUPS, 2 * S5_STATE)
    ys_s = ys_t.transpose(1, 0, 2).reshape(ns, S5_WIDTH)

    y_p = _post(xp, o_p, ys_p, u_p, *post_w, tile=512)
    y_s = _post(xs, o_s, ys_s, u_s, *post_w, tile=ns)

    t3 = lambda a: a.transpose(1, 0, 2)[None]
    return (y_p.reshape(bsz, seq, D_MODEL), y_s.reshape(ns, 1, D_MODEL),
            gdn_p[None], conv_p[None], h_last[None, :, :, :S5_STATE], h_last[None, :, :, S5_STATE:],
            gdn_s[None], conv_s[None], t3(hsr), t3(hsi))
```

```python
import functools

import jax
import jax.numpy as jnp
import numpy as np
from jax import lax
from jax.experimental import pallas as pl
from jax.experimental.pallas import tpu as pltpu

F32 = jnp.float32
BF16 = jnp.bfloat16

D_MODEL = 1024
GDN_HEADS = 4
GDN_HEAD_DIM = 128
GDN_WIDTH = GDN_HEADS * GDN_HEAD_DIM
CONV_W = 4
GDN_CHUNK = 64
S5_WIDTH = D_MODEL - GDN_WIDTH
S5_GROUP = 16
S5_GROUPS = S5_WIDTH // S5_GROUP
S5_STATE = 64
D_FF = 4 * D_MODEL
DEPTH = 1
DN_ALPHA = (2.0 * DEPTH) ** 0.25
NORM_EPS = 1e-6
QKV_W = 3 * GDN_WIDTH
QKVZ_W = QKV_W + GDN_WIDTH
LANE_TILE = 128
GATE_LANES = LANE_TILE
S5_CHUNK = 16
S5_FLAT = S5_CHUNK * S5_GROUP
VMEM_LIMIT = 56 * 1024 * 1024


def _dot(a, b):
    return jnp.dot(a.astype(BF16), b.astype(BF16), preferred_element_type=F32)


def _dot_nt(a, b):
    return lax.dot_general(a.astype(BF16), b.astype(BF16), (((1,), (1,)), ((), ())), preferred_element_type=F32)


def _dot_tn(a, b):
    return lax.dot_general(a.astype(BF16), b.astype(BF16), (((0,), (0,)), ((), ())), preferred_element_type=F32)


def _split(x):
    hi = x.astype(BF16)
    lo = (x - hi.astype(F32)).astype(BF16)
    return hi, lo


def _dot_hi(a, b, dims=(((1,), (0,)), ((), ()))):
    ah, al = _split(a)
    bh, bl = _split(b)
    d = functools.partial(lax.dot_general, dimension_numbers=dims, preferred_element_type=F32)
    return d(ah, bh) + (d(al, bh) + d(ah, bl))


def _dot_x3(a, b_exact):
    a1 = a.astype(BF16)
    r1 = a - a1.astype(F32)
    a2 = r1.astype(BF16)
    a3 = (r1 - a2.astype(F32)).astype(BF16)
    return _dot(a1, b_exact) + (_dot(a2, b_exact) + _dot(a3, b_exact))


def _dot_x3_left(a_exact, b):
    b1 = b.astype(BF16)
    r1 = b - b1.astype(F32)
    b2 = r1.astype(BF16)
    b3 = (r1 - b2.astype(F32)).astype(BF16)
    return _dot(a_exact, b1) + (_dot(a_exact, b2) + _dot(a_exact, b3))


def _complex_powers(base_r, base_i, exponent, nbits):
    shape = exponent.shape
    pr = jnp.ones(shape, F32)
    pi = jnp.zeros(shape, F32)
    br, bi = base_r, base_i
    for j in range(nbits):
        bit = ((exponent >> j) & 1) == 1
        pr, pi = jnp.where(bit, pr * br - pi * bi, pr), jnp.where(bit, pr * bi + pi * br, pi)
        br, bi = br * br - bi * bi, 2.0 * br * bi
    return pr, pi


def _silu(x):
    return x * jax.nn.sigmoid(x)


def _layernorm(x, g, b):
    mu = jnp.mean(x, axis=-1, keepdims=True)
    xc = x - mu
    var = jnp.mean(xc * xc, axis=-1, keepdims=True)
    return xc * lax.rsqrt(var + NORM_EPS) * g + b


def _l2norm(a):
    return a * lax.rsqrt(jnp.sum(a * a, axis=-1, keepdims=True) + NORM_EPS)


def _compiler_params(semantics):
    return pltpu.CompilerParams(dimension_semantics=semantics, vmem_limit_bytes=VMEM_LIMIT)


def _whole(shape):
    n = len(shape)
    return pl.BlockSpec(shape, lambda *_: (0,) * n)


def _interleave(*gens):
    live = list(gens)
    while live:
        for gen in list(live):
            if next(gen, "done") == "done":
                live.remove(gen)


def _proj_kernel(x_ref, w_ref, qkvz_ref, gates_ref, u_ref):
    r = jnp.dot(x_ref[...].astype(BF16), w_ref[...], preferred_element_type=F32)
    qkvz_ref[...] = r[:, 0:QKVZ_W]
    u_ref[...] = r[:, QKVZ_W:QKVZ_W + S5_WIDTH]
    gates_ref[...] = r[:, QKVZ_W + S5_WIDTH:]


PROJ_PARTS = 4


def _proj_conv_kernel(x_ref, w_ref, wconv_ref, qkvz_ref, gates_ref, u_ref, tail_ref, xpad_a_ref, xpad_b_ref, *,
                      tiles_per_seq):
    tile = x_ref.shape[0]
    dk = GDN_HEAD_DIM
    parts = PROJ_PARTS
    pr = tile // parts
    bufs = (xpad_a_ref, xpad_b_ref)
    last = bufs[(parts - 1) % 2]

    @pl.when(pl.program_id(0) % tiles_per_seq == 0)
    def _():
        last[pr:pr + 8, :] = jnp.zeros((8, QKV_W), F32)

    wc = wconv_ref[...]
    scale = GDN_HEAD_DIM ** -0.5

    def project(p):
        buf, prev = bufs[p % 2], bufs[(p - 1) % 2]
        rows = slice(p * pr, (p + 1) * pr)
        buf[0:8, :] = prev[pr:pr + 8, :]
        xb = x_ref[rows, :].astype(BF16)
        step = 2 * LANE_TILE
        for c0 in range(0, w_ref.shape[1], step):
            c1 = min(c0 + step, w_ref.shape[1])
            r = jnp.dot(xb, w_ref[:, c0:c1], preferred_element_type=F32)
            if c1 <= QKV_W:
                buf[8:8 + pr, c0:c1] = r
            elif c1 <= QKVZ_W:
                qkvz_ref[rows, c0:c1] = r
            elif c1 <= QKVZ_W + S5_WIDTH:
                u_ref[rows, c0 - QKVZ_W:c1 - QKVZ_W] = r
            else:
                gates_ref[rows, :] = r
            yield

    def activate(p):
        buf = bufs[p % 2]
        for lt in range(QKV_W // dk):
            cols = slice(lt * dk, (lt + 1) * dk)
            conv = buf[8:8 + pr, cols] * wc[3:4, cols]
            for j in range(1, CONV_W):
                conv = conv + buf[8 - j:8 - j + pr, cols] * wc[3 - j:4 - j, cols]
            act = _silu(conv)
            if lt < GDN_HEADS:
                act = _l2norm(act) * scale
            elif lt < 2 * GDN_HEADS:
                act = _l2norm(act)
            qkvz_ref[p * pr:(p + 1) * pr, cols] = act
            yield

    _interleave(project(0))
    for p in range(1, parts):
        _interleave(project(p), activate(p - 1))
    _interleave(activate(parts - 1))
    tail_ref[0] = last[pr:pr + 8, :]


def _proj(x, w, tile):
    n = x.shape[0]
    return pl.pallas_call(
        _proj_kernel,
        grid=(n // tile,),
        in_specs=[pl.BlockSpec((tile, D_MODEL), lambda i: (i, 0)),
                  pl.BlockSpec(w.shape, lambda i: (0, 0), pipeline_mode=pl.Buffered(1))],
        out_specs=[pl.BlockSpec((tile, QKVZ_W), lambda i: (i, 0)),
                   pl.BlockSpec((tile, GATE_LANES), lambda i: (i, 0)),
                   pl.BlockSpec((tile, S5_WIDTH), lambda i: (i, 0))],
        out_shape=[jax.ShapeDtypeStruct((n, QKVZ_W), F32),
                   jax.ShapeDtypeStruct((n, GATE_LANES), F32),
                   jax.ShapeDtypeStruct((n, S5_WIDTH), F32)],
        compiler_params=_compiler_params(("arbitrary",)),
        name="proj",
    )(x, w)


def _proj_conv(x, w, wconv, seq, tile):
    n = x.shape[0]
    tiles_per_seq = seq // tile
    assert PROJ_PARTS % 2 == 0 and tile % (8 * PROJ_PARTS) == 0 and seq % tile == 0
    return pl.pallas_call(
        functools.partial(_proj_conv_kernel, tiles_per_seq=tiles_per_seq),
        grid=(n // tile,),
        in_specs=[pl.BlockSpec((tile, D_MODEL), lambda i: (i, 0)),
                  pl.BlockSpec(w.shape, lambda i: (0, 0), pipeline_mode=pl.Buffered(1)),
                  _whole(wconv.shape)],
        out_specs=[pl.BlockSpec((tile, QKVZ_W), lambda i: (i, 0)),
                   pl.BlockSpec((tile, GATE_LANES), lambda i: (i, 0)),
                   pl.BlockSpec((tile, S5_WIDTH), lambda i: (i, 0)),
                   pl.BlockSpec((1, 8, QKV_W), lambda i: (i // tiles_per_seq, 0, 0))],
        out_shape=[jax.ShapeDtypeStruct((n, QKVZ_W), F32),
                   jax.ShapeDtypeStruct((n, GATE_LANES), F32),
                   jax.ShapeDtypeStruct((n, S5_WIDTH), F32),
                   jax.ShapeDtypeStruct((n // seq, 8, QKV_W), F32)],
        scratch_shapes=[pltpu.VMEM((8 + tile // PROJ_PARTS, QKV_W), F32)] * 2,
        compiler_params=_compiler_params(("arbitrary",)),
        name="proj_conv",
    )(x, w, wconv)


GDN_SEQS = 8
GDN_SUB_BLOCKS = (1, 2, 4, 8, 16, 32)
MASK_TRI, MASK_STRICT, MASK_EYE = len(GDN_SUB_BLOCKS), len(GDN_SUB_BLOCKS) + 1, len(GDN_SUB_BLOCKS) + 2
SEL_LOW, SEL_HIGH, SEL_SUB = 0, 1, 2


def _gate_values(logits, a_log, dt_bias):
    beta = jax.nn.sigmoid(logits)
    g = -jnp.exp(a_log) * jax.nn.softplus(logits + dt_bias)
    return beta, g


def _gdn_masks():
    c = GDN_CHUNK
    r = np.arange(c)[:, None]
    col = np.arange(2 * c)[None, :] % c
    sub = [(r // (2 * b) == col // (2 * b)) & ((r // b) % 2 == 1) & ((col // b) % 2 == 0) for b in GDN_SUB_BLOCKS]
    masks = np.stack(sub + [col <= r, col < r, col == r]).astype(np.float32)
    lane = np.broadcast_to(np.arange(2 * c)[None, :], (c, 2 * c))
    selectors = np.stack([lane < c, lane >= c] + sub).astype(np.float32)
    return jnp.asarray(masks), jnp.asarray(selectors, dtype=BF16)


def _gdn_pair_kernel(qkvz_ref, gates_ref, alog_row_ref, dtb_row_ref, normw_ref, masks_ref, sel_ref, o_ref, s_ref):
    step = pl.program_id(1)
    ns = qkvz_ref.shape[0]
    c = GDN_CHUNK
    dk = GDN_HEAD_DIM
    rows = ns * c

    @pl.when(step == 0)
    def _():
        s_ref[...] = jnp.zeros_like(s_ref)

    beta_c, g_c = _gate_values(gates_ref[...].reshape(rows, GATE_LANES), alog_row_ref[...], dtb_row_ref[...])

    ri = lax.broadcasted_iota(jnp.int32, (LANE_TILE, LANE_TILE), 0)
    ci = lax.broadcasted_iota(jnp.int32, (LANE_TILE, LANE_TILE), 1)
    tril_bd = jnp.where(((ri // c) == (ci // c)) & (ci <= ri), 1.0, 0.0).astype(BF16)
    slabs = [_dot_x3_left(tril_bd, g_c[i * LANE_TILE:(i + 1) * LANE_TILE, :]) for i in range(rows // LANE_TILE)]
    gc_c = jnp.concatenate(slabs, axis=0)
    gc_t = [slab.T for slab in slabs]
    gc_t_rolled = [pltpu.roll(t, c, axis=1) for t in gc_t]

    mask = lambda i: masks_ref[i] > 0.5
    low_half = lax.broadcasted_iota(jnp.int32, (1, 2 * c), 1) < c
    normw = normw_ref[...]
    zeros_k = jnp.zeros((c, dk), F32)

    def block_diag(x):
        return jnp.concatenate([x * sel_ref[SEL_LOW], x * sel_ref[SEL_HIGH]], axis=0)

    def pair_product(ph, pl_, yh, yl):
        bdh = block_diag(yh)
        first = jnp.dot(jnp.concatenate([ph, pl_], axis=1), jnp.concatenate([bdh, bdh], axis=0),
                        preferred_element_type=F32)
        return first + jnp.dot(ph, block_diag(yl), preferred_element_type=F32)

    def prep(seqs, pairs):
        for s in seqs:
            r0 = s * c
            lane0 = r0 % GATE_LANES
            blk = r0 // GATE_LANES
            in_place, moved = (gc_t[blk], gc_t_rolled[blk])
            for h0 in range(0, GDN_HEADS, 2):
                heads, a1 = [], None
                for side, h in enumerate((h0, h0 + 1)):
                    q = qkvz_ref[s, :, h * dk:(h + 1) * dk]
                    k = qkvz_ref[s, :, GDN_WIDTH + h * dk:GDN_WIDTH + (h + 1) * dk]
                    v = qkvz_ref[s, :, 2 * GDN_WIDTH + h * dk:2 * GDN_WIDTH + (h + 1) * dk]
                    beta = beta_c[r0:r0 + c, h:h + 1]
                    gcol = gc_c[r0:r0 + c, 4 + h:5 + h]
                    eg = jnp.exp(gcol)
                    g_last = gcol[c - 1:c, :]
                    kb = k * beta
                    k_pad = jnp.concatenate([k, zeros_k] if side == 0 else [zeros_k, k], axis=0)
                    part = _dot_nt(jnp.concatenate([kb, q], axis=0), k_pad)
                    a1 = part if a1 is None else a1 + part
                    src = in_place if (lane0 == 0) == (side == 0) else moved
                    heads.append(dict(s=s, h=h, gcol=gcol, grow=src[4 + h:5 + h, :],
                                      qg=q * eg, kd=k * jnp.exp(g_last - gcol), eg_last=jnp.exp(g_last),
                                      rhs=jnp.concatenate([v * beta, kb * eg], axis=1)))
                ha, hb = heads
                diff = jnp.where(low_half, ha["gcol"], hb["gcol"]) - jnp.where(low_half, ha["grow"], hb["grow"])
                decay = jnp.where(mask(MASK_TRI), jnp.exp(jnp.where(mask(MASK_TRI), diff, 0.0)), 0.0)
                pairs.append(dict(heads=heads,
                                  qk=jnp.where(mask(MASK_TRI), a1[c:2 * c] * decay, 0.0),
                                  lmat=jnp.where(mask(MASK_STRICT), a1[0:c] * decay, 0.0)))
                yield

    def solve(pairs):
        for p in pairs:
            p["d"] = masks_ref[MASK_EYE] - p["lmat"] * masks_ref[0]
            p["lh"], p["ll"] = _split(p["lmat"])
        for lvl in range(1, len(GDN_SUB_BLOCKS)):
            off = sel_ref[SEL_SUB + lvl]
            for p in pairs:
                p["dh"], p["dl"] = _split(p["d"])
                p["y"] = pair_product(p["lh"] * off, p["ll"] * off, p["dh"], p["dl"])
            yield
            for p in pairs:
                yh, yl = _split(p["y"])
                p["d"] = p["d"] - pair_product(p["dh"], p["dl"], yh, yl)
            yield
        for p in pairs:
            dh, dl = _split(p["d"])
            (ra_h, ra_l), (rb_h, rb_l) = (_split(hd["rhs"]) for hd in p["heads"])
            zr = jnp.zeros((c, 2 * dk), BF16)
            bd_h = jnp.concatenate([jnp.concatenate([ra_h, zr], axis=1), jnp.concatenate([zr, rb_h], axis=1)], axis=0)
            bd_l = jnp.concatenate([jnp.concatenate([ra_l, zr], axis=1), jnp.concatenate([zr, rb_l], axis=1)], axis=0)
            sol = (jnp.dot(jnp.concatenate([dh, dl], axis=1), jnp.concatenate([bd_h, bd_h], axis=0),
                           preferred_element_type=F32)
                   + jnp.dot(dh, bd_l, preferred_element_type=F32))
            for side, hd in enumerate(p["heads"]):
                hd["sol"] = sol[:, side * 2 * dk:(side + 1) * 2 * dk]
        yield

    def advance(pairs):
        for p in pairs:
            for hd in p["heads"]:
                hd["s_old"] = s_ref[hd["s"], hd["h"]]
                hd["m1"] = _dot(jnp.concatenate([hd["sol"][:, dk:2 * dk], hd["qg"]], axis=0), hd["s_old"])
        yield
        for p in pairs:
            for hd in p["heads"]:
                hd["v_new"] = hd["sol"][:, 0:dk] - hd["m1"][0:c]
                s_ref[hd["s"], hd["h"]] = hd["s_old"] * hd["eg_last"] + _dot_tn(hd["kd"], hd["v_new"])
        yield
        zv = jnp.zeros((c, dk), F32)
        for p in pairs:
            ha, hb = p["heads"]
            v_bd = jnp.concatenate([jnp.concatenate([ha["v_new"], zv], axis=1),
                                    jnp.concatenate([zv, hb["v_new"]], axis=1)], axis=0)
            o_pair = _dot(p["qk"], v_bd)
            for side, hd in enumerate(p["heads"]):
                s, h = hd["s"], hd["h"]
                o = hd["m1"][c:2 * c] + o_pair[:, side * dk:(side + 1) * dk]
                z = qkvz_ref[s, :, QKV_W + h * dk:QKV_W + (h + 1) * dk]
                o = o * lax.rsqrt(jnp.mean(o * o, axis=-1, keepdims=True) + NORM_EPS) * normw * _silu(z)
                o_ref[s, :, h * dk:(h + 1) * dk] = o
        yield

    pairs = []
    _interleave(prep(range(ns), pairs))
    _interleave(solve(pairs))
    _interleave(advance(pairs))


def _gdn_pair(qkvz3, gates3, alog_row, dtb_row, normw):
    bsz, seq, _ = qkvz3.shape
    ns, c = min(GDN_SEQS, bsz), GDN_CHUNK
    hd = (GDN_HEADS, GDN_HEAD_DIM, GDN_HEAD_DIM)
    masks, selectors = _gdn_masks()
    return pl.pallas_call(
        _gdn_pair_kernel,
        grid=(bsz // ns, seq // c),
        in_specs=[pl.BlockSpec((ns, c, QKVZ_W), lambda b, i: (b, i, 0)),
                  pl.BlockSpec((ns, c, GATE_LANES), lambda b, i: (b, i, 0)),
                  _whole(alog_row.shape), _whole(dtb_row.shape), _whole(normw.shape),
                  _whole(masks.shape), _whole(selectors.shape)],
        out_specs=[pl.BlockSpec((ns, c, GDN_WIDTH), lambda b, i: (b, i, 0)),
                   pl.BlockSpec((ns,) + hd, lambda b, i: (b, 0, 0, 0))],
        out_shape=[jax.ShapeDtypeStruct((bsz, seq, GDN_WIDTH), F32),
                   jax.ShapeDtypeStruct((bsz,) + hd, F32)],
        compiler_params=_compiler_params(("arbitrary", "arbitrary")),
        name="gdn_prompt",
    )(qkvz3, gates3, alog_row, dtb_row, normw, masks, selectors)


def _s5_params_kernel(us_ref, h0r_ref, h0i_ref, ar_row_ref, ai_row_ref, ar_col_ref, ai_col_ref, ldt_ref,
                      br_ref, bi_ref, cr_ref, ci_ref,
                      mt_ref, gt_ref, gts_ref, wf_ref, coef_ref, ys_ref, hsr_ref, hsi_ref):
    L = S5_CHUNK
    P = S5_STATE
    dt = jnp.exp(ldt_ref[0])
    ar_row = ar_row_ref[0] * dt
    ai_row = ai_row_ref[0] * dt
    ar_col = ar_col_ref[0] * dt
    ai_col = ai_col_ref[0] * dt
    first = lax.broadcasted_iota(jnp.int32, (1, 2 * P), 1) < P

    ea = jnp.exp(ar_col)
    lbr = ea * jnp.cos(ai_col)
    lbi = ea * jnp.sin(ai_col)
    lam_r = ar_col_ref[0]
    lam_i = ai_col_ref[0]
    den = lam_r * lam_r + lam_i * lam_i
    fr = ((lbr - 1.0) * lam_r + lbi * lam_i) / den
    fi = (lbi * lam_r - (lbr - 1.0) * lam_i) / den
    b_r = br_ref[0]
    b_i = bi_ref[0]
    bbr = fr * b_r - fi * b_i
    bbi = fr * b_i + fi * b_r

    rt = lax.broadcasted_iota(jnp.int32, (S5_GROUP, S5_FLAT), 0)
    lt = lax.broadcasted_iota(jnp.int32, (S5_GROUP, S5_FLAT), 1)
    tile_mat = jnp.where(lt % S5_GROUP == rt, 1.0, 0.0).astype(BF16)
    bwr = _dot_x3(bbr, tile_mat)
    bwi = _dot_x3(bbi, tile_mat)

    tau_g = (L - 1) - lax.broadcasted_iota(jnp.int32, (P, S5_FLAT), 1) // S5_GROUP
    pgr, pgi = _complex_powers(lbr, lbi, tau_g, 4)
    gtr = pgr * bwr - pgi * bwi
    gti = pgr * bwi + pgi * bwr
    gt_ref[0] = jnp.concatenate([gtr, gti], axis=0).astype(BF16)
    gts_ref[0] = jnp.concatenate([gti, gtr], axis=0).astype(BF16)

    nt = L + 1
    e1 = jnp.exp(ar_row)
    l1r = e1 * jnp.cos(ai_row)
    l1i = e1 * jnp.sin(ai_row)
    tau_w = lax.broadcasted_iota(jnp.int32, (nt * S5_GROUP, 2 * P), 0) // S5_GROUP
    pwr, pwi = _complex_powers(l1r, l1i, tau_w, 5)
    c_r = jnp.concatenate([cr_ref[0]] * nt, axis=0)
    c_i = jnp.concatenate([ci_ref[0]] * nt, axis=0)
    wall = jnp.where(first, c_r * pwr - c_i * pwi, -(c_r * pwi + c_i * pwr))
    wf_ref[0] = wall[S5_GROUP:].astype(BF16)

    zw = _dot_hi(wall[0:S5_FLAT], jnp.concatenate([bwr, bwi], axis=0))
    s_of_lane = lax.broadcasted_iota(jnp.int32, (S5_FLAT, S5_FLAT), 1) // S5_GROUP
    mt = zw
    for j in range(4):
        sh = S5_GROUP << j
        shifted = jnp.concatenate([jnp.zeros((sh, S5_FLAT), F32), mt[0:S5_FLAT - sh]], axis=0)
        mt = jnp.where(((s_of_lane >> j) & 1) == 1, shifted, mt)
    mt_ref[0] = mt.astype(BF16)

    p16r = pwr[L * S5_GROUP:L * S5_GROUP + 1, :]
    p16i = pwi[L * S5_GROUP:L * S5_GROUP + 1, :]
    coef_b = jnp.where(first, -p16i, p16i)
    srow = lax.broadcasted_iota(jnp.int32, (8, 2 * P), 0)
    coef_ref[0] = jnp.where(srow == 0, p16r, jnp.where(srow == 1, coef_b, jnp.where(srow == 2, -coef_b, 0.0)))

    l1r = l1r[:, 0:P]
    l1i = l1i[:, 0:P]
    us = us_ref[0]
    nt_dims = (((1,), (1,)), ((), ()))
    bur = _dot_hi(us, bbr, nt_dims)
    bui = _dot_hi(us, bbi, nt_dims)
    h0r = h0r_ref[0]
    h0i = h0i_ref[0]
    hsr = l1r * h0r - l1i * h0i + bur
    hsi = l1r * h0i + l1i * h0r + bui
    hsr_ref[0] = hsr
    hsi_ref[0] = hsi
    ys_ref[0] = _dot_nt(hsr, cr_ref[0][:, 0:P]) - _dot_nt(hsi, ci_ref[0][:, 0:P])


def _s5_params(us_t, h0r_t, h0i_t, ar, ai, ldt, b_r, b_i, c_r, c_i):
    g = S5_GROUPS
    ns = us_t.shape[1]
    twice = lambda a: jnp.concatenate([a, a], axis=-1)
    ar_row = twice(ar).reshape(g, 1, 2 * S5_STATE)
    ai_row = twice(ai).reshape(g, 1, 2 * S5_STATE)
    ar_col = ar.reshape(g, S5_STATE, 1)
    ai_col = ai.reshape(g, S5_STATE, 1)
    ldt3 = ldt.reshape(g, 1, 1)

    def blk(shape):
        return pl.BlockSpec((1,) + shape, lambda i: (i,) + (0,) * len(shape))

    return pl.pallas_call(
        _s5_params_kernel,
        grid=(g,),
        in_specs=[blk((ns, S5_GROUP)), blk((ns, S5_STATE)), blk((ns, S5_STATE)),
                  blk((1, 2 * S5_STATE)), blk((1, 2 * S5_STATE)), blk((S5_STATE, 1)), blk((S5_STATE, 1)), blk((1, 1)),
                  blk((S5_STATE, S5_GROUP)), blk((S5_STATE, S5_GROUP)),
                  blk((S5_GROUP, 2 * S5_STATE)), blk((S5_GROUP, 2 * S5_STATE))],
        out_specs=[blk((S5_FLAT, S5_FLAT)), blk((2 * S5_STATE, S5_FLAT)), blk((2 * S5_STATE, S5_FLAT)),
                   blk((S5_FLAT, 2 * S5_STATE)), blk((8, 2 * S5_STATE)),
                   blk((ns, S5_GROUP)), blk((ns, S5_STATE)), blk((ns, S5_STATE))],
        out_shape=[jax.ShapeDtypeStruct((g, S5_FLAT, S5_FLAT), BF16),
                   jax.ShapeDtypeStruct((g, 2 * S5_STATE, S5_FLAT), BF16),
                   jax.ShapeDtypeStruct((g, 2 * S5_STATE, S5_FLAT), BF16),
                   jax.ShapeDtypeStruct((g, S5_FLAT, 2 * S5_STATE), BF16),
                   jax.ShapeDtypeStruct((g, 8, 2 * S5_STATE), F32),
                   jax.ShapeDtypeStruct((g, ns, S5_GROUP), F32),
                   jax.ShapeDtypeStruct((g, ns, S5_STATE), F32),
                   jax.ShapeDtypeStruct((g, ns, S5_STATE), F32)],
        compiler_params=_compiler_params(("arbitrary",)),
        name="s5_params",
    )(us_t, h0r_t, h0i_t, ar_row, ai_row, ar_col, ai_col, ldt3, b_r, b_i, twice(c_r), twice(c_i))


PIECES = LANE_TILE // S5_GROUP
RELAYOUT_ROWS = S5_CHUNK * S5_CHUNK


def _piece_transpose(tiles):
    tiles = list(tiles)
    piece = lax.broadcasted_iota(jnp.int32, tiles[0].shape, 1) // S5_GROUP
    d = PIECES // 2
    while d:
        keep_low = (piece & d) == 0
        for k in range(PIECES):
            if k & d:
                continue
            a, b = tiles[k], tiles[k + d]
            tiles[k] = jnp.where(keep_low, a, pltpu.roll(b, S5_GROUP * d, axis=1))
            tiles[k + d] = jnp.where(keep_low, pltpu.roll(a, LANE_TILE - S5_GROUP * d, axis=1), b)
        d //= 2
    return tiles


def _s5_seq_kernel(u_ref, mt_ref, gt_ref, gts_ref, wf_ref, coef_ref, ys_ref, hl_ref,
                   uflat_ref, e_ref, es_ref, hin_ref, yflat_ref, *, nchunk):
    rr = RELAYOUT_ROWS
    ro = lax.broadcasted_iota(jnp.int32, (rr, rr), 0)
    ri = lax.broadcasted_iota(jnp.int32, (rr, rr), 1)
    perm = jnp.where((ro // S5_CHUNK == ri % S5_CHUNK) & (ro % S5_CHUNK == ri // S5_CHUNK), 1.0, 0.0).astype(BF16)

    def gather_in(m, carry):
        rows = pl.ds(pl.multiple_of(m * rr, rr), rr)
        xp = jnp.dot(perm, u_ref[rows, :].astype(BF16), preferred_element_type=F32)
        crow = pl.ds(pl.multiple_of(m * S5_CHUNK, S5_CHUNK), S5_CHUNK)
        for j in range(S5_FLAT // LANE_TILE):
            for cb in range(S5_WIDTH // LANE_TILE):
                tiles = [xp[(PIECES * j + k) * S5_CHUNK:(PIECES * j + k + 1) * S5_CHUNK,
                            cb * LANE_TILE:(cb + 1) * LANE_TILE] for k in range(PIECES)]
                for p, tile in enumerate(_piece_transpose(tiles)):
                    uflat_ref[PIECES * cb + p, crow, j * LANE_TILE:(j + 1) * LANE_TILE] = tile.astype(BF16)
        return carry

    lax.fori_loop(0, nchunk // S5_CHUNK, gather_in, 0, unroll=2)

    for g in range(S5_GROUPS):
        lanes = slice(g * LANE_TILE, (g + 1) * LANE_TILE)
        ug = uflat_ref[g]
        e_ref[:, lanes] = _dot_nt(ug, gt_ref[g])
        es_ref[:, lanes] = _dot_nt(ug, gts_ref[g])

    ca = coef_ref[0:1, :]
    cb = coef_ref[1:2, :]
    cbs = coef_ref[2:3, :]

    def scan(c, carry):
        h, hs = carry
        row = pl.ds(c, 1)
        hin_ref[row, :] = h
        return ca * h + cb * hs + e_ref[row, :], ca * hs + cbs * h + es_ref[row, :]

    zero = jnp.zeros((1, S5_GROUPS * LANE_TILE), F32)
    h_last, _ = lax.fori_loop(0, nchunk, scan, (zero, zero))
    hl_ref[0] = h_last

    for g in range(S5_GROUPS):
        lanes = slice(g * LANE_TILE, (g + 1) * LANE_TILE)
        yflat_ref[g] = _dot_nt(uflat_ref[g], mt_ref[g]) + _dot_nt(hin_ref[:, lanes], wf_ref[g])

    def gather_out(m, carry):
        crow = pl.ds(pl.multiple_of(m * S5_CHUNK, S5_CHUNK), S5_CHUNK)
        by_time = [[None] * (S5_WIDTH // LANE_TILE) for _ in range(S5_CHUNK)]
        for j in range(S5_FLAT // LANE_TILE):
            for cb in range(S5_WIDTH // LANE_TILE):
                tiles = [yflat_ref[PIECES * cb + k, crow, j * LANE_TILE:(j + 1) * LANE_TILE] for k in range(PIECES)]
                for p, tile in enumerate(_piece_transpose(tiles)):
                    by_time[PIECES * j + p][cb] = tile
        z = jnp.concatenate([jnp.concatenate(row, axis=1) for row in by_time], axis=0)
        rows = pl.ds(pl.multiple_of(m * rr, rr), rr)
        ys_ref[rows, :] = _dot_x3_left(perm, z)
        return carry

    lax.fori_loop(0, nchunk // S5_CHUNK, gather_out, 0, unroll=2)


def _s5_seq(u, mt, gt, gts, wf, coef, bsz, seq):
    nchunk = seq // S5_CHUNK
    width = S5_GROUPS * LANE_TILE

    def resident(a):
        return pl.BlockSpec(a.shape, lambda b: (0,) * a.ndim, pipeline_mode=pl.Buffered(1))

    return pl.pallas_call(
        functools.partial(_s5_seq_kernel, nchunk=nchunk),
        grid=(bsz,),
        in_specs=[pl.BlockSpec((seq, S5_WIDTH), lambda b: (b, 0))] + [resident(a) for a in (mt, gt, gts, wf, coef)],
        out_specs=[pl.BlockSpec((seq, S5_WIDTH), lambda b: (b, 0)),
                   pl.BlockSpec((1, 1, width), lambda b: (b, 0, 0))],
        out_shape=[jax.ShapeDtypeStruct((bsz * seq, S5_WIDTH), F32),
                   jax.ShapeDtypeStruct((bsz, 1, width), F32)],
        scratch_shapes=[pltpu.VMEM((S5_GROUPS, nchunk, S5_FLAT), BF16),
                        pltpu.VMEM((nchunk, width), F32), pltpu.VMEM((nchunk, width), F32),
                        pltpu.VMEM((nchunk, width), F32),
                        pltpu.VMEM((S5_GROUPS, nchunk, S5_FLAT), F32)],
        compiler_params=_compiler_params(("arbitrary",)),
        name="s5_seq",
    )(u, mt, gt, gts, wf, coef)


def _sample_pre_kernel(qkvz_ref, gates_ref, conv_ref, wconv_ref, alog_row_ref, dtb_row_ref,
                       newconv_ref, ops_ref, sc_ref):
    dk = GDN_HEAD_DIM
    x_new = qkvz_ref[:, 0:QKV_W]
    wc = wconv_ref[...]
    conv = x_new * wc[3:4, :]
    for j in range(CONV_W - 1):
        conv = conv + conv_ref[j] * wc[j:j + 1, :]
    qkv = _silu(conv)
    newconv_ref[0] = conv_ref[1]
    newconv_ref[1] = conv_ref[2]
    newconv_ref[2] = x_new

    beta_c, g_c = _gate_values(gates_ref[...], alog_row_ref[...], dtb_row_ref[...])
    scale = GDN_HEAD_DIM ** -0.5
    lane = lax.broadcasted_iota(jnp.int32, sc_ref.shape, 1)
    sc = jnp.zeros(sc_ref.shape, F32)
    for h in range(GDN_HEADS):
        q = _l2norm(qkv[:, h * dk:(h + 1) * dk]) * scale
        k = _l2norm(qkv[:, GDN_WIDTH + h * dk:GDN_WIDTH + (h + 1) * dk])
        v = qkv[:, 2 * GDN_WIDTH + h * dk:2 * GDN_WIDTH + (h + 1) * dk]
        beta = beta_c[:, h:h + 1]
        eg = jnp.exp(g_c[:, 4 + h:5 + h])
        cols = slice(h * dk, (h + 1) * dk)
        ops_ref[0, :, cols] = (beta * eg) * k
        ops_ref[1, :, cols] = q * eg
        ops_ref[2, :, cols] = k
        ops_ref[3, :, cols] = beta * v
        sc = jnp.where(lane == h, jnp.sum(q * k, axis=-1, keepdims=True), sc)
        sc = jnp.where(lane == 4 + h, eg, sc)
    sc_ref[...] = sc


def _sample_pre(qkvz_s, gates_s, conv_t, wconv, alog_row, dtb_row):
    ns = qkvz_s.shape[0]
    return pl.pallas_call(
        _sample_pre_kernel,
        out_shape=[jax.ShapeDtypeStruct((CONV_W - 1, ns, QKV_W), F32),
                   jax.ShapeDtypeStruct((4, ns, GDN_WIDTH), F32),
                   jax.ShapeDtypeStruct((ns, GATE_LANES), F32)],
        compiler_params=pltpu.CompilerParams(vmem_limit_bytes=VMEM_LIMIT),
        name="sample_pre",
    )(qkvz_s, gates_s, conv_t, wconv, alog_row, dtb_row)


def _sample_state_kernel(s_ref, ops_ref, sc_ref, z_ref, normw_ref, snew_ref, o_ref, *, nb):
    dk = GDN_HEAD_DIM
    row = lax.broadcasted_iota(jnp.int32, (8, dk), 0)
    units = [(j, h) for j in range(nb) for h in range(GDN_HEADS)]
    m1 = {}
    for j, h in units:
        cols = slice(h * dk, (h + 1) * dk)
        w = ops_ref[0, j:j + 1, cols]
        qg = ops_ref[1, j:j + 1, cols]
        lhs = jnp.where(row == 0, jnp.broadcast_to(w, (8, dk)), jnp.broadcast_to(qg, (8, dk)))
        m1[j, h] = _dot(lhs, s_ref[j, h])
    for j, h in units:
        cols = slice(h * dk, (h + 1) * dk)
        k = ops_ref[2, j:j + 1, cols]
        u = ops_ref[3, j:j + 1, cols]
        v_new = u - m1[j, h][0:1, :]
        qk = sc_ref[j:j + 1, h:h + 1]
        eg = sc_ref[j:j + 1, 4 + h:5 + h]
        o_ref[j:j + 1, cols] = m1[j, h][1:2, :] + qk * v_new
        k8 = jnp.where(row == 0, jnp.broadcast_to(k, (8, dk)), 0.0)
        snew_ref[j, h] = s_ref[j, h] * eg + _dot_tn(k8, jnp.broadcast_to(v_new, (8, dk)))
    normw = normw_ref[...]
    for h in range(GDN_HEADS):
        o = o_ref[:, h * dk:(h + 1) * dk]
        z = z_ref[:, h * dk:(h + 1) * dk]
        o_ref[:, h * dk:(h + 1) * dk] = (o * lax.rsqrt(jnp.mean(o * o, axis=-1, keepdims=True) + NORM_EPS)
                                         * normw * _silu(z))


def _sample_state(state, ops, sc, z, normw, nb=8):
    ns = state.shape[0]
    hd = (GDN_HEADS, GDN_HEAD_DIM, GDN_HEAD_DIM)
    return pl.pallas_call(
        functools.partial(_sample_state_kernel, nb=nb),
        grid=(ns // nb,),
        in_specs=[pl.BlockSpec((nb,) + hd, lambda i: (i, 0, 0, 0)),
                  pl.BlockSpec((4, nb, GDN_WIDTH), lambda i: (0, i, 0)),
                  pl.BlockSpec((nb, GATE_LANES), lambda i: (i, 0)),
                  pl.BlockSpec((nb, GDN_WIDTH), lambda i: (i, 0)),
                  _whole(normw.shape)],
        out_specs=[pl.BlockSpec((nb,) + hd, lambda i: (i, 0, 0, 0)),
                   pl.BlockSpec((nb, GDN_WIDTH), lambda i: (i, 0))],
        out_shape=[jax.ShapeDtypeStruct(state.shape, F32),
                   jax.ShapeDtypeStruct((ns, GDN_WIDTH), F32)],
        compiler_params=_compiler_params(("arbitrary",)),
        name="sample_state",
    )(state, ops, sc, z, normw)


def _post_kernel(x_ref, o_ref, ys_ref, u_ref, d_ref, wglu_ref, bglu_ref, wout_ref, g1_ref, b1_ref,
                 wff1_ref, wff2_ref, g2_ref, b2_ref, y_ref, *, ff_chunk):
    n = x_ref.shape[0]
    halves = [slice(0, n // 2), slice(n // 2, n)] if n >= 256 else [slice(0, n)]
    st = [dict(rows=r) for r in halves]

    def head(s):
        r = s["rows"]
        ys = jax.nn.gelu(ys_ref[r, :] + d_ref[...] * u_ref[r, :])
        ys = ys * jax.nn.sigmoid(_dot(ys, wglu_ref[...]) + bglu_ref[...])
        yield
        mix = _dot(o_ref[r, :], wout_ref[0:GDN_WIDTH, :]) + _dot(ys, wout_ref[GDN_WIDTH:, :])
        yield
        s["x1"] = _layernorm(DN_ALPHA * x_ref[r, :] + mix, g1_ref[...], b1_ref[...])
        s["x1b"] = s["x1"].astype(BF16)
        yield

    def mlp(s):
        acc = jnp.zeros(s["x1"].shape, F32)
        for f in range(D_FF // ff_chunk):
            hcol = jnp.dot(s["x1b"], wff1_ref[:, f * ff_chunk:(f + 1) * ff_chunk], preferred_element_type=F32)
            hcol = jnp.square(jnp.maximum(hcol, 0.0))
            yield
            acc = acc + _dot(hcol, wff2_ref[f * ff_chunk:(f + 1) * ff_chunk, :])
            yield
        s["acc"] = acc

    def tail(s):
        y_ref[s["rows"], :] = _layernorm(DN_ALPHA * s["x1"] + s["acc"], g2_ref[...], b2_ref[...])
        yield

    _interleave(head(st[0]))
    for i, s in enumerate(st):
        others = [head(st[i + 1])] if i + 1 < len(st) else []
        if i > 0:
            others.append(tail(st[i - 1]))
        _interleave(mlp(s), *others)
    _interleave(tail(st[-1]))


def _post(x, o, ys, u, *weights, tile):
    n = x.shape[0]
    tok = lambda w: pl.BlockSpec((tile, w), lambda i: (i, 0))

    def resident(a):
        return pl.BlockSpec(a.shape, lambda i: (0,) * a.ndim, pipeline_mode=pl.Buffered(1))

    return pl.pallas_call(
        functools.partial(_post_kernel, ff_chunk=1024),
        grid=(n // tile,),
        in_specs=[tok(D_MODEL), tok(GDN_WIDTH), tok(S5_WIDTH), tok(S5_WIDTH)] + [resident(a) for a in weights],
        out_specs=tok(D_MODEL),
        out_shape=jax.ShapeDtypeStruct((n, D_MODEL), F32),
        compiler_params=_compiler_params(("arbitrary",)),
        name="post",
    )(x, o, ys, u, *weights)


def kernel(x_prompt, x_sample, state_gdn, state_conv, state_ssm_re, state_ssm_im, w_in, w_conv, gdn_a_log,
           gdn_dt_bias, gdn_norm_w, s5_a_re, s5_a_im, s5_b_re, s5_b_im, s5_c_re, s5_c_im, s5_d, s5_log_dt,
           w_glu, b_glu, w_out, ln1_g, ln1_b, w_ff1, w_ff2, ln2_g, ln2_b):
    bsz, seq, _ = x_prompt.shape
    ns = x_sample.shape[0]
    l = 0

    w = w_in[l]
    i_u = QKVZ_W + 2 * GDN_HEADS
    w_cat = jnp.concatenate([w[:, :QKVZ_W], w[:, i_u:], w[:, QKVZ_W:i_u],
                             jnp.zeros((D_MODEL, GATE_LANES - 2 * GDN_HEADS), F32)], axis=1).astype(BF16)
    lane_pad = (GDN_HEADS, GATE_LANES - 2 * GDN_HEADS)
    alog_row = jnp.pad(gdn_a_log[l], lane_pad).reshape(1, GATE_LANES)
    dtb_row = jnp.pad(gdn_dt_bias[l], lane_pad).reshape(1, GATE_LANES)
    normw = gdn_norm_w[l].reshape(1, GDN_HEAD_DIM)
    wconv = w_conv[l]
    row = lambda a: a.reshape(1, -1)
    post_w = (row(s5_d[l]), w_glu[l].astype(BF16), row(b_glu[l]), w_out[l].astype(BF16), row(ln1_g[l]), row(ln1_b[l]),
              w_ff1[l].astype(BF16), w_ff2[l].astype(BF16), row(ln2_g[l]), row(ln2_b[l]))

    xp = x_prompt.reshape(bsz * seq, D_MODEL)
    xs = x_sample.reshape(ns, D_MODEL)
    qkvz_p, gates_p, u_p, tail_p = _proj_conv(xp, w_cat, wconv, seq, tile=min(512, seq))
    qkvz_s, gates_s, u_s = _proj(xs, w_cat, tile=ns)

    o_p, gdn_p = _gdn_pair(qkvz_p.reshape(bsz, seq, QKVZ_W), gates_p.reshape(bsz, seq, GATE_LANES),
                           alog_row, dtb_row, normw)
    o_p = o_p.reshape(bsz * seq, GDN_WIDTH)
    conv_p = tail_p[:, 8 - (CONV_W - 1):, :]

    conv_t = jnp.transpose(state_conv[l], (1, 0, 2))
    newconv_t, ops_s, sc = _sample_pre(qkvz_s, gates_s, conv_t, wconv, alog_row, dtb_row)
    gdn_s, o_s = _sample_state(state_gdn[l], ops_s, sc, qkvz_s[:, QKV_W:], normw)
    conv_s = jnp.transpose(newconv_t, (1, 0, 2))

    us_t = u_s.reshape(ns, S5_GROUPS, S5_GROUP).transpose(1, 0, 2)
    h0r_t = state_ssm_re[l].transpose(1, 0, 2)
    h0i_t = state_ssm_im[l].transpose(1, 0, 2)
    mt, gt, gts, wf, coef, ys_t, hsr, hsi = _s5_params(us_t, h0r_t, h0i_t, s5_a_re[l], s5_a_im[l], s5_log_dt[l],
                                                       s5_b_re[l], s5_b_im[l], s5_c_re[l], s5_c_im[l])
    coef_rows = coef.transpose(1, 0, 2).reshape(8, S5_GROUPS * LANE_TILE)
    ys_p, h_last = _s5_seq(u_p, mt, gt, gts, wf, coef_rows, bsz, seq)
    h_last = h_last.reshape(bsz, S5_GROUPS, 2 * S5_STATE)
    ys_s = ys_t.transpose(1, 0, 2).reshape(ns, S5_WIDTH)

    y_p = _post(xp, o_p, ys_p, u_p, *post_w, tile=512)
    y_s = _post(xs, o_s, ys_s, u_s, *post_w, tile=ns)

    t3 = lambda a: a.transpose(1, 0, 2)[None]
    return (y_p.reshape(bsz, seq, D_MODEL), y_s.reshape(ns, 1, D_MODEL),
            gdn_p[None], conv_p[None], h_last[None, :, :, :S5_STATE], h_last[None, :, :, S5_STATE:],
            gdn_s[None], conv_s[None], t3(hsr), t3(hsi))
```

```python
import functools

import jax
import jax.numpy as jnp
import numpy as np
from jax import lax
from jax.experimental import pallas as pl
from jax.experimental.pallas import tpu as pltpu

F32 = jnp.float32
BF16 = jnp.bfloat16

D_MODEL = 1024
GDN_HEADS = 4
GDN_HEAD_DIM = 128
GDN_WIDTH = GDN_HEADS * GDN_HEAD_DIM
CONV_W = 4
GDN_CHUNK = 64
S5_WIDTH = D_MODEL - GDN_WIDTH
S5_GROUP = 16
S5_GROUPS = S5_WIDTH // S5_GROUP
S5_STATE = 64
D_FF = 4 * D_MODEL
DEPTH = 1
DN_ALPHA = (2.0 * DEPTH) ** 0.25
NORM_EPS = 1e-6
QKV_W = 3 * GDN_WIDTH
QKVZ_W = QKV_W + GDN_WIDTH
LANE_TILE = 128
GATE_LANES = LANE_TILE
S5_CHUNK = 16
S5_FLAT = S5_CHUNK * S5_GROUP
VMEM_LIMIT = 56 * 1024 * 1024


def _dot(a, b):
    return jnp.dot(a.astype(BF16), b.astype(BF16), preferred_element_type=F32)


def _dot_nt(a, b):
    return lax.dot_general(a.astype(BF16), b.astype(BF16), (((1,), (1,)), ((), ())), preferred_element_type=F32)


def _dot_tn(a, b):
    return lax.dot_general(a.astype(BF16), b.astype(BF16), (((0,), (0,)), ((), ())), preferred_element_type=F32)


def _split(x):
    hi = x.astype(BF16)
    lo = (x - hi.astype(F32)).astype(BF16)
    return hi, lo


def _dot_hi(a, b, dims=(((1,), (0,)), ((), ()))):
    ah, al = _split(a)
    bh, bl = _split(b)
    d = functools.partial(lax.dot_general, dimension_numbers=dims, preferred_element_type=F32)
    return d(ah, bh) + (d(al, bh) + d(ah, bl))


def _dot_x3(a, b_exact):
    a1 = a.astype(BF16)
    r1 = a - a1.astype(F32)
    a2 = r1.astype(BF16)
    a3 = (r1 - a2.astype(F32)).astype(BF16)
    return _dot(a1, b_exact) + (_dot(a2, b_exact) + _dot(a3, b_exact))


def _dot_x3_left(a_exact, b):
    b1 = b.astype(BF16)
    r1 = b - b1.astype(F32)
    b2 = r1.astype(BF16)
    b3 = (r1 - b2.astype(F32)).astype(BF16)
    return _dot(a_exact, b1) + (_dot(a_exact, b2) + _dot(a_exact, b3))


def _complex_powers(base_r, base_i, exponent, nbits):
    shape = exponent.shape
    pr = jnp.ones(shape, F32)
    pi = jnp.zeros(shape, F32)
    br, bi = base_r, base_i
    for j in range(nbits):
        bit = ((exponent >> j) & 1) == 1
        pr, pi = jnp.where(bit, pr * br - pi * bi, pr), jnp.where(bit, pr * bi + pi * br, pi)
        br, bi = br * br - bi * bi, 2.0 * br * bi
    return pr, pi


def _silu(x):
    return x * jax.nn.sigmoid(x)


def _layernorm(x, g, b):
    mu = jnp.mean(x, axis=-1, keepdims=True)
    xc = x - mu
    var = jnp.mean(xc * xc, axis=-1, keepdims=True)
    return xc * lax.rsqrt(var + NORM_EPS) * g + b


def _l2norm(a):
    return a * lax.rsqrt(jnp.sum(a * a, axis=-1, keepdims=True) + NORM_EPS)


def _compiler_params(semantics):
    return pltpu.CompilerParams(dimension_semantics=semantics, vmem_limit_bytes=VMEM_LIMIT)


def _whole(shape):
    n = len(shape)
    return pl.BlockSpec(shape, lambda *_: (0,) * n)


def _interleave(*gens):
    live = list(gens)
    while live:
        for gen in list(live):
            if next(gen, "done") == "done":
                live.remove(gen)


def _proj_kernel(x_ref, w_ref, qkvz_ref, gates_ref, u_ref):
    r = jnp.dot(x_ref[...].astype(BF16), w_ref[...], preferred_element_type=F32)
    qkvz_ref[...] = r[:, 0:QKVZ_W]
    u_ref[...] = r[:, QKVZ_W:QKVZ_W + S5_WIDTH]
    gates_ref[...] = r[:, QKVZ_W + S5_WIDTH:]


PROJ_PARTS = 4


def _proj_conv_kernel(x_ref, w_ref, wconv_ref, qkvz_ref, gates_ref, u_ref, tail_ref, xpad_a_ref, xpad_b_ref, *,
                      tiles_per_seq):
    tile = x_ref.shape[0]
    dk = GDN_HEAD_DIM
    parts = PROJ_PARTS
    pr = tile // parts
    bufs = (xpad_a_ref, xpad_b_ref)
    last = bufs[(parts - 1) % 2]

    @pl.when(pl.program_id(0) % tiles_per_seq == 0)
    def _():
        last[pr:pr + 8, :] = jnp.zeros((8, QKV_W), F32)

    wc = wconv_ref[...]
    scale = GDN_HEAD_DIM ** -0.5

    def project(p):
        buf, prev = bufs[p % 2], bufs[(p - 1) % 2]
        rows = slice(p * pr, (p + 1) * pr)
        buf[0:8, :] = prev[pr:pr + 8, :]
        xb = x_ref[rows, :].astype(BF16)
        step = 2 * LANE_TILE
        for c0 in range(0, w_ref.shape[1], step):
            c1 = min(c0 + step, w_ref.shape[1])
            r = jnp.dot(xb, w_ref[:, c0:c1], preferred_element_type=F32)
            if c1 <= QKV_W:
                buf[8:8 + pr, c0:c1] = r
            elif c1 <= QKVZ_W:
                qkvz_ref[rows, c0:c1] = r
            elif c1 <= QKVZ_W + S5_WIDTH:
                u_ref[rows, c0 - QKVZ_W:c1 - QKVZ_W] = r
            else:
                gates_ref[rows, :] = r
            yield

    def activate(p):
        buf = bufs[p % 2]
        for lt in range(QKV_W // dk):
            cols = slice(lt * dk, (lt + 1) * dk)
            conv = buf[8:8 + pr, cols] * wc[3:4, cols]
            for j in range(1, CONV_W):
                conv = conv + buf[8 - j:8 - j + pr, cols] * wc[3 - j:4 - j, cols]
            act = _silu(conv)
            if lt < GDN_HEADS:
                act = _l2norm(act) * scale
            elif lt < 2 * GDN_HEADS:
                act = _l2norm(act)
            qkvz_ref[p * pr:(p + 1) * pr, cols] = act
            yield

    _interleave(project(0))
    for p in range(1, parts):
        _interleave(project(p), activate(p - 1))
    _interleave(activate(parts - 1))
    tail_ref[0] = last[pr:pr + 8, :]


def _proj(x, w, tile):
    n = x.shape[0]
    return pl.pallas_call(
        _proj_kernel,
        grid=(n // tile,),
        in_specs=[pl.BlockSpec((tile, D_MODEL), lambda i: (i, 0)),
                  pl.BlockSpec(w.shape, lambda i: (0, 0), pipeline_mode=pl.Buffered(1))],
        out_specs=[pl.BlockSpec((tile, QKVZ_W), lambda i: (i, 0)),
                   pl.BlockSpec((tile, GATE_LANES), lambda i: (i, 0)),
                   pl.BlockSpec((tile, S5_WIDTH), lambda i: (i, 0))],
        out_shape=[jax.ShapeDtypeStruct((n, QKVZ_W), F32),
                   jax.ShapeDtypeStruct((n, GATE_LANES), F32),
                   jax.ShapeDtypeStruct((n, S5_WIDTH), F32)],
        compiler_params=_compiler_params(("arbitrary",)),
        name="proj",
    )(x, w)


def _proj_conv(x, w, wconv, seq, tile):
    n = x.shape[0]
    tiles_per_seq = seq // tile
    assert PROJ_PARTS % 2 == 0 and tile % (8 * PROJ_PARTS) == 0 and seq % tile == 0
    return pl.pallas_call(
        functools.partial(_proj_conv_kernel, tiles_per_seq=tiles_per_seq),
        grid=(n // tile,),
        in_specs=[pl.BlockSpec((tile, D_MODEL), lambda i: (i, 0)),
                  pl.BlockSpec(w.shape, lambda i: (0, 0), pipeline_mode=pl.Buffered(1)),
                  _whole(wconv.shape)],
        out_specs=[pl.BlockSpec((tile, QKVZ_W), lambda i: (i, 0)),
                   pl.BlockSpec((tile, GATE_LANES), lambda i: (i, 0)),
                   pl.BlockSpec((tile, S5_WIDTH), lambda i: (i, 0)),
                   pl.BlockSpec((1, 8, QKV_W), lambda i: (i // tiles_per_seq, 0, 0))],
        out_shape=[jax.ShapeDtypeStruct((n, QKVZ_W), F32),
                   jax.ShapeDtypeStruct((n, GATE_LANES), F32),
                   jax.ShapeDtypeStruct((n, S5_WIDTH), F32),
                   jax.ShapeDtypeStruct((n // seq, 8, QKV_W), F32)],
        scratch_shapes=[pltpu.VMEM((8 + tile // PROJ_PARTS, QKV_W), F32)] * 2,
        compiler_params=_compiler_params(("arbitrary",)),
        name="proj_conv",
    )(x, w, wconv)


GDN_SEQS = 8
GDN_SUB_BLOCKS = (1, 2, 4, 8, 16, 32)
MASK_TRI, MASK_STRICT, MASK_EYE = len(GDN_SUB_BLOCKS), len(GDN_SUB_BLOCKS) + 1, len(GDN_SUB_BLOCKS) + 2
SEL_LOW, SEL_HIGH, SEL_SUB = 0, 1, 2


def _gate_values(logits, a_log, dt_bias):
    beta = jax.nn.sigmoid(logits)
    g = -jnp.exp(a_log) * jax.nn.softplus(logits + dt_bias)
    return beta, g


def _gdn_masks():
    c = GDN_CHUNK
    r = np.arange(c)[:, None]
    col = np.arange(2 * c)[None, :] % c
    sub = [(r // (2 * b) == col // (2 * b)) & ((r // b) % 2 == 1) & ((col // b) % 2 == 0) for b in GDN_SUB_BLOCKS]
    masks = np.stack(sub + [col <= r, col < r, col == r]).astype(np.float32)
    lane = np.broadcast_to(np.arange(2 * c)[None, :], (c, 2 * c))
    selectors = np.stack([lane < c, lane >= c] + sub).astype(np.float32)
    return jnp.asarray(masks), jnp.asarray(selectors, dtype=BF16)


def _gdn_pair_kernel(qkvz_ref, gates_ref, alog_row_ref, dtb_row_ref, normw_ref, masks_ref, sel_ref, o_ref, s_ref):
    step = pl.program_id(1)
    ns = qkvz_ref.shape[0]
    c = GDN_CHUNK
    dk = GDN_HEAD_DIM
    rows = ns * c

    @pl.when(step == 0)
    def _():
        s_ref[...] = jnp.zeros_like(s_ref)

    beta_c, g_c = _gate_values(gates_ref[...].reshape(rows, GATE_LANES), alog_row_ref[...], dtb_row_ref[...])

    ri = lax.broadcasted_iota(jnp.int32, (LANE_TILE, LANE_TILE), 0)
    ci = lax.broadcasted_iota(jnp.int32, (LANE_TILE, LANE_TILE), 1)
    tril_bd = jnp.where(((ri // c) == (ci // c)) & (ci <= ri), 1.0, 0.0).astype(BF16)
    slabs = [_dot_x3_left(tril_bd, g_c[i * LANE_TILE:(i + 1) * LANE_TILE, :]) for i in range(rows // LANE_TILE)]
    gc_c = jnp.concatenate(slabs, axis=0)
    gc_t = [slab.T for slab in slabs]
    gc_t_rolled = [pltpu.roll(t, c, axis=1) for t in gc_t]

    mask = lambda i: masks_ref[i] > 0.5
    low_half = lax.broadcasted_iota(jnp.int32, (1, 2 * c), 1) < c
    normw = normw_ref[...]
    zeros_k = jnp.zeros((c, dk), F32)

    def block_diag(x):
        return jnp.concatenate([x * sel_ref[SEL_LOW], x * sel_ref[SEL_HIGH]], axis=0)

    def pair_product(ph, pl_, yh, yl):
        bdh = block_diag(yh)
        first = jnp.dot(jnp.concatenate([ph, pl_], axis=1), jnp.concatenate([bdh, bdh], axis=0),
                        preferred_element_type=F32)
        return first + jnp.dot(ph, block_diag(yl), preferred_element_type=F32)

    def prep(seqs, pairs):
        for s in seqs:
            r0 = s * c
            lane0 = r0 % GATE_LANES
            blk = r0 // GATE_LANES
            in_place, moved = (gc_t[blk], gc_t_rolled[blk])
            for h0 in range(0, GDN_HEADS, 2):
                heads, a1 = [], None
                for side, h in enumerate((h0, h0 + 1)):
                    q = qkvz_ref[s, :, h * dk:(h + 1) * dk]
                    k = qkvz_ref[s, :, GDN_WIDTH + h * dk:GDN_WIDTH + (h + 1) * dk]
                    v = qkvz_ref[s, :, 2 * GDN_WIDTH + h * dk:2 * GDN_WIDTH + (h + 1) * dk]
                    beta = beta_c[r0:r0 + c, h:h + 1]
                    gcol = gc_c[r0:r0 + c, 4 + h:5 + h]
                    eg = jnp.exp(gcol)
                    g_last = gcol[c - 1:c, :]
                    kb = k * beta
                    k_pad = jnp.concatenate([k, zeros_k] if side == 0 else [zeros_k, k], axis=0)
                    part = _dot_nt(jnp.concatenate([kb, q], axis=0), k_pad)
                    a1 = part if a1 is None else a1 + part
                    src = in_place if (lane0 == 0) == (side == 0) else moved
                    heads.append(dict(s=s, h=h, gcol=gcol, grow=src[4 + h:5 + h, :],
                                      qg=q * eg, kd=k * jnp.exp(g_last - gcol), eg_last=jnp.exp(g_last),
                                      rhs=jnp.concatenate([v * beta, kb * eg], axis=1)))
                ha, hb = heads
                diff = jnp.where(low_half, ha["gcol"], hb["gcol"]) - jnp.where(low_half, ha["grow"], hb["grow"])
                decay = jnp.where(mask(MASK_TRI), jnp.exp(jnp.where(mask(MASK_TRI), diff, 0.0)), 0.0)
                pairs.append(dict(heads=heads,
                                  qk=jnp.where(mask(MASK_TRI), a1[c:2 * c] * decay, 0.0),
                                  lmat=jnp.where(mask(MASK_STRICT), a1[0:c] * decay, 0.0)))
                yield

    def solve(pairs):
        for p in pairs:
            p["d"] = masks_ref[MASK_EYE] - p["lmat"] * masks_ref[0]
            p["lh"], p["ll"] = _split(p["lmat"])
        for lvl in range(1, len(GDN_SUB_BLOCKS)):
            off = sel_ref[SEL_SUB + lvl]
            for p in pairs:
                p["dh"], p["dl"] = _split(p["d"])
                p["y"] = pair_product(p["lh"] * off, p["ll"] * off, p["dh"], p["dl"])
            yield
            for p in pairs:
                yh, yl = _split(p["y"])
                p["d"] = p["d"] - pair_product(p["dh"], p["dl"], yh, yl)
            yield
        for p in pairs:
            dh, dl = _split(p["d"])
            (ra_h, ra_l), (rb_h, rb_l) = (_split(hd["rhs"]) for hd in p["heads"])
            zr = jnp.zeros((c, 2 * dk), BF16)
            bd_h = jnp.concatenate([jnp.concatenate([ra_h, zr], axis=1), jnp.concatenate([zr, rb_h], axis=1)], axis=0)
            bd_l = jnp.concatenate([jnp.concatenate([ra_l, zr], axis=1), jnp.concatenate([zr, rb_l], axis=1)], axis=0)
            sol = (jnp.dot(jnp.concatenate([dh, dl], axis=1), jnp.concatenate([bd_h, bd_h], axis=0),
                           preferred_element_type=F32)
                   + jnp.dot(dh, bd_l, preferred_element_type=F32))
            for side, hd in enumerate(p["heads"]):
                hd["sol"] = sol[:, side * 2 * dk:(side + 1) * 2 * dk]
        yield

    def advance(pairs):
        for p in pairs:
            for hd in p["heads"]:
                hd["s_old"] = s_ref[hd["s"], hd["h"]]
                hd["m1"] = _dot(jnp.concatenate([hd["sol"][:, dk:2 * dk], hd["qg"]], axis=0), hd["s_old"])
        yield
        for p in pairs:
            for hd in p["heads"]:
                hd["v_new"] = hd["sol"][:, 0:dk] - hd["m1"][0:c]
                s_ref[hd["s"], hd["h"]] = hd["s_old"] * hd["eg_last"] + _dot_tn(hd["kd"], hd["v_new"])
        yield
        zv = jnp.zeros((c, dk), F32)
        for p in pairs:
            ha, hb = p["heads"]
            v_bd = jnp.concatenate([jnp.concatenate([ha["v_new"], zv], axis=1),
                                    jnp.concatenate([zv, hb["v_new"]], axis=1)], axis=0)
            o_pair = _dot(p["qk"], v_bd)
            for side, hd in enumerate(p["heads"]):
                s, h = hd["s"], hd["h"]
                o = hd["m1"][c:2 * c] + o_pair[:, side * dk:(side + 1) * dk]
                z = qkvz_ref[s, :, QKV_W + h * dk:QKV_W + (h + 1) * dk]
                o = o * lax.rsqrt(jnp.mean(o * o, axis=-1, keepdims=True) + NORM_EPS) * normw * _silu(z)
                o_ref[s, :, h * dk:(h + 1) * dk] = o
        yield

    pairs = []
    _interleave(prep(range(ns), pairs))
    _interleave(solve(pairs))
    _interleave(advance(pairs))


def _gdn_pair(qkvz3, gates3, alog_row, dtb_row, normw):
    bsz, seq, _ = qkvz3.shape
    ns, c = min(GDN_SEQS, bsz), GDN_CHUNK
    hd = (GDN_HEADS, GDN_HEAD_DIM, GDN_HEAD_DIM)
    masks, selectors = _gdn_masks()
    return pl.pallas_call(
        _gdn_pair_kernel,
        grid=(bsz // ns, seq // c),
        in_specs=[pl.BlockSpec((ns, c, QKVZ_W), lambda b, i: (b, i, 0)),
                  pl.BlockSpec((ns, c, GATE_LANES), lambda b, i: (b, i, 0)),
                  _whole(alog_row.shape), _whole(dtb_row.shape), _whole(normw.shape),
                  _whole(masks.shape), _whole(selectors.shape)],
        out_specs=[pl.BlockSpec((ns, c, GDN_WIDTH), lambda b, i: (b, i, 0)),
                   pl.BlockSpec((ns,) + hd, lambda b, i: (b, 0, 0, 0))],
        out_shape=[jax.ShapeDtypeStruct((bsz, seq, GDN_WIDTH), F32),
                   jax.ShapeDtypeStruct((bsz,) + hd, F32)],
        compiler_params=_compiler_params(("arbitrary", "arbitrary")),
        name="gdn_prompt",
    )(qkvz3, gates3, alog_row, dtb_row, normw, masks, selectors)


def _s5_params_kernel(us_ref, h0r_ref, h0i_ref, ar_row_ref, ai_row_ref, ar_col_ref, ai_col_ref, ldt_ref,
                      br_ref, bi_ref, cr_ref, ci_ref,
                      mt_ref, gt_ref, gts_ref, wf_ref, coef_ref, ys_ref, hsr_ref, hsi_ref):
    L = S5_CHUNK
    P = S5_STATE
    dt = jnp.exp(ldt_ref[0])
    ar_row = ar_row_ref[0] * dt
    ai_row = ai_row_ref[0] * dt
    ar_col = ar_col_ref[0] * dt
    ai_col = ai_col_ref[0] * dt
    first = lax.broadcasted_iota(jnp.int32, (1, 2 * P), 1) < P

    ea = jnp.exp(ar_col)
    lbr = ea * jnp.cos(ai_col)
    lbi = ea * jnp.sin(ai_col)
    lam_r = ar_col_ref[0]
    lam_i = ai_col_ref[0]
    den = lam_r * lam_r + lam_i * lam_i
    fr = ((lbr - 1.0) * lam_r + lbi * lam_i) / den
    fi = (lbi * lam_r - (lbr - 1.0) * lam_i) / den
    b_r = br_ref[0]
    b_i = bi_ref[0]
    bbr = fr * b_r - fi * b_i
    bbi = fr * b_i + fi * b_r

    rt = lax.broadcasted_iota(jnp.int32, (S5_GROUP, S5_FLAT), 0)
    lt = lax.broadcasted_iota(jnp.int32, (S5_GROUP, S5_FLAT), 1)
    tile_mat = jnp.where(lt % S5_GROUP == rt, 1.0, 0.0).astype(BF16)
    bwr = _dot_x3(bbr, tile_mat)
    bwi = _dot_x3(bbi, tile_mat)

    tau_g = (L - 1) - lax.broadcasted_iota(jnp.int32, (P, S5_FLAT), 1) // S5_GROUP
    pgr, pgi = _complex_powers(lbr, lbi, tau_g, 4)
    gtr = pgr * bwr - pgi * bwi
    gti = pgr * bwi + pgi * bwr
    gt_ref[0] = jnp.concatenate([gtr, gti], axis=0).astype(BF16)
    gts_ref[0] = jnp.concatenate([gti, gtr], axis=0).astype(BF16)

    nt = L + 1
    e1 = jnp.exp(ar_row)
    l1r = e1 * jnp.cos(ai_row)
    l1i = e1 * jnp.sin(ai_row)
    tau_w = lax.broadcasted_iota(jnp.int32, (nt * S5_GROUP, 2 * P), 0) // S5_GROUP
    pwr, pwi = _complex_powers(l1r, l1i, tau_w, 5)
    c_r = jnp.concatenate([cr_ref[0]] * nt, axis=0)
    c_i = jnp.concatenate([ci_ref[0]] * nt, axis=0)
    wall = jnp.where(first, c_r * pwr - c_i * pwi, -(c_r * pwi + c_i * pwr))
    wf_ref[0] = wall[S5_GROUP:].astype(BF16)

    zw = _dot_hi(wall[0:S5_FLAT], jnp.concatenate([bwr, bwi], axis=0))
    s_of_lane = lax.broadcasted_iota(jnp.int32, (S5_FLAT, S5_FLAT), 1) // S5_GROUP
    mt = zw
    for j in range(4):
        sh = S5_GROUP << j
        shifted = jnp.concatenate([jnp.zeros((sh, S5_FLAT), F32), mt[0:S5_FLAT - sh]], axis=0)
        mt = jnp.where(((s_of_lane >> j) & 1) == 1, shifted, mt)
    mt_ref[0] = mt.astype(BF16)

    p16r = pwr[L * S5_GROUP:L * S5_GROUP + 1, :]
    p16i = pwi[L * S5_GROUP:L * S5_GROUP + 1, :]
    coef_b = jnp.where(first, -p16i, p16i)
    srow = lax.broadcasted_iota(jnp.int32, (8, 2 * P), 0)
    coef_ref[0] = jnp.where(srow == 0, p16r, jnp.where(srow == 1, coef_b, jnp.where(srow == 2, -coef_b, 0.0)))

    l1r = l1r[:, 0:P]
    l1i = l1i[:, 0:P]
    us = us_ref[0]
    nt_dims = (((1,), (1,)), ((), ()))
    bur = _dot_hi(us, bbr, nt_dims)
    bui = _dot_hi(us, bbi, nt_dims)
    h0r = h0r_ref[0]
    h0i = h0i_ref[0]
    hsr = l1r * h0r - l1i * h0i + bur
    hsi = l1r * h0i + l1i * h0r + bui
    hsr_ref[0] = hsr
    hsi_ref[0] = hsi
    ys_ref[0] = _dot_nt(hsr, cr_ref[0][:, 0:P]) - _dot_nt(hsi, ci_ref[0][:, 0:P])


def _s5_params(us_t, h0r_t, h0i_t, ar, ai, ldt, b_r, b_i, c_r, c_i):
    g = S5_GROUPS
    ns = us_t.shape[1]
    twice = lambda a: jnp.concatenate([a, a], axis=-1)
    ar_row = twice(ar).reshape(g, 1, 2 * S5_STATE)
    ai_row = twice(ai).reshape(g, 1, 2 * S5_STATE)
    ar_col = ar.reshape(g, S5_STATE, 1)
    ai_col = ai.reshape(g, S5_STATE, 1)
    ldt3 = ldt.reshape(g, 1, 1)

    def blk(shape):
        return pl.BlockSpec((1,) + shape, lambda i: (i,) + (0,) * len(shape))

    return pl.pallas_call(
        _s5_params_kernel,
        grid=(g,),
        in_specs=[blk((ns, S5_GROUP)), blk((ns, S5_STATE)), blk((ns, S5_STATE)),
                  blk((1, 2 * S5_STATE)), blk((1, 2 * S5_STATE)), blk((S5_STATE, 1)), blk((S5_STATE, 1)), blk((1, 1)),
                  blk((S5_STATE, S5_GROUP)), blk((S5_STATE, S5_GROUP)),
                  blk((S5_GROUP, 2 * S5_STATE)), blk((S5_GROUP, 2 * S5_STATE))],
        out_specs=[blk((S5_FLAT, S5_FLAT)), blk((2 * S5_STATE, S5_FLAT)), blk((2 * S5_STATE, S5_FLAT)),
                   blk((S5_FLAT, 2 * S5_STATE)), blk((8, 2 * S5_STATE)),
                   blk((ns, S5_GROUP)), blk((ns, S5_STATE)), blk((ns, S5_STATE))],
        out_shape=[jax.ShapeDtypeStruct((g, S5_FLAT, S5_FLAT), BF16),
                   jax.ShapeDtypeStruct((g, 2 * S5_STATE, S5_FLAT), BF16),
                   jax.ShapeDtypeStruct((g, 2 * S5_STATE, S5_FLAT), BF16),
                   jax.ShapeDtypeStruct((g, S5_FLAT, 2 * S5_STATE), BF16),
                   jax.ShapeDtypeStruct((g, 8, 2 * S5_STATE), F32),
                   jax.ShapeDtypeStruct((g, ns, S5_GROUP), F32),
                   jax.ShapeDtypeStruct((g, ns, S5_STATE), F32),
                   jax.ShapeDtypeStruct((g, ns, S5_STATE), F32)],
        compiler_params=_compiler_params(("arbitrary",)),
        name="s5_params",
    )(us_t, h0r_t, h0i_t, ar_row, ai_row, ar_col, ai_col, ldt3, b_r, b_i, twice(c_r), twice(c_i))


PIECES = LANE_TILE // S5_GROUP
RELAYOUT_ROWS = S5_CHUNK * S5_CHUNK


def _piece_transpose(tiles):
    tiles = list(tiles)
    piece = lax.broadcasted_iota(jnp.int32, tiles[0].shape, 1) // S5_GROUP
    d = PIECES // 2
    while d:
        keep_low = (piece & d) == 0
        for k in range(PIECES):
            if k & d:
                continue
            a, b = tiles[k], tiles[k + d]
            tiles[k] = jnp.where(keep_low, a, pltpu.roll(b, S5_GROUP * d, axis=1))
            tiles[k + d] = jnp.where(keep_low, pltpu.roll(a, LANE_TILE - S5_GROUP * d, axis=1), b)
        d //= 2
    return tiles


def _s5_seq_kernel(u_ref, mt_ref, gt_ref, gts_ref, wf_ref, coef_ref, ys_ref, hl_ref,
                   uflat_ref, e_ref, es_ref, hin_ref, yflat_ref, *, nchunk):
    rr = RELAYOUT_ROWS
    ro = lax.broadcasted_iota(jnp.int32, (rr, rr), 0)
    ri = lax.broadcasted_iota(jnp.int32, (rr, rr), 1)
    perm = jnp.where((ro // S5_CHUNK == ri % S5_CHUNK) & (ro % S5_CHUNK == ri // S5_CHUNK), 1.0, 0.0).astype(BF16)

    def gather_in(m, carry):
        rows = pl.ds(pl.multiple_of(m * rr, rr), rr)
        xp = jnp.dot(perm, u_ref[rows, :].astype(BF16), preferred_element_type=F32)
        crow = pl.ds(pl.multiple_of(m * S5_CHUNK, S5_CHUNK), S5_CHUNK)
        for j in range(S5_FLAT // LANE_TILE):
            for cb in range(S5_WIDTH // LANE_TILE):
                tiles = [xp[(PIECES * j + k) * S5_CHUNK:(PIECES * j + k + 1) * S5_CHUNK,
                            cb * LANE_TILE:(cb + 1) * LANE_TILE] for k in range(PIECES)]
                for p, tile in enumerate(_piece_transpose(tiles)):
                    uflat_ref[PIECES * cb + p, crow, j * LANE_TILE:(j + 1) * LANE_TILE] = tile.astype(BF16)
        return carry

    lax.fori_loop(0, nchunk // S5_CHUNK, gather_in, 0, unroll=2)

    for g in range(S5_GROUPS):
        lanes = slice(g * LANE_TILE, (g + 1) * LANE_TILE)
        ug = uflat_ref[g]
        e_ref[:, lanes] = _dot_nt(ug, gt_ref[g])
        es_ref[:, lanes] = _dot_nt(ug, gts_ref[g])

    ca = coef_ref[0:1, :]
    cb = coef_ref[1:2, :]
    cbs = coef_ref[2:3, :]

    def scan(c, carry):
        h, hs = carry
        row = pl.ds(c, 1)
        hin_ref[row, :] = h
        return ca * h + cb * hs + e_ref[row, :], ca * hs + cbs * h + es_ref[row, :]

    zero = jnp.zeros((1, S5_GROUPS * LANE_TILE), F32)
    h_last, _ = lax.fori_loop(0, nchunk, scan, (zero, zero))
    hl_ref[0] = h_last

    for g in range(S5_GROUPS):
        lanes = slice(g * LANE_TILE, (g + 1) * LANE_TILE)
        yflat_ref[g] = _dot_nt(uflat_ref[g], mt_ref[g]) + _dot_nt(hin_ref[:, lanes], wf_ref[g])

    def gather_out(m, carry):
        crow = pl.ds(pl.multiple_of(m * S5_CHUNK, S5_CHUNK), S5_CHUNK)
        by_time = [[None] * (S5_WIDTH // LANE_TILE) for _ in range(S5_CHUNK)]
        for j in range(S5_FLAT // LANE_TILE):
            for cb in range(S5_WIDTH // LANE_TILE):
                tiles = [yflat_ref[PIECES * cb + k, crow, j * LANE_TILE:(j + 1) * LANE_TILE] for k in range(PIECES)]
                for p, tile in enumerate(_piece_transpose(tiles)):
                    by_time[PIECES * j + p][cb] = tile
        z = jnp.concatenate([jnp.concatenate(row, axis=1) for row in by_time], axis=0)
        rows = pl.ds(pl.multiple_of(m * rr, rr), rr)
        ys_ref[rows, :] = _dot_x3_left(perm, z)
        return carry

    lax.fori_loop(0, nchunk // S5_CHUNK, gather_out, 0, unroll=2)


def _s5_seq(u, mt, gt, gts, wf, coef, bsz, seq):
    nchunk = seq // S5_CHUNK
    width = S5_GROUPS * LANE_TILE

    def resident(a):
        return pl.BlockSpec(a.shape, lambda b: (0,) * a.ndim, pipeline_mode=pl.Buffered(1))

    return pl.pallas_call(
        functools.partial(_s5_seq_kernel, nchunk=nchunk),
        grid=(bsz,),
        in_specs=[pl.BlockSpec((seq, S5_WIDTH), lambda b: (b, 0))] + [resident(a) for a in (mt, gt, gts, wf, coef)],
        out_specs=[pl.BlockSpec((seq, S5_WIDTH), lambda b: (b, 0)),
                   pl.BlockSpec((1, 1, width), lambda b: (b, 0, 0))],
        out_shape=[jax.ShapeDtypeStruct((bsz * seq, S5_WIDTH), F32),
                   jax.ShapeDtypeStruct((bsz, 1, width), F32)],
        scratch_shapes=[pltpu.VMEM((S5_GROUPS, nchunk, S5_FLAT), BF16),
                        pltpu.VMEM((nchunk, width), F32), pltpu.VMEM((nchunk, width), F32),
                        pltpu.VMEM((nchunk, width), F32),
                        pltpu.VMEM((S5_GROUPS, nchunk, S5_FLAT), F32)],
        compiler_params=_compiler_params(("arbitrary",)),
        name="s5_seq",
    )(u, mt, gt, gts, wf, coef)


def _sample_pre_kernel(qkvz_ref, gates_ref, conv_ref, wconv_ref, alog_row_ref, dtb_row_ref,
                       newconv_ref, ops_ref, sc_ref):
    dk = GDN_HEAD_DIM
    x_new = qkvz_ref[:, 0:QKV_W]
    wc = wconv_ref[...]
    conv = x_new * wc[3:4, :]
    for j in range(CONV_W - 1):
        conv = conv + conv_ref[j] * wc[j:j + 1, :]
    qkv = _silu(conv)
    newconv_ref[0] = conv_ref[1]
    newconv_ref[1] = conv_ref[2]
    newconv_ref[2] = x_new

    beta_c, g_c = _gate_values(gates_ref[...], alog_row_ref[...], dtb_row_ref[...])
    scale = GDN_HEAD_DIM ** -0.5
    lane = lax.broadcasted_iota(jnp.int32, sc_ref.shape, 1)
    sc = jnp.zeros(sc_ref.shape, F32)
    for h in range(GDN_HEADS):
        q = _l2norm(qkv[:, h * dk:(h + 1) * dk]) * scale
        k = _l2norm(qkv[:, GDN_WIDTH + h * dk:GDN_WIDTH + (h + 1) * dk])
        v = qkv[:, 2 * GDN_WIDTH + h * dk:2 * GDN_WIDTH + (h + 1) * dk]
        beta = beta_c[:, h:h + 1]
        eg = jnp.exp(g_c[:, 4 + h:5 + h])
        cols = slice(h * dk, (h + 1) * dk)
        ops_ref[0, :, cols] = (beta * eg) * k
        ops_ref[1, :, cols] = q * eg
        ops_ref[2, :, cols] = k
        ops_ref[3, :, cols] = beta * v
        sc = jnp.where(lane == h, jnp.sum(q * k, axis=-1, keepdims=True), sc)
        sc = jnp.where(lane == 4 + h, eg, sc)
    sc_ref[...] = sc


def _sample_pre(qkvz_s, gates_s, conv_t, wconv, alog_row, dtb_row):
    ns = qkvz_s.shape[0]
    return pl.pallas_call(
        _sample_pre_kernel,
        out_shape=[jax.ShapeDtypeStruct((CONV_W - 1, ns, QKV_W), F32),
                   jax.ShapeDtypeStruct((4, ns, GDN_WIDTH), F32),
                   jax.ShapeDtypeStruct((ns, GATE_LANES), F32)],
        compiler_params=pltpu.CompilerParams(vmem_limit_bytes=VMEM_LIMIT),
        name="sample_pre",
    )(qkvz_s, gates_s, conv_t, wconv, alog_row, dtb_row)


def _sample_state_kernel(s_ref, ops_ref, sc_ref, z_ref, normw_ref, snew_ref, o_ref, *, nb):
    dk = GDN_HEAD_DIM
    row = lax.broadcasted_iota(jnp.int32, (8, dk), 0)
    units = [(j, h) for j in range(nb) for h in range(GDN_HEADS)]
    m1 = {}
    for j, h in units:
        cols = slice(h * dk, (h + 1) * dk)
        w = ops_ref[0, j:j + 1, cols]
        qg = ops_ref[1, j:j + 1, cols]
        lhs = jnp.where(row == 0, jnp.broadcast_to(w, (8, dk)), jnp.broadcast_to(qg, (8, dk)))
        m1[j, h] = _dot(lhs, s_ref[j, h])
    for j, h in units:
        cols = slice(h * dk, (h + 1) * dk)
        k = ops_ref[2, j:j + 1, cols]
        u = ops_ref[3, j:j + 1, cols]
        v_new = u - m1[j, h][0:1, :]
        qk = sc_ref[j:j + 1, h:h + 1]
        eg = sc_ref[j:j + 1, 4 + h:5 + h]
        o_ref[j:j + 1, cols] = m1[j, h][1:2, :] + qk * v_new
        k8 = jnp.where(row == 0, jnp.broadcast_to(k, (8, dk)), 0.0)
        snew_ref[j, h] = s_ref[j, h] * eg + _dot_tn(k8, jnp.broadcast_to(v_new, (8, dk)))
    normw = normw_ref[...]
    for h in range(GDN_HEADS):
        o = o_ref[:, h * dk:(h + 1) * dk]
        z = z_ref[:, h * dk:(h + 1) * dk]
        o_ref[:, h * dk:(h + 1) * dk] = (o * lax.rsqrt(jnp.mean(o * o, axis=-1, keepdims=True) + NORM_EPS)
                                         * normw * _silu(z))


def _sample_state(state, ops, sc, z, normw, nb=8):
    ns = state.shape[0]
    hd = (GDN_HEADS, GDN_HEAD_DIM, GDN_HEAD_DIM)
    return pl.pallas_call(
        functools.partial(_sample_state_kernel, nb=nb),
        grid=(ns // nb,),
        in_specs=[pl.BlockSpec((nb,) + hd, lambda i: (i, 0, 0, 0)),
                  pl.BlockSpec((4, nb, GDN_WIDTH), lambda i: (0, i, 0)),
                  pl.BlockSpec((nb, GATE_LANES), lambda i: (i, 0)),
                  pl.BlockSpec((nb, GDN_WIDTH), lambda i: (i, 0)),
                  _whole(normw.shape)],
        out_specs=[pl.BlockSpec((nb,) + hd, lambda i: (i, 0, 0, 0)),
                   pl.BlockSpec((nb, GDN_WIDTH), lambda i: (i, 0))],
        out_shape=[jax.ShapeDtypeStruct(state.shape, F32),
                   jax.ShapeDtypeStruct((ns, GDN_WIDTH), F32)],
        compiler_params=_compiler_params(("arbitrary",)),
        name="sample_state",
    )(state, ops, sc, z, normw)


POST_PART_ROWS = 256


def _post_kernel(x_ref, o_ref, ys_ref, u_ref, d_ref, wglu_ref, bglu_ref, wout_ref, g1_ref, b1_ref,
                 wff1_ref, wff2_ref, g2_ref, b2_ref, y_ref, *, ff_chunk):
    n = x_ref.shape[0]
    parts = max(n // POST_PART_ROWS, 1)
    halves = [slice(i * n // parts, (i + 1) * n // parts) for i in range(parts)]
    st = [dict(rows=r) for r in halves]

    def head(s):
        r = s["rows"]
        ys = jax.nn.gelu(ys_ref[r, :] + d_ref[...] * u_ref[r, :])
        ys = ys * jax.nn.sigmoid(_dot(ys, wglu_ref[...]) + bglu_ref[...])
        yield
        mix = _dot(o_ref[r, :], wout_ref[0:GDN_WIDTH, :]) + _dot(ys, wout_ref[GDN_WIDTH:, :])
        yield
        s["x1"] = _layernorm(DN_ALPHA * x_ref[r, :] + mix, g1_ref[...], b1_ref[...])
        s["x1b"] = s["x1"].astype(BF16)
        yield

    def mlp(s):
        acc = jnp.zeros(s["x1"].shape, F32)
        for f in range(D_FF // ff_chunk):
            hcol = jnp.dot(s["x1b"], wff1_ref[:, f * ff_chunk:(f + 1) * ff_chunk], preferred_element_type=F32)
            hcol = jnp.square(jnp.maximum(hcol, 0.0))
            yield
            acc = acc + _dot(hcol, wff2_ref[f * ff_chunk:(f + 1) * ff_chunk, :])
            yield
        s["acc"] = acc

    def tail(s):
        y_ref[s["rows"], :] = _layernorm(DN_ALPHA * s["x1"] + s["acc"], g2_ref[...], b2_ref[...])
        yield

    _interleave(head(st[0]))
    for i, s in enumerate(st):
        others = [head(st[i + 1])] if i + 1 < len(st) else []
        if i > 0:
            others.append(tail(st[i - 1]))
        _interleave(mlp(s), *others)
    _interleave(tail(st[-1]))


def _post(x, o, ys, u, *weights, tile):
    n = x.shape[0]
    assert n % tile == 0
    tok = lambda w: pl.BlockSpec((tile, w), lambda i: (i, 0))

    def resident(a):
        return pl.BlockSpec(a.shape, lambda i: (0,) * a.ndim, pipeline_mode=pl.Buffered(1))

    return pl.pallas_call(
        functools.partial(_post_kernel, ff_chunk=1024),
        grid=(n // tile,),
        in_specs=[tok(D_MODEL), tok(GDN_WIDTH), tok(S5_WIDTH), tok(S5_WIDTH)] + [resident(a) for a in weights],
        out_specs=tok(D_MODEL),
        out_shape=jax.ShapeDtypeStruct((n, D_MODEL), F32),
        compiler_params=_compiler_params(("arbitrary",)),
        name="post",
    )(x, o, ys, u, *weights)


def kernel(x_prompt, x_sample, state_gdn, state_conv, state_ssm_re, state_ssm_im, w_in, w_conv, gdn_a_log,
           gdn_dt_bias, gdn_norm_w, s5_a_re, s5_a_im, s5_b_re, s5_b_im, s5_c_re, s5_c_im, s5_d, s5_log_dt,
           w_glu, b_glu, w_out, ln1_g, ln1_b, w_ff1, w_ff2, ln2_g, ln2_b):
    bsz, seq, _ = x_prompt.shape
    ns = x_sample.shape[0]
    l = 0

    w = w_in[l]
    i_u = QKVZ_W + 2 * GDN_HEADS
    w_cat = jnp.concatenate([w[:, :QKVZ_W], w[:, i_u:], w[:, QKVZ_W:i_u],
                             jnp.zeros((D_MODEL, GATE_LANES - 2 * GDN_HEADS), F32)], axis=1).astype(BF16)
    lane_pad = (GDN_HEADS, GATE_LANES - 2 * GDN_HEADS)
    alog_row = jnp.pad(gdn_a_log[l], lane_pad).reshape(1, GATE_LANES)
    dtb_row = jnp.pad(gdn_dt_bias[l], lane_pad).reshape(1, GATE_LANES)
    normw = gdn_norm_w[l].reshape(1, GDN_HEAD_DIM)
    wconv = w_conv[l]
    row = lambda a: a.reshape(1, -1)
    post_w = (row(s5_d[l]), w_glu[l].astype(BF16), row(b_glu[l]), w_out[l].astype(BF16), row(ln1_g[l]), row(ln1_b[l]),
              w_ff1[l].astype(BF16), w_ff2[l].astype(BF16), row(ln2_g[l]), row(ln2_b[l]))

    xp = x_prompt.reshape(bsz * seq, D_MODEL)
    xs = x_sample.reshape(ns, D_MODEL)
    qkvz_p, gates_p, u_p, tail_p = _proj_conv(xp, w_cat, wconv, seq, tile=min(512, seq))
    qkvz_s, gates_s, u_s = _proj(xs, w_cat, tile=ns)

    o_p, gdn_p = _gdn_pair(qkvz_p.reshape(bsz, seq, QKVZ_W), gates_p.reshape(bsz, seq, GATE_LANES),
                           alog_row, dtb_row, normw)
    o_p = o_p.reshape(bsz * seq, GDN_WIDTH)
    conv_p = tail_p[:, 8 - (CONV_W - 1):, :]

    conv_t = jnp.transpose(state_conv[l], (1, 0, 2))
    newconv_t, ops_s, sc = _sample_pre(qkvz_s, gates_s, conv_t, wconv, alog_row, dtb_row)
    gdn_s, o_s = _sample_state(state_gdn[l], ops_s, sc, qkvz_s[:, QKV_W:], normw)
    conv_s = jnp.transpose(newconv_t, (1, 0, 2))

    us_t = u_s.reshape(ns, S5_GROUPS, S5_GROUP).transpose(1, 0, 2)
    h0r_t = state_ssm_re[l].transpose(1, 0, 2)
    h0i_t = state_ssm_im[l].transpose(1, 0, 2)
    mt, gt, gts, wf, coef, ys_t, hsr, hsi = _s5_params(us_t, h0r_t, h0i_t, s5_a_re[l], s5_a_im[l], s5_log_dt[l],
                                                       s5_b_re[l], s5_b_im[l], s5_c_re[l], s5_c_im[l])
    coef_rows = coef.transpose(1, 0, 2).reshape(8, S5_GROUPS * LANE_TILE)
    ys_p, h_last = _s5_seq(u_p, mt, gt, gts, wf, coef_rows, bsz, seq)
    h_last = h_last.reshape(bsz, S5_GROUPS, 2 * S5_STATE)
    ys_s = ys_t.transpose(1, 0, 2).reshape(ns, S5_WIDTH)

    y_p = _post(xp, o_p, ys_p, u_p, *post_w, tile=min(1024, bsz * seq))
    y_s = _post(xs, o_s, ys_s, u_s, *post_w, tile=ns)

    t3 = lambda a: a.transpose(1, 0, 2)[None]
    return (y_p.reshape(bsz, seq, D_MODEL), y_s.reshape(ns, 1, D_MODEL),
            gdn_p[None], conv_p[None], h_last[None, :, :, :S5_STATE], h_last[None, :, :, S5_STATE:],
            gdn_s[None], conv_s[None], t3(hsr), t3(hsi))
```

```python
import functools

import jax
import jax.numpy as jnp
import numpy as np
from jax import lax
from jax.experimental import pallas as pl
from jax.experimental.pallas import tpu as pltpu

F32 = jnp.float32
BF16 = jnp.bfloat16

D_MODEL = 1024
GDN_HEADS = 4
GDN_HEAD_DIM = 128
GDN_WIDTH = GDN_HEADS * GDN_HEAD_DIM
CONV_W = 4
GDN_CHUNK = 64
S5_WIDTH = D_MODEL - GDN_WIDTH
S5_GROUP = 16
S5_GROUPS = S5_WIDTH // S5_GROUP
S5_STATE = 64
D_FF = 4 * D_MODEL
DEPTH = 1
DN_ALPHA = (2.0 * DEPTH) ** 0.25
NORM_EPS = 1e-6
QKV_W = 3 * GDN_WIDTH
QKVZ_W = QKV_W + GDN_WIDTH
LANE_TILE = 128
GATE_LANES = LANE_TILE
S5_CHUNK = 16
S5_FLAT = S5_CHUNK * S5_GROUP
VMEM_LIMIT = 56 * 1024 * 1024


def _dot(a, b):
    return jnp.dot(a.astype(BF16), b.astype(BF16), preferred_element_type=F32)


def _dot_nt(a, b):
    return lax.dot_general(a.astype(BF16), b.astype(BF16), (((1,), (1,)), ((), ())), preferred_element_type=F32)


def _dot_tn(a, b):
    return lax.dot_general(a.astype(BF16), b.astype(BF16), (((0,), (0,)), ((), ())), preferred_element_type=F32)


def _split(x):
    hi = x.astype(BF16)
    lo = (x - hi.astype(F32)).astype(BF16)
    return hi, lo


def _dot_hi(a, b, dims=(((1,), (0,)), ((), ()))):
    ah, al = _split(a)
    bh, bl = _split(b)
    d = functools.partial(lax.dot_general, dimension_numbers=dims, preferred_element_type=F32)
    return d(ah, bh) + (d(al, bh) + d(ah, bl))


def _dot_x3(a, b_exact):
    a1 = a.astype(BF16)
    r1 = a - a1.astype(F32)
    a2 = r1.astype(BF16)
    a3 = (r1 - a2.astype(F32)).astype(BF16)
    return _dot(a1, b_exact) + (_dot(a2, b_exact) + _dot(a3, b_exact))


def _dot_x3_left(a_exact, b):
    b1 = b.astype(BF16)
    r1 = b - b1.astype(F32)
    b2 = r1.astype(BF16)
    b3 = (r1 - b2.astype(F32)).astype(BF16)
    return _dot(a_exact, b1) + (_dot(a_exact, b2) + _dot(a_exact, b3))


def _complex_powers(base_r, base_i, exponent, nbits):
    shape = exponent.shape
    pr = jnp.ones(shape, F32)
    pi = jnp.zeros(shape, F32)
    br, bi = base_r, base_i
    for j in range(nbits):
        bit = ((exponent >> j) & 1) == 1
        pr, pi = jnp.where(bit, pr * br - pi * bi, pr), jnp.where(bit, pr * bi + pi * br, pi)
        br, bi = br * br - bi * bi, 2.0 * br * bi
    return pr, pi


def _silu(x):
    h = 0.5 * x
    return h + h * jnp.tanh(h)


def _layernorm(x, g, b):
    mu = jnp.mean(x, axis=-1, keepdims=True)
    xc = x - mu
    var = jnp.mean(xc * xc, axis=-1, keepdims=True)
    return xc * lax.rsqrt(var + NORM_EPS) * g + b


def _l2norm(a):
    return a * lax.rsqrt(jnp.sum(a * a, axis=-1, keepdims=True) + NORM_EPS)


def _compiler_params(semantics):
    return pltpu.CompilerParams(dimension_semantics=semantics, vmem_limit_bytes=VMEM_LIMIT)


def _whole(shape):
    n = len(shape)
    return pl.BlockSpec(shape, lambda *_: (0,) * n)


def _interleave(*gens):
    live = list(gens)
    while live:
        for gen in list(live):
            if next(gen, "done") == "done":
                live.remove(gen)


def _proj_kernel(x_ref, w_ref, qkvz_ref, gates_ref, u_ref):
    r = jnp.dot(x_ref[...].astype(BF16), w_ref[...], preferred_element_type=F32)
    qkvz_ref[...] = r[:, 0:QKVZ_W]
    u_ref[...] = r[:, QKVZ_W:QKVZ_W + S5_WIDTH]
    gates_ref[...] = r[:, QKVZ_W + S5_WIDTH:]


PROJ_PARTS = 4


def _proj_conv_kernel(x_ref, w_ref, wconv_ref, qkvz_ref, gates_ref, u_ref, tail_ref, xpad_a_ref, xpad_b_ref, *,
                      tiles_per_seq):
    tile = x_ref.shape[0]
    dk = GDN_HEAD_DIM
    parts = PROJ_PARTS
    pr = tile // parts
    bufs = (xpad_a_ref, xpad_b_ref)
    last = bufs[(parts - 1) % 2]

    @pl.when(pl.program_id(0) % tiles_per_seq == 0)
    def _():
        last[pr:pr + 8, :] = jnp.zeros((8, QKV_W), F32)

    wc = wconv_ref[...]
    scale = GDN_HEAD_DIM ** -0.5

    def project(p):
        buf, prev = bufs[p % 2], bufs[(p - 1) % 2]
        rows = slice(p * pr, (p + 1) * pr)
        buf[0:8, :] = prev[pr:pr + 8, :]
        xb = x_ref[rows, :].astype(BF16)
        step = 2 * LANE_TILE
        for c0 in range(0, w_ref.shape[1], step):
            c1 = min(c0 + step, w_ref.shape[1])
            r = jnp.dot(xb, w_ref[:, c0:c1], preferred_element_type=F32)
            if c1 <= QKV_W:
                buf[8:8 + pr, c0:c1] = r
            elif c1 <= QKVZ_W:
                qkvz_ref[rows, c0:c1] = r
            elif c1 <= QKVZ_W + S5_WIDTH:
                u_ref[rows, c0 - QKVZ_W:c1 - QKVZ_W] = r
            else:
                gates_ref[rows, :] = r
            yield

    def activate(p):
        buf = bufs[p % 2]
        for lt in range(QKV_W // dk):
            cols = slice(lt * dk, (lt + 1) * dk)
            conv = buf[8:8 + pr, cols] * wc[3:4, cols]
            for j in range(1, CONV_W):
                conv = conv + buf[8 - j:8 - j + pr, cols] * wc[3 - j:4 - j, cols]
            act = _silu(conv)
            if lt < GDN_HEADS:
                act = _l2norm(act) * scale
            elif lt < 2 * GDN_HEADS:
                act = _l2norm(act)
            qkvz_ref[p * pr:(p + 1) * pr, cols] = act
            yield

    _interleave(project(0))
    for p in range(1, parts):
        _interleave(project(p), activate(p - 1))
    _interleave(activate(parts - 1))
    tail_ref[0] = last[pr:pr + 8, :]


def _proj(x, w, tile):
    n = x.shape[0]
    return pl.pallas_call(
        _proj_kernel,
        grid=(n // tile,),
        in_specs=[pl.BlockSpec((tile, D_MODEL), lambda i: (i, 0)),
                  pl.BlockSpec(w.shape, lambda i: (0, 0), pipeline_mode=pl.Buffered(1))],
        out_specs=[pl.BlockSpec((tile, QKVZ_W), lambda i: (i, 0)),
                   pl.BlockSpec((tile, GATE_LANES), lambda i: (i, 0)),
                   pl.BlockSpec((tile, S5_WIDTH), lambda i: (i, 0))],
        out_shape=[jax.ShapeDtypeStruct((n, QKVZ_W), F32),
                   jax.ShapeDtypeStruct((n, GATE_LANES), F32),
                   jax.ShapeDtypeStruct((n, S5_WIDTH), F32)],
        compiler_params=_compiler_params(("arbitrary",)),
        name="proj",
    )(x, w)


def _proj_conv(x, w, wconv, seq, tile):
    n = x.shape[0]
    tiles_per_seq = seq // tile
    assert PROJ_PARTS % 2 == 0 and tile % (8 * PROJ_PARTS) == 0 and seq % tile == 0
    return pl.pallas_call(
        functools.partial(_proj_conv_kernel, tiles_per_seq=tiles_per_seq),
        grid=(n // tile,),
        in_specs=[pl.BlockSpec((tile, D_MODEL), lambda i: (i, 0)),
                  pl.BlockSpec(w.shape, lambda i: (0, 0), pipeline_mode=pl.Buffered(1)),
                  _whole(wconv.shape)],
        out_specs=[pl.BlockSpec((tile, QKVZ_W), lambda i: (i, 0)),
                   pl.BlockSpec((tile, GATE_LANES), lambda i: (i, 0)),
                   pl.BlockSpec((tile, S5_WIDTH), lambda i: (i, 0)),
                   pl.BlockSpec((1, 8, QKV_W), lambda i: (i // tiles_per_seq, 0, 0))],
        out_shape=[jax.ShapeDtypeStruct((n, QKVZ_W), F32),
                   jax.ShapeDtypeStruct((n, GATE_LANES), F32),
                   jax.ShapeDtypeStruct((n, S5_WIDTH), F32),
                   jax.ShapeDtypeStruct((n // seq, 8, QKV_W), F32)],
        scratch_shapes=[pltpu.VMEM((8 + tile // PROJ_PARTS, QKV_W), F32)] * 2,
        compiler_params=_compiler_params(("arbitrary",)),
        name="proj_conv",
    )(x, w, wconv)


GDN_SEQS = 8
GDN_SUB_BLOCKS = (1, 2, 4, 8, 16, 32)
MASK_TRI, MASK_STRICT, MASK_EYE = len(GDN_SUB_BLOCKS), len(GDN_SUB_BLOCKS) + 1, len(GDN_SUB_BLOCKS) + 2
SEL_LOW, SEL_HIGH, SEL_SUB = 0, 1, 2


def _gate_values(logits, a_log, dt_bias):
    beta = jax.nn.sigmoid(logits)
    g = -jnp.exp(a_log) * jax.nn.softplus(logits + dt_bias)
    return beta, g


def _gdn_masks():
    c = GDN_CHUNK
    r = np.arange(c)[:, None]
    col = np.arange(2 * c)[None, :] % c
    sub = [(r // (2 * b) == col // (2 * b)) & ((r // b) % 2 == 1) & ((col // b) % 2 == 0) for b in GDN_SUB_BLOCKS]
    masks = np.stack(sub + [col <= r, col < r, col == r]).astype(np.float32)
    lane = np.broadcast_to(np.arange(2 * c)[None, :], (c, 2 * c))
    selectors = np.stack([lane < c, lane >= c] + sub).astype(np.float32)
    return jnp.asarray(masks), jnp.asarray(selectors, dtype=BF16)


def _gdn_pair_kernel(qkvz_ref, gates_ref, alog_row_ref, dtb_row_ref, normw_ref, masks_ref, sel_ref, o_ref, s_ref):
    step = pl.program_id(1)
    ns = qkvz_ref.shape[0]
    c = GDN_CHUNK
    dk = GDN_HEAD_DIM
    rows = ns * c

    @pl.when(step == 0)
    def _():
        s_ref[...] = jnp.zeros_like(s_ref)

    beta_c, g_c = _gate_values(gates_ref[...].reshape(rows, GATE_LANES), alog_row_ref[...], dtb_row_ref[...])

    ri = lax.broadcasted_iota(jnp.int32, (LANE_TILE, LANE_TILE), 0)
    ci = lax.broadcasted_iota(jnp.int32, (LANE_TILE, LANE_TILE), 1)
    tril_bd = jnp.where(((ri // c) == (ci // c)) & (ci <= ri), 1.0, 0.0).astype(BF16)
    slabs = [_dot_x3_left(tril_bd, g_c[i * LANE_TILE:(i + 1) * LANE_TILE, :]) for i in range(rows // LANE_TILE)]
    gc_c = jnp.concatenate(slabs, axis=0)
    gc_t = [slab.T for slab in slabs]
    gc_t_rolled = [pltpu.roll(t, c, axis=1) for t in gc_t]

    mask = lambda i: masks_ref[i] > 0.5
    low_half = lax.broadcasted_iota(jnp.int32, (1, 2 * c), 1) < c
    normw = normw_ref[...]
    zeros_k = jnp.zeros((c, dk), F32)

    def block_diag(x):
        return jnp.concatenate([x * sel_ref[SEL_LOW], x * sel_ref[SEL_HIGH]], axis=0)

    def pair_product(ph, pl_, yh, yl):
        bdh = block_diag(yh)
        first = jnp.dot(jnp.concatenate([ph, pl_], axis=1), jnp.concatenate([bdh, bdh], axis=0),
                        preferred_element_type=F32)
        return first + jnp.dot(ph, block_diag(yl), preferred_element_type=F32)

    def prep(seqs, pairs):
        for s in seqs:
            r0 = s * c
            lane0 = r0 % GATE_LANES
            blk = r0 // GATE_LANES
            in_place, moved = (gc_t[blk], gc_t_rolled[blk])
            for h0 in range(0, GDN_HEADS, 2):
                heads, a1 = [], None
                for side, h in enumerate((h0, h0 + 1)):
                    q = qkvz_ref[s, :, h * dk:(h + 1) * dk]
                    k = qkvz_ref[s, :, GDN_WIDTH + h * dk:GDN_WIDTH + (h + 1) * dk]
                    v = qkvz_ref[s, :, 2 * GDN_WIDTH + h * dk:2 * GDN_WIDTH + (h + 1) * dk]
                    beta = beta_c[r0:r0 + c, h:h + 1]
                    gcol = gc_c[r0:r0 + c, 4 + h:5 + h]
                    eg = jnp.exp(gcol)
                    g_last = gcol[c - 1:c, :]
                    kb = k * beta
                    k_pad = jnp.concatenate([k, zeros_k] if side == 0 else [zeros_k, k], axis=0)
                    part = _dot_nt(jnp.concatenate([kb, q], axis=0), k_pad)
                    a1 = part if a1 is None else a1 + part
                    src = in_place if (lane0 == 0) == (side == 0) else moved
                    heads.append(dict(s=s, h=h, gcol=gcol, grow=src[4 + h:5 + h, :],
                                      qg=q * eg, kd=k * jnp.exp(g_last - gcol), eg_last=jnp.exp(g_last),
                                      rhs=jnp.concatenate([v * beta, kb * eg], axis=1)))
                ha, hb = heads
                diff = jnp.where(low_half, ha["gcol"], hb["gcol"]) - jnp.where(low_half, ha["grow"], hb["grow"])
                decay = jnp.where(mask(MASK_TRI), jnp.exp(jnp.where(mask(MASK_TRI), diff, 0.0)), 0.0)
                pairs.append(dict(heads=heads,
                                  qk=jnp.where(mask(MASK_TRI), a1[c:2 * c] * decay, 0.0),
                                  lmat=jnp.where(mask(MASK_STRICT), a1[0:c] * decay, 0.0)))
                yield

    def solve(pairs):
        for p in pairs:
            p["d"] = masks_ref[MASK_EYE] - p["lmat"] * masks_ref[0]
            p["lh"], p["ll"] = _split(p["lmat"])
        for lvl in range(1, len(GDN_SUB_BLOCKS)):
            off = sel_ref[SEL_SUB + lvl]
            for p in pairs:
                p["dh"], p["dl"] = _split(p["d"])
                p["y"] = pair_product(p["lh"] * off, p["ll"] * off, p["dh"], p["dl"])
            yield
            for p in pairs:
                yh, yl = _split(p["y"])
                p["d"] = p["d"] - pair_product(p["dh"], p["dl"], yh, yl)
            yield
        for p in pairs:
            dh, dl = _split(p["d"])
            (ra_h, ra_l), (rb_h, rb_l) = (_split(hd["rhs"]) for hd in p["heads"])
            zr = jnp.zeros((c, 2 * dk), BF16)
            bd_h = jnp.concatenate([jnp.concatenate([ra_h, zr], axis=1), jnp.concatenate([zr, rb_h], axis=1)], axis=0)
            bd_l = jnp.concatenate([jnp.concatenate([ra_l, zr], axis=1), jnp.concatenate([zr, rb_l], axis=1)], axis=0)
            sol = (jnp.dot(jnp.concatenate([dh, dl], axis=1), jnp.concatenate([bd_h, bd_h], axis=0),
                           preferred_element_type=F32)
                   + jnp.dot(dh, bd_l, preferred_element_type=F32))
            for side, hd in enumerate(p["heads"]):
                hd["sol"] = sol[:, side * 2 * dk:(side + 1) * 2 * dk]
        yield

    def advance(pairs):
        for p in pairs:
            for hd in p["heads"]:
                hd["s_old"] = s_ref[hd["s"], hd["h"]]
                hd["m1"] = _dot(jnp.concatenate([hd["sol"][:, dk:2 * dk], hd["qg"]], axis=0), hd["s_old"])
        yield
        for p in pairs:
            for hd in p["heads"]:
                hd["v_new"] = hd["sol"][:, 0:dk] - hd["m1"][0:c]
                s_ref[hd["s"], hd["h"]] = hd["s_old"] * hd["eg_last"] + _dot_tn(hd["kd"], hd["v_new"])
        yield
        zv = jnp.zeros((c, dk), F32)
        for p in pairs:
            ha, hb = p["heads"]
            v_bd = jnp.concatenate([jnp.concatenate([ha["v_new"], zv], axis=1),
                                    jnp.concatenate([zv, hb["v_new"]], axis=1)], axis=0)
            o_pair = _dot(p["qk"], v_bd)
            for side, hd in enumerate(p["heads"]):
                s, h = hd["s"], hd["h"]
                o = hd["m1"][c:2 * c] + o_pair[:, side * dk:(side + 1) * dk]
                z = qkvz_ref[s, :, QKV_W + h * dk:QKV_W + (h + 1) * dk]
                o = o * lax.rsqrt(jnp.mean(o * o, axis=-1, keepdims=True) + NORM_EPS) * normw * _silu(z)
                o_ref[s, :, h * dk:(h + 1) * dk] = o
        yield

    half = max(ns // 2, 1)
    wave_a, wave_b = [], []
    _interleave(prep(range(0, half), wave_a))
    _interleave(solve(wave_a), prep(range(half, ns), wave_b))
    _interleave(solve(wave_b), advance(wave_a))
    _interleave(advance(wave_b))


def _gdn_pair(qkvz3, gates3, alog_row, dtb_row, normw):
    bsz, seq, _ = qkvz3.shape
    ns, c = min(GDN_SEQS, bsz), GDN_CHUNK
    hd = (GDN_HEADS, GDN_HEAD_DIM, GDN_HEAD_DIM)
    masks, selectors = _gdn_masks()
    return pl.pallas_call(
        _gdn_pair_kernel,
        grid=(bsz // ns, seq // c),
        in_specs=[pl.BlockSpec((ns, c, QKVZ_W), lambda b, i: (b, i, 0)),
                  pl.BlockSpec((ns, c, GATE_LANES), lambda b, i: (b, i, 0)),
                  _whole(alog_row.shape), _whole(dtb_row.shape), _whole(normw.shape),
                  _whole(masks.shape), _whole(selectors.shape)],
        out_specs=[pl.BlockSpec((ns, c, GDN_WIDTH), lambda b, i: (b, i, 0)),
                   pl.BlockSpec((ns,) + hd, lambda b, i: (b, 0, 0, 0))],
        out_shape=[jax.ShapeDtypeStruct((bsz, seq, GDN_WIDTH), F32),
                   jax.ShapeDtypeStruct((bsz,) + hd, F32)],
        compiler_params=_compiler_params(("arbitrary", "arbitrary")),
        name="gdn_prompt",
    )(qkvz3, gates3, alog_row, dtb_row, normw, masks, selectors)


def _s5_params_kernel(us_ref, h0r_ref, h0i_ref, ar_row_ref, ai_row_ref, ar_col_ref, ai_col_ref, ldt_ref,
                      br_ref, bi_ref, cr_ref, ci_ref,
                      mt_ref, gt_ref, gts_ref, wf_ref, coef_ref, ys_ref, hsr_ref, hsi_ref):
    L = S5_CHUNK
    P = S5_STATE
    dt = jnp.exp(ldt_ref[0])
    ar_row = ar_row_ref[0] * dt
    ai_row = ai_row_ref[0] * dt
    ar_col = ar_col_ref[0] * dt
    ai_col = ai_col_ref[0] * dt
    first = lax.broadcasted_iota(jnp.int32, (1, 2 * P), 1) < P

    ea = jnp.exp(ar_col)
    lbr = ea * jnp.cos(ai_col)
    lbi = ea * jnp.sin(ai_col)
    lam_r = ar_col_ref[0]
    lam_i = ai_col_ref[0]
    den = lam_r * lam_r + lam_i * lam_i
    fr = ((lbr - 1.0) * lam_r + lbi * lam_i) / den
    fi = (lbi * lam_r - (lbr - 1.0) * lam_i) / den
    b_r = br_ref[0]
    b_i = bi_ref[0]
    bbr = fr * b_r - fi * b_i
    bbi = fr * b_i + fi * b_r

    rt = lax.broadcasted_iota(jnp.int32, (S5_GROUP, S5_FLAT), 0)
    lt = lax.broadcasted_iota(jnp.int32, (S5_GROUP, S5_FLAT), 1)
    tile_mat = jnp.where(lt % S5_GROUP == rt, 1.0, 0.0).astype(BF16)
    bwr = _dot_x3(bbr, tile_mat)
    bwi = _dot_x3(bbi, tile_mat)

    tau_g = (L - 1) - lax.broadcasted_iota(jnp.int32, (P, S5_FLAT), 1) // S5_GROUP
    pgr, pgi = _complex_powers(lbr, lbi, tau_g, 4)
    gtr = pgr * bwr - pgi * bwi
    gti = pgr * bwi + pgi * bwr
    gt_ref[0] = jnp.concatenate([gtr, gti], axis=0).astype(BF16)
    gts_ref[0] = jnp.concatenate([gti, gtr], axis=0).astype(BF16)

    nt = L + 1
    e1 = jnp.exp(ar_row)
    l1r = e1 * jnp.cos(ai_row)
    l1i = e1 * jnp.sin(ai_row)
    tau_w = lax.broadcasted_iota(jnp.int32, (nt * S5_GROUP, 2 * P), 0) // S5_GROUP
    pwr, pwi = _complex_powers(l1r, l1i, tau_w, 5)
    c_r = jnp.concatenate([cr_ref[0]] * nt, axis=0)
    c_i = jnp.concatenate([ci_ref[0]] * nt, axis=0)
    wall = jnp.where(first, c_r * pwr - c_i * pwi, -(c_r * pwi + c_i * pwr))
    wf_ref[0] = wall[S5_GROUP:].astype(BF16)

    zw = _dot_hi(wall[0:S5_FLAT], jnp.concatenate([bwr, bwi], axis=0))
    s_of_lane = lax.broadcasted_iota(jnp.int32, (S5_FLAT, S5_FLAT), 1) // S5_GROUP
    mt = zw
    for j in range(4):
        sh = S5_GROUP << j
        shifted = jnp.concatenate([jnp.zeros((sh, S5_FLAT), F32), mt[0:S5_FLAT - sh]], axis=0)
        mt = jnp.where(((s_of_lane >> j) & 1) == 1, shifted, mt)
    mt_ref[0] = mt.astype(BF16)

    p16r = pwr[L * S5_GROUP:L * S5_GROUP + 1, :]
    p16i = pwi[L * S5_GROUP:L * S5_GROUP + 1, :]
    coef_b = jnp.where(first, -p16i, p16i)
    srow = lax.broadcasted_iota(jnp.int32, (8, 2 * P), 0)
    coef_ref[0] = jnp.where(srow == 0, p16r, jnp.where(srow == 1, coef_b, jnp.where(srow == 2, -coef_b, 0.0)))

    l1r = l1r[:, 0:P]
    l1i = l1i[:, 0:P]
    us = us_ref[0]
    nt_dims = (((1,), (1,)), ((), ()))
    bur = _dot_hi(us, bbr, nt_dims)
    bui = _dot_hi(us, bbi, nt_dims)
    h0r = h0r_ref[0]
    h0i = h0i_ref[0]
    hsr = l1r * h0r - l1i * h0i + bur
    hsi = l1r * h0i + l1i * h0r + bui
    hsr_ref[0] = hsr
    hsi_ref[0] = hsi
    ys_ref[0] = _dot_nt(hsr, cr_ref[0][:, 0:P]) - _dot_nt(hsi, ci_ref[0][:, 0:P])


def _s5_params(us_t, h0r_t, h0i_t, ar, ai, ldt, b_r, b_i, c_r, c_i):
    g = S5_GROUPS
    ns = us_t.shape[1]
    twice = lambda a: jnp.concatenate([a, a], axis=-1)
    ar_row = twice(ar).reshape(g, 1, 2 * S5_STATE)
    ai_row = twice(ai).reshape(g, 1, 2 * S5_STATE)
    ar_col = ar.reshape(g, S5_STATE, 1)
    ai_col = ai.reshape(g, S5_STATE, 1)
    ldt3 = ldt.reshape(g, 1, 1)

    def blk(shape):
        return pl.BlockSpec((1,) + shape, lambda i: (i,) + (0,) * len(shape))

    return pl.pallas_call(
        _s5_params_kernel,
        grid=(g,),
        in_specs=[blk((ns, S5_GROUP)), blk((ns, S5_STATE)), blk((ns, S5_STATE)),
                  blk((1, 2 * S5_STATE)), blk((1, 2 * S5_STATE)), blk((S5_STATE, 1)), blk((S5_STATE, 1)), blk((1, 1)),
                  blk((S5_STATE, S5_GROUP)), blk((S5_STATE, S5_GROUP)),
                  blk((S5_GROUP, 2 * S5_STATE)), blk((S5_GROUP, 2 * S5_STATE))],
        out_specs=[blk((S5_FLAT, S5_FLAT)), blk((2 * S5_STATE, S5_FLAT)), blk((2 * S5_STATE, S5_FLAT)),
                   blk((S5_FLAT, 2 * S5_STATE)), blk((8, 2 * S5_STATE)),
                   blk((ns, S5_GROUP)), blk((ns, S5_STATE)), blk((ns, S5_STATE))],
        out_shape=[jax.ShapeDtypeStruct((g, S5_FLAT, S5_FLAT), BF16),
                   jax.ShapeDtypeStruct((g, 2 * S5_STATE, S5_FLAT), BF16),
                   jax.ShapeDtypeStruct((g, 2 * S5_STATE, S5_FLAT), BF16),
                   jax.ShapeDtypeStruct((g, S5_FLAT, 2 * S5_STATE), BF16),
                   jax.ShapeDtypeStruct((g, 8, 2 * S5_STATE), F32),
                   jax.ShapeDtypeStruct((g, ns, S5_GROUP), F32),
                   jax.ShapeDtypeStruct((g, ns, S5_STATE), F32),
                   jax.ShapeDtypeStruct((g, ns, S5_STATE), F32)],
        compiler_params=_compiler_params(("arbitrary",)),
        name="s5_params",
    )(us_t, h0r_t, h0i_t, ar_row, ai_row, ar_col, ai_col, ldt3, b_r, b_i, twice(c_r), twice(c_i))


PIECES = LANE_TILE // S5_GROUP
RELAYOUT_ROWS = S5_CHUNK * S5_CHUNK


def _piece_transpose(tiles):
    tiles = list(tiles)
    piece = lax.broadcasted_iota(jnp.int32, tiles[0].shape, 1) // S5_GROUP
    d = PIECES // 2
    while d:
        keep_low = (piece & d) == 0
        for k in range(PIECES):
            if k & d:
                continue
            a, b = tiles[k], tiles[k + d]
            tiles[k] = jnp.where(keep_low, a, pltpu.roll(b, S5_GROUP * d, axis=1))
            tiles[k + d] = jnp.where(keep_low, pltpu.roll(a, LANE_TILE - S5_GROUP * d, axis=1), b)
        d //= 2
    return tiles


def _s5_seq_kernel(u_ref, mt_ref, gt_ref, gts_ref, wf_ref, coef_ref, ys_ref, hl_ref,
                   uflat_ref, e_ref, es_ref, hin_ref, yflat_ref, *, nchunk):
    rr = RELAYOUT_ROWS
    ro = lax.broadcasted_iota(jnp.int32, (rr, rr), 0)
    ri = lax.broadcasted_iota(jnp.int32, (rr, rr), 1)
    perm = jnp.where((ro // S5_CHUNK == ri % S5_CHUNK) & (ro % S5_CHUNK == ri // S5_CHUNK), 1.0, 0.0).astype(BF16)

    def gather_in(m, carry):
        rows = pl.ds(pl.multiple_of(m * rr, rr), rr)
        xp = jnp.dot(perm, u_ref[rows, :].astype(BF16), preferred_element_type=F32)
        crow = pl.ds(pl.multiple_of(m * S5_CHUNK, S5_CHUNK), S5_CHUNK)
        for j in range(S5_FLAT // LANE_TILE):
            for cb in range(S5_WIDTH // LANE_TILE):
                tiles = [xp[(PIECES * j + k) * S5_CHUNK:(PIECES * j + k + 1) * S5_CHUNK,
                            cb * LANE_TILE:(cb + 1) * LANE_TILE] for k in range(PIECES)]
                for p, tile in enumerate(_piece_transpose(tiles)):
                    uflat_ref[PIECES * cb + p, crow, j * LANE_TILE:(j + 1) * LANE_TILE] = tile.astype(BF16)
        return carry

    lax.fori_loop(0, nchunk // S5_CHUNK, gather_in, 0, unroll=2)

    for g in range(S5_GROUPS):
        lanes = slice(g * LANE_TILE, (g + 1) * LANE_TILE)
        ug = uflat_ref[g]
        e_ref[:, lanes] = _dot_nt(ug, gt_ref[g])
        es_ref[:, lanes] = _dot_nt(ug, gts_ref[g])

    ca = coef_ref[0:1, :]
    cb = coef_ref[1:2, :]
    cbs = coef_ref[2:3, :]

    def scan(c, carry):
        h, hs = carry
        row = pl.ds(c, 1)
        hin_ref[row, :] = h
        return ca * h + cb * hs + e_ref[row, :], ca * hs + cbs * h + es_ref[row, :]

    zero = jnp.zeros((1, S5_GROUPS * LANE_TILE), F32)
    h_last, _ = lax.fori_loop(0, nchunk, scan, (zero, zero))
    hl_ref[0] = h_last

    for g in range(S5_GROUPS):
        lanes = slice(g * LANE_TILE, (g + 1) * LANE_TILE)
        yflat_ref[g] = _dot_nt(uflat_ref[g], mt_ref[g]) + _dot_nt(hin_ref[:, lanes], wf_ref[g])

    def gather_out(m, carry):
        crow = pl.ds(pl.multiple_of(m * S5_CHUNK, S5_CHUNK), S5_CHUNK)
        by_time = [[None] * (S5_WIDTH // LANE_TILE) for _ in range(S5_CHUNK)]
        for j in range(S5_FLAT // LANE_TILE):
            for cb in range(S5_WIDTH // LANE_TILE):
                tiles = [yflat_ref[PIECES * cb + k, crow, j * LANE_TILE:(j + 1) * LANE_TILE] for k in range(PIECES)]
                for p, tile in enumerate(_piece_transpose(tiles)):
                    by_time[PIECES * j + p][cb] = tile
        z = jnp.concatenate([jnp.concatenate(row, axis=1) for row in by_time], axis=0)
        rows = pl.ds(pl.multiple_of(m * rr, rr), rr)
        ys_ref[rows, :] = _dot_x3_left(perm, z)
        return carry

    lax.fori_loop(0, nchunk // S5_CHUNK, gather_out, 0, unroll=2)


def _s5_seq(u, mt, gt, gts, wf, coef, bsz, seq):
    nchunk = seq // S5_CHUNK
    width = S5_GROUPS * LANE_TILE

    def resident(a):
        return pl.BlockSpec(a.shape, lambda b: (0,) * a.ndim, pipeline_mode=pl.Buffered(1))

    return pl.pallas_call(
        functools.partial(_s5_seq_kernel, nchunk=nchunk),
        grid=(bsz,),
        in_specs=[pl.BlockSpec((seq, S5_WIDTH), lambda b: (b, 0))] + [resident(a) for a in (mt, gt, gts, wf, coef)],
        out_specs=[pl.BlockSpec((seq, S5_WIDTH), lambda b: (b, 0)),
                   pl.BlockSpec((1, 1, width), lambda b: (b, 0, 0))],
        out_shape=[jax.ShapeDtypeStruct((bsz * seq, S5_WIDTH), F32),
                   jax.ShapeDtypeStruct((bsz, 1, width), F32)],
        scratch_shapes=[pltpu.VMEM((S5_GROUPS, nchunk, S5_FLAT), BF16),
                        pltpu.VMEM((nchunk, width), F32), pltpu.VMEM((nchunk, width), F32),
                        pltpu.VMEM((nchunk, width), F32),
                        pltpu.VMEM((S5_GROUPS, nchunk, S5_FLAT), F32)],
        compiler_params=_compiler_params(("arbitrary",)),
        name="s5_seq",
    )(u, mt, gt, gts, wf, coef)


def _sample_pre_kernel(qkvz_ref, gates_ref, conv_ref, wconv_ref, alog_row_ref, dtb_row_ref,
                       newconv_ref, ops_ref, sc_ref):
    dk = GDN_HEAD_DIM
    x_new = qkvz_ref[:, 0:QKV_W]
    wc = wconv_ref[...]
    conv = x_new * wc[3:4, :]
    for j in range(CONV_W - 1):
        conv = conv + conv_ref[j] * wc[j:j + 1, :]
    qkv = _silu(conv)
    newconv_ref[0] = conv_ref[1]
    newconv_ref[1] = conv_ref[2]
    newconv_ref[2] = x_new

    beta_c, g_c = _gate_values(gates_ref[...], alog_row_ref[...], dtb_row_ref[...])
    scale = GDN_HEAD_DIM ** -0.5
    lane = lax.broadcasted_iota(jnp.int32, sc_ref.shape, 1)
    sc = jnp.zeros(sc_ref.shape, F32)
    for h in range(GDN_HEADS):
        q = _l2norm(qkv[:, h * dk:(h + 1) * dk]) * scale
        k = _l2norm(qkv[:, GDN_WIDTH + h * dk:GDN_WIDTH + (h + 1) * dk])
        v = qkv[:, 2 * GDN_WIDTH + h * dk:2 * GDN_WIDTH + (h + 1) * dk]
        beta = beta_c[:, h:h + 1]
        eg = jnp.exp(g_c[:, 4 + h:5 + h])
        cols = slice(h * dk, (h + 1) * dk)
        ops_ref[0, :, cols] = (beta * eg) * k
        ops_ref[1, :, cols] = q * eg
        ops_ref[2, :, cols] = k
        ops_ref[3, :, cols] = beta * v
        sc = jnp.where(lane == h, jnp.sum(q * k, axis=-1, keepdims=True), sc)
        sc = jnp.where(lane == 4 + h, eg, sc)
    sc_ref[...] = sc


def _sample_pre(qkvz_s, gates_s, conv_t, wconv, alog_row, dtb_row):
    ns = qkvz_s.shape[0]
    return pl.pallas_call(
        _sample_pre_kernel,
        out_shape=[jax.ShapeDtypeStruct((CONV_W - 1, ns, QKV_W), F32),
                   jax.ShapeDtypeStruct((4, ns, GDN_WIDTH), F32),
                   jax.ShapeDtypeStruct((ns, GATE_LANES), F32)],
        compiler_params=pltpu.CompilerParams(vmem_limit_bytes=VMEM_LIMIT),
        name="sample_pre",
    )(qkvz_s, gates_s, conv_t, wconv, alog_row, dtb_row)


def _sample_state_kernel(s_ref, ops_ref, sc_ref, z_ref, normw_ref, snew_ref, o_ref, *, nb):
    dk = GDN_HEAD_DIM
    row = lax.broadcasted_iota(jnp.int32, (8, dk), 0)
    units = [(j, h) for j in range(nb) for h in range(GDN_HEADS)]
    m1 = {}
    for j, h in units:
        cols = slice(h * dk, (h + 1) * dk)
        w = ops_ref[0, j:j + 1, cols]
        qg = ops_ref[1, j:j + 1, cols]
        lhs = jnp.where(row == 0, jnp.broadcast_to(w, (8, dk)), jnp.broadcast_to(qg, (8, dk)))
        m1[j, h] = _dot(lhs, s_ref[j, h])
    for j, h in units:
        cols = slice(h * dk, (h + 1) * dk)
        k = ops_ref[2, j:j + 1, cols]
        u = ops_ref[3, j:j + 1, cols]
        v_new = u - m1[j, h][0:1, :]
        qk = sc_ref[j:j + 1, h:h + 1]
        eg = sc_ref[j:j + 1, 4 + h:5 + h]
        o_ref[j:j + 1, cols] = m1[j, h][1:2, :] + qk * v_new
        k8 = jnp.where(row == 0, jnp.broadcast_to(k, (8, dk)), 0.0)
        snew_ref[j, h] = s_ref[j, h] * eg + _dot_tn(k8, jnp.broadcast_to(v_new, (8, dk)))
    normw = normw_ref[...]
    for h in range(GDN_HEADS):
        o = o_ref[:, h * dk:(h + 1) * dk]
        z = z_ref[:, h * dk:(h + 1) * dk]
        o_ref[:, h * dk:(h + 1) * dk] = (o * lax.rsqrt(jnp.mean(o * o, axis=-1, keepdims=True) + NORM_EPS)
                                         * normw * _silu(z))


def _sample_state(state, ops, sc, z, normw, nb=8):
    ns = state.shape[0]
    hd = (GDN_HEADS, GDN_HEAD_DIM, GDN_HEAD_DIM)
    return pl.pallas_call(
        functools.partial(_sample_state_kernel, nb=nb),
        grid=(ns // nb,),
        in_specs=[pl.BlockSpec((nb,) + hd, lambda i: (i, 0, 0, 0)),
                  pl.BlockSpec((4, nb, GDN_WIDTH), lambda i: (0, i, 0)),
                  pl.BlockSpec((nb, GATE_LANES), lambda i: (i, 0)),
                  pl.BlockSpec((nb, GDN_WIDTH), lambda i: (i, 0)),
                  _whole(normw.shape)],
        out_specs=[pl.BlockSpec((nb,) + hd, lambda i: (i, 0, 0, 0)),
                   pl.BlockSpec((nb, GDN_WIDTH), lambda i: (i, 0))],
        out_shape=[jax.ShapeDtypeStruct(state.shape, F32),
                   jax.ShapeDtypeStruct((ns, GDN_WIDTH), F32)],
        compiler_params=_compiler_params(("arbitrary",)),
        name="sample_state",
    )(state, ops, sc, z, normw)


def _post_kernel(x_ref, o_ref, ys_ref, u_ref, d_ref, wglu_ref, bglu_ref, wout_ref, g1_ref, b1_ref,
                 wff1_ref, wff2_ref, g2_ref, b2_ref, y_ref, *, ff_chunk):
    n = x_ref.shape[0]
    halves = [slice(0, n // 2), slice(n // 2, n)] if n >= 256 else [slice(0, n)]
    st = [dict(rows=r) for r in halves]

    def head(s):
        r = s["rows"]
        ys = jax.nn.gelu(ys_ref[r, :] + d_ref[...] * u_ref[r, :])
        ys = ys * jax.nn.sigmoid(_dot(ys, wglu_ref[...]) + bglu_ref[...])
        yield
        mix = _dot(o_ref[r, :], wout_ref[0:GDN_WIDTH, :]) + _dot(ys, wout_ref[GDN_WIDTH:, :])
        yield
        s["x1"] = _layernorm(DN_ALPHA * x_ref[r, :] + mix, g1_ref[...], b1_ref[...])
        s["x1b"] = s["x1"].astype(BF16)
        yield

    def mlp(s):
        acc = jnp.zeros(s["x1"].shape, F32)
        for f in range(D_FF // ff_chunk):
            hcol = jnp.dot(s["x1b"], wff1_ref[:, f * ff_chunk:(f + 1) * ff_chunk], preferred_element_type=F32)
            hcol = jnp.square(jnp.maximum(hcol, 0.0))
            yield
            acc = acc + _dot(hcol, wff2_ref[f * ff_chunk:(f + 1) * ff_chunk, :])
            yield
        s["acc"] = acc

    def tail(s):
        y_ref[s["rows"], :] = _layernorm(DN_ALPHA * s["x1"] + s["acc"], g2_ref[...], b2_ref[...])
        yield

    _interleave(head(st[0]))
    for i, s in enumerate(st):
        others = [head(st[i + 1])] if i + 1 < len(st) else []
        if i > 0:
            others.append(tail(st[i - 1]))
        _interleave(mlp(s), *others)
    _interleave(tail(st[-1]))


def _post(x, o, ys, u, *weights, tile):
    n = x.shape[0]
    tok = lambda w: pl.BlockSpec((tile, w), lambda i: (i, 0))

    def resident(a):
        return pl.BlockSpec(a.shape, lambda i: (0,) * a.ndim, pipeline_mode=pl.Buffered(1))

    return pl.pallas_call(
        functools.partial(_post_kernel, ff_chunk=1024),
        grid=(n // tile,),
        in_specs=[tok(D_MODEL), tok(GDN_WIDTH), tok(S5_WIDTH), tok(S5_WIDTH)] + [resident(a) for a in weights],
        out_specs=tok(D_MODEL),
        out_shape=jax.ShapeDtypeStruct((n, D_MODEL), F32),
        compiler_params=_compiler_params(("arbitrary",)),
        name="post",
    )(x, o, ys, u, *weights)


def kernel(x_prompt, x_sample, state_gdn, state_conv, state_ssm_re, state_ssm_im, w_in, w_conv, gdn_a_log,
           gdn_dt_bias, gdn_norm_w, s5_a_re, s5_a_im, s5_b_re, s5_b_im, s5_c_re, s5_c_im, s5_d, s5_log_dt,
           w_glu, b_glu, w_out, ln1_g, ln1_b, w_ff1, w_ff2, ln2_g, ln2_b):
    bsz, seq, _ = x_prompt.shape
    ns = x_sample.shape[0]
    l = 0

    w = w_in[l]
    i_u = QKVZ_W + 2 * GDN_HEADS
    w_cat = jnp.concatenate([w[:, :QKVZ_W], w[:, i_u:], w[:, QKVZ_W:i_u],
                             jnp.zeros((D_MODEL, GATE_LANES - 2 * GDN_HEADS), F32)], axis=1).astype(BF16)
    lane_pad = (GDN_HEADS, GATE_LANES - 2 * GDN_HEADS)
    alog_row = jnp.pad(gdn_a_log[l], lane_pad).reshape(1, GATE_LANES)
    dtb_row = jnp.pad(gdn_dt_bias[l], lane_pad).reshape(1, GATE_LANES)
    normw = gdn_norm_w[l].reshape(1, GDN_HEAD_DIM)
    wconv = w_conv[l]
    row = lambda a: a.reshape(1, -1)
    post_w = (row(s5_d[l]), w_glu[l].astype(BF16), row(b_glu[l]), w_out[l].astype(BF16), row(ln1_g[l]), row(ln1_b[l]),
              w_ff1[l].astype(BF16), w_ff2[l].astype(BF16), row(ln2_g[l]), row(ln2_b[l]))

    xp = x_prompt.reshape(bsz * seq, D_MODEL)
    xs = x_sample.reshape(ns, D_MODEL)
    qkvz_p, gates_p, u_p, tail_p = _proj_conv(xp, w_cat, wconv, seq, tile=min(512, seq))
    qkvz_s, gates_s, u_s = _proj(xs, w_cat, tile=ns)

    o_p, gdn_p = _gdn_pair(qkvz_p.reshape(bsz, seq, QKVZ_W), gates_p.reshape(bsz, seq, GATE_LANES),
                           alog_row, dtb_row, normw)
    o_p = o_p.reshape(bsz * seq, GDN_WIDTH)
    conv_p = tail_p[:, 8 - (CONV_W - 1):, :]

    conv_t = jnp.transpose(state_conv[l], (1, 0, 2))
    newconv_t, ops_s, sc = _sample_pre(qkvz_s, gates_s, conv_t, wconv, alog_row, dtb_row)
    gdn_s, o_s = _sample_state(state_gdn[l], ops_s, sc, qkvz_s[:, QKV_W:], normw)
    conv_s = jnp.transpose(newconv_t, (1, 0, 2))

    us_t = u_s.reshape(ns, S5_GROUPS, S5_GROUP).transpose(1, 0, 2)
    h0r_t = state_ssm_re[l].transpose(1, 0, 2)
    h0i_t = state_ssm_im[l].transpose(1, 0, 2)
    mt, gt, gts, wf, coef, ys_t, hsr, hsi = _s5_params(us_t, h0r_t, h0i_t, s5_a_re[l], s5_a_im[l], s5_log_dt[l],
                                                       s5_b_re[l], s5_b_im[l], s5_c_re[l], s5_c_im[l])
    coef_rows = coef.transpose(1, 0, 2).reshape(8, S5_GROUPS * LANE_TILE)
    ys_p, h_last = _s5_seq(u_p, mt, gt, gts, wf, coef_rows, bsz, seq)
    h_last = h_last.reshape(bsz, S5_GROUPS, 2 * S5_STATE)
    ys_s = ys_t.transpose(1, 0, 2).reshape(ns, S5_WIDTH)

    y_p = _post(xp, o_p, ys_p, u_p, *post_w, tile=512)
    y_s = _post(xs, o_s, ys_s, u_s, *post_w, tile=ns)

    t3 = lambda a: a.transpose(1, 0, 2)[None]
    return (y_p.reshape(bsz, seq, D_MODEL), y_s.reshape(ns, 1, D_MODEL),
            gdn_p[None], conv_p[None], h_last[None, :, :, :S5_STATE], h_last[None, :, :, S5_STATE:],
            gdn_s[None], conv_s[None], t3(hsr), t3(hsi))
```
